```python
import functools
import math
import jax
import jax.numpy as jnp
from jax import lax
import numpy as np

D_MODEL = 2048
BATCH = 8
SEQ = 2048
DEPTH = 2
DEC_BATCH = 128
DEC_SEQ = 4
PAST_LEN = 2048
PAGE_SIZE = 128

ATTN_WIDTH = D_MODEL // 2
SSM_WIDTH = D_MODEL - ATTN_WIDTH
HEAD_DIM = 128
N_HEADS = ATTN_WIDTH // HEAD_DIM
N_KV_HEADS = N_HEADS // 2
KV_GROUP = N_HEADS // N_KV_HEADS
KV_WIDTH = N_KV_HEADS * HEAD_DIM
PROJ_WIDTH = ATTN_WIDTH + 2 * KV_WIDTH + SSM_WIDTH
MOBA_BLOCK = 256
MOBA_TOPK = 3
Q_CHUNK = 16
SSM_GROUP_CH = 16
SSM_GROUPS = SSM_WIDTH // SSM_GROUP_CH
SSM_STATE = 64
DT_MIN = 1e-3
DT_MAX = 1e-1
N_EXPERTS = 16
N_EXPERT_GROUPS = 4
EXPERTS_PER_GROUP = N_EXPERTS // N_EXPERT_GROUPS
MOE_TOPK = 2
D_FF_EXPERT = D_MODEL // 2
MOE_BLOCK = 256
DN_ALPHA = (2 * DEPTH) ** 0.25
DN_BETA = (8 * DEPTH) ** -0.25
LN_EPS = 1e-5
RMS_EPS = 1e-6
NEG_INF = float('-inf')

kernel_name = 'hybrid_moba_s5_grouped_moe_step'


def _layer_norm(x, g, b):
    xf = x.astype(jnp.float32)
    mu = jnp.mean(xf, axis=-1, keepdims=True)
    var = jnp.mean(jnp.square(xf - mu), axis=-1, keepdims=True)
    return ((xf - mu) * lax.rsqrt(var + LN_EPS) * g + b).astype(x.dtype)


def _rms_norm(x, g):
    xf = x.astype(jnp.float32)
    return (xf * lax.rsqrt(jnp.mean(jnp.square(xf), axis=-1, keepdims=True) + RMS_EPS) * g).astype(x.dtype)


def _alibi_slopes():
    return 2.0 ** (-8.0 * jnp.arange(1, N_HEADS + 1, dtype=jnp.float32) / N_HEADS)


def _kv_head_of_q():
    return jnp.arange(N_HEADS) // KV_GROUP


def _in_proj(x, w_in):
    bsz, s = x.shape[:2]
    h = jnp.einsum('bsd,dp->bsp', x, w_in)
    q = h[..., :ATTN_WIDTH].reshape(bsz, s, N_HEADS, HEAD_DIM) * (HEAD_DIM ** -0.5)
    k = h[..., ATTN_WIDTH:ATTN_WIDTH + KV_WIDTH].reshape(bsz, s, N_KV_HEADS, HEAD_DIM)
    v = h[..., ATTN_WIDTH + KV_WIDTH:ATTN_WIDTH + 2 * KV_WIDTH].reshape(bsz, s, N_KV_HEADS, HEAD_DIM)
    u = h[..., ATTN_WIDTH + 2 * KV_WIDTH:]
    return q, k, v, u


def _select_blocks(q, k_mean, n_past):
    g = jnp.einsum('bqhd,bnhd->bqhn', q.astype(jnp.float32), k_mean[:, :, _kv_head_of_q()])
    nb = k_mean.shape[1]
    g = jnp.where(jnp.arange(nb) < n_past, g, NEG_INF)
    _, sel = lax.top_k(g, min(MOBA_TOPK, nb))
    return sel, sel < n_past


def _moba_core(q, qpos, k_sel, v_sel, pos_sel, ok_sel, k_loc, v_loc, pos_loc):
    slopes = _alibi_slopes()
    kv_h = _kv_head_of_q()
    qf = q.astype(jnp.float32)
    s_sel = jnp.einsum('bqhd,bqhld->bqhl', qf, k_sel.astype(jnp.float32))
    d_sel = (qpos[None, :, None, None] - pos_sel).astype(jnp.float32)
    s_sel = jnp.where(ok_sel, s_sel - slopes[:, None] * d_sel, NEG_INF)
    s_loc = jnp.einsum('bqhd,blhd->bqhl', qf, k_loc[:, :, kv_h].astype(jnp.float32))
    d_loc = qpos[:, None] - pos_loc[None, :]
    s_loc = jnp.where((d_loc >= 0)[None, :, None, :],
                      s_loc - slopes[:, None] * d_loc[:, None, :].astype(jnp.float32), NEG_INF)
    p = jax.nn.softmax(jnp.concatenate([s_sel, s_loc], axis=-1), axis=-1)
    n_sel = s_sel.shape[-1]
    o = (jnp.einsum('bqhl,bqhld->bqhd', p[..., :n_sel], v_sel.astype(jnp.float32))
         + jnp.einsum('bqhl,blhd->bqhd', p[..., n_sel:], v_loc[:, :, kv_h].astype(jnp.float32)))
    return o.astype(q.dtype)


def _moba_prompt(q, k, v):
    bsz, s = q.shape[:2]
    n_blk = -(-s // MOBA_BLOCK)
    pad = n_blk * MOBA_BLOCK - s
    kb = jnp.pad(k, ((0, 0), (0, pad), (0, 0), (0, 0))).reshape(bsz, n_blk, MOBA_BLOCK, N_KV_HEADS, HEAD_DIM)
    vb = jnp.pad(v, ((0, 0), (0, pad), (0, 0), (0, 0))).reshape(bsz, n_blk, MOBA_BLOCK, N_KV_HEADS, HEAD_DIM)
    k_mean = jnp.mean(kb, axis=2, dtype=jnp.float32)
    n_chunks = s // Q_CHUNK
    bidx = jnp.arange(bsz)[:, None, None, None]
    hidx = _kv_head_of_q()[None, None, :, None]

    def one_chunk(args):
        q_c, c = args
        start = c * Q_CHUNK
        own = start // MOBA_BLOCK
        qpos = start + jnp.arange(Q_CHUNK)
        sel, ok = _select_blocks(q_c, k_mean, own)
        n_k = sel.shape[-1]
        k_sel = kb[bidx, sel, :, hidx].reshape(bsz, Q_CHUNK, N_HEADS, n_k * MOBA_BLOCK, HEAD_DIM)
        v_sel = vb[bidx, sel, :, hidx].reshape(bsz, Q_CHUNK, N_HEADS, n_k * MOBA_BLOCK, HEAD_DIM)
        pos_sel = (sel[..., None] * MOBA_BLOCK + jnp.arange(MOBA_BLOCK)).reshape(bsz, Q_CHUNK, N_HEADS, -1)
        ok_sel = jnp.broadcast_to(ok[..., None], sel.shape + (MOBA_BLOCK,)).reshape(bsz, Q_CHUNK, N_HEADS, -1)
        k_loc = lax.dynamic_index_in_dim(kb, own, axis=1, keepdims=False)
        v_loc = lax.dynamic_index_in_dim(vb, own, axis=1, keepdims=False)
        pos_loc = own * MOBA_BLOCK + jnp.arange(MOBA_BLOCK)
        return _moba_core(q_c, qpos, k_sel, v_sel, pos_sel, ok_sel, k_loc, v_loc, pos_loc)

    qc = q.reshape(bsz, n_chunks, Q_CHUNK, N_HEADS, HEAD_DIM).swapaxes(0, 1)
    out = lax.map(one_chunk, (qc, jnp.arange(n_chunks)))
    return out.swapaxes(0, 1).reshape(bsz, s, N_HEADS, HEAD_DIM)


def _moba_sample(q, k_new, v_new, k_pool, v_pool, page_table):
    n_dec, t_new = q.shape[:2]
    n_pages = page_table.shape[1]
    past = n_pages * PAGE_SIZE
    ppb = MOBA_BLOCK // PAGE_SIZE
    n_blk = -(-n_pages // ppb)
    own = past // MOBA_BLOCK
    pt = jnp.pad(page_table, ((0, 0), (0, n_blk * ppb - n_pages)))
    page_ok = (jnp.arange(n_blk * ppb) < n_pages).astype(jnp.float32)
    page_sum = jnp.sum(k_pool, axis=1, dtype=jnp.float32)
    k_mean = (page_sum[pt] * page_ok[None, :, None, None]).reshape(
        n_dec, n_blk, ppb, N_KV_HEADS, HEAD_DIM).sum(axis=2) / MOBA_BLOCK
    sel, ok = _select_blocks(q, k_mean, own)
    n_k = sel.shape[-1]
    r_pages = n_pages - own * ppb
    own_pages = page_table[:, own * ppb:]
    k_loc = jnp.concatenate([k_pool[own_pages].reshape(n_dec, r_pages * PAGE_SIZE, N_KV_HEADS, HEAD_DIM),
                             k_new.astype(k_pool.dtype)], axis=1)
    v_loc = jnp.concatenate([v_pool[own_pages].reshape(n_dec, r_pages * PAGE_SIZE, N_KV_HEADS, HEAD_DIM),
                             v_new.astype(v_pool.dtype)], axis=1)
    pos_loc = own * MOBA_BLOCK + jnp.arange(r_pages * PAGE_SIZE + t_new)
    qpos = past + jnp.arange(t_new)
    bidx = jnp.arange(n_dec)[:, None, None, None]
    hidx = _kv_head_of_q()[None, :, None, None]

    def one_token(args):
        q_t, qpos_t, sel_t, ok_t = args
        pages = pt[bidx, sel_t[..., None] * ppb + jnp.arange(ppb)]
        k_sel = k_pool[pages, :, hidx].reshape(n_dec, 1, N_HEADS, n_k * MOBA_BLOCK, HEAD_DIM)
        v_sel = v_pool[pages, :, hidx].reshape(n_dec, 1, N_HEADS, n_k * MOBA_BLOCK, HEAD_DIM)
        pos_sel = (sel_t[..., None] * MOBA_BLOCK + jnp.arange(MOBA_BLOCK)).reshape(n_dec, 1, N_HEADS, -1)
        ok_sel = jnp.broadcast_to(ok_t[..., None], sel_t.shape + (MOBA_BLOCK,)).reshape(n_dec, 1, N_HEADS, -1)
        o = _moba_core(q_t[:, None], qpos_t[None], k_sel, v_sel, pos_sel, ok_sel, k_loc, v_loc, pos_loc)
        return o[:, 0]

    out = lax.map(one_token, (q.swapaxes(0, 1), qpos, sel.swapaxes(0, 1), ok.swapaxes(0, 1)))
    return out.swapaxes(0, 1)


def _complex_affine(left, right):
    ar1, ai1, br1, bi1 = left
    ar2, ai2, br2, bi2 = right
    return (ar2 * ar1 - ai2 * ai1,
            ar2 * ai1 + ai2 * ar1,
            ar2 * br1 - ai2 * bi1 + br2,
            ar2 * bi1 + ai2 * br1 + bi2)


def _s5_mixer(u, s0_re, s0_im, a_re, a_im, log_dt, b_re, b_im, c_re, c_im, d_skip, w_glu):
    bsz, s = u.shape[:2]
    f32 = jnp.float32
    a_re = a_re.astype(f32)
    a_im = a_im.astype(f32)
    dt = jnp.exp(log_dt.astype(f32))
    mag = jnp.exp(a_re * dt)
    ab_re = mag * jnp.cos(a_im * dt)
    ab_im = mag * jnp.sin(a_im * dt)
    den = a_re * a_re + a_im * a_im
    f_re = ((ab_re - 1.0) * a_re + ab_im * a_im) / den
    f_im = (ab_im * a_re - (ab_re - 1.0) * a_im) / den
    br = b_re.astype(f32)
    bi = b_im.astype(f32)
    bb_re = f_re[..., None] * br - f_im[..., None] * bi
    bb_im = f_re[..., None] * bi + f_im[..., None] * br
    uf = u.astype(f32).reshape(bsz, s, SSM_GROUPS, SSM_GROUP_CH)
    x_re = jnp.einsum('bsgc,gpc->bsgp', uf, bb_re)
    x_im = jnp.einsum('bsgc,gpc->bsgp', uf, bb_im)
    if s0_re is not None:
        s0r = s0_re.astype(f32)
        s0i = s0_im.astype(f32)
        x_re = x_re.at[:, 0].add(ab_re * s0r - ab_im * s0i)
        x_im = x_im.at[:, 0].add(ab_re * s0i + ab_im * s0r)
    shp = x_re.shape
    _, _, h_re, h_im = lax.associative_scan(
        _complex_affine,
        (jnp.broadcast_to(ab_re, shp), jnp.broadcast_to(ab_im, shp), x_re, x_im), axis=1)
    y = (jnp.einsum('bsgp,gcp->bsgc', h_re, c_re.astype(f32))
         - jnp.einsum('bsgp,gcp->bsgc', h_im, c_im.astype(f32))
         + d_skip.astype(f32).reshape(SSM_GROUPS, SSM_GROUP_CH) * uf)
    y = jax.nn.gelu(y.reshape(bsz, s, SSM_WIDTH))
    y = y * jax.nn.sigmoid(jnp.einsum('bsc,ce->bse', y, w_glu.astype(f32)))
    return y.astype(u.dtype), h_re[:, -1], h_im[:, -1]


def _moe_ffn(x, router_w, router_b, w_gate, w_up, w_down):
    shp = x.shape
    xt = x.reshape(-1, D_MODEL)
    n = xt.shape[0]
    probs = jax.nn.softmax(jnp.dot(xt.astype(jnp.float32), router_w.astype(jnp.float32)), axis=-1)
    biased = (probs + router_b.astype(jnp.float32)).reshape(n, N_EXPERT_GROUPS, EXPERTS_PER_GROUP)
    grp = jnp.argmax(lax.top_k(biased, MOE_TOPK)[0].sum(-1), axis=-1)
    in_grp = jnp.take_along_axis(biased, grp[:, None, None], axis=1)[:, 0]
    _, loc = lax.top_k(in_grp, MOE_TOPK)
    eid = grp[:, None] * EXPERTS_PER_GROUP + loc
    gate = jnp.take_along_axis(probs, eid, axis=1)
    gate = gate / jnp.sum(gate, axis=-1, keepdims=True)
    n_assign = n * MOE_TOPK
    e_flat = eid.reshape(-1)
    tok = jnp.arange(n_assign) // MOE_TOPK
    order = jnp.argsort(e_flat)
    e_s = e_flat[order]
    counts = jnp.bincount(e_flat, length=N_EXPERTS)
    padded = (counts + MOE_BLOCK - 1) // MOE_BLOCK * MOE_BLOCK
    pad_end = jnp.cumsum(padded)
    pad_start = pad_end - padded
    start = jnp.cumsum(counts) - counts
    dest = pad_start[e_s] + jnp.arange(n_assign) - start[e_s]
    n_blocks = -(-(n_assign + N_EXPERTS * (MOE_BLOCK - 1)) // MOE_BLOCK)
    n_slots = n_blocks * MOE_BLOCK
    slot_tok = jnp.full((n_slots,), n, jnp.int32).at[dest].set(tok[order].astype(jnp.int32))
    slot_gate = jnp.zeros((n_slots,), jnp.float32).at[dest].set(gate.reshape(-1)[order])
    blk_exp = jnp.minimum(jnp.searchsorted(pad_end, jnp.arange(n_blocks) * MOE_BLOCK, side='right'), N_EXPERTS - 1)
    x_pad = jnp.concatenate([xt, jnp.zeros((1, D_MODEL), xt.dtype)], axis=0)
    xs = x_pad[slot_tok].reshape(n_blocks, MOE_BLOCK, D_MODEL)

    def expert_block(args):
        xb, e = args
        h = jax.nn.silu(xb @ w_gate[e]) * (xb @ w_up[e])
        return h @ w_down[e]

    ys = lax.map(expert_block, (xs, blk_exp)).reshape(n_slots, D_MODEL)
    out = jnp.zeros((n + 1, D_MODEL), jnp.float32).at[slot_tok].add(ys.astype(jnp.float32) * slot_gate[:, None])
    return out[:n].astype(x.dtype).reshape(shp)


def _trunk_layer(x, attend, s0_re, s0_im, w_in, w_out, attn_g, ssm_g, a_re, a_im, log_dt, b_re, b_im,
                 c_re, c_im, d_skip, w_glu, ln1_g, ln1_b, router_w, router_b, w_gate, w_up, w_down, ln2_g, ln2_b):
    bsz, s = x.shape[:2]
    q, k, v, u = _in_proj(x, w_in)
    att = attend(q, k, v).reshape(bsz, s, ATTN_WIDTH)
    ssm, h_re, h_im = _s5_mixer(u, s0_re, s0_im, a_re, a_im, log_dt, b_re, b_im, c_re, c_im, d_skip, w_glu)
    merged = jnp.concatenate([_rms_norm(att, attn_g), _rms_norm(ssm, ssm_g)], axis=-1)
    mix = jnp.einsum('bsm,md->bsd', merged, w_out)
    x = _layer_norm(DN_ALPHA * x + mix, ln1_g, ln1_b)
    x = _layer_norm(DN_ALPHA * x + _moe_ffn(x, router_w, router_b, w_gate, w_up, w_down), ln2_g, ln2_b)
    return x, k, v, h_re, h_im


def setup_inputs(seed: int = 0) -> dict:
    key = jax.random.key(seed)
    ks = jax.random.split(key, 29)
    f32 = jnp.float32
    n_pages = PAST_LEN // PAGE_SIZE
    n_pool = (DEC_BATCH * n_pages * 5) // 4
    nrm = lambda k, shp: jax.random.normal(k, shp, f32)
    col_scale = jnp.concatenate([jnp.ones((ATTN_WIDTH + KV_WIDTH,), f32), jnp.full((KV_WIDTH,), DN_BETA, f32),
                                 jnp.ones((SSM_WIDTH,), f32)])
    page_table = jax.random.permutation(ks[6], n_pool)[:DEC_BATCH * n_pages].reshape(DEC_BATCH, n_pages).astype(jnp.int32)
    a_im0 = math.pi * jnp.arange(SSM_STATE, dtype=f32)
    return {
        'x_prompt': nrm(ks[0], (BATCH, SEQ, D_MODEL)),
        'x_sample': nrm(ks[1], (DEC_BATCH, DEC_SEQ, D_MODEL)),
        'cache_k': nrm(ks[2], (DEPTH, n_pool, PAGE_SIZE, N_KV_HEADS, HEAD_DIM)),
        'cache_v': nrm(ks[3], (DEPTH, n_pool, PAGE_SIZE, N_KV_HEADS, HEAD_DIM)) * DN_BETA,
        'state_ssm_re': nrm(ks[4], (DEPTH, DEC_BATCH, SSM_GROUPS, SSM_STATE)) * 0.1,
        'state_ssm_im': nrm(ks[5], (DEPTH, DEC_BATCH, SSM_GROUPS, SSM_STATE)) * 0.1,
        'page_table': page_table,
        'w_in': nrm(ks[7], (DEPTH, D_MODEL, PROJ_WIDTH)) * (D_MODEL ** -0.5) * col_scale,
        'w_out': nrm(ks[8], (DEPTH, D_MODEL, D_MODEL)) * (D_MODEL ** -0.5) * DN_BETA,
        'attn_norm_g': 1.0 + 0.02 * nrm(ks[9], (DEPTH, ATTN_WIDTH)),
        'ssm_norm_g': 1.0 + 0.02 * nrm(ks[10], (DEPTH, SSM_WIDTH)),
        'ssm_a_re': -0.5 + 0.01 * nrm(ks[11], (DEPTH, SSM_GROUPS, SSM_STATE)),
        'ssm_a_im': a_im0 + 0.01 * nrm(ks[12], (DEPTH, SSM_GROUPS, SSM_STATE)),
        'ssm_log_dt': jax.random.uniform(ks[13], (DEPTH, SSM_GROUPS, SSM_STATE), f32,
                                         minval=math.log(DT_MIN), maxval=math.log(DT_MAX)),
        'ssm_b_re': nrm(ks[14], (DEPTH, SSM_GROUPS, SSM_STATE, SSM_GROUP_CH)) * (2 * SSM_GROUP_CH) ** -0.5,
        'ssm_b_im': nrm(ks[15], (DEPTH, SSM_GROUPS, SSM_STATE, SSM_GROUP_CH)) * (2 * SSM_GROUP_CH) ** -0.5,
        'ssm_c_re': nrm(ks[16], (DEPTH, SSM_GROUPS, SSM_GROUP_CH, SSM_STATE)) * (2 * SSM_STATE) ** -0.5,
        'ssm_c_im': nrm(ks[17], (DEPTH, SSM_GROUPS, SSM_GROUP_CH, SSM_STATE)) * (2 * SSM_STATE) ** -0.5,
        'ssm_d': nrm(ks[18], (DEPTH, SSM_WIDTH)),
        'ssm_w_glu': nrm(ks[19], (DEPTH, SSM_WIDTH, SSM_WIDTH)) * (SSM_WIDTH ** -0.5),
        'ln1_g': 1.0 + 0.02 * nrm(ks[20], (DEPTH, D_MODEL)),
        'ln1_b': 0.02 * nrm(ks[21], (DEPTH, D_MODEL)),
        'router_w': nrm(ks[22], (D_MODEL, N_EXPERTS)) * (D_MODEL ** -0.5),
        'router_b': 0.01 * nrm(ks[23], (N_EXPERTS,)),
        'moe_w_gate': nrm(ks[24], (DEPTH, N_EXPERTS, D_MODEL, D_FF_EXPERT)) * (D_MODEL ** -0.5),
        'moe_w_up': nrm(ks[25], (DEPTH, N_EXPERTS, D_MODEL, D_FF_EXPERT)) * (D_MODEL ** -0.5) * DN_BETA,
        'moe_w_down': nrm(ks[26], (DEPTH, N_EXPERTS, D_FF_EXPERT, D_MODEL)) * (D_FF_EXPERT ** -0.5) * DN_BETA,
        'ln2_g': 1.0 + 0.02 * nrm(ks[27], (DEPTH, D_MODEL)),
        'ln2_b': 0.02 * nrm(ks[28], (DEPTH, D_MODEL)),
    }


def reference(x_prompt, x_sample, cache_k, cache_v, state_ssm_re, state_ssm_im, page_table,
              w_in, w_out, attn_norm_g, ssm_norm_g, ssm_a_re, ssm_a_im, ssm_log_dt, ssm_b_re, ssm_b_im,
              ssm_c_re, ssm_c_im, ssm_d, ssm_w_glu, ln1_g, ln1_b, router_w, router_b,
              moe_w_gate, moe_w_up, moe_w_down, ln2_g, ln2_b):
    xp = x_prompt
    xs = x_sample
    kp_l, vp_l, hrp_l, hip_l = [], [], [], []
    ks_l, vs_l, hrs_l, his_l = [], [], [], []
    for l in range(DEPTH):
        lp = (w_in[l], w_out[l], attn_norm_g[l], ssm_norm_g[l], ssm_a_re[l], ssm_a_im[l], ssm_log_dt[l],
              ssm_b_re[l], ssm_b_im[l], ssm_c_re[l], ssm_c_im[l], ssm_d[l], ssm_w_glu[l], ln1_g[l], ln1_b[l],
              router_w, router_b, moe_w_gate[l], moe_w_up[l], moe_w_down[l], ln2_g[l], ln2_b[l])
        xp, kp, vp, hrp, hip = _trunk_layer(xp, _moba_prompt, None, None, *lp)
        sample_attend = functools.partial(_moba_sample, k_pool=cache_k[l], v_pool=cache_v[l], page_table=page_table)
        xs, ksm, vsm, hrs, his = _trunk_layer(xs, sample_attend, state_ssm_re[l], state_ssm_im[l], *lp)
        kp_l.append(kp); vp_l.append(vp); hrp_l.append(hrp); hip_l.append(hip)
        ks_l.append(ksm); vs_l.append(vsm); hrs_l.append(hrs); his_l.append(his)
    return (xp, xs,
            jnp.stack(kp_l), jnp.stack(vp_l), jnp.stack(hrp_l), jnp.stack(hip_l),
            jnp.stack(ks_l), jnp.stack(vs_l), jnp.stack(hrs_l), jnp.stack(his_l))
```

```python
import functools
import math

import jax
import jax.numpy as jnp
from jax import lax
from jax.experimental import pallas as pl
from jax.experimental.pallas import tpu as pltpu

F32 = jnp.float32
BF16 = jnp.bfloat16
I32 = jnp.int32
HIGHEST = lax.Precision.HIGHEST

D_MODEL = 2048
ATTN_WIDTH = 1024
SSM_WIDTH = 1024
HEAD_DIM = 128
N_HEADS = 8
N_KV_HEADS = 4
KV_WIDTH = N_KV_HEADS * HEAD_DIM
PROJ_WIDTH = ATTN_WIDTH + 2 * KV_WIDTH + SSM_WIDTH
MOBA_BLOCK = 256
MOBA_TOPK = 3
PAGE_SIZE = 128
SSM_GROUP_CH = 16
SSM_GROUPS = 64
SSM_STATE = 64
N_EXPERTS = 16
N_EXPERT_GROUPS = 4
EXPERTS_PER_GROUP = 4
D_FF = 1024
LN_EPS = 1e-5
RMS_EPS = 1e-6
NEG_INF = float("-inf")
Q_SCALE = HEAD_DIM ** -0.5

LANES = 128
SUBLANES = 8
VMEM_LIMIT = 56 * 1024 * 1024

TILE_T = 32
SLAB = D_MODEL // LANES
SSM_CHUNK = 8 * SSM_GROUP_CH
SSM_CHUNK_STATES = 8 * SSM_STATE
N_SSM_CHUNKS = SSM_WIDTH // SSM_CHUNK
MOE_BLOCK = 256


def _cparams(sem, vmem=VMEM_LIMIT):
    return pltpu.CompilerParams(dimension_semantics=sem, vmem_limit_bytes=vmem)


def _in_proj_prompt_kernel(x_ref, w_ref, q_ref, k_ref, v_ref, u_ref, *, n_b, tt):
    rows = n_b * tt
    x = x_ref[...].reshape(rows, D_MODEL).astype(BF16)
    h = jnp.dot(x, w_ref[...], preferred_element_type=F32)
    q_ref[...] = h[:, :ATTN_WIDTH].reshape(n_b, tt, ATTN_WIDTH)
    k_ref[...] = h[:, ATTN_WIDTH:ATTN_WIDTH + KV_WIDTH].reshape(n_b, tt, KV_WIDTH)
    v_ref[...] = h[:, ATTN_WIDTH + KV_WIDTH:ATTN_WIDTH + 2 * KV_WIDTH].reshape(n_b, tt, KV_WIDTH)
    u0 = ATTN_WIDTH + 2 * KV_WIDTH
    for c in range(N_SSM_CHUNKS):
        for b in range(n_b):
            u_ref[c, pl.ds(b, tt, stride=n_b), :] = h[b * tt:(b + 1) * tt, u0 + c * LANES:u0 + (c + 1) * LANES]


def _in_proj_prompt(x, w16):
    n_b, seq, _ = x.shape
    tt = TILE_T
    nt = seq // tt
    kern = functools.partial(_in_proj_prompt_kernel, n_b=n_b, tt=tt)
    return pl.pallas_call(
        kern,
        out_shape=(
            jax.ShapeDtypeStruct((n_b, seq, ATTN_WIDTH), F32),
            jax.ShapeDtypeStruct((n_b, seq, KV_WIDTH), F32),
            jax.ShapeDtypeStruct((n_b, seq, KV_WIDTH), F32),
            jax.ShapeDtypeStruct((N_SSM_CHUNKS, seq * n_b, LANES), F32),
        ),
        grid=(nt,),
        in_specs=[
            pl.BlockSpec((n_b, tt, D_MODEL), lambda i: (0, i, 0)),
            pl.BlockSpec((D_MODEL, PROJ_WIDTH), lambda i: (0, 0)),
        ],
        out_specs=(
            pl.BlockSpec((n_b, tt, ATTN_WIDTH), lambda i: (0, i, 0)),
            pl.BlockSpec((n_b, tt, KV_WIDTH), lambda i: (0, i, 0)),
            pl.BlockSpec((n_b, tt, KV_WIDTH), lambda i: (0, i, 0)),
            pl.BlockSpec((N_SSM_CHUNKS, tt * n_b, LANES), lambda i: (0, i, 0)),
        ),
        compiler_params=_cparams(("arbitrary",)),
        name="in_proj_prompt",
    )(x, w16)


def _matmul_kernel(x_ref, w_ref, o_ref):
    o_ref[...] = jnp.dot(x_ref[...].astype(BF16), w_ref[...], preferred_element_type=F32)


def _in_proj_sample(x, w16):
    m, k = x.shape
    n = w16.shape[1]
    tm = 256
    return pl.pallas_call(
        _matmul_kernel,
        out_shape=jax.ShapeDtypeStruct((m, n), F32),
        grid=(m // tm,),
        in_specs=[pl.BlockSpec((tm, k), lambda i: (i, 0)), pl.BlockSpec((k, n), lambda i: (0, 0))],
        out_specs=pl.BlockSpec((tm, n), lambda i: (i, 0)),
        compiler_params=_cparams(("arbitrary",)),
        name="in_proj_sample",
    )(x, w16)


def _topk_mask(gate, valid, k):
    nb = gate.shape[1]
    ids = lax.broadcasted_iota(I32, gate.shape, 1)
    gm = jnp.where(valid, gate, NEG_INF)
    cnt = jnp.zeros(gate.shape, I32)
    for j in range(nb):
        gj = gm[:, j:j + 1]
        beats = (gj > gm) | ((gj == gm) & (j < ids))
        cnt = cnt + beats.astype(I32)
    return valid & (cnt < k)


def _attn_prompt_kernel(slopes_ref, q_ref, k_ref, v_ref, o_ref, kmean_ref, k16_ref, v16_ref, *, n_blk):
    g = pl.program_id(1)
    qi = pl.program_id(2)
    blk = MOBA_BLOCK

    @pl.when(qi == 0)
    def _():
        for j in range(n_blk):
            kb = k_ref[j * blk:(j + 1) * blk, :]
            kmean_ref[j:j + 1, :] = jnp.sum(kb, axis=0, keepdims=True) * (1.0 / blk)
        k16_ref[...] = k_ref[...].astype(BF16)
        v16_ref[...] = v_ref[...].astype(BF16)

    kmean = kmean_ref[...]
    r = lax.broadcasted_iota(I32, (blk, blk), 0)
    c = lax.broadcasted_iota(I32, (blk, blk), 1)
    dbase = (r - c).astype(F32)
    causal = c <= r
    blk_ids = lax.broadcasted_iota(I32, (blk, n_blk), 1)
    own = pl.multiple_of(qi * blk, blk)

    for hh in range(N_HEADS // N_KV_HEADS):
        slope = slopes_ref[g * (N_HEADS // N_KV_HEADS) + hh]
        qh = q_ref[:, hh * HEAD_DIM:(hh + 1) * HEAD_DIM] * Q_SCALE
        gate = lax.dot_general(qh, kmean, (((1,), (1,)), ((), ())), precision=HIGHEST,
                               preferred_element_type=F32)
        sel = _topk_mask(gate, blk_ids < qi, MOBA_TOPK).astype(F32)
        q16 = qh.astype(BF16)

        kb = k16_ref[pl.ds(own, blk), :]
        vb = v16_ref[pl.ds(own, blk), :]
        s = lax.dot_general(q16, kb, (((1,), (1,)), ((), ())), preferred_element_type=F32)
        s = jnp.where(causal, s - slope * dbase, NEG_INF)
        m = jnp.max(s, axis=1, keepdims=True)
        p = jnp.exp(s - m)
        l = jnp.sum(p, axis=1, keepdims=True)
        acc = jnp.dot(p.astype(BF16), vb, preferred_element_type=F32)

        def body(j, carry, q16=q16, sel=sel, slope=slope):
            m, l, acc = carry
            off = pl.multiple_of(j * blk, blk)
            kb = k16_ref[pl.ds(off, blk), :]
            vb = v16_ref[pl.ds(off, blk), :]
            s = lax.dot_general(q16, kb, (((1,), (1,)), ((), ())), preferred_element_type=F32)
            dist = ((qi - j) * blk).astype(F32)
            s = s - slope * (dbase + dist)
            selj = jnp.sum(jnp.where(blk_ids == j, sel, 0.0), axis=1, keepdims=True) > 0.5
            s = jnp.where(selj, s, NEG_INF)
            m_new = jnp.maximum(m, jnp.max(s, axis=1, keepdims=True))
            a = jnp.exp(m - m_new)
            p = jnp.exp(s - m_new)
            l = a * l + jnp.sum(p, axis=1, keepdims=True)
            acc = a * acc + jnp.dot(p.astype(BF16), vb, preferred_element_type=F32)
            return m_new, l, acc

        m, l, acc = lax.fori_loop(0, qi, body, (m, l, acc))
        o_ref[:, hh * HEAD_DIM:(hh + 1) * HEAD_DIM] = acc / l


def _attn_prompt(q, k, v, slopes):
    n_b, seq, _ = q.shape
    n_blk = seq // MOBA_BLOCK
    gq = N_HEADS // N_KV_HEADS
    kern = functools.partial(_attn_prompt_kernel, n_blk=n_blk)
    return pl.pallas_call(
        kern,
        out_shape=jax.ShapeDtypeStruct((n_b, seq, ATTN_WIDTH), F32),
        grid_spec=pltpu.PrefetchScalarGridSpec(
            num_scalar_prefetch=1,
            grid=(n_b, N_KV_HEADS, n_blk),
            in_specs=[
                pl.BlockSpec((None, MOBA_BLOCK, gq * HEAD_DIM), lambda b, g, i, s: (b, i, g)),
                pl.BlockSpec((None, seq, HEAD_DIM), lambda b, g, i, s: (b, 0, g)),
                pl.BlockSpec((None, seq, HEAD_DIM), lambda b, g, i, s: (b, 0, g)),
            ],
            out_specs=pl.BlockSpec((None, MOBA_BLOCK, gq * HEAD_DIM), lambda b, g, i, s: (b, i, g)),
            scratch_shapes=[
                pltpu.VMEM((n_blk, HEAD_DIM), F32),
                pltpu.VMEM((seq, HEAD_DIM), BF16),
                pltpu.VMEM((seq, HEAD_DIM), BF16),
            ],
        ),
        compiler_params=_cparams(("arbitrary", "arbitrary", "arbitrary")),
        name="moba_prompt",
    )(slopes, q, k, v)


def _attn_sample_kernel(pt_ref, slopes_ref, q_ref, kn_ref, vn_ref, ka_ref, kb_ref, va_ref, vb_ref, o_ref,
                        m_scr, l_scr, o_scr, km_scr, *, n_blk, t_new, past):
    j = pl.program_id(1)
    gq = N_HEADS // N_KV_HEADS
    rows = gq * t_new
    blk = MOBA_BLOCK
    row_id = lax.broadcasted_iota(I32, (rows, 1), 0)
    t_row = row_id % t_new
    col = lax.broadcasted_iota(I32, (rows, blk), 1)

    def slope_col(g):
        sc = jnp.zeros((rows, 1), F32)
        for hh in range(gq):
            sc = jnp.where(row_id // t_new == hh, slopes_ref[g * gq + hh], sc)
        return sc

    for g in range(N_KV_HEADS):
        k_g = jnp.concatenate([ka_ref[pl.ds(g, PAGE_SIZE, stride=N_KV_HEADS), :],
                               kb_ref[pl.ds(g, PAGE_SIZE, stride=N_KV_HEADS), :]], axis=0)
        v_g = jnp.concatenate([va_ref[pl.ds(g, PAGE_SIZE, stride=N_KV_HEADS), :],
                               vb_ref[pl.ds(g, PAGE_SIZE, stride=N_KV_HEADS), :]], axis=0)
        km_scr[g, pl.ds(j, 1), :] = jnp.sum(k_g, axis=0, keepdims=True) * (1.0 / blk)
        q8 = q_ref[g] * Q_SCALE
        s = lax.dot_general(q8.astype(BF16), k_g.astype(BF16), (((1,), (1,)), ((), ())),
                            preferred_element_type=F32)
        dist = (past + t_row - j * blk - col).astype(F32)
        s = s - slope_col(g) * dist
        m = jnp.max(s, axis=1, keepdims=True)
        p = jnp.exp(s - m)
        l = jnp.sum(p, axis=1, keepdims=True)
        o = jnp.dot(p.astype(BF16), v_g.astype(BF16), preferred_element_type=F32)
        m_scr[g, j] = jnp.broadcast_to(m, (rows, HEAD_DIM))
        l_scr[g, j] = jnp.broadcast_to(l, (rows, HEAD_DIM))
        o_scr[g, j] = o

    @pl.when(j == n_blk - 1)
    def _():
        blk_ids = lax.broadcasted_iota(I32, (rows, n_blk), 1)
        for g in range(N_KV_HEADS):
            q8 = q_ref[g] * Q_SCALE
            gate = lax.dot_general(q8, km_scr[g], (((1,), (1,)), ((), ())), precision=HIGHEST,
                                   preferred_element_type=F32)
            sel = _topk_mask(gate, blk_ids >= 0, MOBA_TOPK)
            kn = kn_ref[g]
            vn = vn_ref[g]
            sc = slope_col(g)
            s_own = []
            for tk in range(t_new):
                sv = jnp.sum(q8 * kn[tk:tk + 1, :], axis=1, keepdims=True)
                sv = sv - sc * (t_row - tk).astype(F32)
                s_own.append(jnp.where(t_row >= tk, sv, NEG_INF))
            m_tot = s_own[0]
            for tk in range(1, t_new):
                m_tot = jnp.maximum(m_tot, s_own[tk])
            m_tot = jnp.broadcast_to(m_tot, (rows, HEAD_DIM))
            for jj in range(n_blk):
                m_tot = jnp.where(sel[:, jj:jj + 1], jnp.maximum(m_tot, m_scr[g, jj]), m_tot)
            num = jnp.zeros((rows, HEAD_DIM), F32)
            den = jnp.zeros((rows, HEAD_DIM), F32)
            for jj in range(n_blk):
                w = jnp.where(sel[:, jj:jj + 1], jnp.exp(m_scr[g, jj] - m_tot), 0.0)
                num = num + w * o_scr[g, jj]
                den = den + w * l_scr[g, jj]
            for tk in range(t_new):
                pw = jnp.exp(jnp.broadcast_to(s_own[tk], (rows, HEAD_DIM)) - m_tot)
                num = num + pw * vn[tk:tk + 1, :]
                den = den + pw
            o_ref[g] = num / den


def _attn_sample(q8, k_new, v_new, cache_k_flat, cache_v_flat, page_table, slopes, layer, n_pool):
    n_dec, _, rows, _ = q8.shape
    t_new = k_new.shape[2]
    n_pages = page_table.shape[1]
    ppb = MOBA_BLOCK // PAGE_SIZE
    n_blk = n_pages // ppb
    past = n_pages * PAGE_SIZE
    page_rows = PAGE_SIZE * N_KV_HEADS
    base = layer * n_pool
    pt_flat = page_table.reshape(-1)
    kern = functools.partial(_attn_sample_kernel, n_blk=n_blk, t_new=t_new, past=past)

    def page_spec(which):
        return pl.BlockSpec((page_rows, HEAD_DIM),
                            lambda b, j, pt, sl: (base + pt[b * n_pages + ppb * j + which], 0))

    seq_spec_q = pl.BlockSpec((None, N_KV_HEADS, rows, HEAD_DIM), lambda b, j, pt, sl: (b, 0, 0, 0))
    seq_spec_n = pl.BlockSpec((None, N_KV_HEADS, t_new, HEAD_DIM), lambda b, j, pt, sl: (b, 0, 0, 0))
    return pl.pallas_call(
        kern,
        out_shape=jax.ShapeDtypeStruct((n_dec, N_KV_HEADS, rows, HEAD_DIM), F32),
        grid_spec=pltpu.PrefetchScalarGridSpec(
            num_scalar_prefetch=2,
            grid=(n_dec, n_blk),
            in_specs=[seq_spec_q, seq_spec_n, seq_spec_n,
                      page_spec(0), page_spec(1), page_spec(0), page_spec(1)],
            out_specs=seq_spec_q,
            scratch_shapes=[
                pltpu.VMEM((N_KV_HEADS, n_blk, rows, HEAD_DIM), F32),
                pltpu.VMEM((N_KV_HEADS, n_blk, rows, HEAD_DIM), F32),
                pltpu.VMEM((N_KV_HEADS, n_blk, rows, HEAD_DIM), F32),
                pltpu.VMEM((N_KV_HEADS, n_blk, HEAD_DIM), F32),
            ],
        ),
        compiler_params=_cparams(("arbitrary", "arbitrary")),
        name="moba_sample",
    )(pt_flat, slopes, q8, k_new, v_new, cache_k_flat, cache_k_flat, cache_v_flat, cache_v_flat)


def _ssm_kernel(u_ref, bre_ref, bim_ref, cre_ref, cim_ref, are_ref, aim_ref, d_ref, s0r_ref, s0i_ref,
                y_ref, hr_ref, hi_ref, xr_ref, xi_ref, str_ref, sti_ref, *, nb, tc):
    ti = pl.program_id(1)

    @pl.when(ti == 0)
    def _():
        str_ref[...] = s0r_ref[...]
        sti_ref[...] = s0i_ref[...]

    u = u_ref[...]
    u16 = u.astype(BF16)
    xr_ref[...] = jnp.dot(u16, bre_ref[...], preferred_element_type=F32)
    xi_ref[...] = jnp.dot(u16, bim_ref[...], preferred_element_type=F32)
    a_re = jnp.broadcast_to(are_ref[...], (nb, SSM_CHUNK_STATES))
    a_im = jnp.broadcast_to(aim_ref[...], (nb, SSM_CHUNK_STATES))

    def step(t, carry):
        h_re, h_im = carry
        sl = pl.ds(pl.multiple_of(t * nb, nb), nb)
        n_re = (a_re * h_re - a_im * h_im) + xr_ref[sl, :]
        n_im = (a_re * h_im + a_im * h_re) + xi_ref[sl, :]
        xr_ref[sl, :] = n_re
        xi_ref[sl, :] = n_im
        return n_re, n_im

    h_re, h_im = lax.fori_loop(0, tc, step, (str_ref[...], sti_ref[...]))
    str_ref[...] = h_re
    sti_ref[...] = h_im
    y = (jnp.dot(xr_ref[...].astype(BF16), cre_ref[...], preferred_element_type=F32)
         - jnp.dot(xi_ref[...].astype(BF16), cim_ref[...], preferred_element_type=F32)
         + d_ref[...] * u)
    y_ref[...] = jax.nn.gelu(y)

    @pl.when(ti == pl.num_programs(1) - 1)
    def _():
        hr_ref[...] = h_re
        hi_ref[...] = h_im


def _ssm(u_chunks, s0_re, s0_im, prm, nb, tc):
    n_rows = u_chunks.shape[1]
    nt = n_rows // (tc * nb)
    rows = tc * nb
    kern = functools.partial(_ssm_kernel, nb=nb, tc=tc)
    cs = SSM_CHUNK_STATES
    n_states = SSM_GROUPS * SSM_STATE
    return pl.pallas_call(
        kern,
        out_shape=(
            jax.ShapeDtypeStruct((N_SSM_CHUNKS, n_rows, LANES), F32),
            jax.ShapeDtypeStruct((nb, n_states), F32),
            jax.ShapeDtypeStruct((nb, n_states), F32),
        ),
        grid=(N_SSM_CHUNKS, nt),
        in_specs=[
            pl.BlockSpec((None, rows, LANES), lambda c, t: (c, t, 0)),
            pl.BlockSpec((None, SSM_CHUNK, cs), lambda c, t: (c, 0, 0)),
            pl.BlockSpec((None, SSM_CHUNK, cs), lambda c, t: (c, 0, 0)),
            pl.BlockSpec((None, cs, SSM_CHUNK), lambda c, t: (c, 0, 0)),
            pl.BlockSpec((None, cs, SSM_CHUNK), lambda c, t: (c, 0, 0)),
            pl.BlockSpec((1, cs), lambda c, t: (0, c)),
            pl.BlockSpec((1, cs), lambda c, t: (0, c)),
            pl.BlockSpec((1, SSM_CHUNK), lambda c, t: (0, c)),
            pl.BlockSpec((nb, cs), lambda c, t: (0, c)),
            pl.BlockSpec((nb, cs), lambda c, t: (0, c)),
        ],
        out_specs=(
            pl.BlockSpec((None, rows, LANES), lambda c, t: (c, t, 0)),
            pl.BlockSpec((nb, cs), lambda c, t: (0, c)),
            pl.BlockSpec((nb, cs), lambda c, t: (0, c)),
        ),
        scratch_shapes=[
            pltpu.VMEM((rows, cs), F32),
            pltpu.VMEM((rows, cs), F32),
            pltpu.VMEM((nb, cs), F32),
            pltpu.VMEM((nb, cs), F32),
        ],
        compiler_params=_cparams(("arbitrary", "arbitrary")),
        name="s5_mixer",
    )(u_chunks, prm["bbd_re"], prm["bbd_im"], prm["cbd_re"], prm["cbd_im"], prm["ab_re"], prm["ab_im"],
      prm["d"], s0_re, s0_im)


def _ssm_params(a_re, a_im, log_dt, b_re, b_im, c_re, c_im, d_skip):
    dt = jnp.exp(log_dt)
    mag = jnp.exp(a_re * dt)
    ab_re = mag * jnp.cos(a_im * dt)
    ab_im = mag * jnp.sin(a_im * dt)
    den = a_re * a_re + a_im * a_im
    f_re = ((ab_re - 1.0) * a_re + ab_im * a_im) / den
    f_im = (ab_im * a_re - (ab_re - 1.0) * a_im) / den
    bb_re = f_re[..., None] * b_re - f_im[..., None] * b_im
    bb_im = f_re[..., None] * b_im + f_im[..., None] * b_re
    gpc = SSM_CHUNK // SSM_GROUP_CH
    eye = jnp.eye(gpc, dtype=F32)

    def b_blockdiag(bb):
        x = bb.reshape(N_SSM_CHUNKS, gpc, SSM_STATE, SSM_GROUP_CH)
        m = jnp.einsum("kgpc,gh->kgchp", x, eye)
        return m.reshape(N_SSM_CHUNKS, gpc * SSM_GROUP_CH, gpc * SSM_STATE).astype(BF16)

    def c_blockdiag(cc):
        x = cc.reshape(N_SSM_CHUNKS, gpc, SSM_GROUP_CH, SSM_STATE)
        m = jnp.einsum("kgcp,gh->kgphc", x, eye)
        return m.reshape(N_SSM_CHUNKS, gpc * SSM_STATE, gpc * SSM_GROUP_CH).astype(BF16)

    return {
        "bbd_re": b_blockdiag(bb_re), "bbd_im": b_blockdiag(bb_im),
        "cbd_re": c_blockdiag(c_re), "cbd_im": c_blockdiag(c_im),
        "ab_re": ab_re.reshape(1, -1), "ab_im": ab_im.reshape(1, -1),
        "d": d_skip.reshape(1, -1),
    }


def _rank_lt(vals, a):
    cnt = jnp.zeros(vals[a].shape, I32)
    for b in range(len(vals)):
        if b == a:
            continue
        before = (vals[b] >= vals[a]) if b < a else (vals[b] > vals[a])
        cnt = cnt + before.astype(I32)
    return cnt


def _router(x1, rwt_ref, rb_ref, eid_ref, gate_ref):
    logits = lax.dot_general(rwt_ref[...], x1, (((1,), (1,)), ((), ())), precision=HIGHEST,
                             preferred_element_type=F32)
    mx = jnp.max(logits, axis=0, keepdims=True)
    ex = jnp.exp(logits - mx)
    probs = ex / jnp.sum(ex, axis=0, keepdims=True)
    biased = probs + rb_ref[...]
    prow = [probs[e:e + 1, :] for e in range(N_EXPERTS)]
    brow = [biased[e:e + 1, :] for e in range(N_EXPERTS)]
    ranks = []
    gscore = []
    for gi in range(N_EXPERT_GROUPS):
        vals = brow[gi * EXPERTS_PER_GROUP:(gi + 1) * EXPERTS_PER_GROUP]
        rk = [_rank_lt(vals, a) for a in range(EXPERTS_PER_GROUP)]
        ranks.append(rk)
        sc = jnp.zeros(vals[0].shape, F32)
        for a in range(EXPERTS_PER_GROUP):
            sc = sc + jnp.where(rk[a] < 2, vals[a], 0.0)
        gscore.append(sc)
    e0 = jnp.zeros(prow[0].shape, I32)
    e1 = jnp.zeros(prow[0].shape, I32)
    p0 = jnp.zeros(prow[0].shape, F32)
    p1 = jnp.zeros(prow[0].shape, F32)
    for gi in range(N_EXPERT_GROUPS):
        chosen = _rank_lt(gscore, gi) == 0
        for a in range(EXPERTS_PER_GROUP):
            e = gi * EXPERTS_PER_GROUP + a
            first = chosen & (ranks[gi][a] == 0)
            second = chosen & (ranks[gi][a] == 1)
            e0 = jnp.where(first, e, e0)
            e1 = jnp.where(second, e, e1)
            p0 = jnp.where(first, prow[e], p0)
            p1 = jnp.where(second, prow[e], p1)
    tot = p0 + p1
    eid_ref[0:1, :] = e0
    eid_ref[1:2, :] = e1
    gate_ref[0:1, :] = p0 / tot
    gate_ref[1:2, :] = p1 / tot


def _postmix_kernel(att_ref, y_ref, x_ref, wglu_ref, wout_ref, ag_ref, sg_ref, lg_ref, lb_ref, rwt_ref, rb_ref,
                    x1_ref, eid_ref, gate_ref, ybuf_ref, *, n_b, tt, alpha):
    rows = x1_ref.shape[0] if n_b is None else n_b * tt
    if n_b is None:
        att = att_ref[...]
        x = x_ref[...]
        for c in range(N_SSM_CHUNKS):
            ybuf_ref[:, c * LANES:(c + 1) * LANES] = y_ref[c]
    else:
        att = att_ref[...].reshape(rows, ATTN_WIDTH)
        x = x_ref[...].reshape(rows, D_MODEL)
        for c in range(N_SSM_CHUNKS):
            for b in range(n_b):
                ybuf_ref[b * tt:(b + 1) * tt, c * LANES:(c + 1) * LANES] = y_ref[c, pl.ds(b, tt, stride=n_b), :]
    y = ybuf_ref[...]
    z = jnp.dot(y.astype(BF16), wglu_ref[...], preferred_element_type=F32)
    sg = y * jax.nn.sigmoid(z)
    ssm_n = sg * lax.rsqrt(jnp.mean(sg * sg, axis=-1, keepdims=True) + RMS_EPS) * sg_ref[...]
    att_n = att * lax.rsqrt(jnp.mean(att * att, axis=-1, keepdims=True) + RMS_EPS) * ag_ref[...]
    mix = (jnp.dot(att_n.astype(BF16), wout_ref[0:ATTN_WIDTH, :], preferred_element_type=F32)
           + jnp.dot(ssm_n.astype(BF16), wout_ref[ATTN_WIDTH:, :], preferred_element_type=F32))
    r = alpha * x + mix
    mu = jnp.mean(r, axis=-1, keepdims=True)
    rc = r - mu
    var = jnp.mean(rc * rc, axis=-1, keepdims=True)
    x1 = rc * lax.rsqrt(var + LN_EPS) * lg_ref[...] + lb_ref[...]
    if n_b is None:
        x1_ref[...] = x1
    else:
        x1_ref[...] = x1.reshape(n_b, tt, D_MODEL)
    _router(x1, rwt_ref, rb_ref, eid_ref, gate_ref)


def _postmix(att, y_chunks, x, lw, alpha, prompt):
    if prompt:
        n_b, seq, _ = x.shape
        tt = TILE_T
        nt = seq // tt
        rows = n_b * tt
        n_tok = n_b * seq
        att_spec = pl.BlockSpec((n_b, tt, ATTN_WIDTH), lambda i: (0, i, 0))
        x_spec = pl.BlockSpec((n_b, tt, D_MODEL), lambda i: (0, i, 0))
        x1_shape = jax.ShapeDtypeStruct((n_b, seq, D_MODEL), F32)
        kern = functools.partial(_postmix_kernel, n_b=n_b, tt=tt, alpha=alpha)
    else:
        n_tok = x.shape[0]
        rows = 256
        nt = n_tok // rows
        att_spec = pl.BlockSpec((rows, ATTN_WIDTH), lambda i: (i, 0))
        x_spec = pl.BlockSpec((rows, D_MODEL), lambda i: (i, 0))
        x1_shape = jax.ShapeDtypeStruct((n_tok, D_MODEL), F32)
        kern = functools.partial(_postmix_kernel, n_b=None, tt=None, alpha=alpha)
    full = lambda shp: pl.BlockSpec(shp, lambda i: tuple(0 for _ in shp))
    return pl.pallas_call(
        kern,
        out_shape=(x1_shape, jax.ShapeDtypeStruct((2, n_tok), I32), jax.ShapeDtypeStruct((2, n_tok), F32)),
        grid=(nt,),
        in_specs=[
            att_spec,
            pl.BlockSpec((N_SSM_CHUNKS, rows, LANES), lambda i: (0, i, 0)),
            x_spec,
            full((SSM_WIDTH, SSM_WIDTH)),
            full((D_MODEL, D_MODEL)),
            full((1, ATTN_WIDTH)),
            full((1, SSM_WIDTH)),
            full((1, D_MODEL)),
            full((1, D_MODEL)),
            full((N_EXPERTS, D_MODEL)),
            full((N_EXPERTS, 1)),
        ],
        out_specs=(x_spec, pl.BlockSpec((2, rows), lambda i: (0, i)), pl.BlockSpec((2, rows), lambda i: (0, i))),
        scratch_shapes=[pltpu.VMEM((rows, SSM_WIDTH), F32)],
        compiler_params=_cparams(("arbitrary",)),
        name="postmix_prompt" if prompt else "postmix_sample",
    )(att, y_chunks, x, lw["w_glu"], lw["w_out"], lw["attn_g"], lw["ssm_g"], lw["ln1_g"], lw["ln1_b"],
      lw["router_wt"], lw["router_b"])


def _slab_rows(row):
    start = row * SLAB
    return pl.ds(start if isinstance(start, int) else pl.multiple_of(start, SLAB), SLAB)


def _slab_copy(src_ref, src_row, dst_ref, dst_row, sem):
    return pltpu.make_async_copy(src_ref.at[_slab_rows(src_row), :], dst_ref.at[_slab_rows(dst_row), :], sem)


def _dispatch_kernel(dest_ref, pad_ref, xp_ref, xs_ref, out_ref, slab_ref, sem, *, n_b, tt, n_prompt_tiles, n_blocks):
    i = pl.program_id(0)
    rows = n_b * tt

    @pl.when(i < n_prompt_tiles)
    def _():
        for s in range(SLAB):
            for b in range(n_b):
                slab_ref[pl.ds(b * tt * SLAB + s, tt, stride=SLAB), :] = xp_ref[b, :, s * LANES:(s + 1) * LANES]

    @pl.when(i >= n_prompt_tiles)
    def _():
        for s in range(SLAB):
            slab_ref[pl.ds(s, rows, stride=SLAB), :] = xs_ref[:, s * LANES:(s + 1) * LANES]

    base = i * (2 * rows)

    def start(r, c):
        for k in range(2):
            _slab_copy(slab_ref, r, out_ref, dest_ref[base + 2 * r + k], sem).start()
        return c

    lax.fori_loop(0, rows, start, 0)

    def wait(r, c):
        for k in range(2):
            _slab_copy(slab_ref, r, out_ref, dest_ref[base + 2 * r + k], sem).wait()
        return c

    lax.fori_loop(0, rows, wait, 0)

    @pl.when(i == pl.num_programs(0) - 1)
    def _():
        slab_ref[...] = jnp.zeros(slab_ref.shape, F32)

        def per_expert(e, c):
            lo = pad_ref[e]
            hi = pad_ref[N_EXPERTS + e]

            def zs(s, c2):
                _slab_copy(slab_ref, 0, out_ref, s, sem).start()
                return c2

            lax.fori_loop(lo, hi, zs, 0)

            def zw(s, c2):
                _slab_copy(slab_ref, 0, out_ref, s, sem).wait()
                return c2

            lax.fori_loop(lo, hi, zw, 0)
            return c

        lax.fori_loop(0, N_EXPERTS, per_expert, 0)

        blk_rows = MOE_BLOCK * SLAB

        def block_copy(b):
            start = pl.multiple_of(b * blk_rows, blk_rows)
            return pltpu.make_async_copy(slab_ref.at[pl.ds(0, blk_rows), :], out_ref.at[pl.ds(start, blk_rows), :], sem)

        def zbs(b, c):
            block_copy(b).start()
            return c

        def zbw(b, c):
            block_copy(b).wait()
            return c

        first_unused = pad_ref[2 * N_EXPERTS - 1] // MOE_BLOCK
        lax.fori_loop(first_unused, n_blocks, zbs, 0)
        lax.fori_loop(first_unused, n_blocks, zbw, 0)


def _dispatch(x1_p, x1_s, dest, pad_bounds, n_slots):
    n_b, seq, _ = x1_p.shape
    tt = TILE_T
    rows = n_b * tt
    assert rows == MOE_BLOCK
    ntp = seq // tt
    nts = x1_s.shape[0] // rows
    kern = functools.partial(_dispatch_kernel, n_b=n_b, tt=tt, n_prompt_tiles=ntp, n_blocks=n_slots // MOE_BLOCK)
    return pl.pallas_call(
        kern,
        out_shape=jax.ShapeDtypeStruct((n_slots * SLAB, LANES), F32),
        grid_spec=pltpu.PrefetchScalarGridSpec(
            num_scalar_prefetch=2,
            grid=(ntp + nts,),
            in_specs=[
                pl.BlockSpec((n_b, tt, D_MODEL), lambda i, d, p: (0, jnp.minimum(i, ntp - 1), 0)),
                pl.BlockSpec((rows, D_MODEL), lambda i, d, p: (jnp.maximum(i - ntp, 0), 0)),
            ],
            out_specs=pl.BlockSpec(memory_space=pl.ANY),
            scratch_shapes=[
                pltpu.VMEM((rows * SLAB, LANES), F32),
                pltpu.SemaphoreType.DMA(()),
            ],
        ),
        compiler_params=_cparams(("arbitrary",)),
        name="moe_dispatch",
    )(dest, pad_bounds, x1_p, x1_s)


def _ffn_kernel(bexp_ref, nused_ref, xs_ref, wg_ref, wu_ref, wd_ref, ys_ref, xb_ref):
    i = pl.program_id(0)
    rows = MOE_BLOCK

    @pl.when(i < nused_ref[0])
    def _():
        for s in range(SLAB):
            xb_ref[:, s * LANES:(s + 1) * LANES] = xs_ref[pl.ds(s, rows, stride=SLAB), :].astype(BF16)
        xb = xb_ref[...]
        g = jnp.dot(xb, wg_ref[...], preferred_element_type=F32)
        u = jnp.dot(xb, wu_ref[...], preferred_element_type=F32)
        h = (g * jax.nn.sigmoid(g) * u).astype(BF16)
        y = jnp.dot(h, wd_ref[...], preferred_element_type=F32)
        for s in range(SLAB):
            ys_ref[pl.ds(s, rows, stride=SLAB), :] = y[:, s * LANES:(s + 1) * LANES]

    @pl.when(i >= nused_ref[0])
    def _():
        ys_ref[...] = jnp.zeros(ys_ref.shape, F32)


def _ffn(xs, w_gate16, w_up16, w_down16, blk_exp, n_used, n_blocks):
    rows = MOE_BLOCK

    def x_map(i, be, nu):
        return (jnp.minimum(i, nu[0] - 1), 0)

    return pl.pallas_call(
        _ffn_kernel,
        out_shape=jax.ShapeDtypeStruct(xs.shape, F32),
        grid_spec=pltpu.PrefetchScalarGridSpec(
            num_scalar_prefetch=2,
            grid=(n_blocks,),
            in_specs=[
                pl.BlockSpec((rows * SLAB, LANES), x_map),
                pl.BlockSpec((None, D_MODEL, D_FF), lambda i, be, nu: (be[i], 0, 0)),
                pl.BlockSpec((None, D_MODEL, D_FF), lambda i, be, nu: (be[i], 0, 0)),
                pl.BlockSpec((None, D_FF, D_MODEL), lambda i, be, nu: (be[i], 0, 0)),
            ],
            out_specs=pl.BlockSpec((rows * SLAB, LANES), lambda i, be, nu: (i, 0)),
            scratch_shapes=[pltpu.VMEM((rows, D_MODEL), BF16)],
        ),
        compiler_params=_cparams(("arbitrary",)),
        name="moe_ffn",
    )(blk_exp, n_used, xs, w_gate16, w_up16, w_down16)


def _combine_kernel(dest_ref, x1_ref, gcol_ref, lg_ref, lb_ref, ys_ref, x2_ref, y0_ref, y1_ref, sem, *, n_b, tt, alpha):
    i = pl.program_id(0)
    rows = y0_ref.shape[0] // SLAB
    base = i * (2 * rows)

    def start(r, c):
        _slab_copy(ys_ref, dest_ref[base + 2 * r], y0_ref, r, sem).start()
        _slab_copy(ys_ref, dest_ref[base + 2 * r + 1], y1_ref, r, sem).start()
        return c

    lax.fori_loop(0, rows, start, 0)

    def wait(r, c):
        _slab_copy(ys_ref, dest_ref[base + 2 * r], y0_ref, r, sem).wait()
        _slab_copy(ys_ref, dest_ref[base + 2 * r + 1], y1_ref, r, sem).wait()
        return c

    lax.fori_loop(0, rows, wait, 0)

    g0 = gcol_ref[:, 0:1]
    g1 = gcol_ref[:, 1:2]
    if n_b is None:
        x1 = x1_ref[...]
    else:
        x1 = x1_ref[...].reshape(rows, D_MODEL)
    parts = []
    for s in range(SLAB):
        moe = g0 * y0_ref[pl.ds(s, rows, stride=SLAB), :] + g1 * y1_ref[pl.ds(s, rows, stride=SLAB), :]
        parts.append(alpha * x1[:, s * LANES:(s + 1) * LANES] + moe)
    r = jnp.concatenate(parts, axis=1)
    mu = jnp.mean(r, axis=-1, keepdims=True)
    rc = r - mu
    var = jnp.mean(rc * rc, axis=-1, keepdims=True)
    x2 = rc * lax.rsqrt(var + LN_EPS) * lg_ref[...] + lb_ref[...]
    if n_b is None:
        x2_ref[...] = x2
    else:
        x2_ref[...] = x2.reshape(n_b, tt, D_MODEL)


def _combine(x1, gcol, dest, ys, ln_g, ln_b, alpha, prompt):
    if prompt:
        n_b, seq, _ = x1.shape
        tt = TILE_T
        rows = n_b * tt
        nt = seq // tt
        x_spec = pl.BlockSpec((n_b, tt, D_MODEL), lambda i, d: (0, i, 0))
        kern = functools.partial(_combine_kernel, n_b=n_b, tt=tt, alpha=alpha)
    else:
        rows = 256
        nt = x1.shape[0] // rows
        x_spec = pl.BlockSpec((rows, D_MODEL), lambda i, d: (i, 0))
        kern = functools.partial(_combine_kernel, n_b=None, tt=None, alpha=alpha)
    return pl.pallas_call(
        kern,
        out_shape=jax.ShapeDtypeStruct(x1.shape, F32),
        grid_spec=pltpu.PrefetchScalarGridSpec(
            num_scalar_prefetch=1,
            grid=(nt,),
            in_specs=[
                x_spec,
                pl.BlockSpec((rows, 2), lambda i, d: (i, 0)),
                pl.BlockSpec((1, D_MODEL), lambda i, d: (0, 0)),
                pl.BlockSpec((1, D_MODEL), lambda i, d: (0, 0)),
                pl.BlockSpec(memory_space=pl.ANY),
            ],
            out_specs=x_spec,
            scratch_shapes=[
                pltpu.VMEM((rows * SLAB, LANES), F32),
                pltpu.VMEM((rows * SLAB, LANES), F32),
                pltpu.SemaphoreType.DMA(()),
            ],
        ),
        compiler_params=_cparams(("arbitrary",)),
        name="moe_combine_prompt" if prompt else "moe_combine_sample",
    )(dest, x1, gcol, ln_g, ln_b, ys)


def _moe_plan(eid_t, n_blocks):
    e_flat = eid_t.T.reshape(-1)
    onehot = (e_flat[:, None] == jnp.arange(N_EXPERTS, dtype=I32)[None, :]).astype(I32)
    csum = jnp.cumsum(onehot, axis=0)
    rank = jnp.sum(csum * onehot, axis=1) - 1
    counts = csum[-1]
    padded = (counts + MOE_BLOCK - 1) // MOE_BLOCK * MOE_BLOCK
    pad_end = jnp.cumsum(padded)
    pad_start = pad_end - padded
    dest = (jnp.sum(onehot * pad_start[None, :], axis=1) + rank).astype(I32)
    n_used = (pad_end[-1] // MOE_BLOCK).astype(I32)
    first_slot = jnp.arange(n_blocks, dtype=I32) * MOE_BLOCK
    blk_exp = jnp.minimum(jnp.sum((first_slot[:, None] >= pad_end[None, :]).astype(I32), axis=1), N_EXPERTS - 1)
    last_exp = jnp.take(blk_exp, jnp.maximum(n_used - 1, 0))
    blk_exp = jnp.where(jnp.arange(n_blocks) < n_used, blk_exp, last_exp).astype(I32)
    pad_bounds = jnp.concatenate([pad_start + counts, pad_end]).astype(I32)
    return dest, pad_bounds, blk_exp, n_used.reshape(1)


def kernel(x_prompt, x_sample, cache_k, cache_v, state_ssm_re, state_ssm_im, page_table, w_in, w_out, attn_norm_g, ssm_norm_g, ssm_a_re, ssm_a_im, ssm_log_dt, ssm_b_re, ssm_b_im, ssm_c_re, ssm_c_im, ssm_d, ssm_w_glu, ln1_g, ln1_b, router_w, router_b, moe_w_gate, moe_w_up, moe_w_down, ln2_g, ln2_b):
    depth = w_in.shape[0]
    n_b, seq, _ = x_prompt.shape
    n_dec, t_new, _ = x_sample.shape
    n_pool = cache_k.shape[1]
    gq = N_HEADS // N_KV_HEADS
    alpha = (2 * depth) ** 0.25
    n_states = SSM_GROUPS * SSM_STATE
    n_sample = n_dec * t_new
    n_tok = n_b * seq + n_sample
    n_blocks = -(-(2 * n_tok + N_EXPERTS * (MOE_BLOCK - 1)) // MOE_BLOCK)
    n_slots = n_blocks * MOE_BLOCK

    slopes = 2.0 ** (-8.0 * jnp.arange(1, N_HEADS + 1, dtype=F32) / N_HEADS)
    cache_k_flat = cache_k.reshape(-1, HEAD_DIM)
    cache_v_flat = cache_v.reshape(-1, HEAD_DIM)
    router_wt = router_w.T
    router_bc = router_b.reshape(N_EXPERTS, 1)
    zeros_state = jnp.zeros((n_b, n_states), F32)

    xp = x_prompt
    xs = x_sample.transpose(1, 0, 2).reshape(n_sample, D_MODEL)
    outs = {k: [] for k in ("kp", "vp", "hrp", "hip", "ks", "vs", "hrs", "his")}
    for l in range(depth):
        w_in16 = w_in[l].astype(BF16)
        lw = {
            "w_glu": ssm_w_glu[l].astype(BF16), "w_out": w_out[l].astype(BF16),
            "attn_g": attn_norm_g[l].reshape(1, -1), "ssm_g": ssm_norm_g[l].reshape(1, -1),
            "ln1_g": ln1_g[l].reshape(1, -1), "ln1_b": ln1_b[l].reshape(1, -1),
            "router_wt": router_wt, "router_b": router_bc,
        }
        prm = _ssm_params(ssm_a_re[l], ssm_a_im[l], ssm_log_dt[l], ssm_b_re[l], ssm_b_im[l],
                          ssm_c_re[l], ssm_c_im[l], ssm_d[l])

        q_p, k_p, v_p, u_p = _in_proj_prompt(xp, w_in16)
        att_p = _attn_prompt(q_p, k_p, v_p, slopes)
        y_p, hr_p, hi_p = _ssm(u_p, zeros_state, zeros_state, prm, nb=n_b, tc=64)

        h_s = _in_proj_sample(xs, w_in16)
        q_s = h_s[:, :ATTN_WIDTH].reshape(t_new, n_dec, N_KV_HEADS, gq, HEAD_DIM)
        q_s8 = q_s.transpose(1, 2, 3, 0, 4).reshape(n_dec, N_KV_HEADS, gq * t_new, HEAD_DIM)
        k_s = h_s[:, ATTN_WIDTH:ATTN_WIDTH + KV_WIDTH].reshape(t_new, n_dec, N_KV_HEADS, HEAD_DIM)
        v_s = h_s[:, ATTN_WIDTH + KV_WIDTH:ATTN_WIDTH + 2 * KV_WIDTH].reshape(t_new, n_dec, N_KV_HEADS, HEAD_DIM)
        att_s8 = _attn_sample(q_s8, k_s.transpose(1, 2, 0, 3), v_s.transpose(1, 2, 0, 3), cache_k_flat, cache_v_flat,
                              page_table, slopes, l, n_pool)
        att_s = att_s8.reshape(n_dec, N_KV_HEADS, gq, t_new, HEAD_DIM).transpose(3, 0, 1, 2, 4)
        att_s = att_s.reshape(n_sample, ATTN_WIDTH)
        u_s = h_s[:, ATTN_WIDTH + 2 * KV_WIDTH:].reshape(n_sample, N_SSM_CHUNKS, LANES).transpose(1, 0, 2)
        y_s, hr_s, hi_s = _ssm(u_s, state_ssm_re[l].reshape(n_dec, n_states), state_ssm_im[l].reshape(n_dec, n_states),
                               prm, nb=n_dec, tc=t_new)

        x1_p, eid_p, gate_p = _postmix(att_p, y_p, xp, lw, alpha, prompt=True)
        x1_s, eid_s, gate_s = _postmix(att_s, y_s, xs, lw, alpha, prompt=False)

        eid_t = jnp.concatenate([eid_p, eid_s], axis=1)
        dest, pad_bounds, blk_exp, n_used = _moe_plan(eid_t, n_blocks)
        xs_slots = _dispatch(x1_p, x1_s, dest, pad_bounds, n_slots)
        ys_slots = _ffn(xs_slots, moe_w_gate[l].astype(BF16), moe_w_up[l].astype(BF16), moe_w_down[l].astype(BF16),
                        blk_exp, n_used, n_blocks)
        l2g = ln2_g[l].reshape(1, -1)
        l2b = ln2_b[l].reshape(1, -1)
        n_pa = 2 * n_b * seq
        xp = _combine(x1_p, gate_p.T, dest[:n_pa], ys_slots, l2g, l2b, alpha, prompt=True)
        xs = _combine(x1_s, gate_s.T, dest[n_pa:], ys_slots, l2g, l2b, alpha, prompt=False)

        outs["kp"].append(k_p.reshape(n_b, seq, N_KV_HEADS, HEAD_DIM))
        outs["vp"].append(v_p.reshape(n_b, seq, N_KV_HEADS, HEAD_DIM))
        outs["hrp"].append(hr_p.reshape(n_b, SSM_GROUPS, SSM_STATE))
        outs["hip"].append(hi_p.reshape(n_b, SSM_GROUPS, SSM_STATE))
        outs["ks"].append(k_s.transpose(1, 0, 2, 3))
        outs["vs"].append(v_s.transpose(1, 0, 2, 3))
        outs["hrs"].append(hr_s.reshape(n_dec, SSM_GROUPS, SSM_STATE))
        outs["his"].append(hi_s.reshape(n_dec, SSM_GROUPS, SSM_STATE))

    y_sample = xs.reshape(t_new, n_dec, D_MODEL).transpose(1, 0, 2)
    return (xp, y_sample,
            jnp.stack(outs["kp"]), jnp.stack(outs["vp"]), jnp.stack(outs["hrp"]), jnp.stack(outs["hip"]),
            jnp.stack(outs["ks"]), jnp.stack(outs["vs"]), jnp.stack(outs["hrs"]), jnp.stack(outs["his"]))
```

```python
import functools
import math

import jax
import jax.numpy as jnp
from jax import lax
from jax.experimental import pallas as pl
from jax.experimental.pallas import tpu as pltpu

F32 = jnp.float32
BF16 = jnp.bfloat16
I32 = jnp.int32
HIGHEST = lax.Precision.HIGHEST

D_MODEL = 2048
ATTN_WIDTH = 1024
SSM_WIDTH = 1024
HEAD_DIM = 128
N_HEADS = 8
N_KV_HEADS = 4
KV_WIDTH = N_KV_HEADS * HEAD_DIM
PROJ_WIDTH = ATTN_WIDTH + 2 * KV_WIDTH + SSM_WIDTH
MOBA_BLOCK = 256
MOBA_TOPK = 3
PAGE_SIZE = 128
SSM_GROUP_CH = 16
SSM_GROUPS = 64
SSM_STATE = 64
N_EXPERTS = 16
N_EXPERT_GROUPS = 4
EXPERTS_PER_GROUP = 4
D_FF = 1024
LN_EPS = 1e-5
RMS_EPS = 1e-6
NEG_INF = float("-inf")
Q_SCALE = HEAD_DIM ** -0.5

LANES = 128
SUBLANES = 8
VMEM_LIMIT = 56 * 1024 * 1024

TILE_T = 32
SLAB = D_MODEL // LANES
SSM_CHUNK = 8 * SSM_GROUP_CH
SSM_CHUNK_STATES = 8 * SSM_STATE
N_SSM_CHUNKS = SSM_WIDTH // SSM_CHUNK
MOE_BLOCK = 256


def _cparams(sem, vmem=VMEM_LIMIT):
    return pltpu.CompilerParams(dimension_semantics=sem, vmem_limit_bytes=vmem)


def _in_proj_prompt_kernel(x_ref, w_ref, q_ref, k_ref, v_ref, u_ref, *, n_b, tt):
    rows = n_b * tt
    x = x_ref[...].reshape(rows, D_MODEL).astype(BF16)
    h = jnp.dot(x, w_ref[...], preferred_element_type=F32)
    q_ref[...] = h[:, :ATTN_WIDTH].reshape(n_b, tt, ATTN_WIDTH)
    k_ref[...] = h[:, ATTN_WIDTH:ATTN_WIDTH + KV_WIDTH].reshape(n_b, tt, KV_WIDTH)
    v_ref[...] = h[:, ATTN_WIDTH + KV_WIDTH:ATTN_WIDTH + 2 * KV_WIDTH].reshape(n_b, tt, KV_WIDTH)
    u0 = ATTN_WIDTH + 2 * KV_WIDTH
    for c in range(N_SSM_CHUNKS):
        for b in range(n_b):
            u_ref[c, pl.ds(b, tt, stride=n_b), :] = h[b * tt:(b + 1) * tt, u0 + c * LANES:u0 + (c + 1) * LANES]


def _in_proj_prompt(x, w16):
    n_b, seq, _ = x.shape
    tt = TILE_T
    nt = seq // tt
    kern = functools.partial(_in_proj_prompt_kernel, n_b=n_b, tt=tt)
    return pl.pallas_call(
        kern,
        out_shape=(
            jax.ShapeDtypeStruct((n_b, seq, ATTN_WIDTH), F32),
            jax.ShapeDtypeStruct((n_b, seq, KV_WIDTH), F32),
            jax.ShapeDtypeStruct((n_b, seq, KV_WIDTH), F32),
            jax.ShapeDtypeStruct((N_SSM_CHUNKS, seq * n_b, LANES), F32),
        ),
        grid=(nt,),
        in_specs=[
            pl.BlockSpec((n_b, tt, D_MODEL), lambda i: (0, i, 0)),
            pl.BlockSpec((D_MODEL, PROJ_WIDTH), lambda i: (0, 0)),
        ],
        out_specs=(
            pl.BlockSpec((n_b, tt, ATTN_WIDTH), lambda i: (0, i, 0)),
            pl.BlockSpec((n_b, tt, KV_WIDTH), lambda i: (0, i, 0)),
            pl.BlockSpec((n_b, tt, KV_WIDTH), lambda i: (0, i, 0)),
            pl.BlockSpec((N_SSM_CHUNKS, tt * n_b, LANES), lambda i: (0, i, 0)),
        ),
        compiler_params=_cparams(("arbitrary",)),
        name="in_proj_prompt",
    )(x, w16)


def _matmul_kernel(x_ref, w_ref, o_ref):
    o_ref[...] = jnp.dot(x_ref[...].astype(BF16), w_ref[...], preferred_element_type=F32)


def _in_proj_sample(x, w16):
    m, k = x.shape
    n = w16.shape[1]
    tm = 256
    return pl.pallas_call(
        _matmul_kernel,
        out_shape=jax.ShapeDtypeStruct((m, n), F32),
        grid=(m // tm,),
        in_specs=[pl.BlockSpec((tm, k), lambda i: (i, 0)), pl.BlockSpec((k, n), lambda i: (0, 0))],
        out_specs=pl.BlockSpec((tm, n), lambda i: (i, 0)),
        compiler_params=_cparams(("arbitrary",)),
        name="in_proj_sample",
    )(x, w16)


def _topk_mask(gate, valid, k, axis):
    nb = gate.shape[axis]
    ids = lax.broadcasted_iota(I32, gate.shape, axis)
    gm = jnp.where(valid, gate, NEG_INF)
    cnt = jnp.zeros(gate.shape, I32)
    for j in range(nb):
        gj = gm[:, j:j + 1] if axis == 1 else gm[j:j + 1, :]
        beats = (gj > gm) | ((gj == gm) & (j < ids))
        cnt = cnt + beats.astype(I32)
    return valid & (cnt < k)


def _attn_prompt_kernel(slopes_ref, q_ref, k_ref, v_ref, o_ref, kmean_ref, k16_ref, vt1_ref, vt2_ref, acc_ref, *, n_blk):
    g = pl.program_id(1)
    qi = pl.program_id(2)
    blk = MOBA_BLOCK
    gq = N_HEADS // N_KV_HEADS
    width = gq * blk
    nt_dims = (((1,), (1,)), ((), ()))

    @pl.when(qi == 0)
    def _():
        for j in range(n_blk):
            kb = k_ref[j * blk:(j + 1) * blk, :]
            kmean_ref[j:j + 1, :] = jnp.sum(kb, axis=0, keepdims=True) * (1.0 / blk)
            k16_ref[j * blk:(j + 1) * blk, :] = kb.astype(BF16)
            vt = v_ref[j * blk:(j + 1) * blk, :].T.astype(BF16)
            vt1_ref[j] = vt
            vt2_ref[j // 2, :, (j % 2) * blk:(j % 2 + 1) * blk] = vt

    lane = lax.broadcasted_iota(I32, (1, width), 1)
    slope_vec = jnp.zeros((1, width), F32)
    for hh in range(gq):
        slope_vec = jnp.where(lane // blk == hh, slopes_ref[g * gq + hh], slope_vec)
    q_all = jnp.concatenate([q_ref[:, hh * HEAD_DIM:(hh + 1) * HEAD_DIM] for hh in range(gq)], axis=0) * Q_SCALE
    q16 = q_all.astype(BF16)
    blk_ids = lax.broadcasted_iota(I32, (n_blk, width), 0)
    gate_t = lax.dot_general(kmean_ref[...], q_all, nt_dims, precision=HIGHEST, preferred_element_type=F32)
    sel = _topk_mask(gate_t, blk_ids < qi, MOBA_TOPK, axis=0).astype(F32)
    key_id = lax.broadcasted_iota(I32, (blk, width), 0)
    qry_id = lax.broadcasted_iota(I32, (blk, width), 1) % blk
    bias = -slope_vec * (qry_id - key_id).astype(F32)

    own = pl.multiple_of(qi * blk, blk)
    s = lax.dot_general(k16_ref[pl.ds(own, blk), :], q16, nt_dims, preferred_element_type=F32)
    s = jnp.where(key_id <= qry_id, s + bias, NEG_INF)
    m0 = jnp.max(s, axis=0, keepdims=True)
    p = jnp.exp(s - m0)
    l0 = jnp.sum(p, axis=0, keepdims=True)
    acc_ref[...] = jnp.dot(vt1_ref[qi], p.astype(BF16), preferred_element_type=F32)

    def body(t, carry):
        m, l = carry
        off = pl.multiple_of(t * (2 * blk), 2 * blk)
        s = lax.dot_general(k16_ref[pl.ds(off, 2 * blk), :], q16, nt_dims, preferred_element_type=F32)
        half, cst = [], []
        for h in range(2):
            j = 2 * t + h
            selj = jnp.sum(jnp.where(blk_ids == j, sel, 0.0), axis=0, keepdims=True) > 0.5
            half.append(jnp.where(selj, s[h * blk:(h + 1) * blk, :] + bias, NEG_INF))
            cst.append(-slope_vec * ((qi - j) * blk).astype(F32))
        m_new = jnp.maximum(m, jnp.maximum(jnp.max(half[0], axis=0, keepdims=True) + cst[0],
                                           jnp.max(half[1], axis=0, keepdims=True) + cst[1]))
        a = jnp.exp(m - m_new)
        p0 = jnp.exp(half[0] - (m_new - cst[0]))
        p1 = jnp.exp(half[1] - (m_new - cst[1]))
        l = a * l + jnp.sum(p0, axis=0, keepdims=True) + jnp.sum(p1, axis=0, keepdims=True)
        p = jnp.concatenate([p0, p1], axis=0).astype(BF16)
        acc_ref[...] = a * acc_ref[...] + jnp.dot(vt2_ref[t], p, preferred_element_type=F32)
        return m_new, l

    _, l = lax.fori_loop(0, (qi + 1) // 2, body, (m0, l0))
    o_t = acc_ref[...] / l
    for hh in range(gq):
        o_ref[:, hh * HEAD_DIM:(hh + 1) * HEAD_DIM] = o_t[:, hh * blk:(hh + 1) * blk].T


def _attn_prompt(q, k, v, slopes):
    n_b, seq, _ = q.shape
    n_blk = seq // MOBA_BLOCK
    gq = N_HEADS // N_KV_HEADS
    kern = functools.partial(_attn_prompt_kernel, n_blk=n_blk)
    return pl.pallas_call(
        kern,
        out_shape=jax.ShapeDtypeStruct((n_b, seq, ATTN_WIDTH), F32),
        grid_spec=pltpu.PrefetchScalarGridSpec(
            num_scalar_prefetch=1,
            grid=(n_b, N_KV_HEADS, n_blk),
            in_specs=[
                pl.BlockSpec((None, MOBA_BLOCK, gq * HEAD_DIM), lambda b, g, i, s: (b, i, g)),
                pl.BlockSpec((None, seq, HEAD_DIM), lambda b, g, i, s: (b, 0, g)),
                pl.BlockSpec((None, seq, HEAD_DIM), lambda b, g, i, s: (b, 0, g)),
            ],
            out_specs=pl.BlockSpec((None, MOBA_BLOCK, gq * HEAD_DIM), lambda b, g, i, s: (b, i, g)),
            scratch_shapes=[
                pltpu.VMEM((n_blk, HEAD_DIM), F32),
                pltpu.VMEM((seq, HEAD_DIM), BF16),
                pltpu.VMEM((n_blk, HEAD_DIM, MOBA_BLOCK), BF16),
                pltpu.VMEM((n_blk // 2, HEAD_DIM, 2 * MOBA_BLOCK), BF16),
                pltpu.VMEM((HEAD_DIM, gq * MOBA_BLOCK), F32),
            ],
        ),
        compiler_params=_cparams(("arbitrary", "arbitrary", "arbitrary")),
        name="moba_prompt",
    )(slopes, q, k, v)


def _attn_sample_kernel(pt_ref, slopes_ref, q_ref, kn_ref, vn_ref, *rest, n_pages, t_new, past):
    k_refs = rest[:n_pages]
    v_refs = rest[n_pages:2 * n_pages]
    o_ref = rest[2 * n_pages]
    gq = N_HEADS // N_KV_HEADS
    rows = N_HEADS * t_new
    ppb = MOBA_BLOCK // PAGE_SIZE
    n_blk = n_pages // ppb
    page_rows = PAGE_SIZE * N_KV_HEADS
    cols = ppb * page_rows
    nt_dims = (((1,), (1,)), ((), ()))

    row = lax.broadcasted_iota(I32, (rows, 1), 0)
    head = row // t_new
    g_row = head // gq
    t_row = row % t_new
    slope_row = jnp.zeros((rows, 1), F32)
    for h in range(N_HEADS):
        slope_row = jnp.where(head == h, slopes_ref[h], slope_row)
    col = lax.broadcasted_iota(I32, (1, cols), 1)
    g_col = col % N_KV_HEADS
    kpos_col = (col // page_rows) * PAGE_SIZE + (col % page_rows) // N_KV_HEADS
    bias0 = jnp.where(g_row == g_col, -slope_row * (past + t_row - kpos_col).astype(F32), NEG_INF)

    def per_head_rows(fn):
        out = jnp.zeros((rows, HEAD_DIM), F32)
        for g in range(N_KV_HEADS):
            out = jnp.where(g_row == g, fn(g), out)
        return out

    q = q_ref[...] * Q_SCALE
    q16 = q.astype(BF16)
    blk_lane = lax.broadcasted_iota(I32, (rows, n_blk), 1)
    gate = jnp.zeros((rows, n_blk), F32)
    m_blk, l_blk, o_blk = [], [], []
    for j in range(n_blk):
        kblk = jnp.concatenate([k_refs[ppb * j + p][...] for p in range(ppb)], axis=0)
        vblk = jnp.concatenate([v_refs[ppb * j + p][...] for p in range(ppb)], axis=0)
        s = lax.dot_general(q16, kblk.astype(BF16), nt_dims, preferred_element_type=F32) + bias0
        m = jnp.max(s, axis=1, keepdims=True)
        p = jnp.exp(s - m)
        m_blk.append(m + slope_row * float(j * MOBA_BLOCK))
        l_blk.append(jnp.sum(p, axis=1, keepdims=True))
        o_blk.append(jnp.dot(p.astype(BF16), vblk.astype(BF16), preferred_element_type=F32))
        fold = jnp.sum(kblk.reshape(cols // SUBLANES, SUBLANES, HEAD_DIM), axis=0)
        ksum = fold[0:N_KV_HEADS, :] + fold[N_KV_HEADS:, :]
        kmean_rows = per_head_rows(lambda g: ksum[g:g + 1, :]) * (1.0 / MOBA_BLOCK)
        gate = jnp.where(blk_lane == j, jnp.sum(q * kmean_rows, axis=1, keepdims=True), gate)

    sel = _topk_mask(gate, blk_lane >= 0, MOBA_TOPK, axis=1).astype(F32)

    s_own, v_own = [], []
    for tk in range(t_new):
        kn_rows = per_head_rows(lambda g: kn_ref[g, tk:tk + 1, :])
        sv = jnp.sum(q * kn_rows, axis=1, keepdims=True) - slope_row * (t_row - tk).astype(F32)
        s_own.append(jnp.where(t_row >= tk, sv, NEG_INF))
        v_own.append(per_head_rows(lambda g: vn_ref[g, tk:tk + 1, :]))
    m_tot = s_own[0]
    for tk in range(1, t_new):
        m_tot = jnp.maximum(m_tot, s_own[tk])
    for j in range(n_blk):
        m_tot = jnp.where(sel[:, j:j + 1] > 0.5, jnp.maximum(m_tot, m_blk[j]), m_tot)
    num = jnp.zeros((rows, HEAD_DIM), F32)
    den = jnp.zeros((rows, 1), F32)
    for j in range(n_blk):
        w = jnp.where(sel[:, j:j + 1] > 0.5, jnp.exp(m_blk[j] - m_tot), 0.0)
        num = num + w * o_blk[j]
        den = den + w * l_blk[j]
    for tk in range(t_new):
        pw = jnp.exp(s_own[tk] - m_tot)
        num = num + pw * v_own[tk]
        den = den + pw
    o_ref[...] = num / den


def _attn_sample(q, k_new, v_new, cache_k_flat, cache_v_flat, page_table, slopes, layer, n_pool):
    n_dec, rows, _ = q.shape
    t_new = k_new.shape[2]
    n_pages = page_table.shape[1]
    past = n_pages * PAGE_SIZE
    assert past % MOBA_BLOCK == 0
    page_rows = PAGE_SIZE * N_KV_HEADS
    base = layer * n_pool
    pt_flat = page_table.reshape(-1)
    kern = functools.partial(_attn_sample_kernel, n_pages=n_pages, t_new=t_new, past=past)

    def page_spec(p):
        return pl.BlockSpec((page_rows, HEAD_DIM), lambda b, pt, sl: (base + pt[b * n_pages + p], 0))

    q_spec = pl.BlockSpec((None, rows, HEAD_DIM), lambda b, pt, sl: (b, 0, 0))
    new_spec = pl.BlockSpec((None, N_KV_HEADS, t_new, HEAD_DIM), lambda b, pt, sl: (b, 0, 0, 0))
    pages = [page_spec(p) for p in range(n_pages)]
    return pl.pallas_call(
        kern,
        out_shape=jax.ShapeDtypeStruct((n_dec, rows, HEAD_DIM), F32),
        grid_spec=pltpu.PrefetchScalarGridSpec(
            num_scalar_prefetch=2,
            grid=(n_dec,),
            in_specs=[q_spec, new_spec, new_spec] + pages + pages,
            out_specs=q_spec,
        ),
        compiler_params=_cparams(("arbitrary",)),
        name="moba_sample",
    )(pt_flat, slopes, q, k_new, v_new, *([cache_k_flat] * n_pages), *([cache_v_flat] * n_pages))


def _ssm_kernel(u_ref, bre_ref, bim_ref, cre_ref, cim_ref, are_ref, aim_ref, d_ref, s0r_ref, s0i_ref,
                y_ref, hr_ref, hi_ref, xr_ref, xi_ref, str_ref, sti_ref, *, nb, tc):
    ti = pl.program_id(1)

    @pl.when(ti == 0)
    def _():
        str_ref[...] = s0r_ref[...]
        sti_ref[...] = s0i_ref[...]

    u = u_ref[...]
    u16 = u.astype(BF16)
    xr_ref[...] = jnp.dot(u16, bre_ref[...], preferred_element_type=F32)
    xi_ref[...] = jnp.dot(u16, bim_ref[...], preferred_element_type=F32)
    a_re = jnp.broadcast_to(are_ref[...], (nb, SSM_CHUNK_STATES))
    a_im = jnp.broadcast_to(aim_ref[...], (nb, SSM_CHUNK_STATES))

    def step(t, carry):
        h_re, h_im = carry
        sl = pl.ds(pl.multiple_of(t * nb, nb), nb)
        n_re = (a_re * h_re - a_im * h_im) + xr_ref[sl, :]
        n_im = (a_re * h_im + a_im * h_re) + xi_ref[sl, :]
        xr_ref[sl, :] = n_re
        xi_ref[sl, :] = n_im
        return n_re, n_im

    h_re, h_im = lax.fori_loop(0, tc, step, (str_ref[...], sti_ref[...]))
    str_ref[...] = h_re
    sti_ref[...] = h_im
    y = (jnp.dot(xr_ref[...].astype(BF16), cre_ref[...], preferred_element_type=F32)
         - jnp.dot(xi_ref[...].astype(BF16), cim_ref[...], preferred_element_type=F32)
         + d_ref[...] * u)
    y_ref[...] = jax.nn.gelu(y)

    @pl.when(ti == pl.num_programs(1) - 1)
    def _():
        hr_ref[...] = h_re
        hi_ref[...] = h_im


def _ssm(u_chunks, s0_re, s0_im, prm, nb, tc):
    n_rows = u_chunks.shape[1]
    nt = n_rows // (tc * nb)
    rows = tc * nb
    kern = functools.partial(_ssm_kernel, nb=nb, tc=tc)
    cs = SSM_CHUNK_STATES
    n_states = SSM_GROUPS * SSM_STATE
    return pl.pallas_call(
        kern,
        out_shape=(
            jax.ShapeDtypeStruct((N_SSM_CHUNKS, n_rows, LANES), F32),
            jax.ShapeDtypeStruct((nb, n_states), F32),
            jax.ShapeDtypeStruct((nb, n_states), F32),
        ),
        grid=(N_SSM_CHUNKS, nt),
        in_specs=[
            pl.BlockSpec((None, rows, LANES), lambda c, t: (c, t, 0)),
            pl.BlockSpec((None, SSM_CHUNK, cs), lambda c, t: (c, 0, 0)),
            pl.BlockSpec((None, SSM_CHUNK, cs), lambda c, t: (c, 0, 0)),
            pl.BlockSpec((None, cs, SSM_CHUNK), lambda c, t: (c, 0, 0)),
            pl.BlockSpec((None, cs, SSM_CHUNK), lambda c, t: (c, 0, 0)),
            pl.BlockSpec((1, cs), lambda c, t: (0, c)),
            pl.BlockSpec((1, cs), lambda c, t: (0, c)),
            pl.BlockSpec((1, SSM_CHUNK), lambda c, t: (0, c)),
            pl.BlockSpec((nb, cs), lambda c, t: (0, c)),
            pl.BlockSpec((nb, cs), lambda c, t: (0, c)),
        ],
        out_specs=(
            pl.BlockSpec((None, rows, LANES), lambda c, t: (c, t, 0)),
            pl.BlockSpec((nb, cs), lambda c, t: (0, c)),
            pl.BlockSpec((nb, cs), lambda c, t: (0, c)),
        ),
        scratch_shapes=[
            pltpu.VMEM((rows, cs), F32),
            pltpu.VMEM((rows, cs), F32),
            pltpu.VMEM((nb, cs), F32),
            pltpu.VMEM((nb, cs), F32),
        ],
        compiler_params=_cparams(("arbitrary", "arbitrary")),
        name="s5_mixer",
    )(u_chunks, prm["bbd_re"], prm["bbd_im"], prm["cbd_re"], prm["cbd_im"], prm["ab_re"], prm["ab_im"],
      prm["d"], s0_re, s0_im)


def _ssm_params(a_re, a_im, log_dt, b_re, b_im, c_re, c_im, d_skip):
    dt = jnp.exp(log_dt)
    mag = jnp.exp(a_re * dt)
    ab_re = mag * jnp.cos(a_im * dt)
    ab_im = mag * jnp.sin(a_im * dt)
    den = a_re * a_re + a_im * a_im
    f_re = ((ab_re - 1.0) * a_re + ab_im * a_im) / den
    f_im = (ab_im * a_re - (ab_re - 1.0) * a_im) / den
    bb_re = f_re[..., None] * b_re - f_im[..., None] * b_im
    bb_im = f_re[..., None] * b_im + f_im[..., None] * b_re
    gpc = SSM_CHUNK // SSM_GROUP_CH
    eye = jnp.eye(gpc, dtype=F32)

    def b_blockdiag(bb):
        x = bb.reshape(N_SSM_CHUNKS, gpc, SSM_STATE, SSM_GROUP_CH)
        m = jnp.einsum("kgpc,gh->kgchp", x, eye)
        return m.reshape(N_SSM_CHUNKS, gpc * SSM_GROUP_CH, gpc * SSM_STATE).astype(BF16)

    def c_blockdiag(cc):
        x = cc.reshape(N_SSM_CHUNKS, gpc, SSM_GROUP_CH, SSM_STATE)
        m = jnp.einsum("kgcp,gh->kgphc", x, eye)
        return m.reshape(N_SSM_CHUNKS, gpc * SSM_STATE, gpc * SSM_GROUP_CH).astype(BF16)

    return {
        "bbd_re": b_blockdiag(bb_re), "bbd_im": b_blockdiag(bb_im),
        "cbd_re": c_blockdiag(c_re), "cbd_im": c_blockdiag(c_im),
        "ab_re": ab_re.reshape(1, -1), "ab_im": ab_im.reshape(1, -1),
        "d": d_skip.reshape(1, -1),
    }


def _rank_lt(vals, a):
    cnt = jnp.zeros(vals[a].shape, I32)
    for b in range(len(vals)):
        if b == a:
            continue
        before = (vals[b] >= vals[a]) if b < a else (vals[b] > vals[a])
        cnt = cnt + before.astype(I32)
    return cnt


def _router(x1, rwt_ref, rb_ref, eid_ref, gate_ref):
    logits = lax.dot_general(rwt_ref[...], x1, (((1,), (1,)), ((), ())), precision=HIGHEST,
                             preferred_element_type=F32)
    mx = jnp.max(logits, axis=0, keepdims=True)
    ex = jnp.exp(logits - mx)
    probs = ex / jnp.sum(ex, axis=0, keepdims=True)
    biased = probs + rb_ref[...]
    prow = [probs[e:e + 1, :] for e in range(N_EXPERTS)]
    brow = [biased[e:e + 1, :] for e in range(N_EXPERTS)]
    ranks = []
    gscore = []
    for gi in range(N_EXPERT_GROUPS):
        vals = brow[gi * EXPERTS_PER_GROUP:(gi + 1) * EXPERTS_PER_GROUP]
        rk = [_rank_lt(vals, a) for a in range(EXPERTS_PER_GROUP)]
        ranks.append(rk)
        sc = jnp.zeros(vals[0].shape, F32)
        for a in range(EXPERTS_PER_GROUP):
            sc = sc + jnp.where(rk[a] < 2, vals[a], 0.0)
        gscore.append(sc)
    e0 = jnp.zeros(prow[0].shape, I32)
    e1 = jnp.zeros(prow[0].shape, I32)
    p0 = jnp.zeros(prow[0].shape, F32)
    p1 = jnp.zeros(prow[0].shape, F32)
    for gi in range(N_EXPERT_GROUPS):
        chosen = _rank_lt(gscore, gi) == 0
        for a in range(EXPERTS_PER_GROUP):
            e = gi * EXPERTS_PER_GROUP + a
            first = chosen & (ranks[gi][a] == 0)
            second = chosen & (ranks[gi][a] == 1)
            e0 = jnp.where(first, e, e0)
            e1 = jnp.where(second, e, e1)
            p0 = jnp.where(first, prow[e], p0)
            p1 = jnp.where(second, prow[e], p1)
    tot = p0 + p1
    eid_ref[0:1, :] = e0
    eid_ref[1:2, :] = e1
    gate_ref[0:1, :] = p0 / tot
    gate_ref[1:2, :] = p1 / tot


def _postmix_kernel(att_ref, y_ref, x_ref, wglu_ref, wout_ref, ag_ref, sg_ref, lg_ref, lb_ref, rwt_ref, rb_ref,
                    x1_ref, eid_ref, gate_ref, ybuf_ref, *, n_b, tt, alpha):
    rows = x1_ref.shape[0] if n_b is None else n_b * tt
    if n_b is None:
        att = att_ref[...]
        x = x_ref[...]
        for c in range(N_SSM_CHUNKS):
            ybuf_ref[:, c * LANES:(c + 1) * LANES] = y_ref[c]
    else:
        att = att_ref[...].reshape(rows, ATTN_WIDTH)
        x = x_ref[...].reshape(rows, D_MODEL)
        for c in range(N_SSM_CHUNKS):
            for b in range(n_b):
                ybuf_ref[b * tt:(b + 1) * tt, c * LANES:(c + 1) * LANES] = y_ref[c, pl.ds(b, tt, stride=n_b), :]
    y = ybuf_ref[...]
    z = jnp.dot(y.astype(BF16), wglu_ref[...], preferred_element_type=F32)
    sg = y * jax.nn.sigmoid(z)
    ssm_n = sg * lax.rsqrt(jnp.mean(sg * sg, axis=-1, keepdims=True) + RMS_EPS) * sg_ref[...]
    att_n = att * lax.rsqrt(jnp.mean(att * att, axis=-1, keepdims=True) + RMS_EPS) * ag_ref[...]
    mix = (jnp.dot(att_n.astype(BF16), wout_ref[0:ATTN_WIDTH, :], preferred_element_type=F32)
           + jnp.dot(ssm_n.astype(BF16), wout_ref[ATTN_WIDTH:, :], preferred_element_type=F32))
    r = alpha * x + mix
    mu = jnp.mean(r, axis=-1, keepdims=True)
    rc = r - mu
    var = jnp.mean(rc * rc, axis=-1, keepdims=True)
    x1 = rc * lax.rsqrt(var + LN_EPS) * lg_ref[...] + lb_ref[...]
    if n_b is None:
        x1_ref[...] = x1
    else:
        x1_ref[...] = x1.reshape(n_b, tt, D_MODEL)
    _router(x1, rwt_ref, rb_ref, eid_ref, gate_ref)


def _postmix(att, y_chunks, x, lw, alpha, prompt):
    if prompt:
        n_b, seq, _ = x.shape
        tt = TILE_T
        nt = seq // tt
        rows = n_b * tt
        n_tok = n_b * seq
        att_spec = pl.BlockSpec((n_b, tt, ATTN_WIDTH), lambda i: (0, i, 0))
        x_spec = pl.BlockSpec((n_b, tt, D_MODEL), lambda i: (0, i, 0))
        x1_shape = jax.ShapeDtypeStruct((n_b, seq, D_MODEL), F32)
        kern = functools.partial(_postmix_kernel, n_b=n_b, tt=tt, alpha=alpha)
    else:
        n_tok = x.shape[0]
        rows = 256
        nt = n_tok // rows
        att_spec = pl.BlockSpec((rows, ATTN_WIDTH), lambda i: (i, 0))
        x_spec = pl.BlockSpec((rows, D_MODEL), lambda i: (i, 0))
        x1_shape = jax.ShapeDtypeStruct((n_tok, D_MODEL), F32)
        kern = functools.partial(_postmix_kernel, n_b=None, tt=None, alpha=alpha)
    full = lambda shp: pl.BlockSpec(shp, lambda i: tuple(0 for _ in shp))
    return pl.pallas_call(
        kern,
        out_shape=(x1_shape, jax.ShapeDtypeStruct((2, n_tok), I32), jax.ShapeDtypeStruct((2, n_tok), F32)),
        grid=(nt,),
        in_specs=[
            att_spec,
            pl.BlockSpec((N_SSM_CHUNKS, rows, LANES), lambda i: (0, i, 0)),
            x_spec,
            full((SSM_WIDTH, SSM_WIDTH)),
            full((D_MODEL, D_MODEL)),
            full((1, ATTN_WIDTH)),
            full((1, SSM_WIDTH)),
            full((1, D_MODEL)),
            full((1, D_MODEL)),
            full((N_EXPERTS, D_MODEL)),
            full((N_EXPERTS, 1)),
        ],
        out_specs=(x_spec, pl.BlockSpec((2, rows), lambda i: (0, i)), pl.BlockSpec((2, rows), lambda i: (0, i))),
        scratch_shapes=[pltpu.VMEM((rows, SSM_WIDTH), F32)],
        compiler_params=_cparams(("arbitrary",)),
        name="postmix_prompt" if prompt else "postmix_sample",
    )(att, y_chunks, x, lw["w_glu"], lw["w_out"], lw["attn_g"], lw["ssm_g"], lw["ln1_g"], lw["ln1_b"],
      lw["router_wt"], lw["router_b"])


def _slab_rows(row):
    start = row * SLAB
    return pl.ds(start if isinstance(start, int) else pl.multiple_of(start, SLAB), SLAB)


def _slab_copy(src_ref, src_row, dst_ref, dst_row, sem):
    return pltpu.make_async_copy(src_ref.at[_slab_rows(src_row), :], dst_ref.at[_slab_rows(dst_row), :], sem)


def _dispatch_kernel(dest_ref, pad_ref, xp_ref, xs_ref, out_ref, slab_ref, sem, *, n_b, tt, n_prompt_tiles, n_blocks):
    i = pl.program_id(0)
    rows = n_b * tt

    @pl.when(i < n_prompt_tiles)
    def _():
        for s in range(SLAB):
            for b in range(n_b):
                slab_ref[pl.ds(b * tt * SLAB + s, tt, stride=SLAB), :] = xp_ref[b, :, s * LANES:(s + 1) * LANES]

    @pl.when(i >= n_prompt_tiles)
    def _():
        for s in range(SLAB):
            slab_ref[pl.ds(s, rows, stride=SLAB), :] = xs_ref[:, s * LANES:(s + 1) * LANES]

    base = i * (2 * rows)

    def start(r, c):
        for k in range(2):
            _slab_copy(slab_ref, r, out_ref, dest_ref[base + 2 * r + k], sem).start()
        return c

    lax.fori_loop(0, rows, start, 0)

    def wait(r, c):
        for k in range(2):
            _slab_copy(slab_ref, r, out_ref, dest_ref[base + 2 * r + k], sem).wait()
        return c

    lax.fori_loop(0, rows, wait, 0)

    @pl.when(i == pl.num_programs(0) - 1)
    def _():
        slab_ref[...] = jnp.zeros(slab_ref.shape, F32)

        def per_expert(e, c):
            lo = pad_ref[e]
            hi = pad_ref[N_EXPERTS + e]

            def zs(s, c2):
                _slab_copy(slab_ref, 0, out_ref, s, sem).start()
                return c2

            lax.fori_loop(lo, hi, zs, 0)

            def zw(s, c2):
                _slab_copy(slab_ref, 0, out_ref, s, sem).wait()
                return c2

            lax.fori_loop(lo, hi, zw, 0)
            return c

        lax.fori_loop(0, N_EXPERTS, per_expert, 0)

        blk_rows = MOE_BLOCK * SLAB

        def block_copy(b):
            start = pl.multiple_of(b * blk_rows, blk_rows)
            return pltpu.make_async_copy(slab_ref.at[pl.ds(0, blk_rows), :], out_ref.at[pl.ds(start, blk_rows), :], sem)

        def zbs(b, c):
            block_copy(b).start()
            return c

        def zbw(b, c):
            block_copy(b).wait()
            return c

        first_unused = pad_ref[2 * N_EXPERTS - 1] // MOE_BLOCK
        lax.fori_loop(first_unused, n_blocks, zbs, 0)
        lax.fori_loop(first_unused, n_blocks, zbw, 0)


def _dispatch(x1_p, x1_s, dest, pad_bounds, n_slots):
    n_b, seq, _ = x1_p.shape
    tt = TILE_T
    rows = n_b * tt
    assert rows == MOE_BLOCK
    ntp = seq // tt
    nts = x1_s.shape[0] // rows
    kern = functools.partial(_dispatch_kernel, n_b=n_b, tt=tt, n_prompt_tiles=ntp, n_blocks=n_slots // MOE_BLOCK)
    return pl.pallas_call(
        kern,
        out_shape=jax.ShapeDtypeStruct((n_slots * SLAB, LANES), F32),
        grid_spec=pltpu.PrefetchScalarGridSpec(
            num_scalar_prefetch=2,
            grid=(ntp + nts,),
            in_specs=[
                pl.BlockSpec((n_b, tt, D_MODEL), lambda i, d, p: (0, jnp.minimum(i, ntp - 1), 0)),
                pl.BlockSpec((rows, D_MODEL), lambda i, d, p: (jnp.maximum(i - ntp, 0), 0)),
            ],
            out_specs=pl.BlockSpec(memory_space=pl.ANY),
            scratch_shapes=[
                pltpu.VMEM((rows * SLAB, LANES), F32),
                pltpu.SemaphoreType.DMA(()),
            ],
        ),
        compiler_params=_cparams(("arbitrary",)),
        name="moe_dispatch",
    )(dest, pad_bounds, x1_p, x1_s)


def _ffn_kernel(bexp_ref, nused_ref, xs_ref, wg_ref, wu_ref, wd_ref, ys_ref, xb_ref):
    i = pl.program_id(0)
    rows = MOE_BLOCK

    @pl.when(i < nused_ref[0])
    def _():
        for s in range(SLAB):
            xb_ref[:, s * LANES:(s + 1) * LANES] = xs_ref[pl.ds(s, rows, stride=SLAB), :].astype(BF16)
        xb = xb_ref[...]
        g = jnp.dot(xb, wg_ref[...], preferred_element_type=F32)
        u = jnp.dot(xb, wu_ref[...], preferred_element_type=F32)
        h = (g * jax.nn.sigmoid(g) * u).astype(BF16)
        y = jnp.dot(h, wd_ref[...], preferred_element_type=F32)
        for s in range(SLAB):
            ys_ref[pl.ds(s, rows, stride=SLAB), :] = y[:, s * LANES:(s + 1) * LANES]

    @pl.when(i >= nused_ref[0])
    def _():
        ys_ref[...] = jnp.zeros(ys_ref.shape, F32)


def _ffn(xs, w_gate16, w_up16, w_down16, blk_exp, n_used, n_blocks):
    rows = MOE_BLOCK

    def x_map(i, be, nu):
        return (jnp.minimum(i, nu[0] - 1), 0)

    return pl.pallas_call(
        _ffn_kernel,
        out_shape=jax.ShapeDtypeStruct(xs.shape, F32),
        grid_spec=pltpu.PrefetchScalarGridSpec(
            num_scalar_prefetch=2,
            grid=(n_blocks,),
            in_specs=[
                pl.BlockSpec((rows * SLAB, LANES), x_map),
                pl.BlockSpec((None, D_MODEL, D_FF), lambda i, be, nu: (be[i], 0, 0)),
                pl.BlockSpec((None, D_MODEL, D_FF), lambda i, be, nu: (be[i], 0, 0)),
                pl.BlockSpec((None, D_FF, D_MODEL), lambda i, be, nu: (be[i], 0, 0)),
            ],
            out_specs=pl.BlockSpec((rows * SLAB, LANES), lambda i, be, nu: (i, 0)),
            scratch_shapes=[pltpu.VMEM((rows, D_MODEL), BF16)],
        ),
        compiler_params=_cparams(("arbitrary",)),
        name="moe_ffn",
    )(blk_exp, n_used, xs, w_gate16, w_up16, w_down16)


def _combine_kernel(dest_ref, x1_ref, gcol_ref, lg_ref, lb_ref, ys_ref, x2_ref, y0_ref, y1_ref, sem, *, n_b, tt, alpha):
    i = pl.program_id(0)
    rows = y0_ref.shape[0] // SLAB
    base = i * (2 * rows)

    def start(r, c):
        _slab_copy(ys_ref, dest_ref[base + 2 * r], y0_ref, r, sem).start()
        _slab_copy(ys_ref, dest_ref[base + 2 * r + 1], y1_ref, r, sem).start()
        return c

    lax.fori_loop(0, rows, start, 0)

    def wait(r, c):
        _slab_copy(ys_ref, dest_ref[base + 2 * r], y0_ref, r, sem).wait()
        _slab_copy(ys_ref, dest_ref[base + 2 * r + 1], y1_ref, r, sem).wait()
        return c

    lax.fori_loop(0, rows, wait, 0)

    g0 = gcol_ref[:, 0:1]
    g1 = gcol_ref[:, 1:2]
    if n_b is None:
        x1 = x1_ref[...]
    else:
        x1 = x1_ref[...].reshape(rows, D_MODEL)
    parts = []
    for s in range(SLAB):
        moe = g0 * y0_ref[pl.ds(s, rows, stride=SLAB), :] + g1 * y1_ref[pl.ds(s, rows, stride=SLAB), :]
        parts.append(alpha * x1[:, s * LANES:(s + 1) * LANES] + moe)
    r = jnp.concatenate(parts, axis=1)
    mu = jnp.mean(r, axis=-1, keepdims=True)
    rc = r - mu
    var = jnp.mean(rc * rc, axis=-1, keepdims=True)
    x2 = rc * lax.rsqrt(var + LN_EPS) * lg_ref[...] + lb_ref[...]
    if n_b is None:
        x2_ref[...] = x2
    else:
        x2_ref[...] = x2.reshape(n_b, tt, D_MODEL)


def _combine(x1, gcol, dest, ys, ln_g, ln_b, alpha, prompt):
    if prompt:
        n_b, seq, _ = x1.shape
        tt = TILE_T
        rows = n_b * tt
        nt = seq // tt
        x_spec = pl.BlockSpec((n_b, tt, D_MODEL), lambda i, d: (0, i, 0))
        kern = functools.partial(_combine_kernel, n_b=n_b, tt=tt, alpha=alpha)
    else:
        rows = 256
        nt = x1.shape[0] // rows
        x_spec = pl.BlockSpec((rows, D_MODEL), lambda i, d: (i, 0))
        kern = functools.partial(_combine_kernel, n_b=None, tt=None, alpha=alpha)
    return pl.pallas_call(
        kern,
        out_shape=jax.ShapeDtypeStruct(x1.shape, F32),
        grid_spec=pltpu.PrefetchScalarGridSpec(
            num_scalar_prefetch=1,
            grid=(nt,),
            in_specs=[
                x_spec,
                pl.BlockSpec((rows, 2), lambda i, d: (i, 0)),
                pl.BlockSpec((1, D_MODEL), lambda i, d: (0, 0)),
                pl.BlockSpec((1, D_MODEL), lambda i, d: (0, 0)),
                pl.BlockSpec(memory_space=pl.ANY),
            ],
            out_specs=x_spec,
            scratch_shapes=[
                pltpu.VMEM((rows * SLAB, LANES), F32),
                pltpu.VMEM((rows * SLAB, LANES), F32),
                pltpu.SemaphoreType.DMA(()),
            ],
        ),
        compiler_params=_cparams(("arbitrary",)),
        name="moe_combine_prompt" if prompt else "moe_combine_sample",
    )(dest, x1, gcol, ln_g, ln_b, ys)


def _moe_plan(eid_t, n_blocks):
    e_flat = eid_t.T.reshape(-1)
    onehot = (e_flat[:, None] == jnp.arange(N_EXPERTS, dtype=I32)[None, :]).astype(I32)
    csum = jnp.cumsum(onehot, axis=0)
    rank = jnp.sum(csum * onehot, axis=1) - 1
    counts = csum[-1]
    padded = (counts + MOE_BLOCK - 1) // MOE_BLOCK * MOE_BLOCK
    pad_end = jnp.cumsum(padded)
    pad_start = pad_end - padded
    dest = (jnp.sum(onehot * pad_start[None, :], axis=1) + rank).astype(I32)
    n_used = (pad_end[-1] // MOE_BLOCK).astype(I32)
    first_slot = jnp.arange(n_blocks, dtype=I32) * MOE_BLOCK
    blk_exp = jnp.minimum(jnp.sum((first_slot[:, None] >= pad_end[None, :]).astype(I32), axis=1), N_EXPERTS - 1)
    last_exp = jnp.take(blk_exp, jnp.maximum(n_used - 1, 0))
    blk_exp = jnp.where(jnp.arange(n_blocks) < n_used, blk_exp, last_exp).astype(I32)
    pad_bounds = jnp.concatenate([pad_start + counts, pad_end]).astype(I32)
    return dest, pad_bounds, blk_exp, n_used.reshape(1)


def kernel(x_prompt, x_sample, cache_k, cache_v, state_ssm_re, state_ssm_im, page_table, w_in, w_out, attn_norm_g, ssm_norm_g, ssm_a_re, ssm_a_im, ssm_log_dt, ssm_b_re, ssm_b_im, ssm_c_re, ssm_c_im, ssm_d, ssm_w_glu, ln1_g, ln1_b, router_w, router_b, moe_w_gate, moe_w_up, moe_w_down, ln2_g, ln2_b):
    depth = w_in.shape[0]
    n_b, seq, _ = x_prompt.shape
    n_dec, t_new, _ = x_sample.shape
    n_pool = cache_k.shape[1]
    gq = N_HEADS // N_KV_HEADS
    alpha = (2 * depth) ** 0.25
    n_states = SSM_GROUPS * SSM_STATE
    n_sample = n_dec * t_new
    n_tok = n_b * seq + n_sample
    n_blocks = -(-(2 * n_tok + N_EXPERTS * (MOE_BLOCK - 1)) // MOE_BLOCK)
    n_slots = n_blocks * MOE_BLOCK

    slopes = 2.0 ** (-8.0 * jnp.arange(1, N_HEADS + 1, dtype=F32) / N_HEADS)
    cache_k_flat = cache_k.reshape(-1, HEAD_DIM)
    cache_v_flat = cache_v.reshape(-1, HEAD_DIM)
    router_wt = router_w.T
    router_bc = router_b.reshape(N_EXPERTS, 1)
    zeros_state = jnp.zeros((n_b, n_states), F32)

    xp = x_prompt
    xs = x_sample.transpose(1, 0, 2).reshape(n_sample, D_MODEL)
    outs = {k: [] for k in ("kp", "vp", "hrp", "hip", "ks", "vs", "hrs", "his")}
    for l in range(depth):
        w_in16 = w_in[l].astype(BF16)
        lw = {
            "w_glu": ssm_w_glu[l].astype(BF16), "w_out": w_out[l].astype(BF16),
            "attn_g": attn_norm_g[l].reshape(1, -1), "ssm_g": ssm_norm_g[l].reshape(1, -1),
            "ln1_g": ln1_g[l].reshape(1, -1), "ln1_b": ln1_b[l].reshape(1, -1),
            "router_wt": router_wt, "router_b": router_bc,
        }
        prm = _ssm_params(ssm_a_re[l], ssm_a_im[l], ssm_log_dt[l], ssm_b_re[l], ssm_b_im[l],
                          ssm_c_re[l], ssm_c_im[l], ssm_d[l])

        q_p, k_p, v_p, u_p = _in_proj_prompt(xp, w_in16)
        att_p = _attn_prompt(q_p, k_p, v_p, slopes)
        y_p, hr_p, hi_p = _ssm(u_p, zeros_state, zeros_state, prm, nb=n_b, tc=64)

        h_s = _in_proj_sample(xs, w_in16)
        q_s = h_s[:, :ATTN_WIDTH].reshape(t_new, n_dec, N_HEADS, HEAD_DIM)
        q_s = q_s.transpose(1, 2, 0, 3).reshape(n_dec, N_HEADS * t_new, HEAD_DIM)
        k_s = h_s[:, ATTN_WIDTH:ATTN_WIDTH + KV_WIDTH].reshape(t_new, n_dec, N_KV_HEADS, HEAD_DIM)
        v_s = h_s[:, ATTN_WIDTH + KV_WIDTH:ATTN_WIDTH + 2 * KV_WIDTH].reshape(t_new, n_dec, N_KV_HEADS, HEAD_DIM)
        att_s = _attn_sample(q_s, k_s.transpose(1, 2, 0, 3), v_s.transpose(1, 2, 0, 3), cache_k_flat, cache_v_flat,
                             page_table, slopes, l, n_pool)
        att_s = att_s.reshape(n_dec, N_HEADS, t_new, HEAD_DIM).transpose(2, 0, 1, 3).reshape(n_sample, ATTN_WIDTH)
        u_s = h_s[:, ATTN_WIDTH + 2 * KV_WIDTH:].reshape(n_sample, N_SSM_CHUNKS, LANES).transpose(1, 0, 2)
        y_s, hr_s, hi_s = _ssm(u_s, state_ssm_re[l].reshape(n_dec, n_states), state_ssm_im[l].reshape(n_dec, n_states),
                               prm, nb=n_dec, tc=t_new)

        x1_p, eid_p, gate_p = _postmix(att_p, y_p, xp, lw, alpha, prompt=True)
        x1_s, eid_s, gate_s = _postmix(att_s, y_s, xs, lw, alpha, prompt=False)

        eid_t = jnp.concatenate([eid_p, eid_s], axis=1)
        dest, pad_bounds, blk_exp, n_used = _moe_plan(eid_t, n_blocks)
        xs_slots = _dispatch(x1_p, x1_s, dest, pad_bounds, n_slots)
        ys_slots = _ffn(xs_slots, moe_w_gate[l].astype(BF16), moe_w_up[l].astype(BF16), moe_w_down[l].astype(BF16),
                        blk_exp, n_used, n_blocks)
        l2g = ln2_g[l].reshape(1, -1)
        l2b = ln2_b[l].reshape(1, -1)
        n_pa = 2 * n_b * seq
        xp = _combine(x1_p, gate_p.T, dest[:n_pa], ys_slots, l2g, l2b, alpha, prompt=True)
        xs = _combine(x1_s, gate_s.T, dest[n_pa:], ys_slots, l2g, l2b, alpha, prompt=False)

        outs["kp"].append(k_p.reshape(n_b, seq, N_KV_HEADS, HEAD_DIM))
        outs["vp"].append(v_p.reshape(n_b, seq, N_KV_HEADS, HEAD_DIM))
        outs["hrp"].append(hr_p.reshape(n_b, SSM_GROUPS, SSM_STATE))
        outs["hip"].append(hi_p.reshape(n_b, SSM_GROUPS, SSM_STATE))
        outs["ks"].append(k_s.transpose(1, 0, 2, 3))
        outs["vs"].append(v_s.transpose(1, 0, 2, 3))
        outs["hrs"].append(hr_s.reshape(n_dec, SSM_GROUPS, SSM_STATE))
        outs["his"].append(hi_s.reshape(n_dec, SSM_GROUPS, SSM_STATE))

    y_sample = xs.reshape(t_new, n_dec, D_MODEL).transpose(1, 0, 2)
    return (xp, y_sample,
            jnp.stack(outs["kp"]), jnp.stack(outs["vp"]), jnp.stack(outs["hrp"]), jnp.stack(outs["hip"]),
            jnp.stack(outs["ks"]), jnp.stack(outs["vs"]), jnp.stack(outs["hrs"]), jnp.stack(outs["his"]))
```

```python
import functools
import math

import jax
import jax.numpy as jnp
from jax import lax
from jax.experimental import pallas as pl
from jax.experimental.pallas import tpu as pltpu

F32 = jnp.float32
BF16 = jnp.bfloat16
I32 = jnp.int32
HIGHEST = lax.Precision.HIGHEST

D_MODEL = 2048
ATTN_WIDTH = 1024
SSM_WIDTH = 1024
HEAD_DIM = 128
N_HEADS = 8
N_KV_HEADS = 4
KV_WIDTH = N_KV_HEADS * HEAD_DIM
PROJ_WIDTH = ATTN_WIDTH + 2 * KV_WIDTH + SSM_WIDTH
MOBA_BLOCK = 256
MOBA_TOPK = 3
PAGE_SIZE = 128
SSM_GROUP_CH = 16
SSM_GROUPS = 64
SSM_STATE = 64
N_EXPERTS = 16
N_EXPERT_GROUPS = 4
EXPERTS_PER_GROUP = 4
D_FF = 1024
LN_EPS = 1e-5
RMS_EPS = 1e-6
NEG_INF = float("-inf")
Q_SCALE = HEAD_DIM ** -0.5

LANES = 128
SUBLANES = 8
VMEM_LIMIT = 56 * 1024 * 1024

TILE_T = 32
SLAB = D_MODEL // LANES
SSM_CHUNK = 8 * SSM_GROUP_CH
SSM_CHUNK_STATES = 8 * SSM_STATE
N_SSM_CHUNKS = SSM_WIDTH // SSM_CHUNK
MOE_BLOCK = 256


def _cparams(sem, vmem=VMEM_LIMIT):
    return pltpu.CompilerParams(dimension_semantics=sem, vmem_limit_bytes=vmem)


def _in_proj_prompt_kernel(x_ref, w_ref, q_ref, k_ref, v_ref, u_ref, *, n_b, tt):
    rows = n_b * tt
    x = x_ref[...].reshape(rows, D_MODEL).astype(BF16)
    h = jnp.dot(x, w_ref[...], preferred_element_type=F32)
    q_ref[...] = h[:, :ATTN_WIDTH].reshape(n_b, tt, ATTN_WIDTH)
    for b in range(n_b):
        for g in range(N_KV_HEADS):
            k0 = ATTN_WIDTH + g * HEAD_DIM
            v0 = ATTN_WIDTH + KV_WIDTH + g * HEAD_DIM
            k_ref[b, pl.ds(g, tt, stride=N_KV_HEADS), :] = h[b * tt:(b + 1) * tt, k0:k0 + HEAD_DIM]
            v_ref[b, pl.ds(g, tt, stride=N_KV_HEADS), :] = h[b * tt:(b + 1) * tt, v0:v0 + HEAD_DIM]
    u0 = ATTN_WIDTH + 2 * KV_WIDTH
    for c in range(N_SSM_CHUNKS):
        for b in range(n_b):
            u_ref[c, pl.ds(b, tt, stride=n_b), :] = h[b * tt:(b + 1) * tt, u0 + c * LANES:u0 + (c + 1) * LANES]


def _in_proj_prompt(x, w16):
    n_b, seq, _ = x.shape
    tt = TILE_T
    nt = seq // tt
    kern = functools.partial(_in_proj_prompt_kernel, n_b=n_b, tt=tt)
    return pl.pallas_call(
        kern,
        out_shape=(
            jax.ShapeDtypeStruct((n_b, seq, ATTN_WIDTH), F32),
            jax.ShapeDtypeStruct((n_b, seq * N_KV_HEADS, HEAD_DIM), F32),
            jax.ShapeDtypeStruct((n_b, seq * N_KV_HEADS, HEAD_DIM), F32),
            jax.ShapeDtypeStruct((N_SSM_CHUNKS, seq * n_b, LANES), F32),
        ),
        grid=(nt,),
        in_specs=[
            pl.BlockSpec((n_b, tt, D_MODEL), lambda i: (0, i, 0)),
            pl.BlockSpec((D_MODEL, PROJ_WIDTH), lambda i: (0, 0)),
        ],
        out_specs=(
            pl.BlockSpec((n_b, tt, ATTN_WIDTH), lambda i: (0, i, 0)),
            pl.BlockSpec((n_b, tt * N_KV_HEADS, HEAD_DIM), lambda i: (0, i, 0)),
            pl.BlockSpec((n_b, tt * N_KV_HEADS, HEAD_DIM), lambda i: (0, i, 0)),
            pl.BlockSpec((N_SSM_CHUNKS, tt * n_b, LANES), lambda i: (0, i, 0)),
        ),
        compiler_params=_cparams(("arbitrary",)),
        name="in_proj_prompt",
    )(x, w16)


def _matmul_kernel(x_ref, w_ref, o_ref):
    o_ref[...] = jnp.dot(x_ref[...].astype(BF16), w_ref[...], preferred_element_type=F32)


def _in_proj_sample(x, w16):
    m, k = x.shape
    n = w16.shape[1]
    tm = 256
    return pl.pallas_call(
        _matmul_kernel,
        out_shape=jax.ShapeDtypeStruct((m, n), F32),
        grid=(m // tm,),
        in_specs=[pl.BlockSpec((tm, k), lambda i: (i, 0)), pl.BlockSpec((k, n), lambda i: (0, 0))],
        out_specs=pl.BlockSpec((tm, n), lambda i: (i, 0)),
        compiler_params=_cparams(("arbitrary",)),
        name="in_proj_sample",
    )(x, w16)


def _topk_mask(gate, valid, k, axis):
    nb = gate.shape[axis]
    ids = lax.broadcasted_iota(I32, gate.shape, axis)
    gm = jnp.where(valid, gate, NEG_INF)
    cnt = jnp.zeros(gate.shape, I32)
    for j in range(nb):
        gj = gm[:, j:j + 1] if axis == 1 else gm[j:j + 1, :]
        beats = (gj > gm) | ((gj == gm) & (j < ids))
        cnt = cnt + beats.astype(I32)
    return valid & (cnt < k)


def _attn_prompt_kernel(slopes_ref, q_ref, k_ref, v_ref, o_ref, kmean_ref, k16_ref, vt1_ref, vt2_ref, acc_ref, *, n_blk):
    g = pl.program_id(1)
    qi = pl.program_id(2)
    blk = MOBA_BLOCK
    gq = N_HEADS // N_KV_HEADS
    width = gq * blk
    nt_dims = (((1,), (1,)), ((), ()))

    @pl.when(qi == 0)
    def _():
        for j in range(n_blk):
            rows_j = pl.ds(j * blk * N_KV_HEADS + g, blk, stride=N_KV_HEADS)
            kb = k_ref[rows_j, :]
            kmean_ref[j:j + 1, :] = jnp.sum(kb, axis=0, keepdims=True) * (1.0 / blk)
            k16_ref[j * blk:(j + 1) * blk, :] = kb.astype(BF16)
            vt = v_ref[rows_j, :].T.astype(BF16)
            vt1_ref[j] = vt
            vt2_ref[j // 2, :, (j % 2) * blk:(j % 2 + 1) * blk] = vt

    lane = lax.broadcasted_iota(I32, (1, width), 1)
    slope_vec = jnp.zeros((1, width), F32)
    for hh in range(gq):
        slope_vec = jnp.where(lane // blk == hh, slopes_ref[g * gq + hh], slope_vec)
    q_all = jnp.concatenate([q_ref[:, hh * HEAD_DIM:(hh + 1) * HEAD_DIM] for hh in range(gq)], axis=0) * Q_SCALE
    q16 = q_all.astype(BF16)
    blk_ids = lax.broadcasted_iota(I32, (n_blk, width), 0)
    gate_t = lax.dot_general(kmean_ref[...], q_all, nt_dims, precision=HIGHEST, preferred_element_type=F32)
    sel = _topk_mask(gate_t, blk_ids < qi, MOBA_TOPK, axis=0).astype(F32)
    key_id = lax.broadcasted_iota(I32, (blk, width), 0)
    qry_id = lax.broadcasted_iota(I32, (blk, width), 1) % blk
    bias = -slope_vec * (qry_id - key_id).astype(F32)

    own = pl.multiple_of(qi * blk, blk)
    s = lax.dot_general(k16_ref[pl.ds(own, blk), :], q16, nt_dims, preferred_element_type=F32)
    s = jnp.where(key_id <= qry_id, s + bias, NEG_INF)
    m0 = jnp.max(s, axis=0, keepdims=True)
    p = jnp.exp(s - m0)
    l0 = jnp.sum(p, axis=0, keepdims=True)
    acc_ref[...] = jnp.dot(vt1_ref[qi], p.astype(BF16), preferred_element_type=F32)

    def body(t, carry):
        m, l = carry
        off = pl.multiple_of(t * (2 * blk), 2 * blk)
        s = lax.dot_general(k16_ref[pl.ds(off, 2 * blk), :], q16, nt_dims, preferred_element_type=F32)
        half, cst = [], []
        for h in range(2):
            j = 2 * t + h
            selj = jnp.sum(jnp.where(blk_ids == j, sel, 0.0), axis=0, keepdims=True) > 0.5
            half.append(jnp.where(selj, s[h * blk:(h + 1) * blk, :] + bias, NEG_INF))
            cst.append(-slope_vec * ((qi - j) * blk).astype(F32))
        m_new = jnp.maximum(m, jnp.maximum(jnp.max(half[0], axis=0, keepdims=True) + cst[0],
                                           jnp.max(half[1], axis=0, keepdims=True) + cst[1]))
        a = jnp.exp(m - m_new)
        p0 = jnp.exp(half[0] - (m_new - cst[0]))
        p1 = jnp.exp(half[1] - (m_new - cst[1]))
        l = a * l + jnp.sum(p0, axis=0, keepdims=True) + jnp.sum(p1, axis=0, keepdims=True)
        p = jnp.concatenate([p0, p1], axis=0).astype(BF16)
        acc_ref[...] = a * acc_ref[...] + jnp.dot(vt2_ref[t], p, preferred_element_type=F32)
        return m_new, l

    _, l = lax.fori_loop(0, (qi + 1) // 2, body, (m0, l0))
    o_t = acc_ref[...] / l
    for hh in range(gq):
        o_ref[:, hh * HEAD_DIM:(hh + 1) * HEAD_DIM] = o_t[:, hh * blk:(hh + 1) * blk].T


def _attn_prompt(q, k, v, slopes):
    n_b, seq, _ = q.shape
    n_blk = seq // MOBA_BLOCK
    gq = N_HEADS // N_KV_HEADS
    kern = functools.partial(_attn_prompt_kernel, n_blk=n_blk)
    return pl.pallas_call(
        kern,
        out_shape=jax.ShapeDtypeStruct((n_b, seq, ATTN_WIDTH), F32),
        grid_spec=pltpu.PrefetchScalarGridSpec(
            num_scalar_prefetch=1,
            grid=(n_b, N_KV_HEADS, n_blk),
            in_specs=[
                pl.BlockSpec((None, MOBA_BLOCK, gq * HEAD_DIM), lambda b, g, i, s: (b, i, g)),
                pl.BlockSpec((None, seq * N_KV_HEADS, HEAD_DIM), lambda b, g, i, s: (b, 0, 0)),
                pl.BlockSpec((None, seq * N_KV_HEADS, HEAD_DIM), lambda b, g, i, s: (b, 0, 0)),
            ],
            out_specs=pl.BlockSpec((None, MOBA_BLOCK, gq * HEAD_DIM), lambda b, g, i, s: (b, i, g)),
            scratch_shapes=[
                pltpu.VMEM((n_blk, HEAD_DIM), F32),
                pltpu.VMEM((seq, HEAD_DIM), BF16),
                pltpu.VMEM((n_blk, HEAD_DIM, MOBA_BLOCK), BF16),
                pltpu.VMEM((n_blk // 2, HEAD_DIM, 2 * MOBA_BLOCK), BF16),
                pltpu.VMEM((HEAD_DIM, gq * MOBA_BLOCK), F32),
            ],
        ),
        compiler_params=_cparams(("arbitrary", "arbitrary", "arbitrary")),
        name="moba_prompt",
    )(slopes, q, k, v)


def _attn_sample_kernel(pt_ref, slopes_ref, q_ref, kn_ref, vn_ref, *rest, n_pages, t_new, past):
    k_refs = rest[:n_pages]
    v_refs = rest[n_pages:2 * n_pages]
    o_ref = rest[2 * n_pages]
    gq = N_HEADS // N_KV_HEADS
    rows = N_HEADS * t_new
    ppb = MOBA_BLOCK // PAGE_SIZE
    n_blk = n_pages // ppb
    page_rows = PAGE_SIZE * N_KV_HEADS
    cols = ppb * page_rows
    nt_dims = (((1,), (1,)), ((), ()))

    row = lax.broadcasted_iota(I32, (rows, 1), 0)
    head = row // t_new
    g_row = head // gq
    t_row = row % t_new
    slope_row = jnp.zeros((rows, 1), F32)
    for h in range(N_HEADS):
        slope_row = jnp.where(head == h, slopes_ref[h], slope_row)
    col = lax.broadcasted_iota(I32, (1, cols), 1)
    g_col = col % N_KV_HEADS
    kpos_col = (col // page_rows) * PAGE_SIZE + (col % page_rows) // N_KV_HEADS
    bias0 = jnp.where(g_row == g_col, -slope_row * (past + t_row - kpos_col).astype(F32), NEG_INF)

    def per_head_rows(fn):
        out = jnp.zeros((rows, HEAD_DIM), F32)
        for g in range(N_KV_HEADS):
            out = jnp.where(g_row == g, fn(g), out)
        return out

    q = q_ref[...] * Q_SCALE
    q16 = q.astype(BF16)
    blk_lane = lax.broadcasted_iota(I32, (rows, n_blk), 1)
    gate = jnp.zeros((rows, n_blk), F32)
    m_blk, l_blk, o_blk = [], [], []
    for j in range(n_blk):
        kblk = jnp.concatenate([k_refs[ppb * j + p][...] for p in range(ppb)], axis=0)
        vblk = jnp.concatenate([v_refs[ppb * j + p][...] for p in range(ppb)], axis=0)
        s = lax.dot_general(q16, kblk.astype(BF16), nt_dims, preferred_element_type=F32) + bias0
        m = jnp.max(s, axis=1, keepdims=True)
        p = jnp.exp(s - m)
        m_blk.append(m + slope_row * float(j * MOBA_BLOCK))
        l_blk.append(jnp.sum(p, axis=1, keepdims=True))
        o_blk.append(jnp.dot(p.astype(BF16), vblk.astype(BF16), preferred_element_type=F32))
        fold = jnp.sum(kblk.reshape(cols // SUBLANES, SUBLANES, HEAD_DIM), axis=0)
        ksum = fold[0:N_KV_HEADS, :] + fold[N_KV_HEADS:, :]
        kmean_rows = per_head_rows(lambda g: ksum[g:g + 1, :]) * (1.0 / MOBA_BLOCK)
        gate = jnp.where(blk_lane == j, jnp.sum(q * kmean_rows, axis=1, keepdims=True), gate)

    sel = _topk_mask(gate, blk_lane >= 0, MOBA_TOPK, axis=1).astype(F32)

    s_own, v_own = [], []
    for tk in range(t_new):
        kn_rows = per_head_rows(lambda g: kn_ref[g, tk:tk + 1, :])
        sv = jnp.sum(q * kn_rows, axis=1, keepdims=True) - slope_row * (t_row - tk).astype(F32)
        s_own.append(jnp.where(t_row >= tk, sv, NEG_INF))
        v_own.append(per_head_rows(lambda g: vn_ref[g, tk:tk + 1, :]))
    m_tot = s_own[0]
    for tk in range(1, t_new):
        m_tot = jnp.maximum(m_tot, s_own[tk])
    for j in range(n_blk):
        m_tot = jnp.where(sel[:, j:j + 1] > 0.5, jnp.maximum(m_tot, m_blk[j]), m_tot)
    num = jnp.zeros((rows, HEAD_DIM), F32)
    den = jnp.zeros((rows, 1), F32)
    for j in range(n_blk):
        w = jnp.where(sel[:, j:j + 1] > 0.5, jnp.exp(m_blk[j] - m_tot), 0.0)
        num = num + w * o_blk[j]
        den = den + w * l_blk[j]
    for tk in range(t_new):
        pw = jnp.exp(s_own[tk] - m_tot)
        num = num + pw * v_own[tk]
        den = den + pw
    o_ref[...] = num / den


def _attn_sample(q, k_new, v_new, cache_k_flat, cache_v_flat, page_table, slopes, layer, n_pool):
    n_dec, rows, _ = q.shape
    t_new = k_new.shape[2]
    n_pages = page_table.shape[1]
    past = n_pages * PAGE_SIZE
    assert past % MOBA_BLOCK == 0
    page_rows = PAGE_SIZE * N_KV_HEADS
    base = layer * n_pool
    pt_flat = page_table.reshape(-1)
    kern = functools.partial(_attn_sample_kernel, n_pages=n_pages, t_new=t_new, past=past)

    def page_spec(p):
        return pl.BlockSpec((page_rows, HEAD_DIM), lambda b, pt, sl: (base + pt[b * n_pages + p], 0))

    q_spec = pl.BlockSpec((None, rows, HEAD_DIM), lambda b, pt, sl: (b, 0, 0))
    new_spec = pl.BlockSpec((None, N_KV_HEADS, t_new, HEAD_DIM), lambda b, pt, sl: (b, 0, 0, 0))
    pages = [page_spec(p) for p in range(n_pages)]
    return pl.pallas_call(
        kern,
        out_shape=jax.ShapeDtypeStruct((n_dec, rows, HEAD_DIM), F32),
        grid_spec=pltpu.PrefetchScalarGridSpec(
            num_scalar_prefetch=2,
            grid=(n_dec,),
            in_specs=[q_spec, new_spec, new_spec] + pages + pages,
            out_specs=q_spec,
        ),
        compiler_params=_cparams(("arbitrary",)),
        name="moba_sample",
    )(pt_flat, slopes, q, k_new, v_new, *([cache_k_flat] * n_pages), *([cache_v_flat] * n_pages))


def _ssm_kernel(u_ref, b_ref, c_ref, are_ref, aim_ref, d_ref, s0r_ref, s0i_ref,
                y_ref, hr_ref, hi_ref, x_ref, str_ref, sti_ref, *, nb, tc):
    ti = pl.program_id(1)
    cs = SSM_CHUNK_STATES

    @pl.when(ti == 0)
    def _():
        str_ref[...] = s0r_ref[...]
        sti_ref[...] = s0i_ref[...]

    u = u_ref[...]
    x_ref[...] = jnp.dot(u.astype(BF16), b_ref[...], preferred_element_type=F32)
    a_re = jnp.broadcast_to(are_ref[...], (nb, cs))
    a_im = jnp.broadcast_to(aim_ref[...], (nb, cs))

    def step(t, carry):
        h_re, h_im = carry
        sl = pl.ds(pl.multiple_of(t * nb, nb), nb)
        n_re = (a_re * h_re - a_im * h_im) + x_ref[sl, 0:cs]
        n_im = (a_re * h_im + a_im * h_re) + x_ref[sl, cs:2 * cs]
        x_ref[sl, 0:cs] = n_re
        x_ref[sl, cs:2 * cs] = n_im
        return n_re, n_im

    h_re, h_im = lax.fori_loop(0, tc, step, (str_ref[...], sti_ref[...]))
    str_ref[...] = h_re
    sti_ref[...] = h_im
    y = jnp.dot(x_ref[...].astype(BF16), c_ref[...], preferred_element_type=F32) + d_ref[...] * u
    y_ref[...] = jax.nn.gelu(y)

    @pl.when(ti == pl.num_programs(1) - 1)
    def _():
        hr_ref[...] = h_re
        hi_ref[...] = h_im


def _ssm(u_chunks, s0_re, s0_im, prm, nb, tc):
    n_rows = u_chunks.shape[1]
    nt = n_rows // (tc * nb)
    rows = tc * nb
    kern = functools.partial(_ssm_kernel, nb=nb, tc=tc)
    cs = SSM_CHUNK_STATES
    n_states = SSM_GROUPS * SSM_STATE
    return pl.pallas_call(
        kern,
        out_shape=(
            jax.ShapeDtypeStruct((N_SSM_CHUNKS, n_rows, LANES), F32),
            jax.ShapeDtypeStruct((nb, n_states), F32),
            jax.ShapeDtypeStruct((nb, n_states), F32),
        ),
        grid=(N_SSM_CHUNKS, nt),
        in_specs=[
            pl.BlockSpec((None, rows, LANES), lambda c, t: (c, t, 0)),
            pl.BlockSpec((None, SSM_CHUNK, 2 * cs), lambda c, t: (c, 0, 0)),
            pl.BlockSpec((None, 2 * cs, SSM_CHUNK), lambda c, t: (c, 0, 0)),
            pl.BlockSpec((1, cs), lambda c, t: (0, c)),
            pl.BlockSpec((1, cs), lambda c, t: (0, c)),
            pl.BlockSpec((1, SSM_CHUNK), lambda c, t: (0, c)),
            pl.BlockSpec((nb, cs), lambda c, t: (0, c)),
            pl.BlockSpec((nb, cs), lambda c, t: (0, c)),
        ],
        out_specs=(
            pl.BlockSpec((None, rows, LANES), lambda c, t: (c, t, 0)),
            pl.BlockSpec((nb, cs), lambda c, t: (0, c)),
            pl.BlockSpec((nb, cs), lambda c, t: (0, c)),
        ),
        scratch_shapes=[
            pltpu.VMEM((rows, 2 * cs), F32),
            pltpu.VMEM((nb, cs), F32),
            pltpu.VMEM((nb, cs), F32),
        ],
        compiler_params=_cparams(("arbitrary", "arbitrary")),
        name="s5_mixer",
    )(u_chunks, prm["b_cat"], prm["c_cat"], prm["ab_re"], prm["ab_im"], prm["d"], s0_re, s0_im)


def _ssm_params(a_re, a_im, log_dt, b_re, b_im, c_re, c_im, d_skip):
    dt = jnp.exp(log_dt)
    mag = jnp.exp(a_re * dt)
    ab_re = mag * jnp.cos(a_im * dt)
    ab_im = mag * jnp.sin(a_im * dt)
    den = a_re * a_re + a_im * a_im
    f_re = ((ab_re - 1.0) * a_re + ab_im * a_im) / den
    f_im = (ab_im * a_re - (ab_re - 1.0) * a_im) / den
    bb_re = f_re[..., None] * b_re - f_im[..., None] * b_im
    bb_im = f_re[..., None] * b_im + f_im[..., None] * b_re
    gpc = SSM_CHUNK // SSM_GROUP_CH
    eye = jnp.eye(gpc, dtype=F32)

    def b_blockdiag(bb):
        x = bb.reshape(N_SSM_CHUNKS, gpc, SSM_STATE, SSM_GROUP_CH)
        m = jnp.einsum("kgpc,gh->kgchp", x, eye)
        return m.reshape(N_SSM_CHUNKS, gpc * SSM_GROUP_CH, gpc * SSM_STATE).astype(BF16)

    def c_blockdiag(cc):
        x = cc.reshape(N_SSM_CHUNKS, gpc, SSM_GROUP_CH, SSM_STATE)
        m = jnp.einsum("kgcp,gh->kgphc", x, eye)
        return m.reshape(N_SSM_CHUNKS, gpc * SSM_STATE, gpc * SSM_GROUP_CH).astype(BF16)

    return {
        "b_cat": jnp.concatenate([b_blockdiag(bb_re), b_blockdiag(bb_im)], axis=2),
        "c_cat": jnp.concatenate([c_blockdiag(c_re), -c_blockdiag(c_im)], axis=1),
        "ab_re": ab_re.reshape(1, -1), "ab_im": ab_im.reshape(1, -1),
        "d": d_skip.reshape(1, -1),
    }


def _rank_lt(vals, a):
    cnt = jnp.zeros(vals[a].shape, I32)
    for b in range(len(vals)):
        if b == a:
            continue
        before = (vals[b] >= vals[a]) if b < a else (vals[b] > vals[a])
        cnt = cnt + before.astype(I32)
    return cnt


def _router(x1, rwt_ref, rb_ref, eid_ref, gate_ref):
    logits = lax.dot_general(rwt_ref[...], x1, (((1,), (1,)), ((), ())), precision=HIGHEST,
                             preferred_element_type=F32)
    mx = jnp.max(logits, axis=0, keepdims=True)
    ex = jnp.exp(logits - mx)
    probs = ex / jnp.sum(ex, axis=0, keepdims=True)
    biased = probs + rb_ref[...]
    prow = [probs[e:e + 1, :] for e in range(N_EXPERTS)]
    brow = [biased[e:e + 1, :] for e in range(N_EXPERTS)]
    ranks = []
    gscore = []
    for gi in range(N_EXPERT_GROUPS):
        vals = brow[gi * EXPERTS_PER_GROUP:(gi + 1) * EXPERTS_PER_GROUP]
        rk = [_rank_lt(vals, a) for a in range(EXPERTS_PER_GROUP)]
        ranks.append(rk)
        sc = jnp.zeros(vals[0].shape, F32)
        for a in range(EXPERTS_PER_GROUP):
            sc = sc + jnp.where(rk[a] < 2, vals[a], 0.0)
        gscore.append(sc)
    e0 = jnp.zeros(prow[0].shape, I32)
    e1 = jnp.zeros(prow[0].shape, I32)
    p0 = jnp.zeros(prow[0].shape, F32)
    p1 = jnp.zeros(prow[0].shape, F32)
    for gi in range(N_EXPERT_GROUPS):
        chosen = _rank_lt(gscore, gi) == 0
        for a in range(EXPERTS_PER_GROUP):
            e = gi * EXPERTS_PER_GROUP + a
            first = chosen & (ranks[gi][a] == 0)
            second = chosen & (ranks[gi][a] == 1)
            e0 = jnp.where(first, e, e0)
            e1 = jnp.where(second, e, e1)
            p0 = jnp.where(first, prow[e], p0)
            p1 = jnp.where(second, prow[e], p1)
    tot = p0 + p1
    eid_ref[0:1, :] = e0
    eid_ref[1:2, :] = e1
    gate_ref[0:1, :] = p0 / tot
    gate_ref[1:2, :] = p1 / tot


def _postmix_kernel(att_ref, y_ref, x_ref, wglu_ref, wout_ref, ag_ref, sg_ref, lg_ref, lb_ref, rwt_ref, rb_ref,
                    x1_ref, eid_ref, gate_ref, ybuf_ref, *, n_b, tt, alpha):
    rows = x1_ref.shape[0] if n_b is None else n_b * tt
    if n_b is None:
        att = att_ref[...]
        x = x_ref[...]
        for c in range(N_SSM_CHUNKS):
            ybuf_ref[:, c * LANES:(c + 1) * LANES] = y_ref[c]
    else:
        att = att_ref[...].reshape(rows, ATTN_WIDTH)
        x = x_ref[...].reshape(rows, D_MODEL)
        for c in range(N_SSM_CHUNKS):
            for b in range(n_b):
                ybuf_ref[b * tt:(b + 1) * tt, c * LANES:(c + 1) * LANES] = y_ref[c, pl.ds(b, tt, stride=n_b), :]
    y = ybuf_ref[...]
    z = jnp.dot(y.astype(BF16), wglu_ref[...], preferred_element_type=F32)
    sg = y * jax.nn.sigmoid(z)
    ssm_n = sg * lax.rsqrt(jnp.mean(sg * sg, axis=-1, keepdims=True) + RMS_EPS) * sg_ref[...]
    att_n = att * lax.rsqrt(jnp.mean(att * att, axis=-1, keepdims=True) + RMS_EPS) * ag_ref[...]
    mix = (jnp.dot(att_n.astype(BF16), wout_ref[0:ATTN_WIDTH, :], preferred_element_type=F32)
           + jnp.dot(ssm_n.astype(BF16), wout_ref[ATTN_WIDTH:, :], preferred_element_type=F32))
    r = alpha * x + mix
    mu = jnp.mean(r, axis=-1, keepdims=True)
    rc = r - mu
    var = jnp.mean(rc * rc, axis=-1, keepdims=True)
    x1 = rc * lax.rsqrt(var + LN_EPS) * lg_ref[...] + lb_ref[...]
    if n_b is None:
        x1_ref[...] = x1
    else:
        x1_ref[...] = x1.reshape(n_b, tt, D_MODEL)
    _router(x1, rwt_ref, rb_ref, eid_ref, gate_ref)


def _postmix(att, y_chunks, x, lw, alpha, prompt):
    if prompt:
        n_b, seq, _ = x.shape
        tt = TILE_T
        nt = seq // tt
        rows = n_b * tt
        n_tok = n_b * seq
        att_spec = pl.BlockSpec((n_b, tt, ATTN_WIDTH), lambda i: (0, i, 0))
        x_spec = pl.BlockSpec((n_b, tt, D_MODEL), lambda i: (0, i, 0))
        x1_shape = jax.ShapeDtypeStruct((n_b, seq, D_MODEL), F32)
        kern = functools.partial(_postmix_kernel, n_b=n_b, tt=tt, alpha=alpha)
    else:
        n_tok = x.shape[0]
        rows = 256
        nt = n_tok // rows
        att_spec = pl.BlockSpec((rows, ATTN_WIDTH), lambda i: (i, 0))
        x_spec = pl.BlockSpec((rows, D_MODEL), lambda i: (i, 0))
        x1_shape = jax.ShapeDtypeStruct((n_tok, D_MODEL), F32)
        kern = functools.partial(_postmix_kernel, n_b=None, tt=None, alpha=alpha)
    full = lambda shp: pl.BlockSpec(shp, lambda i: tuple(0 for _ in shp))
    return pl.pallas_call(
        kern,
        out_shape=(x1_shape, jax.ShapeDtypeStruct((2, n_tok), I32), jax.ShapeDtypeStruct((2, n_tok), F32)),
        grid=(nt,),
        in_specs=[
            att_spec,
            pl.BlockSpec((N_SSM_CHUNKS, rows, LANES), lambda i: (0, i, 0)),
            x_spec,
            full((SSM_WIDTH, SSM_WIDTH)),
            full((D_MODEL, D_MODEL)),
            full((1, ATTN_WIDTH)),
            full((1, SSM_WIDTH)),
            full((1, D_MODEL)),
            full((1, D_MODEL)),
            full((N_EXPERTS, D_MODEL)),
            full((N_EXPERTS, 1)),
        ],
        out_specs=(x_spec, pl.BlockSpec((2, rows), lambda i: (0, i)), pl.BlockSpec((2, rows), lambda i: (0, i))),
        scratch_shapes=[pltpu.VMEM((rows, SSM_WIDTH), F32)],
        compiler_params=_cparams(("arbitrary",)),
        name="postmix_prompt" if prompt else "postmix_sample",
    )(att, y_chunks, x, lw["w_glu"], lw["w_out"], lw["attn_g"], lw["ssm_g"], lw["ln1_g"], lw["ln1_b"],
      lw["router_wt"], lw["router_b"])


def _slab_rows(row):
    start = row * SLAB
    return pl.ds(start if isinstance(start, int) else pl.multiple_of(start, SLAB), SLAB)


def _slab_copy(src_ref, src_row, dst_ref, dst_row, sem):
    return pltpu.make_async_copy(src_ref.at[_slab_rows(src_row), :], dst_ref.at[_slab_rows(dst_row), :], sem)


def _dispatch_kernel(dest_ref, pad_ref, xp_ref, xs_ref, out_ref, slab_ref, sem, *, n_b, tt, n_prompt_tiles, n_blocks):
    i = pl.program_id(0)
    last = pl.num_programs(0) - 1
    rows = n_b * tt
    slot = i % 2
    slab = slab_ref.at[slot]

    @pl.when(i < n_prompt_tiles)
    def _():
        for s in range(SLAB):
            for b in range(n_b):
                slab[pl.ds(b * tt * SLAB + s, tt, stride=SLAB), :] = xp_ref[b, :, s * LANES:(s + 1) * LANES]

    @pl.when(i >= n_prompt_tiles)
    def _():
        for s in range(SLAB):
            slab[pl.ds(s, rows, stride=SLAB), :] = xs_ref[:, s * LANES:(s + 1) * LANES]

    base = i * (2 * rows)

    def start(r, c):
        for k in range(2):
            _slab_copy(slab, r, out_ref, dest_ref[base + 2 * r + k], sem.at[slot]).start()
        return c

    lax.fori_loop(0, rows, start, 0)

    def wait_tile(which):
        for _ in range(2):
            pltpu.make_async_copy(slab_ref.at[which], slab_ref.at[which], sem.at[which]).wait()

    @pl.when(i > 0)
    def _():
        wait_tile(1 - slot)

    @pl.when(i == last)
    def _():
        wait_tile(slot)
        slab[...] = jnp.zeros(slab.shape, F32)
        zsem = sem.at[slot]

        def per_expert(e, c):
            lo = pad_ref[e]
            hi = pad_ref[N_EXPERTS + e]

            def zs(s, c2):
                _slab_copy(slab, 0, out_ref, s, zsem).start()
                return c2

            lax.fori_loop(lo, hi, zs, 0)

            def zw(s, c2):
                _slab_copy(slab, 0, out_ref, s, zsem).wait()
                return c2

            lax.fori_loop(lo, hi, zw, 0)
            return c

        lax.fori_loop(0, N_EXPERTS, per_expert, 0)

        blk_rows = MOE_BLOCK * SLAB

        def block_copy(b):
            start = pl.multiple_of(b * blk_rows, blk_rows)
            return pltpu.make_async_copy(slab, out_ref.at[pl.ds(start, blk_rows), :], zsem)

        def zbs(b, c):
            block_copy(b).start()
            return c

        def zbw(b, c):
            block_copy(b).wait()
            return c

        first_unused = pad_ref[2 * N_EXPERTS - 1] // MOE_BLOCK
        lax.fori_loop(first_unused, n_blocks, zbs, 0)
        lax.fori_loop(first_unused, n_blocks, zbw, 0)


def _dispatch(x1_p, x1_s, dest, pad_bounds, n_slots):
    n_b, seq, _ = x1_p.shape
    tt = TILE_T
    rows = n_b * tt
    assert rows == MOE_BLOCK
    ntp = seq // tt
    nts = x1_s.shape[0] // rows
    kern = functools.partial(_dispatch_kernel, n_b=n_b, tt=tt, n_prompt_tiles=ntp, n_blocks=n_slots // MOE_BLOCK)
    return pl.pallas_call(
        kern,
        out_shape=jax.ShapeDtypeStruct((n_slots * SLAB, LANES), F32),
        grid_spec=pltpu.PrefetchScalarGridSpec(
            num_scalar_prefetch=2,
            grid=(ntp + nts,),
            in_specs=[
                pl.BlockSpec((n_b, tt, D_MODEL), lambda i, d, p: (0, jnp.minimum(i, ntp - 1), 0)),
                pl.BlockSpec((rows, D_MODEL), lambda i, d, p: (jnp.maximum(i - ntp, 0), 0)),
            ],
            out_specs=pl.BlockSpec(memory_space=pl.ANY),
            scratch_shapes=[
                pltpu.VMEM((2, rows * SLAB, LANES), F32),
                pltpu.SemaphoreType.DMA((2,)),
            ],
        ),
        compiler_params=_cparams(("arbitrary",)),
        name="moe_dispatch",
    )(dest, pad_bounds, x1_p, x1_s)


def _ffn_kernel(bexp_ref, nused_ref, xs_ref, wg_ref, wu_ref, wd_ref, ys_ref, xb_ref):
    i = pl.program_id(0)
    rows = MOE_BLOCK

    @pl.when(i < nused_ref[0])
    def _():
        for s in range(SLAB):
            xb_ref[:, s * LANES:(s + 1) * LANES] = xs_ref[pl.ds(s, rows, stride=SLAB), :].astype(BF16)
        xb = xb_ref[...]
        g = jnp.dot(xb, wg_ref[...], preferred_element_type=F32)
        u = jnp.dot(xb, wu_ref[...], preferred_element_type=F32)
        h = (g * jax.nn.sigmoid(g) * u).astype(BF16)
        y = jnp.dot(h, wd_ref[...], preferred_element_type=F32)
        for s in range(SLAB):
            ys_ref[pl.ds(s, rows, stride=SLAB), :] = y[:, s * LANES:(s + 1) * LANES]

    @pl.when(i >= nused_ref[0])
    def _():
        ys_ref[...] = jnp.zeros(ys_ref.shape, F32)


def _ffn(xs, w_gate16, w_up16, w_down16, blk_exp, n_used, n_blocks):
    rows = MOE_BLOCK

    def x_map(i, be, nu):
        return (jnp.minimum(i, nu[0] - 1), 0)

    return pl.pallas_call(
        _ffn_kernel,
        out_shape=jax.ShapeDtypeStruct(xs.shape, F32),
        grid_spec=pltpu.PrefetchScalarGridSpec(
            num_scalar_prefetch=2,
            grid=(n_blocks,),
            in_specs=[
                pl.BlockSpec((rows * SLAB, LANES), x_map),
                pl.BlockSpec((None, D_MODEL, D_FF), lambda i, be, nu: (be[i], 0, 0)),
                pl.BlockSpec((None, D_MODEL, D_FF), lambda i, be, nu: (be[i], 0, 0)),
                pl.BlockSpec((None, D_FF, D_MODEL), lambda i, be, nu: (be[i], 0, 0)),
            ],
            out_specs=pl.BlockSpec((rows * SLAB, LANES), lambda i, be, nu: (i, 0)),
            scratch_shapes=[pltpu.VMEM((rows, D_MODEL), BF16)],
        ),
        compiler_params=_cparams(("arbitrary",)),
        name="moe_ffn",
    )(blk_exp, n_used, xs, w_gate16, w_up16, w_down16)


def _combine_kernel(dest_ref, x1_ref, gate_ref, lg_ref, lb_ref, ys_ref, x2_ref, y0_ref, y1_ref, sem, *, n_b, tt, alpha):
    i = pl.program_id(0)
    rows = y0_ref.shape[1] // SLAB
    slot = i % 2

    def fetch(tile, which):
        base = tile * (2 * rows)

        def start(r, c):
            _slab_copy(ys_ref, dest_ref[base + 2 * r], y0_ref.at[which], r, sem.at[which]).start()
            _slab_copy(ys_ref, dest_ref[base + 2 * r + 1], y1_ref.at[which], r, sem.at[which]).start()
            return c

        lax.fori_loop(0, rows, start, 0)

    @pl.when(i == 0)
    def _():
        fetch(0, 0)

    @pl.when(i + 1 < pl.num_programs(0))
    def _():
        fetch(i + 1, 1 - slot)

    pltpu.make_async_copy(y0_ref.at[slot], y0_ref.at[slot], sem.at[slot]).wait()
    pltpu.make_async_copy(y1_ref.at[slot], y1_ref.at[slot], sem.at[slot]).wait()

    y0 = y0_ref.at[slot]
    y1 = y1_ref.at[slot]
    gate_rows = jnp.concatenate([gate_ref[...], jnp.zeros((SUBLANES - 2, rows), F32)], axis=0)
    gate_cols = gate_rows.T
    g0 = gate_cols[:, 0:1]
    g1 = gate_cols[:, 1:2]
    if n_b is None:
        x1 = x1_ref[...]
    else:
        x1 = x1_ref[...].reshape(rows, D_MODEL)
    parts = []
    for s in range(SLAB):
        moe = g0 * y0[pl.ds(s, rows, stride=SLAB), :] + g1 * y1[pl.ds(s, rows, stride=SLAB), :]
        parts.append(alpha * x1[:, s * LANES:(s + 1) * LANES] + moe)
    r = jnp.concatenate(parts, axis=1)
    mu = jnp.mean(r, axis=-1, keepdims=True)
    rc = r - mu
    var = jnp.mean(rc * rc, axis=-1, keepdims=True)
    x2 = rc * lax.rsqrt(var + LN_EPS) * lg_ref[...] + lb_ref[...]
    if n_b is None:
        x2_ref[...] = x2
    else:
        x2_ref[...] = x2.reshape(n_b, tt, D_MODEL)


def _combine(x1, gate_t, dest, ys, ln_g, ln_b, alpha, prompt):
    if prompt:
        n_b, seq, _ = x1.shape
        tt = TILE_T
        rows = n_b * tt
        nt = seq // tt
        x_spec = pl.BlockSpec((n_b, tt, D_MODEL), lambda i, d: (0, i, 0))
        kern = functools.partial(_combine_kernel, n_b=n_b, tt=tt, alpha=alpha)
    else:
        rows = 256
        nt = x1.shape[0] // rows
        x_spec = pl.BlockSpec((rows, D_MODEL), lambda i, d: (i, 0))
        kern = functools.partial(_combine_kernel, n_b=None, tt=None, alpha=alpha)
    return pl.pallas_call(
        kern,
        out_shape=jax.ShapeDtypeStruct(x1.shape, F32),
        grid_spec=pltpu.PrefetchScalarGridSpec(
            num_scalar_prefetch=1,
            grid=(nt,),
            in_specs=[
                x_spec,
                pl.BlockSpec((2, rows), lambda i, d: (0, i)),
                pl.BlockSpec((1, D_MODEL), lambda i, d: (0, 0)),
                pl.BlockSpec((1, D_MODEL), lambda i, d: (0, 0)),
                pl.BlockSpec(memory_space=pl.ANY),
            ],
            out_specs=x_spec,
            scratch_shapes=[
                pltpu.VMEM((2, rows * SLAB, LANES), F32),
                pltpu.VMEM((2, rows * SLAB, LANES), F32),
                pltpu.SemaphoreType.DMA((2,)),
            ],
        ),
        compiler_params=_cparams(("arbitrary",)),
        name="moe_combine_prompt" if prompt else "moe_combine_sample",
    )(dest, x1, gate_t, ln_g, ln_b, ys)


def _moe_plan(eid_t, n_blocks):
    e_flat = eid_t.T.reshape(-1)
    onehot = (e_flat[:, None] == jnp.arange(N_EXPERTS, dtype=I32)[None, :]).astype(I32)
    csum = jnp.cumsum(onehot, axis=0)
    rank = jnp.sum(csum * onehot, axis=1) - 1
    counts = csum[-1]
    padded = (counts + MOE_BLOCK - 1) // MOE_BLOCK * MOE_BLOCK
    pad_end = jnp.cumsum(padded)
    pad_start = pad_end - padded
    dest = (jnp.sum(onehot * pad_start[None, :], axis=1) + rank).astype(I32)
    n_used = (pad_end[-1] // MOE_BLOCK).astype(I32)
    first_slot = jnp.arange(n_blocks, dtype=I32) * MOE_BLOCK
    blk_exp = jnp.minimum(jnp.sum((first_slot[:, None] >= pad_end[None, :]).astype(I32), axis=1), N_EXPERTS - 1)
    last_exp = jnp.take(blk_exp, jnp.maximum(n_used - 1, 0))
    blk_exp = jnp.where(jnp.arange(n_blocks) < n_used, blk_exp, last_exp).astype(I32)
    pad_bounds = jnp.concatenate([pad_start + counts, pad_end]).astype(I32)
    return dest, pad_bounds, blk_exp, n_used.reshape(1)


def kernel(x_prompt, x_sample, cache_k, cache_v, state_ssm_re, state_ssm_im, page_table, w_in, w_out, attn_norm_g, ssm_norm_g, ssm_a_re, ssm_a_im, ssm_log_dt, ssm_b_re, ssm_b_im, ssm_c_re, ssm_c_im, ssm_d, ssm_w_glu, ln1_g, ln1_b, router_w, router_b, moe_w_gate, moe_w_up, moe_w_down, ln2_g, ln2_b):
    depth = w_in.shape[0]
    n_b, seq, _ = x_prompt.shape
    n_dec, t_new, _ = x_sample.shape
    n_pool = cache_k.shape[1]
    gq = N_HEADS // N_KV_HEADS
    alpha = (2 * depth) ** 0.25
    n_states = SSM_GROUPS * SSM_STATE
    n_sample = n_dec * t_new
    n_tok = n_b * seq + n_sample
    n_blocks = -(-(2 * n_tok + N_EXPERTS * (MOE_BLOCK - 1)) // MOE_BLOCK)
    n_slots = n_blocks * MOE_BLOCK

    slopes = 2.0 ** (-8.0 * jnp.arange(1, N_HEADS + 1, dtype=F32) / N_HEADS)
    cache_k_flat = cache_k.reshape(-1, HEAD_DIM)
    cache_v_flat = cache_v.reshape(-1, HEAD_DIM)
    router_wt = router_w.T
    router_bc = router_b.reshape(N_EXPERTS, 1)
    zeros_state = jnp.zeros((n_b, n_states), F32)

    xp = x_prompt
    xs = x_sample.transpose(1, 0, 2).reshape(n_sample, D_MODEL)
    outs = {k: [] for k in ("kp", "vp", "hrp", "hip", "ks", "vs", "hrs", "his")}
    for l in range(depth):
        w_in16 = w_in[l].astype(BF16)
        lw = {
            "w_glu": ssm_w_glu[l].astype(BF16), "w_out": w_out[l].astype(BF16),
            "attn_g": attn_norm_g[l].reshape(1, -1), "ssm_g": ssm_norm_g[l].reshape(1, -1),
            "ln1_g": ln1_g[l].reshape(1, -1), "ln1_b": ln1_b[l].reshape(1, -1),
            "router_wt": router_wt, "router_b": router_bc,
        }
        prm = _ssm_params(ssm_a_re[l], ssm_a_im[l], ssm_log_dt[l], ssm_b_re[l], ssm_b_im[l],
                          ssm_c_re[l], ssm_c_im[l], ssm_d[l])

        q_p, k_p, v_p, u_p = _in_proj_prompt(xp, w_in16)
        att_p = _attn_prompt(q_p, k_p, v_p, slopes)
        y_p, hr_p, hi_p = _ssm(u_p, zeros_state, zeros_state, prm, nb=n_b, tc=128)

        h_s = _in_proj_sample(xs, w_in16)
        q_s = h_s[:, :ATTN_WIDTH].reshape(t_new, n_dec, N_HEADS, HEAD_DIM)
        q_s = q_s.transpose(1, 2, 0, 3).reshape(n_dec, N_HEADS * t_new, HEAD_DIM)
        k_s = h_s[:, ATTN_WIDTH:ATTN_WIDTH + KV_WIDTH].reshape(t_new, n_dec, N_KV_HEADS, HEAD_DIM)
        v_s = h_s[:, ATTN_WIDTH + KV_WIDTH:ATTN_WIDTH + 2 * KV_WIDTH].reshape(t_new, n_dec, N_KV_HEADS, HEAD_DIM)
        att_s = _attn_sample(q_s, k_s.transpose(1, 2, 0, 3), v_s.transpose(1, 2, 0, 3), cache_k_flat, cache_v_flat,
                             page_table, slopes, l, n_pool)
        att_s = att_s.reshape(n_dec, N_HEADS, t_new, HEAD_DIM).transpose(2, 0, 1, 3).reshape(n_sample, ATTN_WIDTH)
        u_s = h_s[:, ATTN_WIDTH + 2 * KV_WIDTH:].reshape(n_sample, N_SSM_CHUNKS, LANES).transpose(1, 0, 2)
        y_s, hr_s, hi_s = _ssm(u_s, state_ssm_re[l].reshape(n_dec, n_states), state_ssm_im[l].reshape(n_dec, n_states),
                               prm, nb=n_dec, tc=t_new)

        x1_p, eid_p, gate_p = _postmix(att_p, y_p, xp, lw, alpha, prompt=True)
        x1_s, eid_s, gate_s = _postmix(att_s, y_s, xs, lw, alpha, prompt=False)

        eid_t = jnp.concatenate([eid_p, eid_s], axis=1)
        dest, pad_bounds, blk_exp, n_used = _moe_plan(eid_t, n_blocks)
        xs_slots = _dispatch(x1_p, x1_s, dest, pad_bounds, n_slots)
        ys_slots = _ffn(xs_slots, moe_w_gate[l].astype(BF16), moe_w_up[l].astype(BF16), moe_w_down[l].astype(BF16),
                        blk_exp, n_used, n_blocks)
        l2g = ln2_g[l].reshape(1, -1)
        l2b = ln2_b[l].reshape(1, -1)
        n_pa = 2 * n_b * seq
        xp = _combine(x1_p, gate_p, dest[:n_pa], ys_slots, l2g, l2b, alpha, prompt=True)
        xs = _combine(x1_s, gate_s, dest[n_pa:], ys_slots, l2g, l2b, alpha, prompt=False)

        outs["kp"].append(k_p.reshape(n_b, seq, N_KV_HEADS, HEAD_DIM))
        outs["vp"].append(v_p.reshape(n_b, seq, N_KV_HEADS, HEAD_DIM))
        outs["hrp"].append(hr_p.reshape(n_b, SSM_GROUPS, SSM_STATE))
        outs["hip"].append(hi_p.reshape(n_b, SSM_GROUPS, SSM_STATE))
        outs["ks"].append(k_s.transpose(1, 0, 2, 3))
        outs["vs"].append(v_s.transpose(1, 0, 2, 3))
        outs["hrs"].append(hr_s.reshape(n_dec, SSM_GROUPS, SSM_STATE))
        outs["his"].append(hi_s.reshape(n_dec, SSM_GROUPS, SSM_STATE))

    y_sample = xs.reshape(t_new, n_dec, D_MODEL).transpose(1, 0, 2)
    return (xp, y_sample,
            jnp.stack(outs["kp"]), jnp.stack(outs["vp"]), jnp.stack(outs["hrp"]), jnp.stack(outs["hip"]),
            jnp.stack(outs["ks"]), jnp.stack(outs["vs"]), jnp.stack(outs["hrs"]), jnp.stack(outs["his"]))
```

```python
import functools
import math

import jax
import jax.numpy as jnp
from jax import lax
from jax.experimental import pallas as pl
from jax.experimental.pallas import tpu as pltpu

F32 = jnp.float32
BF16 = jnp.bfloat16
I32 = jnp.int32
HIGHEST = lax.Precision.HIGHEST

D_MODEL = 2048
ATTN_WIDTH = 1024
SSM_WIDTH = 1024
HEAD_DIM = 128
N_HEADS = 8
N_KV_HEADS = 4
KV_WIDTH = N_KV_HEADS * HEAD_DIM
PROJ_WIDTH = ATTN_WIDTH + 2 * KV_WIDTH + SSM_WIDTH
MOBA_BLOCK = 256
MOBA_TOPK = 3
PAGE_SIZE = 128
SSM_GROUP_CH = 16
SSM_GROUPS = 64
SSM_STATE = 64
N_EXPERTS = 16
N_EXPERT_GROUPS = 4
EXPERTS_PER_GROUP = 4
D_FF = 1024
LN_EPS = 1e-5
RMS_EPS = 1e-6
NEG_INF = float("-inf")
Q_SCALE = HEAD_DIM ** -0.5

LANES = 128
SUBLANES = 8
VMEM_LIMIT = 56 * 1024 * 1024

TILE_T = 32
SLAB = D_MODEL // LANES
SSM_CHUNK = 8 * SSM_GROUP_CH
SSM_CHUNK_STATES = 8 * SSM_STATE
N_SSM_CHUNKS = SSM_WIDTH // SSM_CHUNK
MOE_BLOCK = 256
POSTMIX_SPLIT = 2


def _cparams(sem, vmem=VMEM_LIMIT):
    return pltpu.CompilerParams(dimension_semantics=sem, vmem_limit_bytes=vmem)


def _in_proj_prompt_kernel(x_ref, w_ref, q_ref, k_ref, v_ref, u_ref, *, n_b, tt):
    rows = n_b * tt
    x = x_ref[...].reshape(rows, D_MODEL).astype(BF16)
    h = jnp.dot(x, w_ref[...], preferred_element_type=F32)
    q_ref[...] = h[:, :ATTN_WIDTH].reshape(n_b, tt, ATTN_WIDTH)
    for b in range(n_b):
        for g in range(N_KV_HEADS):
            k0 = ATTN_WIDTH + g * HEAD_DIM
            v0 = ATTN_WIDTH + KV_WIDTH + g * HEAD_DIM
            k_ref[b, pl.ds(g, tt, stride=N_KV_HEADS), :] = h[b * tt:(b + 1) * tt, k0:k0 + HEAD_DIM]
            v_ref[b, pl.ds(g, tt, stride=N_KV_HEADS), :] = h[b * tt:(b + 1) * tt, v0:v0 + HEAD_DIM]
    u0 = ATTN_WIDTH + 2 * KV_WIDTH
    for c in range(N_SSM_CHUNKS):
        for b in range(n_b):
            u_ref[c, pl.ds(b, tt, stride=n_b), :] = h[b * tt:(b + 1) * tt, u0 + c * LANES:u0 + (c + 1) * LANES]


def _in_proj_prompt(x, w16, layer):
    n_b, seq, _ = x.shape
    tt = TILE_T
    nt = seq // tt
    kern = functools.partial(_in_proj_prompt_kernel, n_b=n_b, tt=tt)
    return pl.pallas_call(
        kern,
        out_shape=(
            jax.ShapeDtypeStruct((n_b, seq, ATTN_WIDTH), F32),
            jax.ShapeDtypeStruct((n_b, seq * N_KV_HEADS, HEAD_DIM), F32),
            jax.ShapeDtypeStruct((n_b, seq * N_KV_HEADS, HEAD_DIM), F32),
            jax.ShapeDtypeStruct((N_SSM_CHUNKS, seq * n_b, LANES), F32),
        ),
        grid=(nt,),
        in_specs=[
            pl.BlockSpec((n_b, tt, D_MODEL), lambda i: (0, i, 0)),
            pl.BlockSpec((None, D_MODEL, PROJ_WIDTH), lambda i: (layer, 0, 0)),
        ],
        out_specs=(
            pl.BlockSpec((n_b, tt, ATTN_WIDTH), lambda i: (0, i, 0)),
            pl.BlockSpec((n_b, tt * N_KV_HEADS, HEAD_DIM), lambda i: (0, i, 0)),
            pl.BlockSpec((n_b, tt * N_KV_HEADS, HEAD_DIM), lambda i: (0, i, 0)),
            pl.BlockSpec((N_SSM_CHUNKS, tt * n_b, LANES), lambda i: (0, i, 0)),
        ),
        compiler_params=_cparams(("arbitrary",)),
        name="in_proj_prompt",
    )(x, w16)


def _matmul_kernel(x_ref, w_ref, o_ref):
    o_ref[...] = jnp.dot(x_ref[...].astype(BF16), w_ref[...], preferred_element_type=F32)


def _in_proj_sample(x, w16, layer):
    m, k = x.shape
    n = w16.shape[2]
    tm = 256
    return pl.pallas_call(
        _matmul_kernel,
        out_shape=jax.ShapeDtypeStruct((m, n), F32),
        grid=(m // tm,),
        in_specs=[pl.BlockSpec((tm, k), lambda i: (i, 0)), pl.BlockSpec((None, k, n), lambda i: (layer, 0, 0))],
        out_specs=pl.BlockSpec((tm, n), lambda i: (i, 0)),
        compiler_params=_cparams(("arbitrary",)),
        name="in_proj_sample",
    )(x, w16)


def _topk_mask(gate, valid, k, axis):
    nb = gate.shape[axis]
    ids = lax.broadcasted_iota(I32, gate.shape, axis)
    gm = jnp.where(valid, gate, NEG_INF)
    cnt = jnp.zeros(gate.shape, I32)
    for j in range(nb):
        gj = gm[:, j:j + 1] if axis == 1 else gm[j:j + 1, :]
        beats = (gj > gm) | ((gj == gm) & (j < ids))
        cnt = cnt + beats.astype(I32)
    return valid & (cnt < k)


def _attn_prompt_kernel(slopes_ref, q_ref, k_ref, v_ref, o_ref, kmean_ref, k16_ref, vt1_ref, vt2_ref, acc_ref, *, n_blk):
    g = pl.program_id(1)
    qi = pl.program_id(2)
    blk = MOBA_BLOCK
    gq = N_HEADS // N_KV_HEADS
    width = gq * blk
    nt_dims = (((1,), (1,)), ((), ()))

    @pl.when(qi == 0)
    def _():
        for j in range(n_blk):
            rows_j = pl.ds(j * blk * N_KV_HEADS + g, blk, stride=N_KV_HEADS)
            kb = k_ref[rows_j, :]
            kmean_ref[j:j + 1, :] = jnp.sum(kb, axis=0, keepdims=True) * (1.0 / blk)
            k16_ref[j * blk:(j + 1) * blk, :] = kb.astype(BF16)
            vt = v_ref[rows_j, :].T.astype(BF16)
            vt1_ref[j] = vt
            vt2_ref[j // 2, :, (j % 2) * blk:(j % 2 + 1) * blk] = vt

    lane = lax.broadcasted_iota(I32, (1, width), 1)
    slope_vec = jnp.zeros((1, width), F32)
    for hh in range(gq):
        slope_vec = jnp.where(lane // blk == hh, slopes_ref[g * gq + hh], slope_vec)
    q_all = jnp.concatenate([q_ref[:, hh * HEAD_DIM:(hh + 1) * HEAD_DIM] for hh in range(gq)], axis=0) * Q_SCALE
    q16 = q_all.astype(BF16)
    blk_ids = lax.broadcasted_iota(I32, (n_blk, width), 0)
    gate_t = lax.dot_general(kmean_ref[...], q_all, nt_dims, precision=HIGHEST, preferred_element_type=F32)
    sel = _topk_mask(gate_t, blk_ids < qi, MOBA_TOPK, axis=0).astype(F32)
    key_id = lax.broadcasted_iota(I32, (blk, width), 0)
    qry_id = lax.broadcasted_iota(I32, (blk, width), 1) % blk
    bias = -slope_vec * (qry_id - key_id).astype(F32)

    own = pl.multiple_of(qi * blk, blk)
    s = lax.dot_general(k16_ref[pl.ds(own, blk), :], q16, nt_dims, preferred_element_type=F32)
    s = jnp.where(key_id <= qry_id, s + bias, NEG_INF)
    m0 = jnp.max(s, axis=0, keepdims=True)
    p = jnp.exp(s - m0)
    l0 = jnp.sum(p, axis=0, keepdims=True)
    acc_ref[...] = jnp.dot(vt1_ref[qi], p.astype(BF16), preferred_element_type=F32)

    def body(t, carry):
        m, l = carry
        off = pl.multiple_of(t * (2 * blk), 2 * blk)
        s = lax.dot_general(k16_ref[pl.ds(off, 2 * blk), :], q16, nt_dims, preferred_element_type=F32)
        half, cst = [], []
        for h in range(2):
            j = 2 * t + h
            selj = jnp.sum(jnp.where(blk_ids == j, sel, 0.0), axis=0, keepdims=True) > 0.5
            half.append(jnp.where(selj, s[h * blk:(h + 1) * blk, :] + bias, NEG_INF))
            cst.append(-slope_vec * ((qi - j) * blk).astype(F32))
        m_new = jnp.maximum(m, jnp.maximum(jnp.max(half[0], axis=0, keepdims=True) + cst[0],
                                           jnp.max(half[1], axis=0, keepdims=True) + cst[1]))
        a = jnp.exp(m - m_new)
        p0 = jnp.exp(half[0] - (m_new - cst[0]))
        p1 = jnp.exp(half[1] - (m_new - cst[1]))
        l = a * l + jnp.sum(p0, axis=0, keepdims=True) + jnp.sum(p1, axis=0, keepdims=True)
        p = jnp.concatenate([p0, p1], axis=0).astype(BF16)
        acc_ref[...] = a * acc_ref[...] + jnp.dot(vt2_ref[t], p, preferred_element_type=F32)
        return m_new, l

    _, l = lax.fori_loop(0, (qi + 1) // 2, body, (m0, l0))
    o_t = acc_ref[...] / l
    for hh in range(gq):
        o_ref[:, hh * HEAD_DIM:(hh + 1) * HEAD_DIM] = o_t[:, hh * blk:(hh + 1) * blk].T


def _attn_prompt(q, k, v, slopes):
    n_b, seq, _ = q.shape
    n_blk = seq // MOBA_BLOCK
    gq = N_HEADS // N_KV_HEADS
    kern = functools.partial(_attn_prompt_kernel, n_blk=n_blk)
    return pl.pallas_call(
        kern,
        out_shape=jax.ShapeDtypeStruct((n_b, seq, ATTN_WIDTH), F32),
        grid_spec=pltpu.PrefetchScalarGridSpec(
            num_scalar_prefetch=1,
            grid=(n_b, N_KV_HEADS, n_blk),
            in_specs=[
                pl.BlockSpec((None, MOBA_BLOCK, gq * HEAD_DIM), lambda b, g, i, s: (b, i, g)),
                pl.BlockSpec((None, seq * N_KV_HEADS, HEAD_DIM), lambda b, g, i, s: (b, 0, 0)),
                pl.BlockSpec((None, seq * N_KV_HEADS, HEAD_DIM), lambda b, g, i, s: (b, 0, 0)),
            ],
            out_specs=pl.BlockSpec((None, MOBA_BLOCK, gq * HEAD_DIM), lambda b, g, i, s: (b, i, g)),
            scratch_shapes=[
                pltpu.VMEM((n_blk, HEAD_DIM), F32),
                pltpu.VMEM((seq, HEAD_DIM), BF16),
                pltpu.VMEM((n_blk, HEAD_DIM, MOBA_BLOCK), BF16),
                pltpu.VMEM((n_blk // 2, HEAD_DIM, 2 * MOBA_BLOCK), BF16),
                pltpu.VMEM((HEAD_DIM, gq * MOBA_BLOCK), F32),
            ],
        ),
        compiler_params=_cparams(("arbitrary", "arbitrary", "arbitrary")),
        name="moba_prompt",
    )(slopes, q, k, v)


def _attn_sample_kernel(pt_ref, slopes_ref, q_ref, kn_ref, vn_ref, *rest, n_pages, t_new, past):
    k_refs = rest[:n_pages]
    v_refs = rest[n_pages:2 * n_pages]
    o_ref = rest[2 * n_pages]
    gq = N_HEADS // N_KV_HEADS
    rows = N_HEADS * t_new
    ppb = MOBA_BLOCK // PAGE_SIZE
    n_blk = n_pages // ppb
    page_rows = PAGE_SIZE * N_KV_HEADS
    cols = ppb * page_rows
    nt_dims = (((1,), (1,)), ((), ()))

    row = lax.broadcasted_iota(I32, (rows, 1), 0)
    head = row // t_new
    g_row = head // gq
    t_row = row % t_new
    slope_row = jnp.zeros((rows, 1), F32)
    for h in range(N_HEADS):
        slope_row = jnp.where(head == h, slopes_ref[h], slope_row)
    col = lax.broadcasted_iota(I32, (1, cols), 1)
    g_col = col % N_KV_HEADS
    kpos_col = (col // page_rows) * PAGE_SIZE + (col % page_rows) // N_KV_HEADS
    bias0 = jnp.where(g_row == g_col, -slope_row * (past + t_row - kpos_col).astype(F32), NEG_INF)

    def per_head_rows(fn):
        out = jnp.zeros((rows, HEAD_DIM), F32)
        for g in range(N_KV_HEADS):
            out = jnp.where(g_row == g, fn(g), out)
        return out

    q = q_ref[...] * Q_SCALE
    q16 = q.astype(BF16)
    blk_lane = lax.broadcasted_iota(I32, (rows, n_blk), 1)

    gate = jnp.zeros((rows, n_blk), F32)
    k16, v16 = [], []
    for j in range(n_blk):
        kblk = jnp.concatenate([k_refs[ppb * j + p][...] for p in range(ppb)], axis=0)
        fold = jnp.sum(kblk.reshape(cols // SUBLANES, SUBLANES, HEAD_DIM), axis=0)
        ksum = fold[0:N_KV_HEADS, :] + fold[N_KV_HEADS:, :]
        kmean_rows = per_head_rows(lambda g: ksum[g:g + 1, :]) * (1.0 / MOBA_BLOCK)
        gate = jnp.where(blk_lane == j, jnp.sum(q * kmean_rows, axis=1, keepdims=True), gate)
        k16.append(kblk.astype(BF16))
        v16.extend(v_refs[ppb * j + p][...].astype(BF16) for p in range(ppb))
    sel = _topk_mask(gate, blk_lane >= 0, MOBA_TOPK, axis=1).astype(F32)

    bias = jnp.concatenate(
        [jnp.where(sel[:, j:j + 1] > 0.5, bias0 + slope_row * float(j * MOBA_BLOCK), NEG_INF) for j in range(n_blk)],
        axis=1)
    s = lax.dot_general(q16, jnp.concatenate(k16, axis=0), nt_dims, preferred_element_type=F32) + bias

    s_own, v_own = [], []
    for tk in range(t_new):
        kn_rows = per_head_rows(lambda g: kn_ref[g, tk:tk + 1, :])
        sv = jnp.sum(q * kn_rows, axis=1, keepdims=True) - slope_row * (t_row - tk).astype(F32)
        s_own.append(jnp.where(t_row >= tk, sv, NEG_INF))
        v_own.append(per_head_rows(lambda g: vn_ref[g, tk:tk + 1, :]))
    m = jnp.max(s, axis=1, keepdims=True)
    for tk in range(t_new):
        m = jnp.maximum(m, s_own[tk])
    p = jnp.exp(s - m)
    num = jnp.dot(p.astype(BF16), jnp.concatenate(v16, axis=0), preferred_element_type=F32)
    den = jnp.sum(p, axis=1, keepdims=True)
    for tk in range(t_new):
        pw = jnp.exp(s_own[tk] - m)
        num = num + pw * v_own[tk]
        den = den + pw
    o_ref[...] = num / den


def _attn_sample(q, k_new, v_new, cache_k_flat, cache_v_flat, page_table, slopes, layer, n_pool):
    n_dec, rows, _ = q.shape
    t_new = k_new.shape[2]
    n_pages = page_table.shape[1]
    past = n_pages * PAGE_SIZE
    assert past % MOBA_BLOCK == 0
    page_rows = PAGE_SIZE * N_KV_HEADS
    base = layer * n_pool
    pt_flat = page_table.reshape(-1)
    kern = functools.partial(_attn_sample_kernel, n_pages=n_pages, t_new=t_new, past=past)

    def page_spec(p):
        return pl.BlockSpec((page_rows, HEAD_DIM), lambda b, pt, sl: (base + pt[b * n_pages + p], 0))

    q_spec = pl.BlockSpec((None, rows, HEAD_DIM), lambda b, pt, sl: (b, 0, 0))
    new_spec = pl.BlockSpec((None, N_KV_HEADS, t_new, HEAD_DIM), lambda b, pt, sl: (b, 0, 0, 0))
    pages = [page_spec(p) for p in range(n_pages)]
    return pl.pallas_call(
        kern,
        out_shape=jax.ShapeDtypeStruct((n_dec, rows, HEAD_DIM), F32),
        grid_spec=pltpu.PrefetchScalarGridSpec(
            num_scalar_prefetch=2,
            grid=(n_dec,),
            in_specs=[q_spec, new_spec, new_spec] + pages + pages,
            out_specs=q_spec,
        ),
        compiler_params=_cparams(("arbitrary",)),
        name="moba_sample",
    )(pt_flat, slopes, q, k_new, v_new, *([cache_k_flat] * n_pages), *([cache_v_flat] * n_pages))


def _ssm_kernel(u_ref, bre_ref, bim_ref, cre_ref, cim_ref, are_ref, aim_ref, d_ref, s0r_ref, s0i_ref,
                y_ref, hr_ref, hi_ref, xr_ref, xi_ref, str_ref, sti_ref, *, nb, tc):
    ti = pl.program_id(1)

    @pl.when(ti == 0)
    def _():
        str_ref[...] = s0r_ref[...]
        sti_ref[...] = s0i_ref[...]

    u = u_ref[...]
    u16 = u.astype(BF16)
    xr_ref[...] = jnp.dot(u16, bre_ref[...], preferred_element_type=F32)
    xi_ref[...] = jnp.dot(u16, bim_ref[...], preferred_element_type=F32)
    a_re = jnp.broadcast_to(are_ref[...], (nb, SSM_CHUNK_STATES))
    a_im = jnp.broadcast_to(aim_ref[...], (nb, SSM_CHUNK_STATES))

    def step(t, carry):
        h_re, h_im = carry
        sl = pl.ds(pl.multiple_of(t * nb, nb), nb)
        n_re = (a_re * h_re - a_im * h_im) + xr_ref[sl, :]
        n_im = (a_re * h_im + a_im * h_re) + xi_ref[sl, :]
        xr_ref[sl, :] = n_re
        xi_ref[sl, :] = n_im
        return n_re, n_im

    h_re, h_im = lax.fori_loop(0, tc, step, (str_ref[...], sti_ref[...]))
    str_ref[...] = h_re
    sti_ref[...] = h_im
    y = (jnp.dot(xr_ref[...].astype(BF16), cre_ref[...], preferred_element_type=F32)
         - jnp.dot(xi_ref[...].astype(BF16), cim_ref[...], preferred_element_type=F32)
         + d_ref[...] * u)
    y_ref[...] = jax.nn.gelu(y)

    @pl.when(ti == pl.num_programs(1) - 1)
    def _():
        hr_ref[...] = h_re
        hi_ref[...] = h_im


def _ssm(u_chunks, s0_re, s0_im, prm, nb, tc):
    n_rows = u_chunks.shape[1]
    nt = n_rows // (tc * nb)
    rows = tc * nb
    kern = functools.partial(_ssm_kernel, nb=nb, tc=tc)
    cs = SSM_CHUNK_STATES
    n_states = SSM_GROUPS * SSM_STATE
    return pl.pallas_call(
        kern,
        out_shape=(
            jax.ShapeDtypeStruct((N_SSM_CHUNKS, n_rows, LANES), F32),
            jax.ShapeDtypeStruct((nb, n_states), F32),
            jax.ShapeDtypeStruct((nb, n_states), F32),
        ),
        grid=(N_SSM_CHUNKS, nt),
        in_specs=[
            pl.BlockSpec((None, rows, LANES), lambda c, t: (c, t, 0)),
            pl.BlockSpec((None, SSM_CHUNK, cs), lambda c, t: (c, 0, 0)),
            pl.BlockSpec((None, SSM_CHUNK, cs), lambda c, t: (c, 0, 0)),
            pl.BlockSpec((None, cs, SSM_CHUNK), lambda c, t: (c, 0, 0)),
            pl.BlockSpec((None, cs, SSM_CHUNK), lambda c, t: (c, 0, 0)),
            pl.BlockSpec((1, cs), lambda c, t: (0, c)),
            pl.BlockSpec((1, cs), lambda c, t: (0, c)),
            pl.BlockSpec((1, SSM_CHUNK), lambda c, t: (0, c)),
            pl.BlockSpec((nb, cs), lambda c, t: (0, c)),
            pl.BlockSpec((nb, cs), lambda c, t: (0, c)),
        ],
        out_specs=(
            pl.BlockSpec((None, rows, LANES), lambda c, t: (c, t, 0)),
            pl.BlockSpec((nb, cs), lambda c, t: (0, c)),
            pl.BlockSpec((nb, cs), lambda c, t: (0, c)),
        ),
        scratch_shapes=[
            pltpu.VMEM((rows, cs), F32),
            pltpu.VMEM((rows, cs), F32),
            pltpu.VMEM((nb, cs), F32),
            pltpu.VMEM((nb, cs), F32),
        ],
        compiler_params=_cparams(("arbitrary", "arbitrary")),
        name="s5_mixer",
    )(u_chunks, prm["bbd_re"], prm["bbd_im"], prm["cbd_re"], prm["cbd_im"], prm["ab_re"], prm["ab_im"],
      prm["d"], s0_re, s0_im)


def _ssm_params(a_re, a_im, log_dt, b_re, b_im, c_re, c_im, d_skip):
    dt = jnp.exp(log_dt)
    mag = jnp.exp(a_re * dt)
    ab_re = mag * jnp.cos(a_im * dt)
    ab_im = mag * jnp.sin(a_im * dt)
    den = a_re * a_re + a_im * a_im
    f_re = ((ab_re - 1.0) * a_re + ab_im * a_im) / den
    f_im = (ab_im * a_re - (ab_re - 1.0) * a_im) / den
    bb_re = f_re[..., None] * b_re - f_im[..., None] * b_im
    bb_im = f_re[..., None] * b_im + f_im[..., None] * b_re
    gpc = SSM_CHUNK // SSM_GROUP_CH
    eye = jnp.eye(gpc, dtype=F32)

    def b_blockdiag(bb):
        x = bb.reshape(N_SSM_CHUNKS, gpc, SSM_STATE, SSM_GROUP_CH)
        m = jnp.einsum("kgpc,gh->kgchp", x, eye)
        return m.reshape(N_SSM_CHUNKS, gpc * SSM_GROUP_CH, gpc * SSM_STATE).astype(BF16)

    def c_blockdiag(cc):
        x = cc.reshape(N_SSM_CHUNKS, gpc, SSM_GROUP_CH, SSM_STATE)
        m = jnp.einsum("kgcp,gh->kgphc", x, eye)
        return m.reshape(N_SSM_CHUNKS, gpc * SSM_STATE, gpc * SSM_GROUP_CH).astype(BF16)

    return {
        "bbd_re": b_blockdiag(bb_re), "bbd_im": b_blockdiag(bb_im),
        "cbd_re": c_blockdiag(c_re), "cbd_im": c_blockdiag(c_im),
        "ab_re": ab_re.reshape(1, -1), "ab_im": ab_im.reshape(1, -1),
        "d": d_skip.reshape(1, -1),
    }


def _rank_lt(vals, a):
    cnt = jnp.zeros(vals[a].shape, I32)
    for b in range(len(vals)):
        if b == a:
            continue
        before = (vals[b] >= vals[a]) if b < a else (vals[b] > vals[a])
        cnt = cnt + before.astype(I32)
    return cnt


def _router(x1, rwt_ref, rb_ref, eid_ref, gate_ref, cols):
    logits = lax.dot_general(rwt_ref[...], x1, (((1,), (1,)), ((), ())), precision=HIGHEST,
                             preferred_element_type=F32)
    mx = jnp.max(logits, axis=0, keepdims=True)
    ex = jnp.exp(logits - mx)
    probs = ex / jnp.sum(ex, axis=0, keepdims=True)
    biased = probs + rb_ref[...]
    prow = [probs[e:e + 1, :] for e in range(N_EXPERTS)]
    brow = [biased[e:e + 1, :] for e in range(N_EXPERTS)]
    ranks = []
    gscore = []
    for gi in range(N_EXPERT_GROUPS):
        vals = brow[gi * EXPERTS_PER_GROUP:(gi + 1) * EXPERTS_PER_GROUP]
        rk = [_rank_lt(vals, a) for a in range(EXPERTS_PER_GROUP)]
        ranks.append(rk)
        sc = jnp.zeros(vals[0].shape, F32)
        for a in range(EXPERTS_PER_GROUP):
            sc = sc + jnp.where(rk[a] < 2, vals[a], 0.0)
        gscore.append(sc)
    e0 = jnp.zeros(prow[0].shape, I32)
    e1 = jnp.zeros(prow[0].shape, I32)
    p0 = jnp.zeros(prow[0].shape, F32)
    p1 = jnp.zeros(prow[0].shape, F32)
    for gi in range(N_EXPERT_GROUPS):
        chosen = _rank_lt(gscore, gi) == 0
        for a in range(EXPERTS_PER_GROUP):
            e = gi * EXPERTS_PER_GROUP + a
            first = chosen & (ranks[gi][a] == 0)
            second = chosen & (ranks[gi][a] == 1)
            e0 = jnp.where(first, e, e0)
            e1 = jnp.where(second, e, e1)
            p0 = jnp.where(first, prow[e], p0)
            p1 = jnp.where(second, prow[e], p1)
    tot = p0 + p1
    eid_ref[0:1, cols] = e0
    eid_ref[1:2, cols] = e1
    gate_ref[0:1, cols] = p0 / tot
    gate_ref[1:2, cols] = p1 / tot


def _postmix_kernel(att_ref, y_ref, x_ref, wglu_ref, wout_ref, ag_ref, sg_ref, lg_ref, lb_ref, rwt_ref, rb_ref,
                    x1_ref, eid_ref, gate_ref, ybuf_ref, *, n_b, tt, alpha):
    rows = x1_ref.shape[0] if n_b is None else n_b * tt
    if n_b is None:
        att = att_ref[...]
        x = x_ref[...]
        for c in range(N_SSM_CHUNKS):
            ybuf_ref[:, c * LANES:(c + 1) * LANES] = y_ref[c]
    else:
        for c in range(N_SSM_CHUNKS):
            for b in range(n_b):
                ybuf_ref[b * tt:(b + 1) * tt, c * LANES:(c + 1) * LANES] = y_ref[c, pl.ds(b, tt, stride=n_b), :]
    hrows = rows // POSTMIX_SPLIT
    for hf in range(POSTMIX_SPLIT):
        r0 = hf * hrows
        if n_b is None:
            att_h = att[r0:r0 + hrows, :]
            x_h = x[r0:r0 + hrows, :]
        else:
            att_h = att_ref[hf * (n_b // POSTMIX_SPLIT):(hf + 1) * (n_b // POSTMIX_SPLIT)].reshape(hrows, ATTN_WIDTH)
            x_h = x_ref[hf * (n_b // POSTMIX_SPLIT):(hf + 1) * (n_b // POSTMIX_SPLIT)].reshape(hrows, D_MODEL)
        y = ybuf_ref[r0:r0 + hrows, :]
        z = jnp.dot(y.astype(BF16), wglu_ref[...], preferred_element_type=F32)
        sg = y * jax.nn.sigmoid(z)
        ssm_n = sg * lax.rsqrt(jnp.mean(sg * sg, axis=-1, keepdims=True) + RMS_EPS) * sg_ref[...]
        att_n = att_h * lax.rsqrt(jnp.mean(att_h * att_h, axis=-1, keepdims=True) + RMS_EPS) * ag_ref[...]
        mix = (jnp.dot(att_n.astype(BF16), wout_ref[0:ATTN_WIDTH, :], preferred_element_type=F32)
               + jnp.dot(ssm_n.astype(BF16), wout_ref[ATTN_WIDTH:, :], preferred_element_type=F32))
        r = alpha * x_h + mix
        mu = jnp.mean(r, axis=-1, keepdims=True)
        rc = r - mu
        var = jnp.mean(rc * rc, axis=-1, keepdims=True)
        x1 = rc * lax.rsqrt(var + LN_EPS) * lg_ref[...] + lb_ref[...]
        if n_b is None:
            x1_ref[r0:r0 + hrows, :] = x1
        else:
            nbh = n_b // POSTMIX_SPLIT
            x1_ref[hf * nbh:(hf + 1) * nbh] = x1.reshape(nbh, tt, D_MODEL)
        _router(x1, rwt_ref, rb_ref, eid_ref, gate_ref, slice(r0, r0 + hrows))


def _postmix(att, y_chunks, x, lw, layer, alpha, prompt):
    if prompt:
        n_b, seq, _ = x.shape
        tt = TILE_T
        nt = seq // tt
        rows = n_b * tt
        n_tok = n_b * seq
        att_spec = pl.BlockSpec((n_b, tt, ATTN_WIDTH), lambda i: (0, i, 0))
        x_spec = pl.BlockSpec((n_b, tt, D_MODEL), lambda i: (0, i, 0))
        x1_shape = jax.ShapeDtypeStruct((n_b, seq, D_MODEL), F32)
        kern = functools.partial(_postmix_kernel, n_b=n_b, tt=tt, alpha=alpha)
    else:
        n_tok = x.shape[0]
        rows = 256
        nt = n_tok // rows
        att_spec = pl.BlockSpec((rows, ATTN_WIDTH), lambda i: (i, 0))
        x_spec = pl.BlockSpec((rows, D_MODEL), lambda i: (i, 0))
        x1_shape = jax.ShapeDtypeStruct((n_tok, D_MODEL), F32)
        kern = functools.partial(_postmix_kernel, n_b=None, tt=None, alpha=alpha)
    full = lambda shp: pl.BlockSpec(shp, lambda i: tuple(0 for _ in shp))
    return pl.pallas_call(
        kern,
        out_shape=(x1_shape, jax.ShapeDtypeStruct((2, n_tok), I32), jax.ShapeDtypeStruct((2, n_tok), F32)),
        grid=(nt,),
        in_specs=[
            att_spec,
            pl.BlockSpec((N_SSM_CHUNKS, rows, LANES), lambda i: (0, i, 0)),
            x_spec,
            pl.BlockSpec((None, SSM_WIDTH, SSM_WIDTH), lambda i: (layer, 0, 0)),
            pl.BlockSpec((None, D_MODEL, D_MODEL), lambda i: (layer, 0, 0)),
            full((1, ATTN_WIDTH)),
            full((1, SSM_WIDTH)),
            full((1, D_MODEL)),
            full((1, D_MODEL)),
            full((N_EXPERTS, D_MODEL)),
            full((N_EXPERTS, 1)),
        ],
        out_specs=(x_spec, pl.BlockSpec((2, rows), lambda i: (0, i)), pl.BlockSpec((2, rows), lambda i: (0, i))),
        scratch_shapes=[pltpu.VMEM((rows, SSM_WIDTH), F32)],
        compiler_params=_cparams(("arbitrary",)),
        name="postmix_prompt" if prompt else "postmix_sample",
    )(att, y_chunks, x, lw["w_glu"], lw["w_out"], lw["attn_g"], lw["ssm_g"], lw["ln1_g"], lw["ln1_b"],
      lw["router_wt"], lw["router_b"])


def _slab_rows(row):
    start = row * SLAB
    return pl.ds(start if isinstance(start, int) else pl.multiple_of(start, SLAB), SLAB)


def _slab_copy(src_ref, src_row, dst_ref, dst_row, sem):
    return pltpu.make_async_copy(src_ref.at[_slab_rows(src_row), :], dst_ref.at[_slab_rows(dst_row), :], sem)


def _dispatch_kernel(dest_ref, pad_ref, xp_ref, xs_ref, out_ref, slab_ref, sem, *, n_b, tt, n_prompt_tiles, n_blocks):
    i = pl.program_id(0)
    last = pl.num_programs(0) - 1
    rows = n_b * tt
    slot = i % 2
    slab = slab_ref.at[slot]

    @pl.when(i < n_prompt_tiles)
    def _():
        for s in range(SLAB):
            for b in range(n_b):
                slab[pl.ds(b * tt * SLAB + s, tt, stride=SLAB), :] = xp_ref[b, :, s * LANES:(s + 1) * LANES]

    @pl.when(i >= n_prompt_tiles)
    def _():
        for s in range(SLAB):
            slab[pl.ds(s, rows, stride=SLAB), :] = xs_ref[:, s * LANES:(s + 1) * LANES]

    base = i * (2 * rows)

    def start(r, c):
        for k in range(2):
            _slab_copy(slab, r, out_ref, dest_ref[base + 2 * r + k], sem.at[slot]).start()
        return c

    lax.fori_loop(0, rows, start, 0)

    def wait_tile(which):
        for _ in range(2):
            pltpu.make_async_copy(slab_ref.at[which], slab_ref.at[which], sem.at[which]).wait()

    @pl.when(i > 0)
    def _():
        wait_tile(1 - slot)

    @pl.when(i == last)
    def _():
        wait_tile(slot)
        slab[...] = jnp.zeros(slab.shape, F32)
        zsem = sem.at[slot]

        def per_expert(e, c):
            lo = pad_ref[e]
            hi = pad_ref[N_EXPERTS + e]

            def zs(s, c2):
                _slab_copy(slab, 0, out_ref, s, zsem).start()
                return c2

            lax.fori_loop(lo, hi, zs, 0)

            def zw(s, c2):
                _slab_copy(slab, 0, out_ref, s, zsem).wait()
                return c2

            lax.fori_loop(lo, hi, zw, 0)
            return c

        lax.fori_loop(0, N_EXPERTS, per_expert, 0)

        blk_rows = MOE_BLOCK * SLAB

        def block_copy(b):
            start = pl.multiple_of(b * blk_rows, blk_rows)
            return pltpu.make_async_copy(slab, out_ref.at[pl.ds(start, blk_rows), :], zsem)

        def zbs(b, c):
            block_copy(b).start()
            return c

        def zbw(b, c):
            block_copy(b).wait()
            return c

        first_unused = pad_ref[2 * N_EXPERTS - 1] // MOE_BLOCK
        lax.fori_loop(first_unused, n_blocks, zbs, 0)
        lax.fori_loop(first_unused, n_blocks, zbw, 0)


def _dispatch(x1_p, x1_s, dest, pad_bounds, n_slots):
    n_b, seq, _ = x1_p.shape
    tt = TILE_T
    rows = n_b * tt
    assert rows == MOE_BLOCK
    ntp = seq // tt
    nts = x1_s.shape[0] // rows
    kern = functools.partial(_dispatch_kernel, n_b=n_b, tt=tt, n_prompt_tiles=ntp, n_blocks=n_slots // MOE_BLOCK)
    return pl.pallas_call(
        kern,
        out_shape=jax.ShapeDtypeStruct((n_slots * SLAB, LANES), F32),
        grid_spec=pltpu.PrefetchScalarGridSpec(
            num_scalar_prefetch=2,
            grid=(ntp + nts,),
            in_specs=[
                pl.BlockSpec((n_b, tt, D_MODEL), lambda i, d, p: (0, jnp.minimum(i, ntp - 1), 0)),
                pl.BlockSpec((rows, D_MODEL), lambda i, d, p: (jnp.maximum(i - ntp, 0), 0)),
            ],
            out_specs=pl.BlockSpec(memory_space=pl.ANY),
            scratch_shapes=[
                pltpu.VMEM((2, rows * SLAB, LANES), F32),
                pltpu.SemaphoreType.DMA((2,)),
            ],
        ),
        compiler_params=_cparams(("arbitrary",)),
        name="moe_dispatch",
    )(dest, pad_bounds, x1_p, x1_s)


def _ffn_kernel(bexp_ref, nused_ref, xs_ref, wg_ref, wu_ref, wd_ref, ys_ref, xb_ref):
    i = pl.program_id(0)
    rows = MOE_BLOCK

    @pl.when(i < nused_ref[0])
    def _():
        for s in range(SLAB):
            xb_ref[:, s * LANES:(s + 1) * LANES] = xs_ref[pl.ds(s, rows, stride=SLAB), :].astype(BF16)
        xb = xb_ref[...]
        g = jnp.dot(xb, wg_ref[...], preferred_element_type=F32)
        u = jnp.dot(xb, wu_ref[...], preferred_element_type=F32)
        h = (g * jax.nn.sigmoid(g) * u).astype(BF16)
        y = jnp.dot(h, wd_ref[...], preferred_element_type=F32)
        for s in range(SLAB):
            ys_ref[pl.ds(s, rows, stride=SLAB), :] = y[:, s * LANES:(s + 1) * LANES]

    @pl.when(i >= nused_ref[0])
    def _():
        ys_ref[...] = jnp.zeros(ys_ref.shape, F32)


def _ffn(xs, w_gate16, w_up16, w_down16, blk_exp, n_used, n_blocks, layer):
    rows = MOE_BLOCK
    e0 = layer * N_EXPERTS

    def x_map(i, be, nu):
        return (jnp.minimum(i, nu[0] - 1), 0)

    return pl.pallas_call(
        _ffn_kernel,
        out_shape=jax.ShapeDtypeStruct(xs.shape, F32),
        grid_spec=pltpu.PrefetchScalarGridSpec(
            num_scalar_prefetch=2,
            grid=(n_blocks,),
            in_specs=[
                pl.BlockSpec((rows * SLAB, LANES), x_map),
                pl.BlockSpec((None, D_MODEL, D_FF), lambda i, be, nu: (e0 + be[i], 0, 0)),
                pl.BlockSpec((None, D_MODEL, D_FF), lambda i, be, nu: (e0 + be[i], 0, 0)),
                pl.BlockSpec((None, D_FF, D_MODEL), lambda i, be, nu: (e0 + be[i], 0, 0)),
            ],
            out_specs=pl.BlockSpec((rows * SLAB, LANES), lambda i, be, nu: (i, 0)),
            scratch_shapes=[pltpu.VMEM((rows, D_MODEL), BF16)],
        ),
        compiler_params=_cparams(("arbitrary",)),
        name="moe_ffn",
    )(blk_exp, n_used, xs, w_gate16, w_up16, w_down16)


def _combine_kernel(dest_ref, x1_ref, gate_ref, lg_ref, lb_ref, ys_ref, x2_ref, y0_ref, y1_ref, sem, *, n_b, tt, alpha):
    i = pl.program_id(0)
    rows = y0_ref.shape[1] // SLAB
    slot = i % 2

    def fetch(tile, which):
        base = tile * (2 * rows)

        def start(r, c):
            _slab_copy(ys_ref, dest_ref[base + 2 * r], y0_ref.at[which], r, sem.at[which]).start()
            _slab_copy(ys_ref, dest_ref[base + 2 * r + 1], y1_ref.at[which], r, sem.at[which]).start()
            return c

        lax.fori_loop(0, rows, start, 0)

    @pl.when(i == 0)
    def _():
        fetch(0, 0)

    @pl.when(i + 1 < pl.num_programs(0))
    def _():
        fetch(i + 1, 1 - slot)

    pltpu.make_async_copy(y0_ref.at[slot], y0_ref.at[slot], sem.at[slot]).wait()
    pltpu.make_async_copy(y1_ref.at[slot], y1_ref.at[slot], sem.at[slot]).wait()

    y0 = y0_ref.at[slot]
    y1 = y1_ref.at[slot]
    gate_rows = jnp.concatenate([gate_ref[...], jnp.zeros((SUBLANES - 2, rows), F32)], axis=0)
    gate_cols = gate_rows.T
    g0 = gate_cols[:, 0:1]
    g1 = gate_cols[:, 1:2]
    if n_b is None:
        x1 = x1_ref[...]
    else:
        x1 = x1_ref[...].reshape(rows, D_MODEL)
    parts = []
    for s in range(SLAB):
        moe = g0 * y0[pl.ds(s, rows, stride=SLAB), :] + g1 * y1[pl.ds(s, rows, stride=SLAB), :]
        parts.append(alpha * x1[:, s * LANES:(s + 1) * LANES] + moe)
    r = jnp.concatenate(parts, axis=1)
    mu = jnp.mean(r, axis=-1, keepdims=True)
    rc = r - mu
    var = jnp.mean(rc * rc, axis=-1, keepdims=True)
    x2 = rc * lax.rsqrt(var + LN_EPS) * lg_ref[...] + lb_ref[...]
    if n_b is None:
        x2_ref[...] = x2
    else:
        x2_ref[...] = x2.reshape(n_b, tt, D_MODEL)


def _combine(x1, gate_t, dest, ys, ln_g, ln_b, alpha, prompt):
    if prompt:
        n_b, seq, _ = x1.shape
        tt = TILE_T
        rows = n_b * tt
        nt = seq // tt
        x_spec = pl.BlockSpec((n_b, tt, D_MODEL), lambda i, d: (0, i, 0))
        kern = functools.partial(_combine_kernel, n_b=n_b, tt=tt, alpha=alpha)
    else:
        rows = 256
        nt = x1.shape[0] // rows
        x_spec = pl.BlockSpec((rows, D_MODEL), lambda i, d: (i, 0))
        kern = functools.partial(_combine_kernel, n_b=None, tt=None, alpha=alpha)
    return pl.pallas_call(
        kern,
        out_shape=jax.ShapeDtypeStruct(x1.shape, F32),
        grid_spec=pltpu.PrefetchScalarGridSpec(
            num_scalar_prefetch=1,
            grid=(nt,),
            in_specs=[
                x_spec,
                pl.BlockSpec((2, rows), lambda i, d: (0, i)),
                pl.BlockSpec((1, D_MODEL), lambda i, d: (0, 0)),
                pl.BlockSpec((1, D_MODEL), lambda i, d: (0, 0)),
                pl.BlockSpec(memory_space=pl.ANY),
            ],
            out_specs=x_spec,
            scratch_shapes=[
                pltpu.VMEM((2, rows * SLAB, LANES), F32),
                pltpu.VMEM((2, rows * SLAB, LANES), F32),
                pltpu.SemaphoreType.DMA((2,)),
            ],
        ),
        compiler_params=_cparams(("arbitrary",)),
        name="moe_combine_prompt" if prompt else "moe_combine_sample",
    )(dest, x1, gate_t, ln_g, ln_b, ys)


def _moe_plan(eid_t, n_blocks):
    e_flat = eid_t.T.reshape(-1)
    onehot = (e_flat[:, None] == jnp.arange(N_EXPERTS, dtype=I32)[None, :]).astype(I32)
    csum = jnp.cumsum(onehot, axis=0)
    rank = jnp.sum(csum * onehot, axis=1) - 1
    counts = csum[-1]
    padded = (counts + MOE_BLOCK - 1) // MOE_BLOCK * MOE_BLOCK
    pad_end = jnp.cumsum(padded)
    pad_start = pad_end - padded
    dest = (jnp.sum(onehot * pad_start[None, :], axis=1) + rank).astype(I32)
    n_used = (pad_end[-1] // MOE_BLOCK).astype(I32)
    first_slot = jnp.arange(n_blocks, dtype=I32) * MOE_BLOCK
    blk_exp = jnp.minimum(jnp.sum((first_slot[:, None] >= pad_end[None, :]).astype(I32), axis=1), N_EXPERTS - 1)
    last_exp = jnp.take(blk_exp, jnp.maximum(n_used - 1, 0))
    blk_exp = jnp.where(jnp.arange(n_blocks) < n_used, blk_exp, last_exp).astype(I32)
    pad_bounds = jnp.concatenate([pad_start + counts, pad_end]).astype(I32)
    return dest, pad_bounds, blk_exp, n_used.reshape(1)


def kernel(x_prompt, x_sample, cache_k, cache_v, state_ssm_re, state_ssm_im, page_table, w_in, w_out, attn_norm_g, ssm_norm_g, ssm_a_re, ssm_a_im, ssm_log_dt, ssm_b_re, ssm_b_im, ssm_c_re, ssm_c_im, ssm_d, ssm_w_glu, ln1_g, ln1_b, router_w, router_b, moe_w_gate, moe_w_up, moe_w_down, ln2_g, ln2_b):
    depth = w_in.shape[0]
    n_b, seq, _ = x_prompt.shape
    n_dec, t_new, _ = x_sample.shape
    n_pool = cache_k.shape[1]
    gq = N_HEADS // N_KV_HEADS
    alpha = (2 * depth) ** 0.25
    n_states = SSM_GROUPS * SSM_STATE
    n_sample = n_dec * t_new
    n_tok = n_b * seq + n_sample
    n_blocks = -(-(2 * n_tok + N_EXPERTS * (MOE_BLOCK - 1)) // MOE_BLOCK)
    n_slots = n_blocks * MOE_BLOCK

    slopes = 2.0 ** (-8.0 * jnp.arange(1, N_HEADS + 1, dtype=F32) / N_HEADS)
    cache_k_flat = cache_k.reshape(-1, HEAD_DIM)
    cache_v_flat = cache_v.reshape(-1, HEAD_DIM)
    router_wt = router_w.T
    router_bc = router_b.reshape(N_EXPERTS, 1)
    zeros_state = jnp.zeros((n_b, n_states), F32)
    w_in16 = w_in.astype(BF16)
    w_glu16 = ssm_w_glu.astype(BF16)
    w_out16 = w_out.astype(BF16)
    w_gate16 = moe_w_gate.astype(BF16).reshape(depth * N_EXPERTS, D_MODEL, D_FF)
    w_up16 = moe_w_up.astype(BF16).reshape(depth * N_EXPERTS, D_MODEL, D_FF)
    w_down16 = moe_w_down.astype(BF16).reshape(depth * N_EXPERTS, D_FF, D_MODEL)

    xp = x_prompt
    xs = x_sample.transpose(1, 0, 2).reshape(n_sample, D_MODEL)
    outs = {k: [] for k in ("kp", "vp", "hrp", "hip", "ks", "vs", "hrs", "his")}
    for l in range(depth):
        lw = {
            "w_glu": w_glu16, "w_out": w_out16,
            "attn_g": attn_norm_g[l].reshape(1, -1), "ssm_g": ssm_norm_g[l].reshape(1, -1),
            "ln1_g": ln1_g[l].reshape(1, -1), "ln1_b": ln1_b[l].reshape(1, -1),
            "router_wt": router_wt, "router_b": router_bc,
        }
        prm = _ssm_params(ssm_a_re[l], ssm_a_im[l], ssm_log_dt[l], ssm_b_re[l], ssm_b_im[l],
                          ssm_c_re[l], ssm_c_im[l], ssm_d[l])

        q_p, k_p, v_p, u_p = _in_proj_prompt(xp, w_in16, l)
        att_p = _attn_prompt(q_p, k_p, v_p, slopes)
        y_p, hr_p, hi_p = _ssm(u_p, zeros_state, zeros_state, prm, nb=n_b, tc=128)

        h_s = _in_proj_sample(xs, w_in16, l)
        q_s = h_s[:, :ATTN_WIDTH].reshape(t_new, n_dec, N_HEADS, HEAD_DIM)
        q_s = q_s.transpose(1, 2, 0, 3).reshape(n_dec, N_HEADS * t_new, HEAD_DIM)
        k_s = h_s[:, ATTN_WIDTH:ATTN_WIDTH + KV_WIDTH].reshape(t_new, n_dec, N_KV_HEADS, HEAD_DIM)
        v_s = h_s[:, ATTN_WIDTH + KV_WIDTH:ATTN_WIDTH + 2 * KV_WIDTH].reshape(t_new, n_dec, N_KV_HEADS, HEAD_DIM)
        att_s = _attn_sample(q_s, k_s.transpose(1, 2, 0, 3), v_s.transpose(1, 2, 0, 3), cache_k_flat, cache_v_flat,
                             page_table, slopes, l, n_pool)
        att_s = att_s.reshape(n_dec, N_HEADS, t_new, HEAD_DIM).transpose(2, 0, 1, 3).reshape(n_sample, ATTN_WIDTH)
        u_s = h_s[:, ATTN_WIDTH + 2 * KV_WIDTH:].reshape(n_sample, N_SSM_CHUNKS, LANES).transpose(1, 0, 2)
        y_s, hr_s, hi_s = _ssm(u_s, state_ssm_re[l].reshape(n_dec, n_states), state_ssm_im[l].reshape(n_dec, n_states),
                               prm, nb=n_dec, tc=t_new)

        x1_p, eid_p, gate_p = _postmix(att_p, y_p, xp, lw, l, alpha, prompt=True)
        x1_s, eid_s, gate_s = _postmix(att_s, y_s, xs, lw, l, alpha, prompt=False)

        eid_t = jnp.concatenate([eid_p, eid_s], axis=1)
        dest, pad_bounds, blk_exp, n_used = _moe_plan(eid_t, n_blocks)
        xs_slots = _dispatch(x1_p, x1_s, dest, pad_bounds, n_slots)
        ys_slots = _ffn(xs_slots, w_gate16, w_up16, w_down16, blk_exp, n_used, n_blocks, l)
        l2g = ln2_g[l].reshape(1, -1)
        l2b = ln2_b[l].reshape(1, -1)
        n_pa = 2 * n_b * seq
        xp = _combine(x1_p, gate_p, dest[:n_pa], ys_slots, l2g, l2b, alpha, prompt=True)
        xs = _combine(x1_s, gate_s, dest[n_pa:], ys_slots, l2g, l2b, alpha, prompt=False)

        outs["kp"].append(k_p.reshape(n_b, seq, N_KV_HEADS, HEAD_DIM))
        outs["vp"].append(v_p.reshape(n_b, seq, N_KV_HEADS, HEAD_DIM))
        outs["hrp"].append(hr_p.reshape(n_b, SSM_GROUPS, SSM_STATE))
        outs["hip"].append(hi_p.reshape(n_b, SSM_GROUPS, SSM_STATE))
        outs["ks"].append(k_s.transpose(1, 0, 2, 3))
        outs["vs"].append(v_s.transpose(1, 0, 2, 3))
        outs["hrs"].append(hr_s.reshape(n_dec, SSM_GROUPS, SSM_STATE))
        outs["his"].append(hi_s.reshape(n_dec, SSM_GROUPS, SSM_STATE))

    y_sample = xs.reshape(t_new, n_dec, D_MODEL).transpose(1, 0, 2)
    return (xp, y_sample,
            jnp.stack(outs["kp"]), jnp.stack(outs["vp"]), jnp.stack(outs["hrp"]), jnp.stack(outs["hip"]),
            jnp.stack(outs["ks"]), jnp.stack(outs["vs"]), jnp.stack(outs["hrs"]), jnp.stack(outs["his"]))
```

```python
import functools
import math

import jax
import jax.numpy as jnp
from jax import lax
from jax.experimental import pallas as pl
from jax.experimental.pallas import tpu as pltpu

F32 = jnp.float32
BF16 = jnp.bfloat16
I32 = jnp.int32
HIGHEST = lax.Precision.HIGHEST

D_MODEL = 2048
ATTN_WIDTH = 1024
SSM_WIDTH = 1024
HEAD_DIM = 128
N_HEADS = 8
N_KV_HEADS = 4
KV_WIDTH = N_KV_HEADS * HEAD_DIM
PROJ_WIDTH = ATTN_WIDTH + 2 * KV_WIDTH + SSM_WIDTH
MOBA_BLOCK = 256
MOBA_TOPK = 3
PAGE_SIZE = 128
SSM_GROUP_CH = 16
SSM_GROUPS = 64
SSM_STATE = 64
N_EXPERTS = 16
N_EXPERT_GROUPS = 4
EXPERTS_PER_GROUP = 4
D_FF = 1024
LN_EPS = 1e-5
RMS_EPS = 1e-6
NEG_INF = float("-inf")
Q_SCALE = HEAD_DIM ** -0.5

LANES = 128
SUBLANES = 8
VMEM_LIMIT = 56 * 1024 * 1024

TILE_T = 32
SLAB = D_MODEL // LANES
XSLAB = D_MODEL // (2 * LANES)
SSM_CHUNK = 8 * SSM_GROUP_CH
SSM_CHUNK_STATES = 8 * SSM_STATE
N_SSM_CHUNKS = SSM_WIDTH // SSM_CHUNK
MOE_BLOCK = 256
POSTMIX_SPLIT = 2


def _cparams(sem, vmem=VMEM_LIMIT):
    return pltpu.CompilerParams(dimension_semantics=sem, vmem_limit_bytes=vmem)


def _in_proj_prompt_kernel(x_ref, w_ref, q_ref, k_ref, v_ref, u_ref, *, n_b, tt):
    rows = n_b * tt
    x = x_ref[...].reshape(rows, D_MODEL).astype(BF16)
    h = jnp.dot(x, w_ref[...], preferred_element_type=F32)
    q_ref[...] = h[:, :ATTN_WIDTH].reshape(n_b, tt, ATTN_WIDTH)
    for b in range(n_b):
        for g in range(N_KV_HEADS):
            k0 = ATTN_WIDTH + g * HEAD_DIM
            v0 = ATTN_WIDTH + KV_WIDTH + g * HEAD_DIM
            k_ref[b, pl.ds(g, tt, stride=N_KV_HEADS), :] = h[b * tt:(b + 1) * tt, k0:k0 + HEAD_DIM]
            v_ref[b, pl.ds(g, tt, stride=N_KV_HEADS), :] = h[b * tt:(b + 1) * tt, v0:v0 + HEAD_DIM]
    u0 = ATTN_WIDTH + 2 * KV_WIDTH
    for c in range(N_SSM_CHUNKS):
        for b in range(n_b):
            u_ref[c, pl.ds(b, tt, stride=n_b), :] = h[b * tt:(b + 1) * tt, u0 + c * LANES:u0 + (c + 1) * LANES]


def _in_proj_prompt(x, w16, layer):
    n_b, seq, _ = x.shape
    tt = TILE_T
    nt = seq // tt
    kern = functools.partial(_in_proj_prompt_kernel, n_b=n_b, tt=tt)
    return pl.pallas_call(
        kern,
        out_shape=(
            jax.ShapeDtypeStruct((n_b, seq, ATTN_WIDTH), F32),
            jax.ShapeDtypeStruct((n_b, seq * N_KV_HEADS, HEAD_DIM), F32),
            jax.ShapeDtypeStruct((n_b, seq * N_KV_HEADS, HEAD_DIM), F32),
            jax.ShapeDtypeStruct((N_SSM_CHUNKS, seq * n_b, LANES), F32),
        ),
        grid=(nt,),
        in_specs=[
            pl.BlockSpec((n_b, tt, D_MODEL), lambda i: (0, i, 0)),
            pl.BlockSpec((None, D_MODEL, PROJ_WIDTH), lambda i: (layer, 0, 0)),
        ],
        out_specs=(
            pl.BlockSpec((n_b, tt, ATTN_WIDTH), lambda i: (0, i, 0)),
            pl.BlockSpec((n_b, tt * N_KV_HEADS, HEAD_DIM), lambda i: (0, i, 0)),
            pl.BlockSpec((n_b, tt * N_KV_HEADS, HEAD_DIM), lambda i: (0, i, 0)),
            pl.BlockSpec((N_SSM_CHUNKS, tt * n_b, LANES), lambda i: (0, i, 0)),
        ),
        compiler_params=_cparams(("arbitrary",)),
        name="in_proj_prompt",
    )(x, w16)


def _matmul_kernel(x_ref, w_ref, o_ref):
    o_ref[...] = jnp.dot(x_ref[...].astype(BF16), w_ref[...], preferred_element_type=F32)


def _in_proj_sample(x, w16, layer):
    m, k = x.shape
    n = w16.shape[2]
    tm = 256
    return pl.pallas_call(
        _matmul_kernel,
        out_shape=jax.ShapeDtypeStruct((m, n), F32),
        grid=(m // tm,),
        in_specs=[pl.BlockSpec((tm, k), lambda i: (i, 0)), pl.BlockSpec((None, k, n), lambda i: (layer, 0, 0))],
        out_specs=pl.BlockSpec((tm, n), lambda i: (i, 0)),
        compiler_params=_cparams(("arbitrary",)),
        name="in_proj_sample",
    )(x, w16)


def _topk_mask(gate, valid, k, axis):
    nb = gate.shape[axis]
    ids = lax.broadcasted_iota(I32, gate.shape, axis)
    gm = jnp.where(valid, gate, NEG_INF)
    cnt = jnp.zeros(gate.shape, I32)
    for j in range(nb):
        gj = gm[:, j:j + 1] if axis == 1 else gm[j:j + 1, :]
        beats = (gj > gm) | ((gj == gm) & (j < ids))
        cnt = cnt + beats.astype(I32)
    return valid & (cnt < k)


def _attn_prompt_kernel(slopes_ref, q_ref, k_ref, v_ref, o_ref, kmean_ref, k16_ref, vt1_ref, vt2_ref, acc_ref, s_ref,
                        *, n_blk):
    g = pl.program_id(1)
    qi = pl.program_id(2)
    blk = MOBA_BLOCK
    gq = N_HEADS // N_KV_HEADS
    width = gq * blk
    nt_dims = (((1,), (1,)), ((), ()))

    @pl.when(qi == 0)
    def _():
        for j in range(n_blk):
            rows_j = pl.ds(j * blk * N_KV_HEADS + g, blk, stride=N_KV_HEADS)
            kb = k_ref[rows_j, :]
            kmean_ref[j:j + 1, :] = jnp.sum(kb, axis=0, keepdims=True) * (1.0 / blk)
            k16_ref[j * blk:(j + 1) * blk, :] = kb.astype(BF16)
            vt = v_ref[rows_j, :].T.astype(BF16)
            vt1_ref[j] = vt
            vt2_ref[j // 2, :, (j % 2) * blk:(j % 2 + 1) * blk] = vt

    lane = lax.broadcasted_iota(I32, (1, width), 1)
    slope_vec = jnp.zeros((1, width), F32)
    for hh in range(gq):
        slope_vec = jnp.where(lane // blk == hh, slopes_ref[g * gq + hh], slope_vec)
    q_all = jnp.concatenate([q_ref[:, hh * HEAD_DIM:(hh + 1) * HEAD_DIM] for hh in range(gq)], axis=0) * Q_SCALE
    q16 = q_all.astype(BF16)
    blk_ids = lax.broadcasted_iota(I32, (n_blk, width), 0)
    gate_t = lax.dot_general(kmean_ref[...], q_all, nt_dims, precision=HIGHEST, preferred_element_type=F32)
    sel = _topk_mask(gate_t, blk_ids < qi, MOBA_TOPK, axis=0).astype(F32)
    key_id = lax.broadcasted_iota(I32, (blk, width), 0)
    qry_id = lax.broadcasted_iota(I32, (blk, width), 1) % blk
    bias = -slope_vec * (qry_id - key_id).astype(F32)

    own = pl.multiple_of(qi * blk, blk)
    s = lax.dot_general(k16_ref[pl.ds(own, blk), :], q16, nt_dims, preferred_element_type=F32)
    s = jnp.where(key_id <= qry_id, s + bias, NEG_INF)
    m0 = jnp.max(s, axis=0, keepdims=True)
    p = jnp.exp(s - m0)
    l0 = jnp.sum(p, axis=0, keepdims=True)
    acc_ref[...] = jnp.dot(vt1_ref[qi], p.astype(BF16), preferred_element_type=F32)

    n_pairs = (qi + 1) // 2

    def pair_scores(t):
        off = pl.multiple_of(t * (2 * blk), 2 * blk)
        return lax.dot_general(k16_ref[pl.ds(off, 2 * blk), :], q16, nt_dims, preferred_element_type=F32)

    s_ref[0] = pair_scores(0)

    def body(t, carry):
        m, l = carry
        s = s_ref[t % 2]
        s_ref[(t + 1) % 2] = pair_scores(jnp.minimum(t + 1, n_blk // 2 - 1))
        half, cst = [], []
        for h in range(2):
            j = 2 * t + h
            selj = jnp.sum(jnp.where(blk_ids == j, sel, 0.0), axis=0, keepdims=True) > 0.5
            half.append(jnp.where(selj, s[h * blk:(h + 1) * blk, :] + bias, NEG_INF))
            cst.append(-slope_vec * ((qi - j) * blk).astype(F32))
        m_new = jnp.maximum(m, jnp.maximum(jnp.max(half[0], axis=0, keepdims=True) + cst[0],
                                           jnp.max(half[1], axis=0, keepdims=True) + cst[1]))
        a = jnp.exp(m - m_new)
        p0 = jnp.exp(half[0] - (m_new - cst[0]))
        p1 = jnp.exp(half[1] - (m_new - cst[1]))
        l = a * l + jnp.sum(p0, axis=0, keepdims=True) + jnp.sum(p1, axis=0, keepdims=True)
        p = jnp.concatenate([p0, p1], axis=0).astype(BF16)
        acc_ref[...] = a * acc_ref[...] + jnp.dot(vt2_ref[t], p, preferred_element_type=F32)
        return m_new, l

    _, l = lax.fori_loop(0, n_pairs, body, (m0, l0))
    o_t = acc_ref[...] / l
    for hh in range(gq):
        o_ref[:, hh * HEAD_DIM:(hh + 1) * HEAD_DIM] = o_t[:, hh * blk:(hh + 1) * blk].T


def _attn_prompt(q, k, v, slopes):
    n_b, seq, _ = q.shape
    n_blk = seq // MOBA_BLOCK
    gq = N_HEADS // N_KV_HEADS
    kern = functools.partial(_attn_prompt_kernel, n_blk=n_blk)
    return pl.pallas_call(
        kern,
        out_shape=jax.ShapeDtypeStruct((n_b, seq, ATTN_WIDTH), F32),
        grid_spec=pltpu.PrefetchScalarGridSpec(
            num_scalar_prefetch=1,
            grid=(n_b, N_KV_HEADS, n_blk),
            in_specs=[
                pl.BlockSpec((None, MOBA_BLOCK, gq * HEAD_DIM), lambda b, g, i, s: (b, i, g)),
                pl.BlockSpec((None, seq * N_KV_HEADS, HEAD_DIM), lambda b, g, i, s: (b, 0, 0)),
                pl.BlockSpec((None, seq * N_KV_HEADS, HEAD_DIM), lambda b, g, i, s: (b, 0, 0)),
            ],
            out_specs=pl.BlockSpec((None, MOBA_BLOCK, gq * HEAD_DIM), lambda b, g, i, s: (b, i, g)),
            scratch_shapes=[
                pltpu.VMEM((n_blk, HEAD_DIM), F32),
                pltpu.VMEM((seq, HEAD_DIM), BF16),
                pltpu.VMEM((n_blk, HEAD_DIM, MOBA_BLOCK), BF16),
                pltpu.VMEM((n_blk // 2, HEAD_DIM, 2 * MOBA_BLOCK), BF16),
                pltpu.VMEM((HEAD_DIM, gq * MOBA_BLOCK), F32),
                pltpu.VMEM((2, 2 * MOBA_BLOCK, gq * MOBA_BLOCK), F32),
            ],
        ),
        compiler_params=_cparams(("arbitrary", "arbitrary", "arbitrary")),
        name="moba_prompt",
    )(slopes, q, k, v)


def _attn_sample_kernel(pt_ref, slopes_ref, q_ref, kn_ref, vn_ref, *rest, n_pages, t_new, past):
    k_refs = rest[:n_pages]
    v_refs = rest[n_pages:2 * n_pages]
    o_ref = rest[2 * n_pages]
    gq = N_HEADS // N_KV_HEADS
    rows = N_HEADS * t_new
    ppb = MOBA_BLOCK // PAGE_SIZE
    n_blk = n_pages // ppb
    page_rows = PAGE_SIZE * N_KV_HEADS
    cols = ppb * page_rows
    nt_dims = (((1,), (1,)), ((), ()))

    row = lax.broadcasted_iota(I32, (rows, 1), 0)
    head = row // t_new
    g_row = head // gq
    t_row = row % t_new
    slope_row = jnp.zeros((rows, 1), F32)
    for h in range(N_HEADS):
        slope_row = jnp.where(head == h, slopes_ref[h], slope_row)
    col = lax.broadcasted_iota(I32, (1, cols), 1)
    g_col = col % N_KV_HEADS
    kpos_col = (col // page_rows) * PAGE_SIZE + (col % page_rows) // N_KV_HEADS
    bias0 = jnp.where(g_row == g_col, -slope_row * (past + t_row - kpos_col).astype(F32), NEG_INF)

    def per_head_rows(fn):
        out = jnp.zeros((rows, HEAD_DIM), F32)
        for g in range(N_KV_HEADS):
            out = jnp.where(g_row == g, fn(g), out)
        return out

    q = q_ref[...] * Q_SCALE
    q16 = q.astype(BF16)
    blk_lane = lax.broadcasted_iota(I32, (rows, n_blk), 1)

    gate = jnp.zeros((rows, n_blk), F32)
    k16, v16 = [], []
    for j in range(n_blk):
        kblk = jnp.concatenate([k_refs[ppb * j + p][...] for p in range(ppb)], axis=0)
        fold = jnp.sum(kblk.reshape(cols // SUBLANES, SUBLANES, HEAD_DIM), axis=0)
        ksum = fold[0:N_KV_HEADS, :] + fold[N_KV_HEADS:, :]
        kmean_rows = per_head_rows(lambda g: ksum[g:g + 1, :]) * (1.0 / MOBA_BLOCK)
        gate = jnp.where(blk_lane == j, jnp.sum(q * kmean_rows, axis=1, keepdims=True), gate)
        k16.append(kblk.astype(BF16))
        v16.extend(v_refs[ppb * j + p][...].astype(BF16) for p in range(ppb))
    sel = _topk_mask(gate, blk_lane >= 0, MOBA_TOPK, axis=1).astype(F32)

    bias = jnp.concatenate(
        [jnp.where(sel[:, j:j + 1] > 0.5, bias0 + slope_row * float(j * MOBA_BLOCK), NEG_INF) for j in range(n_blk)],
        axis=1)
    s = lax.dot_general(q16, jnp.concatenate(k16, axis=0), nt_dims, preferred_element_type=F32) + bias

    s_own, v_own = [], []
    for tk in range(t_new):
        kn_rows = per_head_rows(lambda g: kn_ref[g, tk:tk + 1, :])
        sv = jnp.sum(q * kn_rows, axis=1, keepdims=True) - slope_row * (t_row - tk).astype(F32)
        s_own.append(jnp.where(t_row >= tk, sv, NEG_INF))
        v_own.append(per_head_rows(lambda g: vn_ref[g, tk:tk + 1, :]))
    m = jnp.max(s, axis=1, keepdims=True)
    for tk in range(t_new):
        m = jnp.maximum(m, s_own[tk])
    p = jnp.exp(s - m)
    num = jnp.dot(p.astype(BF16), jnp.concatenate(v16, axis=0), preferred_element_type=F32)
    den = jnp.sum(p, axis=1, keepdims=True)
    for tk in range(t_new):
        pw = jnp.exp(s_own[tk] - m)
        num = num + pw * v_own[tk]
        den = den + pw
    o_ref[...] = num / den


def _attn_sample(q, k_new, v_new, cache_k_flat, cache_v_flat, page_table, slopes, layer, n_pool):
    n_dec, rows, _ = q.shape
    t_new = k_new.shape[2]
    n_pages = page_table.shape[1]
    past = n_pages * PAGE_SIZE
    assert past % MOBA_BLOCK == 0
    page_rows = PAGE_SIZE * N_KV_HEADS
    base = layer * n_pool
    pt_flat = page_table.reshape(-1)
    kern = functools.partial(_attn_sample_kernel, n_pages=n_pages, t_new=t_new, past=past)

    def page_spec(p):
        return pl.BlockSpec((page_rows, HEAD_DIM), lambda b, pt, sl: (base + pt[b * n_pages + p], 0))

    q_spec = pl.BlockSpec((None, rows, HEAD_DIM), lambda b, pt, sl: (b, 0, 0))
    new_spec = pl.BlockSpec((None, N_KV_HEADS, t_new, HEAD_DIM), lambda b, pt, sl: (b, 0, 0, 0))
    pages = [page_spec(p) for p in range(n_pages)]
    return pl.pallas_call(
        kern,
        out_shape=jax.ShapeDtypeStruct((n_dec, rows, HEAD_DIM), F32),
        grid_spec=pltpu.PrefetchScalarGridSpec(
            num_scalar_prefetch=2,
            grid=(n_dec,),
            in_specs=[q_spec, new_spec, new_spec] + pages + pages,
            out_specs=q_spec,
        ),
        compiler_params=_cparams(("arbitrary",)),
        name="moba_sample",
    )(pt_flat, slopes, q, k_new, v_new, *([cache_k_flat] * n_pages), *([cache_v_flat] * n_pages))


def _ssm_kernel(u_ref, bre_ref, bim_ref, cre_ref, cim_ref, are_ref, aim_ref, d_ref, s0r_ref, s0i_ref,
                y_ref, hr_ref, hi_ref, xr_ref, xi_ref, str_ref, sti_ref, *, nb, tc):
    ti = pl.program_id(1)

    @pl.when(ti == 0)
    def _():
        str_ref[...] = s0r_ref[...]
        sti_ref[...] = s0i_ref[...]

    u = u_ref[...]
    u16 = u.astype(BF16)
    xr_ref[...] = jnp.dot(u16, bre_ref[...], preferred_element_type=F32)
    xi_ref[...] = jnp.dot(u16, bim_ref[...], preferred_element_type=F32)
    a_re = jnp.broadcast_to(are_ref[...], (nb, SSM_CHUNK_STATES))
    a_im = jnp.broadcast_to(aim_ref[...], (nb, SSM_CHUNK_STATES))

    def step(t, carry):
        h_re, h_im = carry
        sl = pl.ds(pl.multiple_of(t * nb, nb), nb)
        n_re = (a_re * h_re - a_im * h_im) + xr_ref[sl, :]
        n_im = (a_re * h_im + a_im * h_re) + xi_ref[sl, :]
        xr_ref[sl, :] = n_re
        xi_ref[sl, :] = n_im
        return n_re, n_im

    h_re, h_im = lax.fori_loop(0, tc, step, (str_ref[...], sti_ref[...]))
    str_ref[...] = h_re
    sti_ref[...] = h_im
    y = (jnp.dot(xr_ref[...].astype(BF16), cre_ref[...], preferred_element_type=F32)
         - jnp.dot(xi_ref[...].astype(BF16), cim_ref[...], preferred_element_type=F32)
         + d_ref[...] * u)
    y_ref[...] = jax.nn.gelu(y)

    @pl.when(ti == pl.num_programs(1) - 1)
    def _():
        hr_ref[...] = h_re
        hi_ref[...] = h_im


def _ssm(u_chunks, s0_re, s0_im, prm, nb, tc):
    n_rows = u_chunks.shape[1]
    nt = n_rows // (tc * nb)
    rows = tc * nb
    kern = functools.partial(_ssm_kernel, nb=nb, tc=tc)
    cs = SSM_CHUNK_STATES
    n_states = SSM_GROUPS * SSM_STATE
    return pl.pallas_call(
        kern,
        out_shape=(
            jax.ShapeDtypeStruct((N_SSM_CHUNKS, n_rows, LANES), F32),
            jax.ShapeDtypeStruct((nb, n_states), F32),
            jax.ShapeDtypeStruct((nb, n_states), F32),
        ),
        grid=(N_SSM_CHUNKS, nt),
        in_specs=[
            pl.BlockSpec((None, rows, LANES), lambda c, t: (c, t, 0)),
            pl.BlockSpec((None, SSM_CHUNK, cs), lambda c, t: (c, 0, 0)),
            pl.BlockSpec((None, SSM_CHUNK, cs), lambda c, t: (c, 0, 0)),
            pl.BlockSpec((None, cs, SSM_CHUNK), lambda c, t: (c, 0, 0)),
            pl.BlockSpec((None, cs, SSM_CHUNK), lambda c, t: (c, 0, 0)),
            pl.BlockSpec((1, cs), lambda c, t: (0, c)),
            pl.BlockSpec((1, cs), lambda c, t: (0, c)),
            pl.BlockSpec((1, SSM_CHUNK), lambda c, t: (0, c)),
            pl.BlockSpec((nb, cs), lambda c, t: (0, c)),
            pl.BlockSpec((nb, cs), lambda c, t: (0, c)),
        ],
        out_specs=(
            pl.BlockSpec((None, rows, LANES), lambda c, t: (c, t, 0)),
            pl.BlockSpec((nb, cs), lambda c, t: (0, c)),
            pl.BlockSpec((nb, cs), lambda c, t: (0, c)),
        ),
        scratch_shapes=[
            pltpu.VMEM((rows, cs), F32),
            pltpu.VMEM((rows, cs), F32),
            pltpu.VMEM((nb, cs), F32),
            pltpu.VMEM((nb, cs), F32),
        ],
        compiler_params=_cparams(("arbitrary", "arbitrary")),
        name="s5_mixer",
    )(u_chunks, prm["bbd_re"], prm["bbd_im"], prm["cbd_re"], prm["cbd_im"], prm["ab_re"], prm["ab_im"],
      prm["d"], s0_re, s0_im)


def _ssm_params(a_re, a_im, log_dt, b_re, b_im, c_re, c_im, d_skip):
    dt = jnp.exp(log_dt)
    mag = jnp.exp(a_re * dt)
    ab_re = mag * jnp.cos(a_im * dt)
    ab_im = mag * jnp.sin(a_im * dt)
    den = a_re * a_re + a_im * a_im
    f_re = ((ab_re - 1.0) * a_re + ab_im * a_im) / den
    f_im = (ab_im * a_re - (ab_re - 1.0) * a_im) / den
    bb_re = f_re[..., None] * b_re - f_im[..., None] * b_im
    bb_im = f_re[..., None] * b_im + f_im[..., None] * b_re
    gpc = SSM_CHUNK // SSM_GROUP_CH
    eye = jnp.eye(gpc, dtype=F32)

    def b_blockdiag(bb):
        x = bb.reshape(N_SSM_CHUNKS, gpc, SSM_STATE, SSM_GROUP_CH)
        m = jnp.einsum("kgpc,gh->kgchp", x, eye)
        return m.reshape(N_SSM_CHUNKS, gpc * SSM_GROUP_CH, gpc * SSM_STATE).astype(BF16)

    def c_blockdiag(cc):
        x = cc.reshape(N_SSM_CHUNKS, gpc, SSM_GROUP_CH, SSM_STATE)
        m = jnp.einsum("kgcp,gh->kgphc", x, eye)
        return m.reshape(N_SSM_CHUNKS, gpc * SSM_STATE, gpc * SSM_GROUP_CH).astype(BF16)

    return {
        "bbd_re": b_blockdiag(bb_re), "bbd_im": b_blockdiag(bb_im),
        "cbd_re": c_blockdiag(c_re), "cbd_im": c_blockdiag(c_im),
        "ab_re": ab_re.reshape(1, -1), "ab_im": ab_im.reshape(1, -1),
        "d": d_skip.reshape(1, -1),
    }


def _rank_lt(vals, a):
    cnt = jnp.zeros(vals[a].shape, I32)
    for b in range(len(vals)):
        if b == a:
            continue
        before = (vals[b] >= vals[a]) if b < a else (vals[b] > vals[a])
        cnt = cnt + before.astype(I32)
    return cnt


def _router(x1, rwt_ref, rb_ref, eid_ref, gate_ref, cols):
    logits = lax.dot_general(rwt_ref[...], x1, (((1,), (1,)), ((), ())), precision=HIGHEST,
                             preferred_element_type=F32)
    mx = jnp.max(logits, axis=0, keepdims=True)
    ex = jnp.exp(logits - mx)
    probs = ex / jnp.sum(ex, axis=0, keepdims=True)
    biased = probs + rb_ref[...]
    prow = [probs[e:e + 1, :] for e in range(N_EXPERTS)]
    brow = [biased[e:e + 1, :] for e in range(N_EXPERTS)]
    ranks = []
    gscore = []
    for gi in range(N_EXPERT_GROUPS):
        vals = brow[gi * EXPERTS_PER_GROUP:(gi + 1) * EXPERTS_PER_GROUP]
        rk = [_rank_lt(vals, a) for a in range(EXPERTS_PER_GROUP)]
        ranks.append(rk)
        sc = jnp.zeros(vals[0].shape, F32)
        for a in range(EXPERTS_PER_GROUP):
            sc = sc + jnp.where(rk[a] < 2, vals[a], 0.0)
        gscore.append(sc)
    e0 = jnp.zeros(prow[0].shape, I32)
    e1 = jnp.zeros(prow[0].shape, I32)
    p0 = jnp.zeros(prow[0].shape, F32)
    p1 = jnp.zeros(prow[0].shape, F32)
    for gi in range(N_EXPERT_GROUPS):
        chosen = _rank_lt(gscore, gi) == 0
        for a in range(EXPERTS_PER_GROUP):
            e = gi * EXPERTS_PER_GROUP + a
            first = chosen & (ranks[gi][a] == 0)
            second = chosen & (ranks[gi][a] == 1)
            e0 = jnp.where(first, e, e0)
            e1 = jnp.where(second, e, e1)
            p0 = jnp.where(first, prow[e], p0)
            p1 = jnp.where(second, prow[e], p1)
    tot = p0 + p1
    eid_ref[0:1, cols] = e0
    eid_ref[1:2, cols] = e1
    gate_ref[0:1, cols] = p0 / tot
    gate_ref[1:2, cols] = p1 / tot


def _postmix_kernel(att_ref, y_ref, x_ref, wglu_ref, wout_ref, ag_ref, sg_ref, lg_ref, lb_ref, rwt_ref, rb_ref,
                    x1_ref, eid_ref, gate_ref, ybuf_ref, *, n_b, tt, alpha):
    rows = x1_ref.shape[0] if n_b is None else n_b * tt
    if n_b is None:
        att = att_ref[...]
        x = x_ref[...]
        for c in range(N_SSM_CHUNKS):
            ybuf_ref[:, c * LANES:(c + 1) * LANES] = y_ref[c]
    else:
        for c in range(N_SSM_CHUNKS):
            for b in range(n_b):
                ybuf_ref[b * tt:(b + 1) * tt, c * LANES:(c + 1) * LANES] = y_ref[c, pl.ds(b, tt, stride=n_b), :]
    hrows = rows // POSTMIX_SPLIT
    for hf in range(POSTMIX_SPLIT):
        r0 = hf * hrows
        if n_b is None:
            att_h = att[r0:r0 + hrows, :]
            x_h = x[r0:r0 + hrows, :]
        else:
            att_h = att_ref[hf * (n_b // POSTMIX_SPLIT):(hf + 1) * (n_b // POSTMIX_SPLIT)].reshape(hrows, ATTN_WIDTH)
            x_h = x_ref[hf * (n_b // POSTMIX_SPLIT):(hf + 1) * (n_b // POSTMIX_SPLIT)].reshape(hrows, D_MODEL)
        y = ybuf_ref[r0:r0 + hrows, :]
        z = jnp.dot(y.astype(BF16), wglu_ref[...], preferred_element_type=F32)
        sg = y * jax.nn.sigmoid(z)
        ssm_n = sg * lax.rsqrt(jnp.mean(sg * sg, axis=-1, keepdims=True) + RMS_EPS) * sg_ref[...]
        att_n = att_h * lax.rsqrt(jnp.mean(att_h * att_h, axis=-1, keepdims=True) + RMS_EPS) * ag_ref[...]
        mix = (jnp.dot(att_n.astype(BF16), wout_ref[0:ATTN_WIDTH, :], preferred_element_type=F32)
               + jnp.dot(ssm_n.astype(BF16), wout_ref[ATTN_WIDTH:, :], preferred_element_type=F32))
        r = alpha * x_h + mix
        mu = jnp.mean(r, axis=-1, keepdims=True)
        rc = r - mu
        var = jnp.mean(rc * rc, axis=-1, keepdims=True)
        x1 = rc * lax.rsqrt(var + LN_EPS) * lg_ref[...] + lb_ref[...]
        if n_b is None:
            x1_ref[r0:r0 + hrows, :] = x1
        else:
            nbh = n_b // POSTMIX_SPLIT
            x1_ref[hf * nbh:(hf + 1) * nbh] = x1.reshape(nbh, tt, D_MODEL)
        _router(x1, rwt_ref, rb_ref, eid_ref, gate_ref, slice(r0, r0 + hrows))


def _postmix(att, y_chunks, x, lw, layer, alpha, prompt):
    if prompt:
        n_b, seq, _ = x.shape
        tt = TILE_T
        nt = seq // tt
        rows = n_b * tt
        n_tok = n_b * seq
        att_spec = pl.BlockSpec((n_b, tt, ATTN_WIDTH), lambda i: (0, i, 0))
        x_spec = pl.BlockSpec((n_b, tt, D_MODEL), lambda i: (0, i, 0))
        x1_shape = jax.ShapeDtypeStruct((n_b, seq, D_MODEL), F32)
        kern = functools.partial(_postmix_kernel, n_b=n_b, tt=tt, alpha=alpha)
    else:
        n_tok = x.shape[0]
        rows = 256
        nt = n_tok // rows
        att_spec = pl.BlockSpec((rows, ATTN_WIDTH), lambda i: (i, 0))
        x_spec = pl.BlockSpec((rows, D_MODEL), lambda i: (i, 0))
        x1_shape = jax.ShapeDtypeStruct((n_tok, D_MODEL), F32)
        kern = functools.partial(_postmix_kernel, n_b=None, tt=None, alpha=alpha)
    full = lambda shp: pl.BlockSpec(shp, lambda i: tuple(0 for _ in shp))
    return pl.pallas_call(
        kern,
        out_shape=(x1_shape, jax.ShapeDtypeStruct((2, n_tok), I32), jax.ShapeDtypeStruct((2, n_tok), F32)),
        grid=(nt,),
        in_specs=[
            att_spec,
            pl.BlockSpec((N_SSM_CHUNKS, rows, LANES), lambda i: (0, i, 0)),
            x_spec,
            pl.BlockSpec((None, SSM_WIDTH, SSM_WIDTH), lambda i: (layer, 0, 0)),
            pl.BlockSpec((None, D_MODEL, D_MODEL), lambda i: (layer, 0, 0)),
            full((1, ATTN_WIDTH)),
            full((1, SSM_WIDTH)),
            full((1, D_MODEL)),
            full((1, D_MODEL)),
            full((N_EXPERTS, D_MODEL)),
            full((N_EXPERTS, 1)),
        ],
        out_specs=(x_spec, pl.BlockSpec((2, rows), lambda i: (0, i)), pl.BlockSpec((2, rows), lambda i: (0, i))),
        scratch_shapes=[pltpu.VMEM((rows, SSM_WIDTH), F32)],
        compiler_params=_cparams(("arbitrary",)),
        name="postmix_prompt" if prompt else "postmix_sample",
    )(att, y_chunks, x, lw["w_glu"], lw["w_out"], lw["attn_g"], lw["ssm_g"], lw["ln1_g"], lw["ln1_b"],
      lw["router_wt"], lw["router_b"])


def _slab_rows(row, slab):
    start = row * slab
    return pl.ds(start if isinstance(start, int) else pl.multiple_of(start, slab), slab)


def _slab_copy(src_ref, src_row, dst_ref, dst_row, sem, slab=SLAB):
    return pltpu.make_async_copy(src_ref.at[_slab_rows(src_row, slab), :], dst_ref.at[_slab_rows(dst_row, slab), :],
                                 sem)


def _pack_bf16_pair(a, b):
    hi = lax.bitcast_convert_type(a.astype(BF16).astype(F32), jnp.uint32)
    lo = lax.bitcast_convert_type(b.astype(BF16).astype(F32), jnp.uint32)
    return hi | (lo >> 16)


def _unpack_bf16_pair(w):
    a = lax.bitcast_convert_type(w & jnp.uint32(0xFFFF0000), F32)
    b = lax.bitcast_convert_type(w << 16, F32)
    return a.astype(BF16), b.astype(BF16)


def _dispatch_kernel(dest_ref, pad_ref, xp_ref, xs_ref, out_ref, slab_ref, sem, *, n_b, tt, n_prompt_tiles, n_blocks):
    i = pl.program_id(0)
    last = pl.num_programs(0) - 1
    rows = n_b * tt
    slot = i % 2
    slab = slab_ref.at[slot]

    half = D_MODEL // 2

    @pl.when(i < n_prompt_tiles)
    def _():
        for s in range(XSLAB):
            for b in range(n_b):
                slab[pl.ds(b * tt * XSLAB + s, tt, stride=XSLAB), :] = _pack_bf16_pair(
                    xp_ref[b, :, s * LANES:(s + 1) * LANES], xp_ref[b, :, half + s * LANES:half + (s + 1) * LANES])

    @pl.when(i >= n_prompt_tiles)
    def _():
        for s in range(XSLAB):
            slab[pl.ds(s, rows, stride=XSLAB), :] = _pack_bf16_pair(
                xs_ref[:, s * LANES:(s + 1) * LANES], xs_ref[:, half + s * LANES:half + (s + 1) * LANES])

    base = i * (2 * rows)

    def start(r, c):
        for k in range(2):
            _slab_copy(slab, r, out_ref, dest_ref[base + 2 * r + k], sem.at[slot], XSLAB).start()
        return c

    lax.fori_loop(0, rows, start, 0)

    def wait_tile(which):
        for _ in range(2):
            pltpu.make_async_copy(slab_ref.at[which], slab_ref.at[which], sem.at[which]).wait()

    @pl.when(i > 0)
    def _():
        wait_tile(1 - slot)

    @pl.when(i == last)
    def _():
        wait_tile(slot)
        slab[...] = jnp.zeros(slab.shape, jnp.uint32)
        zsem = sem.at[slot]

        def per_expert(e, c):
            lo = pad_ref[e]
            hi = pad_ref[N_EXPERTS + e]

            def zs(s, c2):
                _slab_copy(slab, 0, out_ref, s, zsem, XSLAB).start()
                return c2

            lax.fori_loop(lo, hi, zs, 0)

            def zw(s, c2):
                _slab_copy(slab, 0, out_ref, s, zsem, XSLAB).wait()
                return c2

            lax.fori_loop(lo, hi, zw, 0)
            return c

        lax.fori_loop(0, N_EXPERTS, per_expert, 0)

        blk_rows = MOE_BLOCK * XSLAB

        def block_copy(b):
            start = pl.multiple_of(b * blk_rows, blk_rows)
            return pltpu.make_async_copy(slab, out_ref.at[pl.ds(start, blk_rows), :], zsem)

        def zbs(b, c):
            block_copy(b).start()
            return c

        def zbw(b, c):
            block_copy(b).wait()
            return c

        first_unused = pad_ref[2 * N_EXPERTS - 1] // MOE_BLOCK
        lax.fori_loop(first_unused, n_blocks, zbs, 0)
        lax.fori_loop(first_unused, n_blocks, zbw, 0)


def _dispatch(x1_p, x1_s, dest, pad_bounds, n_slots):
    n_b, seq, _ = x1_p.shape
    tt = TILE_T
    rows = n_b * tt
    assert rows == MOE_BLOCK
    ntp = seq // tt
    nts = x1_s.shape[0] // rows
    kern = functools.partial(_dispatch_kernel, n_b=n_b, tt=tt, n_prompt_tiles=ntp, n_blocks=n_slots // MOE_BLOCK)
    return pl.pallas_call(
        kern,
        out_shape=jax.ShapeDtypeStruct((n_slots * XSLAB, LANES), jnp.uint32),
        grid_spec=pltpu.PrefetchScalarGridSpec(
            num_scalar_prefetch=2,
            grid=(ntp + nts,),
            in_specs=[
                pl.BlockSpec((n_b, tt, D_MODEL), lambda i, d, p: (0, jnp.minimum(i, ntp - 1), 0)),
                pl.BlockSpec((rows, D_MODEL), lambda i, d, p: (jnp.maximum(i - ntp, 0), 0)),
            ],
            out_specs=pl.BlockSpec(memory_space=pl.ANY),
            scratch_shapes=[
                pltpu.VMEM((2, rows * XSLAB, LANES), jnp.uint32),
                pltpu.SemaphoreType.DMA((2,)),
            ],
        ),
        compiler_params=_cparams(("arbitrary",)),
        name="moe_dispatch",
    )(dest, pad_bounds, x1_p, x1_s)


def _new_expert(bexp_ref, i):
    return (i == 0) | (bexp_ref[i] != bexp_ref[jnp.maximum(i - 1, 0)])


def _ffn_up_kernel(bexp_ref, nused_ref, xs_ref, wg_ref, wu_ref, h_ref, xb_ref, wg16_ref, wu16_ref):
    i = pl.program_id(0)
    rows = MOE_BLOCK

    @pl.when(_new_expert(bexp_ref, i))
    def _():
        wg16_ref[...] = wg_ref[...].astype(BF16)
        wu16_ref[...] = wu_ref[...].astype(BF16)

    @pl.when(i < nused_ref[0])
    def _():
        half = D_MODEL // 2
        for s in range(XSLAB):
            a, b = _unpack_bf16_pair(xs_ref[pl.ds(s, rows, stride=XSLAB), :])
            xb_ref[:, s * LANES:(s + 1) * LANES] = a
            xb_ref[:, half + s * LANES:half + (s + 1) * LANES] = b
        xb = xb_ref[...]
        g = jnp.dot(xb, wg16_ref[...], preferred_element_type=F32)
        u = jnp.dot(xb, wu16_ref[...], preferred_element_type=F32)
        h_ref[...] = (g * jax.nn.sigmoid(g) * u).astype(BF16)

    @pl.when(i >= nused_ref[0])
    def _():
        h_ref[...] = jnp.zeros(h_ref.shape, BF16)


def _ffn_down_kernel(bexp_ref, nused_ref, h_ref, wd_ref, ys_ref, wd16_ref):
    i = pl.program_id(0)
    rows = MOE_BLOCK

    @pl.when(_new_expert(bexp_ref, i))
    def _():
        wd16_ref[...] = wd_ref[...].astype(BF16)

    @pl.when(i < nused_ref[0])
    def _():
        y = jnp.dot(h_ref[...], wd16_ref[...], preferred_element_type=F32)
        for s in range(SLAB):
            ys_ref[pl.ds(s, rows, stride=SLAB), :] = y[:, s * LANES:(s + 1) * LANES]

    @pl.when(i >= nused_ref[0])
    def _():
        ys_ref[...] = jnp.zeros(ys_ref.shape, F32)


def _ffn(xs, w_gate, w_up, w_down, blk_exp, n_used, n_blocks, layer):
    rows = MOE_BLOCK
    e0 = layer * N_EXPERTS
    n_slots = n_blocks * rows

    def live_map(i, be, nu):
        return (jnp.minimum(i, nu[0] - 1), 0)

    def w_map(i, be, nu):
        return (e0 + be[i], 0, 0)

    h = pl.pallas_call(
        _ffn_up_kernel,
        out_shape=jax.ShapeDtypeStruct((n_slots, D_FF), BF16),
        grid_spec=pltpu.PrefetchScalarGridSpec(
            num_scalar_prefetch=2,
            grid=(n_blocks,),
            in_specs=[
                pl.BlockSpec((rows * XSLAB, LANES), live_map),
                pl.BlockSpec((None, D_MODEL, D_FF), w_map),
                pl.BlockSpec((None, D_MODEL, D_FF), w_map),
            ],
            out_specs=pl.BlockSpec((rows, D_FF), lambda i, be, nu: (i, 0)),
            scratch_shapes=[
                pltpu.VMEM((rows, D_MODEL), BF16),
                pltpu.VMEM((D_MODEL, D_FF), BF16),
                pltpu.VMEM((D_MODEL, D_FF), BF16),
            ],
        ),
        compiler_params=_cparams(("arbitrary",)),
        name="moe_ffn_up",
    )(blk_exp, n_used, xs, w_gate, w_up)
    return pl.pallas_call(
        _ffn_down_kernel,
        out_shape=jax.ShapeDtypeStruct((n_slots * SLAB, LANES), F32),
        grid_spec=pltpu.PrefetchScalarGridSpec(
            num_scalar_prefetch=2,
            grid=(n_blocks,),
            in_specs=[
                pl.BlockSpec((rows, D_FF), live_map),
                pl.BlockSpec((None, D_FF, D_MODEL), w_map),
            ],
            out_specs=pl.BlockSpec((rows * SLAB, LANES), lambda i, be, nu: (i, 0)),
            scratch_shapes=[pltpu.VMEM((D_FF, D_MODEL), BF16)],
        ),
        compiler_params=_cparams(("arbitrary",)),
        name="moe_ffn_down",
    )(blk_exp, n_used, h, w_down)


def _combine_kernel(dest_ref, x1_ref, gate_ref, lg_ref, lb_ref, ys_ref, x2_ref, y0_ref, y1_ref, sem, *, n_b, tt, alpha):
    i = pl.program_id(0)
    rows = y0_ref.shape[1] // SLAB
    slot = i % 2

    def fetch(tile, which):
        base = tile * (2 * rows)

        def start(r, c):
            _slab_copy(ys_ref, dest_ref[base + 2 * r], y0_ref.at[which], r, sem.at[which]).start()
            _slab_copy(ys_ref, dest_ref[base + 2 * r + 1], y1_ref.at[which], r, sem.at[which]).start()
            return c

        lax.fori_loop(0, rows, start, 0)

    @pl.when(i == 0)
    def _():
        fetch(0, 0)

    @pl.when(i + 1 < pl.num_programs(0))
    def _():
        fetch(i + 1, 1 - slot)

    pltpu.make_async_copy(y0_ref.at[slot], y0_ref.at[slot], sem.at[slot]).wait()
    pltpu.make_async_copy(y1_ref.at[slot], y1_ref.at[slot], sem.at[slot]).wait()

    y0 = y0_ref.at[slot]
    y1 = y1_ref.at[slot]
    gate_rows = jnp.concatenate([gate_ref[...], jnp.zeros((SUBLANES - 2, rows), F32)], axis=0)
    gate_cols = gate_rows.T
    g0 = gate_cols[:, 0:1]
    g1 = gate_cols[:, 1:2]
    if n_b is None:
        x1 = x1_ref[...]
    else:
        x1 = x1_ref[...].reshape(rows, D_MODEL)
    parts = []
    for s in range(SLAB):
        moe = g0 * y0[pl.ds(s, rows, stride=SLAB), :] + g1 * y1[pl.ds(s, rows, stride=SLAB), :]
        parts.append(alpha * x1[:, s * LANES:(s + 1) * LANES] + moe)
    r = jnp.concatenate(parts, axis=1)
    mu = jnp.mean(r, axis=-1, keepdims=True)
    rc = r - mu
    var = jnp.mean(rc * rc, axis=-1, keepdims=True)
    x2 = rc * lax.rsqrt(var + LN_EPS) * lg_ref[...] + lb_ref[...]
    if n_b is None:
        x2_ref[...] = x2
    else:
        x2_ref[...] = x2.reshape(n_b, tt, D_MODEL)


def _combine(x1, gate_t, dest, ys, ln_g, ln_b, alpha, prompt):
    if prompt:
        n_b, seq, _ = x1.shape
        tt = TILE_T
        rows = n_b * tt
        nt = seq // tt
        x_spec = pl.BlockSpec((n_b, tt, D_MODEL), lambda i, d: (0, i, 0))
        kern = functools.partial(_combine_kernel, n_b=n_b, tt=tt, alpha=alpha)
    else:
        rows = 256
        nt = x1.shape[0] // rows
        x_spec = pl.BlockSpec((rows, D_MODEL), lambda i, d: (i, 0))
        kern = functools.partial(_combine_kernel, n_b=None, tt=None, alpha=alpha)
    return pl.pallas_call(
        kern,
        out_shape=jax.ShapeDtypeStruct(x1.shape, F32),
        grid_spec=pltpu.PrefetchScalarGridSpec(
            num_scalar_prefetch=1,
            grid=(nt,),
            in_specs=[
                x_spec,
                pl.BlockSpec((2, rows), lambda i, d: (0, i)),
                pl.BlockSpec((1, D_MODEL), lambda i, d: (0, 0)),
                pl.BlockSpec((1, D_MODEL), lambda i, d: (0, 0)),
                pl.BlockSpec(memory_space=pl.ANY),
            ],
            out_specs=x_spec,
            scratch_shapes=[
                pltpu.VMEM((2, rows * SLAB, LANES), F32),
                pltpu.VMEM((2, rows * SLAB, LANES), F32),
                pltpu.SemaphoreType.DMA((2,)),
            ],
        ),
        compiler_params=_cparams(("arbitrary",)),
        name="moe_combine_prompt" if prompt else "moe_combine_sample",
    )(dest, x1, gate_t, ln_g, ln_b, ys)


def _moe_plan(eid_t, n_blocks):
    e_flat = eid_t.T.reshape(-1)
    onehot = (e_flat[:, None] == jnp.arange(N_EXPERTS, dtype=I32)[None, :]).astype(I32)
    csum = jnp.cumsum(onehot, axis=0)
    rank = jnp.sum(csum * onehot, axis=1) - 1
    counts = csum[-1]
    padded = (counts + MOE_BLOCK - 1) // MOE_BLOCK * MOE_BLOCK
    pad_end = jnp.cumsum(padded)
    pad_start = pad_end - padded
    dest = (jnp.sum(onehot * pad_start[None, :], axis=1) + rank).astype(I32)
    n_used = (pad_end[-1] // MOE_BLOCK).astype(I32)
    first_slot = jnp.arange(n_blocks, dtype=I32) * MOE_BLOCK
    blk_exp = jnp.minimum(jnp.sum((first_slot[:, None] >= pad_end[None, :]).astype(I32), axis=1), N_EXPERTS - 1)
    last_exp = jnp.take(blk_exp, jnp.maximum(n_used - 1, 0))
    blk_exp = jnp.where(jnp.arange(n_blocks) < n_used, blk_exp, last_exp).astype(I32)
    pad_bounds = jnp.concatenate([pad_start + counts, pad_end]).astype(I32)
    return dest, pad_bounds, blk_exp, n_used.reshape(1)


def kernel(x_prompt, x_sample, cache_k, cache_v, state_ssm_re, state_ssm_im, page_table, w_in, w_out, attn_norm_g, ssm_norm_g, ssm_a_re, ssm_a_im, ssm_log_dt, ssm_b_re, ssm_b_im, ssm_c_re, ssm_c_im, ssm_d, ssm_w_glu, ln1_g, ln1_b, router_w, router_b, moe_w_gate, moe_w_up, moe_w_down, ln2_g, ln2_b):
    depth = w_in.shape[0]
    n_b, seq, _ = x_prompt.shape
    n_dec, t_new, _ = x_sample.shape
    n_pool = cache_k.shape[1]
    gq = N_HEADS // N_KV_HEADS
    alpha = (2 * depth) ** 0.25
    n_states = SSM_GROUPS * SSM_STATE
    n_sample = n_dec * t_new
    n_tok = n_b * seq + n_sample
    n_blocks = -(-(2 * n_tok + N_EXPERTS * (MOE_BLOCK - 1)) // MOE_BLOCK)
    n_slots = n_blocks * MOE_BLOCK

    slopes = 2.0 ** (-8.0 * jnp.arange(1, N_HEADS + 1, dtype=F32) / N_HEADS)
    cache_k_flat = cache_k.reshape(-1, HEAD_DIM)
    cache_v_flat = cache_v.reshape(-1, HEAD_DIM)
    router_wt = router_w.T
    router_bc = router_b.reshape(N_EXPERTS, 1)
    zeros_state = jnp.zeros((n_b, n_states), F32)
    w_in16 = w_in.astype(BF16)
    w_glu16 = ssm_w_glu.astype(BF16)
    w_out16 = w_out.astype(BF16)
    w_gate = moe_w_gate.reshape(depth * N_EXPERTS, D_MODEL, D_FF)
    w_up = moe_w_up.reshape(depth * N_EXPERTS, D_MODEL, D_FF)
    w_down = moe_w_down.reshape(depth * N_EXPERTS, D_FF, D_MODEL)

    xp = x_prompt
    xs = x_sample.transpose(1, 0, 2).reshape(n_sample, D_MODEL)
    outs = {k: [] for k in ("kp", "vp", "hrp", "hip", "ks", "vs", "hrs", "his")}
    for l in range(depth):
        lw = {
            "w_glu": w_glu16, "w_out": w_out16,
            "attn_g": attn_norm_g[l].reshape(1, -1), "ssm_g": ssm_norm_g[l].reshape(1, -1),
            "ln1_g": ln1_g[l].reshape(1, -1), "ln1_b": ln1_b[l].reshape(1, -1),
            "router_wt": router_wt, "router_b": router_bc,
        }
        prm = _ssm_params(ssm_a_re[l], ssm_a_im[l], ssm_log_dt[l], ssm_b_re[l], ssm_b_im[l],
                          ssm_c_re[l], ssm_c_im[l], ssm_d[l])

        q_p, k_p, v_p, u_p = _in_proj_prompt(xp, w_in16, l)
        att_p = _attn_prompt(q_p, k_p, v_p, slopes)
        y_p, hr_p, hi_p = _ssm(u_p, zeros_state, zeros_state, prm, nb=n_b, tc=128)

        h_s = _in_proj_sample(xs, w_in16, l)
        q_s = h_s[:, :ATTN_WIDTH].reshape(t_new, n_dec, N_HEADS, HEAD_DIM)
        q_s = q_s.transpose(1, 2, 0, 3).reshape(n_dec, N_HEADS * t_new, HEAD_DIM)
        k_s = h_s[:, ATTN_WIDTH:ATTN_WIDTH + KV_WIDTH].reshape(t_new, n_dec, N_KV_HEADS, HEAD_DIM)
        v_s = h_s[:, ATTN_WIDTH + KV_WIDTH:ATTN_WIDTH + 2 * KV_WIDTH].reshape(t_new, n_dec, N_KV_HEADS, HEAD_DIM)
        att_s = _attn_sample(q_s, k_s.transpose(1, 2, 0, 3), v_s.transpose(1, 2, 0, 3), cache_k_flat, cache_v_flat,
                             page_table, slopes, l, n_pool)
        att_s = att_s.reshape(n_dec, N_HEADS, t_new, HEAD_DIM).transpose(2, 0, 1, 3).reshape(n_sample, ATTN_WIDTH)
        u_s = h_s[:, ATTN_WIDTH + 2 * KV_WIDTH:].reshape(n_sample, N_SSM_CHUNKS, LANES).transpose(1, 0, 2)
        y_s, hr_s, hi_s = _ssm(u_s, state_ssm_re[l].reshape(n_dec, n_states), state_ssm_im[l].reshape(n_dec, n_states),
                               prm, nb=n_dec, tc=t_new)

        x1_p, eid_p, gate_p = _postmix(att_p, y_p, xp, lw, l, alpha, prompt=True)
        x1_s, eid_s, gate_s = _postmix(att_s, y_s, xs, lw, l, alpha, prompt=False)

        eid_t = jnp.concatenate([eid_p, eid_s], axis=1)
        dest, pad_bounds, blk_exp, n_used = _moe_plan(eid_t, n_blocks)
        xs_slots = _dispatch(x1_p, x1_s, dest, pad_bounds, n_slots)
        ys_slots = _ffn(xs_slots, w_gate, w_up, w_down, blk_exp, n_used, n_blocks, l)
        l2g = ln2_g[l].reshape(1, -1)
        l2b = ln2_b[l].reshape(1, -1)
        n_pa = 2 * n_b * seq
        xp = _combine(x1_p, gate_p, dest[:n_pa], ys_slots, l2g, l2b, alpha, prompt=True)
        xs = _combine(x1_s, gate_s, dest[n_pa:], ys_slots, l2g, l2b, alpha, prompt=False)

        outs["kp"].append(k_p.reshape(n_b, seq, N_KV_HEADS, HEAD_DIM))
        outs["vp"].append(v_p.reshape(n_b, seq, N_KV_HEADS, HEAD_DIM))
        outs["hrp"].append(hr_p.reshape(n_b, SSM_GROUPS, SSM_STATE))
        outs["hip"].append(hi_p.reshape(n_b, SSM_GROUPS, SSM_STATE))
        outs["ks"].append(k_s.transpose(1, 0, 2, 3))
        outs["vs"].append(v_s.transpose(1, 0, 2, 3))
        outs["hrs"].append(hr_s.reshape(n_dec, SSM_GROUPS, SSM_STATE))
        outs["his"].append(hi_s.reshape(n_dec, SSM_GROUPS, SSM_STATE))

    y_sample = xs.reshape(t_new, n_dec, D_MODEL).transpose(1, 0, 2)
    return (xp, y_sample,
            jnp.stack(outs["kp"]), jnp.stack(outs["vp"]), jnp.stack(outs["hrp"]), jnp.stack(outs["hip"]),
            jnp.stack(outs["ks"]), jnp.stack(outs["vs"]), jnp.stack(outs["hrs"]), jnp.stack(outs["his"]))
```

```python
import functools
import math

import jax
import jax.numpy as jnp
from jax import lax
from jax.experimental import pallas as pl
from jax.experimental.pallas import tpu as pltpu

F32 = jnp.float32
BF16 = jnp.bfloat16
I32 = jnp.int32
HIGHEST = lax.Precision.HIGHEST

D_MODEL = 2048
ATTN_WIDTH = 1024
SSM_WIDTH = 1024
HEAD_DIM = 128
N_HEADS = 8
N_KV_HEADS = 4
KV_WIDTH = N_KV_HEADS * HEAD_DIM
PROJ_WIDTH = ATTN_WIDTH + 2 * KV_WIDTH + SSM_WIDTH
MOBA_BLOCK = 256
MOBA_TOPK = 3
PAGE_SIZE = 128
SSM_GROUP_CH = 16
SSM_GROUPS = 64
SSM_STATE = 64
N_EXPERTS = 16
N_EXPERT_GROUPS = 4
EXPERTS_PER_GROUP = 4
D_FF = 1024
LN_EPS = 1e-5
RMS_EPS = 1e-6
NEG_INF = float("-inf")
Q_SCALE = HEAD_DIM ** -0.5

LANES = 128
SUBLANES = 8
VMEM_LIMIT = 56 * 1024 * 1024

TILE_T = 32
SLAB = D_MODEL // LANES
XSLAB = D_MODEL // (2 * LANES)
SSM_CHUNK = 8 * SSM_GROUP_CH
SSM_CHUNK_STATES = 8 * SSM_STATE
N_SSM_CHUNKS = SSM_WIDTH // SSM_CHUNK
MOE_BLOCK = 256
POSTMIX_SPLIT = 2


def _cparams(sem, vmem=VMEM_LIMIT):
    return pltpu.CompilerParams(dimension_semantics=sem, vmem_limit_bytes=vmem)


def _in_proj_prompt_kernel(x_ref, w_ref, q_ref, k_ref, v_ref, u_ref, *, n_b, tt):
    rows = n_b * tt
    x = x_ref[...].reshape(rows, D_MODEL).astype(BF16)
    h = jnp.dot(x, w_ref[...], preferred_element_type=F32)
    q_ref[...] = h[:, :ATTN_WIDTH].reshape(n_b, tt, ATTN_WIDTH)
    for b in range(n_b):
        for g in range(N_KV_HEADS):
            k0 = ATTN_WIDTH + g * HEAD_DIM
            v0 = ATTN_WIDTH + KV_WIDTH + g * HEAD_DIM
            k_ref[b, pl.ds(g, tt, stride=N_KV_HEADS), :] = h[b * tt:(b + 1) * tt, k0:k0 + HEAD_DIM]
            v_ref[b, pl.ds(g, tt, stride=N_KV_HEADS), :] = h[b * tt:(b + 1) * tt, v0:v0 + HEAD_DIM]
    u0 = ATTN_WIDTH + 2 * KV_WIDTH
    for c in range(N_SSM_CHUNKS):
        for b in range(n_b):
            u_ref[c, pl.ds(b, tt, stride=n_b), :] = h[b * tt:(b + 1) * tt, u0 + c * LANES:u0 + (c + 1) * LANES]


def _in_proj_prompt(x, w16, layer):
    n_b, seq, _ = x.shape
    tt = TILE_T
    nt = seq // tt
    kern = functools.partial(_in_proj_prompt_kernel, n_b=n_b, tt=tt)
    return pl.pallas_call(
        kern,
        out_shape=(
            jax.ShapeDtypeStruct((n_b, seq, ATTN_WIDTH), F32),
            jax.ShapeDtypeStruct((n_b, seq * N_KV_HEADS, HEAD_DIM), F32),
            jax.ShapeDtypeStruct((n_b, seq * N_KV_HEADS, HEAD_DIM), F32),
            jax.ShapeDtypeStruct((N_SSM_CHUNKS, seq * n_b, LANES), F32),
        ),
        grid=(nt,),
        in_specs=[
            pl.BlockSpec((n_b, tt, D_MODEL), lambda i: (0, i, 0)),
            pl.BlockSpec((None, D_MODEL, PROJ_WIDTH), lambda i: (layer, 0, 0)),
        ],
        out_specs=(
            pl.BlockSpec((n_b, tt, ATTN_WIDTH), lambda i: (0, i, 0)),
            pl.BlockSpec((n_b, tt * N_KV_HEADS, HEAD_DIM), lambda i: (0, i, 0)),
            pl.BlockSpec((n_b, tt * N_KV_HEADS, HEAD_DIM), lambda i: (0, i, 0)),
            pl.BlockSpec((N_SSM_CHUNKS, tt * n_b, LANES), lambda i: (0, i, 0)),
        ),
        compiler_params=_cparams(("arbitrary",)),
        name="in_proj_prompt",
    )(x, w16)


def _matmul_kernel(x_ref, w_ref, o_ref):
    o_ref[...] = jnp.dot(x_ref[...].astype(BF16), w_ref[...], preferred_element_type=F32)


def _in_proj_sample(x, w16, layer):
    m, k = x.shape
    n = w16.shape[2]
    tm = 256
    return pl.pallas_call(
        _matmul_kernel,
        out_shape=jax.ShapeDtypeStruct((m, n), F32),
        grid=(m // tm,),
        in_specs=[pl.BlockSpec((tm, k), lambda i: (i, 0)), pl.BlockSpec((None, k, n), lambda i: (layer, 0, 0))],
        out_specs=pl.BlockSpec((tm, n), lambda i: (i, 0)),
        compiler_params=_cparams(("arbitrary",)),
        name="in_proj_sample",
    )(x, w16)


def _topk_mask(gate, valid, k, axis):
    nb = gate.shape[axis]
    ids = lax.broadcasted_iota(I32, gate.shape, axis)
    gm = jnp.where(valid, gate, NEG_INF)
    cnt = jnp.zeros(gate.shape, I32)
    for j in range(nb):
        gj = gm[:, j:j + 1] if axis == 1 else gm[j:j + 1, :]
        beats = (gj > gm) | ((gj == gm) & (j < ids))
        cnt = cnt + beats.astype(I32)
    return valid & (cnt < k)


def _attn_prompt_kernel(slopes_ref, q_ref, k_ref, v_ref, o_ref, kmean_ref, k16_ref, vt_ref, bias_ref, acc_ref, s_ref,
                        *, n_blk):
    g = pl.program_id(1)
    ti = pl.program_id(2)
    blk = MOBA_BLOCK
    pair = 2 * blk
    n_pair = n_blk // 2
    gq = N_HEADS // N_KV_HEADS
    width = gq * pair
    nt_dims = (((1,), (1,)), ((), ()))

    lane = lax.broadcasted_iota(I32, (1, width), 1)
    qloc = lane % pair
    slope_vec = jnp.zeros((1, width), F32)
    for hh in range(gq):
        slope_vec = jnp.where(lane // pair == hh, slopes_ref[g * gq + hh], slope_vec)

    @pl.when(ti == 0)
    def _():
        for j in range(n_blk):
            rows_j = pl.ds(j * blk * N_KV_HEADS + g, blk, stride=N_KV_HEADS)
            kb = k_ref[rows_j, :]
            kmean_ref[j:j + 1, :] = jnp.sum(kb, axis=0, keepdims=True) * (1.0 / blk)
            k16_ref[j * blk:(j + 1) * blk, :] = kb.astype(BF16)
            vt_ref[j // 2, :, (j % 2) * blk:(j % 2 + 1) * blk] = v_ref[rows_j, :].T.astype(BF16)
        pair_key = lax.broadcasted_iota(I32, (pair, width), 0)
        bias_ref[...] = -slope_vec * (qloc - pair_key).astype(F32)

    q_all = jnp.concatenate([q_ref[:, hh * HEAD_DIM:(hh + 1) * HEAD_DIM] for hh in range(gq)], axis=0) * Q_SCALE
    q16 = q_all.astype(BF16)
    blk_ids = lax.broadcasted_iota(I32, (n_blk, width), 0)
    own_blk = 2 * ti + qloc // blk
    gate_t = lax.dot_general(kmean_ref[...], q_all, nt_dims, precision=HIGHEST, preferred_element_type=F32)
    sel = _topk_mask(gate_t, blk_ids < own_blk, MOBA_TOPK, axis=0).astype(F32)

    def sel_row(j):
        return jnp.sum(jnp.where(blk_ids == j, sel, 0.0), axis=0, keepdims=True) > 0.5

    def pair_scores(t):
        off = pl.multiple_of(t * pair, pair)
        return lax.dot_general(k16_ref[pl.ds(off, pair), :], q16, nt_dims, preferred_element_type=F32)

    def softmax_pair(s, vis0, vis1, m_prev, cst):
        h0 = jnp.where(vis0, s[0:blk, :] + bias_ref[0:blk, :], NEG_INF)
        h1 = jnp.where(vis1, s[blk:pair, :] + bias_ref[blk:pair, :], NEG_INF)
        mx = jnp.maximum(jnp.max(h0, axis=0, keepdims=True), jnp.max(h1, axis=0, keepdims=True)) + cst
        m_new = mx if m_prev is None else jnp.maximum(m_prev, mx)
        off = m_new - cst
        p0 = jnp.exp(h0 - off)
        p1 = jnp.exp(h1 - off)
        psum = jnp.sum(p0, axis=0, keepdims=True) + jnp.sum(p1, axis=0, keepdims=True)
        return m_new, jnp.concatenate([p0, p1], axis=0).astype(BF16), psum

    key_id = lax.broadcasted_iota(I32, (blk, width), 0)
    vis_top = ((qloc < blk) & (key_id <= qloc)) | ((qloc >= blk) & sel_row(2 * ti))
    vis_bot = key_id + blk <= qloc
    m0, p, l0 = softmax_pair(pair_scores(ti), vis_top, vis_bot, None, jnp.zeros((1, width), F32))
    acc_ref[...] = jnp.dot(vt_ref[ti], p, preferred_element_type=F32)

    s_ref[0] = pair_scores(0)

    def body(t, carry):
        m, l = carry
        s = s_ref[t % 2]
        s_ref[(t + 1) % 2] = pair_scores(jnp.minimum(t + 1, n_pair - 1))
        cst = -slope_vec * ((ti - t) * pair).astype(F32)
        m_new, p, psum = softmax_pair(s, sel_row(2 * t), sel_row(2 * t + 1), m, cst)
        a = jnp.exp(m - m_new)
        acc_ref[...] = a * acc_ref[...] + jnp.dot(vt_ref[t], p, preferred_element_type=F32)
        return m_new, a * l + psum

    _, l = lax.fori_loop(0, ti, body, (m0, l0))
    o_t = acc_ref[...] / l
    for hh in range(gq):
        o_ref[:, hh * HEAD_DIM:(hh + 1) * HEAD_DIM] = o_t[:, hh * pair:(hh + 1) * pair].T


def _attn_prompt(q, k, v, slopes):
    n_b, seq, _ = q.shape
    n_blk = seq // MOBA_BLOCK
    assert n_blk % 2 == 0
    gq = N_HEADS // N_KV_HEADS
    pair = 2 * MOBA_BLOCK
    kern = functools.partial(_attn_prompt_kernel, n_blk=n_blk)
    return pl.pallas_call(
        kern,
        out_shape=jax.ShapeDtypeStruct((n_b, seq, ATTN_WIDTH), F32),
        grid_spec=pltpu.PrefetchScalarGridSpec(
            num_scalar_prefetch=1,
            grid=(n_b, N_KV_HEADS, n_blk // 2),
            in_specs=[
                pl.BlockSpec((None, pair, gq * HEAD_DIM), lambda b, g, i, s: (b, i, g)),
                pl.BlockSpec((None, seq * N_KV_HEADS, HEAD_DIM), lambda b, g, i, s: (b, 0, 0)),
                pl.BlockSpec((None, seq * N_KV_HEADS, HEAD_DIM), lambda b, g, i, s: (b, 0, 0)),
            ],
            out_specs=pl.BlockSpec((None, pair, gq * HEAD_DIM), lambda b, g, i, s: (b, i, g)),
            scratch_shapes=[
                pltpu.VMEM((n_blk, HEAD_DIM), F32),
                pltpu.VMEM((seq, HEAD_DIM), BF16),
                pltpu.VMEM((n_blk // 2, HEAD_DIM, pair), BF16),
                pltpu.VMEM((pair, gq * pair), F32),
                pltpu.VMEM((HEAD_DIM, gq * pair), F32),
                pltpu.VMEM((2, pair, gq * pair), F32),
            ],
        ),
        compiler_params=_cparams(("arbitrary", "arbitrary", "arbitrary")),
        name="moba_prompt",
    )(slopes, q, k, v)


def _attn_sample_kernel(pt_ref, slopes_ref, q_ref, kn_ref, vn_ref, *rest, n_pages, t_new, past):
    k_refs = rest[:n_pages]
    v_refs = rest[n_pages:2 * n_pages]
    o_ref = rest[2 * n_pages]
    gq = N_HEADS // N_KV_HEADS
    rows = N_HEADS * t_new
    ppb = MOBA_BLOCK // PAGE_SIZE
    n_blk = n_pages // ppb
    page_rows = PAGE_SIZE * N_KV_HEADS
    cols = ppb * page_rows
    nt_dims = (((1,), (1,)), ((), ()))

    row = lax.broadcasted_iota(I32, (rows, 1), 0)
    head = row // t_new
    g_row = head // gq
    t_row = row % t_new
    slope_row = jnp.zeros((rows, 1), F32)
    for h in range(N_HEADS):
        slope_row = jnp.where(head == h, slopes_ref[h], slope_row)
    col = lax.broadcasted_iota(I32, (1, cols), 1)
    g_col = col % N_KV_HEADS
    kpos_col = (col // page_rows) * PAGE_SIZE + (col % page_rows) // N_KV_HEADS
    bias0 = jnp.where(g_row == g_col, -slope_row * (past + t_row - kpos_col).astype(F32), NEG_INF)

    def per_head_rows(fn):
        out = jnp.zeros((rows, HEAD_DIM), F32)
        for g in range(N_KV_HEADS):
            out = jnp.where(g_row == g, fn(g), out)
        return out

    q = q_ref[...] * Q_SCALE
    q16 = q.astype(BF16)
    blk_lane = lax.broadcasted_iota(I32, (rows, n_blk), 1)

    gate = jnp.zeros((rows, n_blk), F32)
    k16, v16 = [], []
    for j in range(n_blk):
        kblk = jnp.concatenate([k_refs[ppb * j + p][...] for p in range(ppb)], axis=0)
        fold = jnp.sum(kblk.reshape(cols // SUBLANES, SUBLANES, HEAD_DIM), axis=0)
        ksum = fold[0:N_KV_HEADS, :] + fold[N_KV_HEADS:, :]
        kmean_rows = per_head_rows(lambda g: ksum[g:g + 1, :]) * (1.0 / MOBA_BLOCK)
        gate = jnp.where(blk_lane == j, jnp.sum(q * kmean_rows, axis=1, keepdims=True), gate)
        k16.append(kblk.astype(BF16))
        v16.extend(v_refs[ppb * j + p][...].astype(BF16) for p in range(ppb))
    sel = _topk_mask(gate, blk_lane >= 0, MOBA_TOPK, axis=1).astype(F32)

    bias = jnp.concatenate(
        [jnp.where(sel[:, j:j + 1] > 0.5, bias0 + slope_row * float(j * MOBA_BLOCK), NEG_INF) for j in range(n_blk)],
        axis=1)
    s = lax.dot_general(q16, jnp.concatenate(k16, axis=0), nt_dims, preferred_element_type=F32) + bias

    s_own, v_own = [], []
    for tk in range(t_new):
        kn_rows = per_head_rows(lambda g: kn_ref[g, tk:tk + 1, :])
        sv = jnp.sum(q * kn_rows, axis=1, keepdims=True) - slope_row * (t_row - tk).astype(F32)
        s_own.append(jnp.where(t_row >= tk, sv, NEG_INF))
        v_own.append(per_head_rows(lambda g: vn_ref[g, tk:tk + 1, :]))
    m = jnp.max(s, axis=1, keepdims=True)
    for tk in range(t_new):
        m = jnp.maximum(m, s_own[tk])
    p = jnp.exp(s - m)
    num = jnp.dot(p.astype(BF16), jnp.concatenate(v16, axis=0), preferred_element_type=F32)
    den = jnp.sum(p, axis=1, keepdims=True)
    for tk in range(t_new):
        pw = jnp.exp(s_own[tk] - m)
        num = num + pw * v_own[tk]
        den = den + pw
    o_ref[...] = num / den


def _attn_sample(q, k_new, v_new, cache_k_flat, cache_v_flat, page_table, slopes, layer, n_pool):
    n_dec, rows, _ = q.shape
    t_new = k_new.shape[2]
    n_pages = page_table.shape[1]
    past = n_pages * PAGE_SIZE
    assert past % MOBA_BLOCK == 0
    page_rows = PAGE_SIZE * N_KV_HEADS
    base = layer * n_pool
    pt_flat = page_table.reshape(-1)
    kern = functools.partial(_attn_sample_kernel, n_pages=n_pages, t_new=t_new, past=past)

    def page_spec(p):
        return pl.BlockSpec((page_rows, HEAD_DIM), lambda b, pt, sl: (base + pt[b * n_pages + p], 0))

    q_spec = pl.BlockSpec((None, rows, HEAD_DIM), lambda b, pt, sl: (b, 0, 0))
    new_spec = pl.BlockSpec((None, N_KV_HEADS, t_new, HEAD_DIM), lambda b, pt, sl: (b, 0, 0, 0))
    pages = [page_spec(p) for p in range(n_pages)]
    return pl.pallas_call(
        kern,
        out_shape=jax.ShapeDtypeStruct((n_dec, rows, HEAD_DIM), F32),
        grid_spec=pltpu.PrefetchScalarGridSpec(
            num_scalar_prefetch=2,
            grid=(n_dec,),
            in_specs=[q_spec, new_spec, new_spec] + pages + pages,
            out_specs=q_spec,
        ),
        compiler_params=_cparams(("arbitrary",)),
        name="moba_sample",
    )(pt_flat, slopes, q, k_new, v_new, *([cache_k_flat] * n_pages), *([cache_v_flat] * n_pages))


def _ssm_kernel(u_ref, bre_ref, bim_ref, cre_ref, cim_ref, are_ref, aim_ref, d_ref, s0r_ref, s0i_ref,
                y_ref, hr_ref, hi_ref, xr_ref, xi_ref, str_ref, sti_ref, *, nb, tc):
    ti = pl.program_id(1)

    @pl.when(ti == 0)
    def _():
        str_ref[...] = s0r_ref[...]
        sti_ref[...] = s0i_ref[...]

    u = u_ref[...]
    u16 = u.astype(BF16)
    xr_ref[...] = jnp.dot(u16, bre_ref[...], preferred_element_type=F32)
    xi_ref[...] = jnp.dot(u16, bim_ref[...], preferred_element_type=F32)
    a_re = jnp.broadcast_to(are_ref[...], (nb, SSM_CHUNK_STATES))
    a_im = jnp.broadcast_to(aim_ref[...], (nb, SSM_CHUNK_STATES))

    def step(t, carry):
        h_re, h_im = carry
        sl = pl.ds(pl.multiple_of(t * nb, nb), nb)
        n_re = (a_re * h_re - a_im * h_im) + xr_ref[sl, :]
        n_im = (a_re * h_im + a_im * h_re) + xi_ref[sl, :]
        xr_ref[sl, :] = n_re
        xi_ref[sl, :] = n_im
        return n_re, n_im

    h_re, h_im = lax.fori_loop(0, tc, step, (str_ref[...], sti_ref[...]))
    str_ref[...] = h_re
    sti_ref[...] = h_im
    y = (jnp.dot(xr_ref[...].astype(BF16), cre_ref[...], preferred_element_type=F32)
         - jnp.dot(xi_ref[...].astype(BF16), cim_ref[...], preferred_element_type=F32)
         + d_ref[...] * u)
    y_ref[...] = jax.nn.gelu(y)

    @pl.when(ti == pl.num_programs(1) - 1)
    def _():
        hr_ref[...] = h_re
        hi_ref[...] = h_im


def _ssm(u_chunks, s0_re, s0_im, prm, nb, tc):
    n_rows = u_chunks.shape[1]
    nt = n_rows // (tc * nb)
    rows = tc * nb
    kern = functools.partial(_ssm_kernel, nb=nb, tc=tc)
    cs = SSM_CHUNK_STATES
    n_states = SSM_GROUPS * SSM_STATE
    return pl.pallas_call(
        kern,
        out_shape=(
            jax.ShapeDtypeStruct((N_SSM_CHUNKS, n_rows, LANES), F32),
            jax.ShapeDtypeStruct((nb, n_states), F32),
            jax.ShapeDtypeStruct((nb, n_states), F32),
        ),
        grid=(N_SSM_CHUNKS, nt),
        in_specs=[
            pl.BlockSpec((None, rows, LANES), lambda c, t: (c, t, 0)),
            pl.BlockSpec((None, SSM_CHUNK, cs), lambda c, t: (c, 0, 0)),
            pl.BlockSpec((None, SSM_CHUNK, cs), lambda c, t: (c, 0, 0)),
            pl.BlockSpec((None, cs, SSM_CHUNK), lambda c, t: (c, 0, 0)),
            pl.BlockSpec((None, cs, SSM_CHUNK), lambda c, t: (c, 0, 0)),
            pl.BlockSpec((1, cs), lambda c, t: (0, c)),
            pl.BlockSpec((1, cs), lambda c, t: (0, c)),
            pl.BlockSpec((1, SSM_CHUNK), lambda c, t: (0, c)),
            pl.BlockSpec((nb, cs), lambda c, t: (0, c)),
            pl.BlockSpec((nb, cs), lambda c, t: (0, c)),
        ],
        out_specs=(
            pl.BlockSpec((None, rows, LANES), lambda c, t: (c, t, 0)),
            pl.BlockSpec((nb, cs), lambda c, t: (0, c)),
            pl.BlockSpec((nb, cs), lambda c, t: (0, c)),
        ),
        scratch_shapes=[
            pltpu.VMEM((rows, cs), F32),
            pltpu.VMEM((rows, cs), F32),
            pltpu.VMEM((nb, cs), F32),
            pltpu.VMEM((nb, cs), F32),
        ],
        compiler_params=_cparams(("arbitrary", "arbitrary")),
        name="s5_mixer",
    )(u_chunks, prm["bbd_re"], prm["bbd_im"], prm["cbd_re"], prm["cbd_im"], prm["ab_re"], prm["ab_im"],
      prm["d"], s0_re, s0_im)


def _ssm_params(a_re, a_im, log_dt, b_re, b_im, c_re, c_im, d_skip):
    dt = jnp.exp(log_dt)
    mag = jnp.exp(a_re * dt)
    ab_re = mag * jnp.cos(a_im * dt)
    ab_im = mag * jnp.sin(a_im * dt)
    den = a_re * a_re + a_im * a_im
    f_re = ((ab_re - 1.0) * a_re + ab_im * a_im) / den
    f_im = (ab_im * a_re - (ab_re - 1.0) * a_im) / den
    bb_re = f_re[..., None] * b_re - f_im[..., None] * b_im
    bb_im = f_re[..., None] * b_im + f_im[..., None] * b_re
    gpc = SSM_CHUNK // SSM_GROUP_CH
    eye = jnp.eye(gpc, dtype=F32)

    def b_blockdiag(bb):
        x = bb.reshape(N_SSM_CHUNKS, gpc, SSM_STATE, SSM_GROUP_CH)
        m = jnp.einsum("kgpc,gh->kgchp", x, eye)
        return m.reshape(N_SSM_CHUNKS, gpc * SSM_GROUP_CH, gpc * SSM_STATE).astype(BF16)

    def c_blockdiag(cc):
        x = cc.reshape(N_SSM_CHUNKS, gpc, SSM_GROUP_CH, SSM_STATE)
        m = jnp.einsum("kgcp,gh->kgphc", x, eye)
        return m.reshape(N_SSM_CHUNKS, gpc * SSM_STATE, gpc * SSM_GROUP_CH).astype(BF16)

    return {
        "bbd_re": b_blockdiag(bb_re), "bbd_im": b_blockdiag(bb_im),
        "cbd_re": c_blockdiag(c_re), "cbd_im": c_blockdiag(c_im),
        "ab_re": ab_re.reshape(1, -1), "ab_im": ab_im.reshape(1, -1),
        "d": d_skip.reshape(1, -1),
    }


def _rank_lt(vals, a):
    cnt = jnp.zeros(vals[a].shape, I32)
    for b in range(len(vals)):
        if b == a:
            continue
        before = (vals[b] >= vals[a]) if b < a else (vals[b] > vals[a])
        cnt = cnt + before.astype(I32)
    return cnt


def _router(x1, rwt_ref, rb_ref, eid_ref, gate_ref, cols):
    logits = lax.dot_general(rwt_ref[...], x1, (((1,), (1,)), ((), ())), precision=HIGHEST,
                             preferred_element_type=F32)
    mx = jnp.max(logits, axis=0, keepdims=True)
    ex = jnp.exp(logits - mx)
    probs = ex / jnp.sum(ex, axis=0, keepdims=True)
    biased = probs + rb_ref[...]
    prow = [probs[e:e + 1, :] for e in range(N_EXPERTS)]
    brow = [biased[e:e + 1, :] for e in range(N_EXPERTS)]
    ranks = []
    gscore = []
    for gi in range(N_EXPERT_GROUPS):
        vals = brow[gi * EXPERTS_PER_GROUP:(gi + 1) * EXPERTS_PER_GROUP]
        rk = [_rank_lt(vals, a) for a in range(EXPERTS_PER_GROUP)]
        ranks.append(rk)
        sc = jnp.zeros(vals[0].shape, F32)
        for a in range(EXPERTS_PER_GROUP):
            sc = sc + jnp.where(rk[a] < 2, vals[a], 0.0)
        gscore.append(sc)
    e0 = jnp.zeros(prow[0].shape, I32)
    e1 = jnp.zeros(prow[0].shape, I32)
    p0 = jnp.zeros(prow[0].shape, F32)
    p1 = jnp.zeros(prow[0].shape, F32)
    for gi in range(N_EXPERT_GROUPS):
        chosen = _rank_lt(gscore, gi) == 0
        for a in range(EXPERTS_PER_GROUP):
            e = gi * EXPERTS_PER_GROUP + a
            first = chosen & (ranks[gi][a] == 0)
            second = chosen & (ranks[gi][a] == 1)
            e0 = jnp.where(first, e, e0)
            e1 = jnp.where(second, e, e1)
            p0 = jnp.where(first, prow[e], p0)
            p1 = jnp.where(second, prow[e], p1)
    tot = p0 + p1
    eid_ref[0:1, cols] = e0
    eid_ref[1:2, cols] = e1
    gate_ref[0:1, cols] = p0 / tot
    gate_ref[1:2, cols] = p1 / tot


def _postmix_kernel(att_ref, y_ref, x_ref, wglu_ref, wout_ref, ag_ref, sg_ref, lg_ref, lb_ref, rwt_ref, rb_ref,
                    x1_ref, eid_ref, gate_ref, ybuf_ref, *, n_b, tt, alpha):
    rows = x1_ref.shape[0] if n_b is None else n_b * tt
    if n_b is None:
        att = att_ref[...]
        x = x_ref[...]
        for c in range(N_SSM_CHUNKS):
            ybuf_ref[:, c * LANES:(c + 1) * LANES] = y_ref[c]
    else:
        for c in range(N_SSM_CHUNKS):
            for b in range(n_b):
                ybuf_ref[b * tt:(b + 1) * tt, c * LANES:(c + 1) * LANES] = y_ref[c, pl.ds(b, tt, stride=n_b), :]
    hrows = rows // POSTMIX_SPLIT
    for hf in range(POSTMIX_SPLIT):
        r0 = hf * hrows
        if n_b is None:
            att_h = att[r0:r0 + hrows, :]
            x_h = x[r0:r0 + hrows, :]
        else:
            att_h = att_ref[hf * (n_b // POSTMIX_SPLIT):(hf + 1) * (n_b // POSTMIX_SPLIT)].reshape(hrows, ATTN_WIDTH)
            x_h = x_ref[hf * (n_b // POSTMIX_SPLIT):(hf + 1) * (n_b // POSTMIX_SPLIT)].reshape(hrows, D_MODEL)
        y = ybuf_ref[r0:r0 + hrows, :]
        z = jnp.dot(y.astype(BF16), wglu_ref[...], preferred_element_type=F32)
        sg = y * jax.nn.sigmoid(z)
        ssm_n = sg * lax.rsqrt(jnp.mean(sg * sg, axis=-1, keepdims=True) + RMS_EPS) * sg_ref[...]
        att_n = att_h * lax.rsqrt(jnp.mean(att_h * att_h, axis=-1, keepdims=True) + RMS_EPS) * ag_ref[...]
        mix = (jnp.dot(att_n.astype(BF16), wout_ref[0:ATTN_WIDTH, :], preferred_element_type=F32)
               + jnp.dot(ssm_n.astype(BF16), wout_ref[ATTN_WIDTH:, :], preferred_element_type=F32))
        r = alpha * x_h + mix
        mu = jnp.mean(r, axis=-1, keepdims=True)
        rc = r - mu
        var = jnp.mean(rc * rc, axis=-1, keepdims=True)
        x1 = rc * lax.rsqrt(var + LN_EPS) * lg_ref[...] + lb_ref[...]
        if n_b is None:
            x1_ref[r0:r0 + hrows, :] = x1
        else:
            nbh = n_b // POSTMIX_SPLIT
            x1_ref[hf * nbh:(hf + 1) * nbh] = x1.reshape(nbh, tt, D_MODEL)
        _router(x1, rwt_ref, rb_ref, eid_ref, gate_ref, slice(r0, r0 + hrows))


def _postmix(att, y_chunks, x, lw, layer, alpha, prompt):
    if prompt:
        n_b, seq, _ = x.shape
        tt = TILE_T
        nt = seq // tt
        rows = n_b * tt
        n_tok = n_b * seq
        att_spec = pl.BlockSpec((n_b, tt, ATTN_WIDTH), lambda i: (0, i, 0))
        x_spec = pl.BlockSpec((n_b, tt, D_MODEL), lambda i: (0, i, 0))
        x1_shape = jax.ShapeDtypeStruct((n_b, seq, D_MODEL), F32)
        kern = functools.partial(_postmix_kernel, n_b=n_b, tt=tt, alpha=alpha)
    else:
        n_tok = x.shape[0]
        rows = 256
        nt = n_tok // rows
        att_spec = pl.BlockSpec((rows, ATTN_WIDTH), lambda i: (i, 0))
        x_spec = pl.BlockSpec((rows, D_MODEL), lambda i: (i, 0))
        x1_shape = jax.ShapeDtypeStruct((n_tok, D_MODEL), F32)
        kern = functools.partial(_postmix_kernel, n_b=None, tt=None, alpha=alpha)
    full = lambda shp: pl.BlockSpec(shp, lambda i: tuple(0 for _ in shp))
    return pl.pallas_call(
        kern,
        out_shape=(x1_shape, jax.ShapeDtypeStruct((2, n_tok), I32), jax.ShapeDtypeStruct((2, n_tok), F32)),
        grid=(nt,),
        in_specs=[
            att_spec,
            pl.BlockSpec((N_SSM_CHUNKS, rows, LANES), lambda i: (0, i, 0)),
            x_spec,
            pl.BlockSpec((None, SSM_WIDTH, SSM_WIDTH), lambda i: (layer, 0, 0)),
            pl.BlockSpec((None, D_MODEL, D_MODEL), lambda i: (layer, 0, 0)),
            full((1, ATTN_WIDTH)),
            full((1, SSM_WIDTH)),
            full((1, D_MODEL)),
            full((1, D_MODEL)),
            full((N_EXPERTS, D_MODEL)),
            full((N_EXPERTS, 1)),
        ],
        out_specs=(x_spec, pl.BlockSpec((2, rows), lambda i: (0, i)), pl.BlockSpec((2, rows), lambda i: (0, i))),
        scratch_shapes=[pltpu.VMEM((rows, SSM_WIDTH), F32)],
        compiler_params=_cparams(("arbitrary",)),
        name="postmix_prompt" if prompt else "postmix_sample",
    )(att, y_chunks, x, lw["w_glu"], lw["w_out"], lw["attn_g"], lw["ssm_g"], lw["ln1_g"], lw["ln1_b"],
      lw["router_wt"], lw["router_b"])


def _slab_rows(row, slab):
    start = row * slab
    return pl.ds(start if isinstance(start, int) else pl.multiple_of(start, slab), slab)


def _slab_copy(src_ref, src_row, dst_ref, dst_row, sem, slab=SLAB):
    return pltpu.make_async_copy(src_ref.at[_slab_rows(src_row, slab), :], dst_ref.at[_slab_rows(dst_row, slab), :],
                                 sem)


def _pack_bf16_pair(a, b):
    hi = lax.bitcast_convert_type(a.astype(BF16).astype(F32), jnp.uint32)
    lo = lax.bitcast_convert_type(b.astype(BF16).astype(F32), jnp.uint32)
    return hi | (lo >> 16)


def _unpack_bf16_pair(w):
    a = lax.bitcast_convert_type(w & jnp.uint32(0xFFFF0000), F32)
    b = lax.bitcast_convert_type(w << 16, F32)
    return a.astype(BF16), b.astype(BF16)


def _dispatch_kernel(dest_ref, pad_ref, xp_ref, xs_ref, out_ref, slab_ref, sem, *, n_b, tt, n_prompt_tiles, n_blocks):
    i = pl.program_id(0)
    last = pl.num_programs(0) - 1
    rows = n_b * tt
    slot = i % 2
    slab = slab_ref.at[slot]

    half = D_MODEL // 2

    @pl.when(i < n_prompt_tiles)
    def _():
        for s in range(XSLAB):
            for b in range(n_b):
                slab[pl.ds(b * tt * XSLAB + s, tt, stride=XSLAB), :] = _pack_bf16_pair(
                    xp_ref[b, :, s * LANES:(s + 1) * LANES], xp_ref[b, :, half + s * LANES:half + (s + 1) * LANES])

    @pl.when(i >= n_prompt_tiles)
    def _():
        for s in range(XSLAB):
            slab[pl.ds(s, rows, stride=XSLAB), :] = _pack_bf16_pair(
                xs_ref[:, s * LANES:(s + 1) * LANES], xs_ref[:, half + s * LANES:half + (s + 1) * LANES])

    base = i * (2 * rows)

    def start(r, c):
        for k in range(2):
            _slab_copy(slab, r, out_ref, dest_ref[base + 2 * r + k], sem.at[slot], XSLAB).start()
        return c

    lax.fori_loop(0, rows, start, 0)

    def wait_tile(which):
        for _ in range(2):
            pltpu.make_async_copy(slab_ref.at[which], slab_ref.at[which], sem.at[which]).wait()

    @pl.when(i > 0)
    def _():
        wait_tile(1 - slot)

    @pl.when(i == last)
    def _():
        wait_tile(slot)
        slab[...] = jnp.zeros(slab.shape, jnp.uint32)
        zsem = sem.at[slot]

        def per_expert(e, c):
            lo = pad_ref[e]
            hi = pad_ref[N_EXPERTS + e]

            def zs(s, c2):
                _slab_copy(slab, 0, out_ref, s, zsem, XSLAB).start()
                return c2

            lax.fori_loop(lo, hi, zs, 0)

            def zw(s, c2):
                _slab_copy(slab, 0, out_ref, s, zsem, XSLAB).wait()
                return c2

            lax.fori_loop(lo, hi, zw, 0)
            return c

        lax.fori_loop(0, N_EXPERTS, per_expert, 0)

        blk_rows = MOE_BLOCK * XSLAB

        def block_copy(b):
            start = pl.multiple_of(b * blk_rows, blk_rows)
            return pltpu.make_async_copy(slab, out_ref.at[pl.ds(start, blk_rows), :], zsem)

        def zbs(b, c):
            block_copy(b).start()
            return c

        def zbw(b, c):
            block_copy(b).wait()
            return c

        first_unused = pad_ref[2 * N_EXPERTS - 1] // MOE_BLOCK
        lax.fori_loop(first_unused, n_blocks, zbs, 0)
        lax.fori_loop(first_unused, n_blocks, zbw, 0)


def _dispatch(x1_p, x1_s, dest, pad_bounds, n_slots):
    n_b, seq, _ = x1_p.shape
    tt = TILE_T
    rows = n_b * tt
    assert rows == MOE_BLOCK
    ntp = seq // tt
    nts = x1_s.shape[0] // rows
    kern = functools.partial(_dispatch_kernel, n_b=n_b, tt=tt, n_prompt_tiles=ntp, n_blocks=n_slots // MOE_BLOCK)
    return pl.pallas_call(
        kern,
        out_shape=jax.ShapeDtypeStruct((n_slots * XSLAB, LANES), jnp.uint32),
        grid_spec=pltpu.PrefetchScalarGridSpec(
            num_scalar_prefetch=2,
            grid=(ntp + nts,),
            in_specs=[
                pl.BlockSpec((n_b, tt, D_MODEL), lambda i, d, p: (0, jnp.minimum(i, ntp - 1), 0)),
                pl.BlockSpec((rows, D_MODEL), lambda i, d, p: (jnp.maximum(i - ntp, 0), 0)),
            ],
            out_specs=pl.BlockSpec(memory_space=pl.ANY),
            scratch_shapes=[
                pltpu.VMEM((2, rows * XSLAB, LANES), jnp.uint32),
                pltpu.SemaphoreType.DMA((2,)),
            ],
        ),
        compiler_params=_cparams(("arbitrary",)),
        name="moe_dispatch",
    )(dest, pad_bounds, x1_p, x1_s)


def _new_expert(bexp_ref, i):
    return (i == 0) | (bexp_ref[i] != bexp_ref[jnp.maximum(i - 1, 0)])


def _ffn_up_kernel(bexp_ref, nused_ref, xs_ref, wg_ref, wu_ref, h_ref, xb_ref, wg16_ref, wu16_ref):
    i = pl.program_id(0)
    rows = MOE_BLOCK

    @pl.when(_new_expert(bexp_ref, i))
    def _():
        wg16_ref[...] = wg_ref[...].astype(BF16)
        wu16_ref[...] = wu_ref[...].astype(BF16)

    @pl.when(i < nused_ref[0])
    def _():
        half = D_MODEL // 2
        for s in range(XSLAB):
            a, b = _unpack_bf16_pair(xs_ref[pl.ds(s, rows, stride=XSLAB), :])
            xb_ref[:, s * LANES:(s + 1) * LANES] = a
            xb_ref[:, half + s * LANES:half + (s + 1) * LANES] = b
        xb = xb_ref[...]
        g = jnp.dot(xb, wg16_ref[...], preferred_element_type=F32)
        u = jnp.dot(xb, wu16_ref[...], preferred_element_type=F32)
        h_ref[...] = (g * jax.nn.sigmoid(g) * u).astype(BF16)

    @pl.when(i >= nused_ref[0])
    def _():
        h_ref[...] = jnp.zeros(h_ref.shape, BF16)


def _ffn_down_kernel(bexp_ref, nused_ref, h_ref, wd_ref, ys_ref, wd16_ref):
    i = pl.program_id(0)
    rows = MOE_BLOCK

    @pl.when(_new_expert(bexp_ref, i))
    def _():
        wd16_ref[...] = wd_ref[...].astype(BF16)

    @pl.when(i < nused_ref[0])
    def _():
        y = jnp.dot(h_ref[...], wd16_ref[...], preferred_element_type=F32)
        for s in range(SLAB):
            ys_ref[pl.ds(s, rows, stride=SLAB), :] = y[:, s * LANES:(s + 1) * LANES]

    @pl.when(i >= nused_ref[0])
    def _():
        ys_ref[...] = jnp.zeros(ys_ref.shape, F32)


def _ffn(xs, w_gate, w_up, w_down, blk_exp, n_used, n_blocks, layer):
    rows = MOE_BLOCK
    e0 = layer * N_EXPERTS
    n_slots = n_blocks * rows

    def live_map(i, be, nu):
        return (jnp.minimum(i, nu[0] - 1), 0)

    def w_map(i, be, nu):
        return (e0 + be[i], 0, 0)

    h = pl.pallas_call(
        _ffn_up_kernel,
        out_shape=jax.ShapeDtypeStruct((n_slots, D_FF), BF16),
        grid_spec=pltpu.PrefetchScalarGridSpec(
            num_scalar_prefetch=2,
            grid=(n_blocks,),
            in_specs=[
                pl.BlockSpec((rows * XSLAB, LANES), live_map),
                pl.BlockSpec((None, D_MODEL, D_FF), w_map),
                pl.BlockSpec((None, D_MODEL, D_FF), w_map),
            ],
            out_specs=pl.BlockSpec((rows, D_FF), lambda i, be, nu: (i, 0)),
            scratch_shapes=[
                pltpu.VMEM((rows, D_MODEL), BF16),
                pltpu.VMEM((D_MODEL, D_FF), BF16),
                pltpu.VMEM((D_MODEL, D_FF), BF16),
            ],
        ),
        compiler_params=_cparams(("arbitrary",)),
        name="moe_ffn_up",
    )(blk_exp, n_used, xs, w_gate, w_up)
    return pl.pallas_call(
        _ffn_down_kernel,
        out_shape=jax.ShapeDtypeStruct((n_slots * SLAB, LANES), F32),
        grid_spec=pltpu.PrefetchScalarGridSpec(
            num_scalar_prefetch=2,
            grid=(n_blocks,),
            in_specs=[
                pl.BlockSpec((rows, D_FF), live_map),
                pl.BlockSpec((None, D_FF, D_MODEL), w_map),
            ],
            out_specs=pl.BlockSpec((rows * SLAB, LANES), lambda i, be, nu: (i, 0)),
            scratch_shapes=[pltpu.VMEM((D_FF, D_MODEL), BF16)],
        ),
        compiler_params=_cparams(("arbitrary",)),
        name="moe_ffn_down",
    )(blk_exp, n_used, h, w_down)


def _combine_kernel(dest_ref, x1_ref, gate_ref, lg_ref, lb_ref, ys_ref, x2_ref, y0_ref, y1_ref, sem, *, n_b, tt, alpha):
    i = pl.program_id(0)
    rows = y0_ref.shape[1] // SLAB
    slot = i % 2

    def fetch(tile, which):
        base = tile * (2 * rows)

        def start(r, c):
            _slab_copy(ys_ref, dest_ref[base + 2 * r], y0_ref.at[which], r, sem.at[which]).start()
            _slab_copy(ys_ref, dest_ref[base + 2 * r + 1], y1_ref.at[which], r, sem.at[which]).start()
            return c

        lax.fori_loop(0, rows, start, 0)

    @pl.when(i == 0)
    def _():
        fetch(0, 0)

    @pl.when(i + 1 < pl.num_programs(0))
    def _():
        fetch(i + 1, 1 - slot)

    pltpu.make_async_copy(y0_ref.at[slot], y0_ref.at[slot], sem.at[slot]).wait()
    pltpu.make_async_copy(y1_ref.at[slot], y1_ref.at[slot], sem.at[slot]).wait()

    y0 = y0_ref.at[slot]
    y1 = y1_ref.at[slot]
    gate_rows = jnp.concatenate([gate_ref[...], jnp.zeros((SUBLANES - 2, rows), F32)], axis=0)
    gate_cols = gate_rows.T
    g0 = gate_cols[:, 0:1]
    g1 = gate_cols[:, 1:2]
    if n_b is None:
        x1 = x1_ref[...]
    else:
        x1 = x1_ref[...].reshape(rows, D_MODEL)
    parts = []
    for s in range(SLAB):
        moe = g0 * y0[pl.ds(s, rows, stride=SLAB), :] + g1 * y1[pl.ds(s, rows, stride=SLAB), :]
        parts.append(alpha * x1[:, s * LANES:(s + 1) * LANES] + moe)
    r = jnp.concatenate(parts, axis=1)
    mu = jnp.mean(r, axis=-1, keepdims=True)
    rc = r - mu
    var = jnp.mean(rc * rc, axis=-1, keepdims=True)
    x2 = rc * lax.rsqrt(var + LN_EPS) * lg_ref[...] + lb_ref[...]
    if n_b is None:
        x2_ref[...] = x2
    else:
        x2_ref[...] = x2.reshape(n_b, tt, D_MODEL)


def _combine(x1, gate_t, dest, ys, ln_g, ln_b, alpha, prompt):
    if prompt:
        n_b, seq, _ = x1.shape
        tt = TILE_T
        rows = n_b * tt
        nt = seq // tt
        x_spec = pl.BlockSpec((n_b, tt, D_MODEL), lambda i, d: (0, i, 0))
        kern = functools.partial(_combine_kernel, n_b=n_b, tt=tt, alpha=alpha)
    else:
        rows = 256
        nt = x1.shape[0] // rows
        x_spec = pl.BlockSpec((rows, D_MODEL), lambda i, d: (i, 0))
        kern = functools.partial(_combine_kernel, n_b=None, tt=None, alpha=alpha)
    return pl.pallas_call(
        kern,
        out_shape=jax.ShapeDtypeStruct(x1.shape, F32),
        grid_spec=pltpu.PrefetchScalarGridSpec(
            num_scalar_prefetch=1,
            grid=(nt,),
            in_specs=[
                x_spec,
                pl.BlockSpec((2, rows), lambda i, d: (0, i)),
                pl.BlockSpec((1, D_MODEL), lambda i, d: (0, 0)),
                pl.BlockSpec((1, D_MODEL), lambda i, d: (0, 0)),
                pl.BlockSpec(memory_space=pl.ANY),
            ],
            out_specs=x_spec,
            scratch_shapes=[
                pltpu.VMEM((2, rows * SLAB, LANES), F32),
                pltpu.VMEM((2, rows * SLAB, LANES), F32),
                pltpu.SemaphoreType.DMA((2,)),
            ],
        ),
        compiler_params=_cparams(("arbitrary",)),
        name="moe_combine_prompt" if prompt else "moe_combine_sample",
    )(dest, x1, gate_t, ln_g, ln_b, ys)


def _moe_plan(eid_t, n_blocks):
    e_flat = eid_t.T.reshape(-1)
    onehot = (e_flat[:, None] == jnp.arange(N_EXPERTS, dtype=I32)[None, :]).astype(I32)
    csum = jnp.cumsum(onehot, axis=0)
    rank = jnp.sum(csum * onehot, axis=1) - 1
    counts = csum[-1]
    padded = (counts + MOE_BLOCK - 1) // MOE_BLOCK * MOE_BLOCK
    pad_end = jnp.cumsum(padded)
    pad_start = pad_end - padded
    dest = (jnp.sum(onehot * pad_start[None, :], axis=1) + rank).astype(I32)
    n_used = (pad_end[-1] // MOE_BLOCK).astype(I32)
    first_slot = jnp.arange(n_blocks, dtype=I32) * MOE_BLOCK
    blk_exp = jnp.minimum(jnp.sum((first_slot[:, None] >= pad_end[None, :]).astype(I32), axis=1), N_EXPERTS - 1)
    last_exp = jnp.take(blk_exp, jnp.maximum(n_used - 1, 0))
    blk_exp = jnp.where(jnp.arange(n_blocks) < n_used, blk_exp, last_exp).astype(I32)
    pad_bounds = jnp.concatenate([pad_start + counts, pad_end]).astype(I32)
    return dest, pad_bounds, blk_exp, n_used.reshape(1)


def kernel(x_prompt, x_sample, cache_k, cache_v, state_ssm_re, state_ssm_im, page_table, w_in, w_out, attn_norm_g, ssm_norm_g, ssm_a_re, ssm_a_im, ssm_log_dt, ssm_b_re, ssm_b_im, ssm_c_re, ssm_c_im, ssm_d, ssm_w_glu, ln1_g, ln1_b, router_w, router_b, moe_w_gate, moe_w_up, moe_w_down, ln2_g, ln2_b):
    depth = w_in.shape[0]
    n_b, seq, _ = x_prompt.shape
    n_dec, t_new, _ = x_sample.shape
    n_pool = cache_k.shape[1]
    gq = N_HEADS // N_KV_HEADS
    alpha = (2 * depth) ** 0.25
    n_states = SSM_GROUPS * SSM_STATE
    n_sample = n_dec * t_new
    n_tok = n_b * seq + n_sample
    n_blocks = -(-(2 * n_tok + N_EXPERTS * (MOE_BLOCK - 1)) // MOE_BLOCK)
    n_slots = n_blocks * MOE_BLOCK

    slopes = 2.0 ** (-8.0 * jnp.arange(1, N_HEADS + 1, dtype=F32) / N_HEADS)
    cache_k_flat = cache_k.reshape(-1, HEAD_DIM)
    cache_v_flat = cache_v.reshape(-1, HEAD_DIM)
    router_wt = router_w.T
    router_bc = router_b.reshape(N_EXPERTS, 1)
    zeros_state = jnp.zeros((n_b, n_states), F32)
    w_in16 = w_in.astype(BF16)
    w_glu16 = ssm_w_glu.astype(BF16)
    w_out16 = w_out.astype(BF16)
    w_gate = moe_w_gate.reshape(depth * N_EXPERTS, D_MODEL, D_FF)
    w_up = moe_w_up.reshape(depth * N_EXPERTS, D_MODEL, D_FF)
    w_down = moe_w_down.reshape(depth * N_EXPERTS, D_FF, D_MODEL)

    xp = x_prompt
    xs = x_sample.transpose(1, 0, 2).reshape(n_sample, D_MODEL)
    outs = {k: [] for k in ("kp", "vp", "hrp", "hip", "ks", "vs", "hrs", "his")}
    for l in range(depth):
        lw = {
            "w_glu": w_glu16, "w_out": w_out16,
            "attn_g": attn_norm_g[l].reshape(1, -1), "ssm_g": ssm_norm_g[l].reshape(1, -1),
            "ln1_g": ln1_g[l].reshape(1, -1), "ln1_b": ln1_b[l].reshape(1, -1),
            "router_wt": router_wt, "router_b": router_bc,
        }
        prm = _ssm_params(ssm_a_re[l], ssm_a_im[l], ssm_log_dt[l], ssm_b_re[l], ssm_b_im[l],
                          ssm_c_re[l], ssm_c_im[l], ssm_d[l])

        q_p, k_p, v_p, u_p = _in_proj_prompt(xp, w_in16, l)
        att_p = _attn_prompt(q_p, k_p, v_p, slopes)
        y_p, hr_p, hi_p = _ssm(u_p, zeros_state, zeros_state, prm, nb=n_b, tc=128)

        h_s = _in_proj_sample(xs, w_in16, l)
        q_s = h_s[:, :ATTN_WIDTH].reshape(t_new, n_dec, N_HEADS, HEAD_DIM)
        q_s = q_s.transpose(1, 2, 0, 3).reshape(n_dec, N_HEADS * t_new, HEAD_DIM)
        k_s = h_s[:, ATTN_WIDTH:ATTN_WIDTH + KV_WIDTH].reshape(t_new, n_dec, N_KV_HEADS, HEAD_DIM)
        v_s = h_s[:, ATTN_WIDTH + KV_WIDTH:ATTN_WIDTH + 2 * KV_WIDTH].reshape(t_new, n_dec, N_KV_HEADS, HEAD_DIM)
        att_s = _attn_sample(q_s, k_s.transpose(1, 2, 0, 3), v_s.transpose(1, 2, 0, 3), cache_k_flat, cache_v_flat,
                             page_table, slopes, l, n_pool)
        att_s = att_s.reshape(n_dec, N_HEADS, t_new, HEAD_DIM).transpose(2, 0, 1, 3).reshape(n_sample, ATTN_WIDTH)
        u_s = h_s[:, ATTN_WIDTH + 2 * KV_WIDTH:].reshape(n_sample, N_SSM_CHUNKS, LANES).transpose(1, 0, 2)
        y_s, hr_s, hi_s = _ssm(u_s, state_ssm_re[l].reshape(n_dec, n_states), state_ssm_im[l].reshape(n_dec, n_states),
                               prm, nb=n_dec, tc=t_new)

        x1_p, eid_p, gate_p = _postmix(att_p, y_p, xp, lw, l, alpha, prompt=True)
        x1_s, eid_s, gate_s = _postmix(att_s, y_s, xs, lw, l, alpha, prompt=False)

        eid_t = jnp.concatenate([eid_p, eid_s], axis=1)
        dest, pad_bounds, blk_exp, n_used = _moe_plan(eid_t, n_blocks)
        xs_slots = _dispatch(x1_p, x1_s, dest, pad_bounds, n_slots)
        ys_slots = _ffn(xs_slots, w_gate, w_up, w_down, blk_exp, n_used, n_blocks, l)
        l2g = ln2_g[l].reshape(1, -1)
        l2b = ln2_b[l].reshape(1, -1)
        n_pa = 2 * n_b * seq
        xp = _combine(x1_p, gate_p, dest[:n_pa], ys_slots, l2g, l2b, alpha, prompt=True)
        xs = _combine(x1_s, gate_s, dest[n_pa:], ys_slots, l2g, l2b, alpha, prompt=False)

        outs["kp"].append(k_p.reshape(n_b, seq, N_KV_HEADS, HEAD_DIM))
        outs["vp"].append(v_p.reshape(n_b, seq, N_KV_HEADS, HEAD_DIM))
        outs["hrp"].append(hr_p.reshape(n_b, SSM_GROUPS, SSM_STATE))
        outs["hip"].append(hi_p.reshape(n_b, SSM_GROUPS, SSM_STATE))
        outs["ks"].append(k_s.transpose(1, 0, 2, 3))
        outs["vs"].append(v_s.transpose(1, 0, 2, 3))
        outs["hrs"].append(hr_s.reshape(n_dec, SSM_GROUPS, SSM_STATE))
        outs["his"].append(hi_s.reshape(n_dec, SSM_GROUPS, SSM_STATE))

    y_sample = xs.reshape(t_new, n_dec, D_MODEL).transpose(1, 0, 2)
    return (xp, y_sample,
            jnp.stack(outs["kp"]), jnp.stack(outs["vp"]), jnp.stack(outs["hrp"]), jnp.stack(outs["hip"]),
            jnp.stack(outs["ks"]), jnp.stack(outs["vs"]), jnp.stack(outs["hrs"]), jnp.stack(outs["his"]))
```

```python
import functools
import math

import jax
import jax.numpy as jnp
from jax import lax
from jax.experimental import pallas as pl
from jax.experimental.pallas import tpu as pltpu

F32 = jnp.float32
BF16 = jnp.bfloat16
I32 = jnp.int32
HIGHEST = lax.Precision.HIGHEST

D_MODEL = 2048
ATTN_WIDTH = 1024
SSM_WIDTH = 1024
HEAD_DIM = 128
N_HEADS = 8
N_KV_HEADS = 4
KV_WIDTH = N_KV_HEADS * HEAD_DIM
PROJ_WIDTH = ATTN_WIDTH + 2 * KV_WIDTH + SSM_WIDTH
MOBA_BLOCK = 256
MOBA_TOPK = 3
PAGE_SIZE = 128
SSM_GROUP_CH = 16
SSM_GROUPS = 64
SSM_STATE = 64
N_EXPERTS = 16
N_EXPERT_GROUPS = 4
EXPERTS_PER_GROUP = 4
D_FF = 1024
LN_EPS = 1e-5
RMS_EPS = 1e-6
NEG_INF = float("-inf")
Q_SCALE = HEAD_DIM ** -0.5

LANES = 128
SUBLANES = 8
VMEM_LIMIT = 56 * 1024 * 1024

TILE_T = 32
SLAB = D_MODEL // LANES
XSLAB = D_MODEL // (2 * LANES)
SSM_CHUNK = 8 * SSM_GROUP_CH
SSM_CHUNK_STATES = 8 * SSM_STATE
N_SSM_CHUNKS = SSM_WIDTH // SSM_CHUNK
MOE_BLOCK = 256
POSTMIX_SPLIT = 2
DMA_ISSUE_UNROLL = 8


def _cparams(sem, vmem=VMEM_LIMIT):
    return pltpu.CompilerParams(dimension_semantics=sem, vmem_limit_bytes=vmem)


def _in_proj_prompt_kernel(x_ref, w_ref, q_ref, k_ref, v_ref, u_ref, *, n_b, tt):
    rows = n_b * tt
    x = x_ref[...].reshape(rows, D_MODEL).astype(BF16)
    h = jnp.dot(x, w_ref[...], preferred_element_type=F32)
    q_ref[...] = h[:, :ATTN_WIDTH].reshape(n_b, tt, ATTN_WIDTH)
    for b in range(n_b):
        for g in range(N_KV_HEADS):
            k0 = ATTN_WIDTH + g * HEAD_DIM
            v0 = ATTN_WIDTH + KV_WIDTH + g * HEAD_DIM
            k_ref[b, pl.ds(g, tt, stride=N_KV_HEADS), :] = h[b * tt:(b + 1) * tt, k0:k0 + HEAD_DIM]
            v_ref[b, pl.ds(g, tt, stride=N_KV_HEADS), :] = h[b * tt:(b + 1) * tt, v0:v0 + HEAD_DIM]
    u0 = ATTN_WIDTH + 2 * KV_WIDTH
    for c in range(N_SSM_CHUNKS):
        for b in range(n_b):
            u_ref[c, pl.ds(b, tt, stride=n_b), :] = h[b * tt:(b + 1) * tt, u0 + c * LANES:u0 + (c + 1) * LANES]


def _in_proj_prompt(x, w16, layer):
    n_b, seq, _ = x.shape
    tt = TILE_T
    nt = seq // tt
    kern = functools.partial(_in_proj_prompt_kernel, n_b=n_b, tt=tt)
    return pl.pallas_call(
        kern,
        out_shape=(
            jax.ShapeDtypeStruct((n_b, seq, ATTN_WIDTH), F32),
            jax.ShapeDtypeStruct((n_b, seq * N_KV_HEADS, HEAD_DIM), F32),
            jax.ShapeDtypeStruct((n_b, seq * N_KV_HEADS, HEAD_DIM), F32),
            jax.ShapeDtypeStruct((N_SSM_CHUNKS, seq * n_b, LANES), F32),
        ),
        grid=(nt,),
        in_specs=[
            pl.BlockSpec((n_b, tt, D_MODEL), lambda i: (0, i, 0)),
            pl.BlockSpec((None, D_MODEL, PROJ_WIDTH), lambda i: (layer, 0, 0)),
        ],
        out_specs=(
            pl.BlockSpec((n_b, tt, ATTN_WIDTH), lambda i: (0, i, 0)),
            pl.BlockSpec((n_b, tt * N_KV_HEADS, HEAD_DIM), lambda i: (0, i, 0)),
            pl.BlockSpec((n_b, tt * N_KV_HEADS, HEAD_DIM), lambda i: (0, i, 0)),
            pl.BlockSpec((N_SSM_CHUNKS, tt * n_b, LANES), lambda i: (0, i, 0)),
        ),
        compiler_params=_cparams(("arbitrary",)),
        name="in_proj_prompt",
    )(x, w16)


def _matmul_kernel(x_ref, w_ref, o_ref):
    o_ref[...] = jnp.dot(x_ref[...].astype(BF16), w_ref[...], preferred_element_type=F32)


def _in_proj_sample(x, w16, layer):
    m, k = x.shape
    n = w16.shape[2]
    tm = 256
    return pl.pallas_call(
        _matmul_kernel,
        out_shape=jax.ShapeDtypeStruct((m, n), F32),
        grid=(m // tm,),
        in_specs=[pl.BlockSpec((tm, k), lambda i: (i, 0)), pl.BlockSpec((None, k, n), lambda i: (layer, 0, 0))],
        out_specs=pl.BlockSpec((tm, n), lambda i: (i, 0)),
        compiler_params=_cparams(("arbitrary",)),
        name="in_proj_sample",
    )(x, w16)


def _topk_mask(gate, valid, k, axis):
    nb = gate.shape[axis]
    ids = lax.broadcasted_iota(I32, gate.shape, axis)
    gm = jnp.where(valid, gate, NEG_INF)
    cnt = jnp.zeros(gate.shape, I32)
    for j in range(nb):
        gj = gm[:, j:j + 1] if axis == 1 else gm[j:j + 1, :]
        beats = (gj > gm) | ((gj == gm) & (j < ids))
        cnt = cnt + beats.astype(I32)
    return valid & (cnt < k)


def _attn_prompt_kernel(slopes_ref, q_ref, k_ref, v_ref, o_ref, kmean_ref, k16_ref, vt_ref, bias_ref, acc_ref, s_ref,
                        *, n_blk):
    g = pl.program_id(1)
    ti = pl.program_id(2)
    blk = MOBA_BLOCK
    pair = 2 * blk
    n_pair = n_blk // 2
    gq = N_HEADS // N_KV_HEADS
    width = gq * pair
    nt_dims = (((1,), (1,)), ((), ()))

    lane = lax.broadcasted_iota(I32, (1, width), 1)
    qloc = lane % pair
    slope_vec = jnp.zeros((1, width), F32)
    for hh in range(gq):
        slope_vec = jnp.where(lane // pair == hh, slopes_ref[g * gq + hh], slope_vec)

    @pl.when(ti == 0)
    def _():
        for j in range(n_blk):
            rows_j = pl.ds(j * blk * N_KV_HEADS + g, blk, stride=N_KV_HEADS)
            kb = k_ref[rows_j, :]
            kmean_ref[j:j + 1, :] = jnp.sum(kb, axis=0, keepdims=True) * (1.0 / blk)
            k16_ref[j * blk:(j + 1) * blk, :] = kb.astype(BF16)
            vt_ref[j // 2, :, (j % 2) * blk:(j % 2 + 1) * blk] = v_ref[rows_j, :].T.astype(BF16)
        pair_key = lax.broadcasted_iota(I32, (pair, width), 0)
        bias_ref[...] = -slope_vec * (qloc - pair_key).astype(F32)

    q_all = jnp.concatenate([q_ref[:, hh * HEAD_DIM:(hh + 1) * HEAD_DIM] for hh in range(gq)], axis=0) * Q_SCALE
    q16 = q_all.astype(BF16)
    blk_ids = lax.broadcasted_iota(I32, (n_blk, width), 0)
    own_blk = 2 * ti + qloc // blk
    gate_t = lax.dot_general(kmean_ref[...], q_all, nt_dims, precision=HIGHEST, preferred_element_type=F32)
    sel = _topk_mask(gate_t, blk_ids < own_blk, MOBA_TOPK, axis=0).astype(F32)

    def sel_row(j):
        return jnp.sum(jnp.where(blk_ids == j, sel, 0.0), axis=0, keepdims=True) > 0.5

    def pair_scores(t):
        off = pl.multiple_of(t * pair, pair)
        return lax.dot_general(k16_ref[pl.ds(off, pair), :], q16, nt_dims, preferred_element_type=F32)

    def softmax_pair(s, vis0, vis1, m_prev, cst):
        h0 = jnp.where(vis0, s[0:blk, :] + bias_ref[0:blk, :], NEG_INF)
        h1 = jnp.where(vis1, s[blk:pair, :] + bias_ref[blk:pair, :], NEG_INF)
        mx = jnp.maximum(jnp.max(h0, axis=0, keepdims=True), jnp.max(h1, axis=0, keepdims=True)) + cst
        m_new = mx if m_prev is None else jnp.maximum(m_prev, mx)
        off = m_new - cst
        p0 = jnp.exp(h0 - off)
        p1 = jnp.exp(h1 - off)
        psum = jnp.sum(p0, axis=0, keepdims=True) + jnp.sum(p1, axis=0, keepdims=True)
        return m_new, jnp.concatenate([p0, p1], axis=0).astype(BF16), psum

    key_id = lax.broadcasted_iota(I32, (blk, width), 0)
    vis_top = ((qloc < blk) & (key_id <= qloc)) | ((qloc >= blk) & sel_row(2 * ti))
    vis_bot = key_id + blk <= qloc
    m0, p, l0 = softmax_pair(pair_scores(ti), vis_top, vis_bot, None, jnp.zeros((1, width), F32))
    acc_ref[...] = jnp.dot(vt_ref[ti], p, preferred_element_type=F32)

    s_ref[0] = pair_scores(0)

    def body(t, carry):
        m, l = carry
        s = s_ref[t % 2]
        s_ref[(t + 1) % 2] = pair_scores(jnp.minimum(t + 1, n_pair - 1))
        cst = -slope_vec * ((ti - t) * pair).astype(F32)
        m_new, p, psum = softmax_pair(s, sel_row(2 * t), sel_row(2 * t + 1), m, cst)
        a = jnp.exp(m - m_new)
        acc_ref[...] = a * acc_ref[...] + jnp.dot(vt_ref[t], p, preferred_element_type=F32)
        return m_new, a * l + psum

    _, l = lax.fori_loop(0, ti, body, (m0, l0))
    o_t = acc_ref[...] / l
    for hh in range(gq):
        o_ref[:, hh * HEAD_DIM:(hh + 1) * HEAD_DIM] = o_t[:, hh * pair:(hh + 1) * pair].T


def _attn_prompt(q, k, v, slopes):
    n_b, seq, _ = q.shape
    n_blk = seq // MOBA_BLOCK
    assert n_blk % 2 == 0
    gq = N_HEADS // N_KV_HEADS
    pair = 2 * MOBA_BLOCK
    kern = functools.partial(_attn_prompt_kernel, n_blk=n_blk)
    return pl.pallas_call(
        kern,
        out_shape=jax.ShapeDtypeStruct((n_b, seq, ATTN_WIDTH), F32),
        grid_spec=pltpu.PrefetchScalarGridSpec(
            num_scalar_prefetch=1,
            grid=(n_b, N_KV_HEADS, n_blk // 2),
            in_specs=[
                pl.BlockSpec((None, pair, gq * HEAD_DIM), lambda b, g, i, s: (b, i, g)),
                pl.BlockSpec((None, seq * N_KV_HEADS, HEAD_DIM), lambda b, g, i, s: (b, 0, 0)),
                pl.BlockSpec((None, seq * N_KV_HEADS, HEAD_DIM), lambda b, g, i, s: (b, 0, 0)),
            ],
            out_specs=pl.BlockSpec((None, pair, gq * HEAD_DIM), lambda b, g, i, s: (b, i, g)),
            scratch_shapes=[
                pltpu.VMEM((n_blk, HEAD_DIM), F32),
                pltpu.VMEM((seq, HEAD_DIM), BF16),
                pltpu.VMEM((n_blk // 2, HEAD_DIM, pair), BF16),
                pltpu.VMEM((pair, gq * pair), F32),
                pltpu.VMEM((HEAD_DIM, gq * pair), F32),
                pltpu.VMEM((2, pair, gq * pair), F32),
            ],
        ),
        compiler_params=_cparams(("arbitrary", "arbitrary", "arbitrary")),
        name="moba_prompt",
    )(slopes, q, k, v)


def _attn_sample_kernel(pt_ref, slopes_ref, q_ref, kn_ref, vn_ref, *rest, n_pages, t_new, past):
    k_refs = rest[:n_pages]
    v_refs = rest[n_pages:2 * n_pages]
    o_ref = rest[2 * n_pages]
    gq = N_HEADS // N_KV_HEADS
    rows = N_HEADS * t_new
    ppb = MOBA_BLOCK // PAGE_SIZE
    n_blk = n_pages // ppb
    page_rows = PAGE_SIZE * N_KV_HEADS
    cols = ppb * page_rows
    nt_dims = (((1,), (1,)), ((), ()))

    row = lax.broadcasted_iota(I32, (rows, 1), 0)
    head = row // t_new
    g_row = head // gq
    t_row = row % t_new
    slope_row = jnp.zeros((rows, 1), F32)
    for h in range(N_HEADS):
        slope_row = jnp.where(head == h, slopes_ref[h], slope_row)
    col = lax.broadcasted_iota(I32, (1, cols), 1)
    g_col = col % N_KV_HEADS
    kpos_col = (col // page_rows) * PAGE_SIZE + (col % page_rows) // N_KV_HEADS
    bias0 = jnp.where(g_row == g_col, -slope_row * (past + t_row - kpos_col).astype(F32), NEG_INF)

    def per_head_rows(fn):
        out = jnp.zeros((rows, HEAD_DIM), F32)
        for g in range(N_KV_HEADS):
            out = jnp.where(g_row == g, fn(g), out)
        return out

    q = q_ref[...] * Q_SCALE
    q16 = q.astype(BF16)
    blk_lane = lax.broadcasted_iota(I32, (rows, n_blk), 1)

    gate = jnp.zeros((rows, n_blk), F32)
    k16, v16 = [], []
    for j in range(n_blk):
        kblk = jnp.concatenate([k_refs[ppb * j + p][...] for p in range(ppb)], axis=0)
        fold = jnp.sum(kblk.reshape(cols // SUBLANES, SUBLANES, HEAD_DIM), axis=0)
        ksum = fold[0:N_KV_HEADS, :] + fold[N_KV_HEADS:, :]
        kmean_rows = per_head_rows(lambda g: ksum[g:g + 1, :]) * (1.0 / MOBA_BLOCK)
        gate = jnp.where(blk_lane == j, jnp.sum(q * kmean_rows, axis=1, keepdims=True), gate)
        k16.append(kblk.astype(BF16))
        v16.extend(v_refs[ppb * j + p][...].astype(BF16) for p in range(ppb))
    sel = _topk_mask(gate, blk_lane >= 0, MOBA_TOPK, axis=1).astype(F32)

    bias = jnp.concatenate(
        [jnp.where(sel[:, j:j + 1] > 0.5, bias0 + slope_row * float(j * MOBA_BLOCK), NEG_INF) for j in range(n_blk)],
        axis=1)
    s = lax.dot_general(q16, jnp.concatenate(k16, axis=0), nt_dims, preferred_element_type=F32) + bias

    s_own, v_own = [], []
    for tk in range(t_new):
        kn_rows = per_head_rows(lambda g: kn_ref[g, tk:tk + 1, :])
        sv = jnp.sum(q * kn_rows, axis=1, keepdims=True) - slope_row * (t_row - tk).astype(F32)
        s_own.append(jnp.where(t_row >= tk, sv, NEG_INF))
        v_own.append(per_head_rows(lambda g: vn_ref[g, tk:tk + 1, :]))
    m = jnp.max(s, axis=1, keepdims=True)
    for tk in range(t_new):
        m = jnp.maximum(m, s_own[tk])
    p = jnp.exp(s - m)
    num = jnp.dot(p.astype(BF16), jnp.concatenate(v16, axis=0), preferred_element_type=F32)
    den = jnp.sum(p, axis=1, keepdims=True)
    for tk in range(t_new):
        pw = jnp.exp(s_own[tk] - m)
        num = num + pw * v_own[tk]
        den = den + pw
    o_ref[...] = num / den


def _attn_sample(q, k_new, v_new, cache_k_flat, cache_v_flat, page_table, slopes, layer, n_pool):
    n_dec, rows, _ = q.shape
    t_new = k_new.shape[2]
    n_pages = page_table.shape[1]
    past = n_pages * PAGE_SIZE
    assert past % MOBA_BLOCK == 0
    page_rows = PAGE_SIZE * N_KV_HEADS
    base = layer * n_pool
    pt_flat = page_table.reshape(-1)
    kern = functools.partial(_attn_sample_kernel, n_pages=n_pages, t_new=t_new, past=past)

    def page_spec(p):
        return pl.BlockSpec((page_rows, HEAD_DIM), lambda b, pt, sl: (base + pt[b * n_pages + p], 0))

    q_spec = pl.BlockSpec((None, rows, HEAD_DIM), lambda b, pt, sl: (b, 0, 0))
    new_spec = pl.BlockSpec((None, N_KV_HEADS, t_new, HEAD_DIM), lambda b, pt, sl: (b, 0, 0, 0))
    pages = [page_spec(p) for p in range(n_pages)]
    return pl.pallas_call(
        kern,
        out_shape=jax.ShapeDtypeStruct((n_dec, rows, HEAD_DIM), F32),
        grid_spec=pltpu.PrefetchScalarGridSpec(
            num_scalar_prefetch=2,
            grid=(n_dec,),
            in_specs=[q_spec, new_spec, new_spec] + pages + pages,
            out_specs=q_spec,
        ),
        compiler_params=_cparams(("arbitrary",)),
        name="moba_sample",
    )(pt_flat, slopes, q, k_new, v_new, *([cache_k_flat] * n_pages), *([cache_v_flat] * n_pages))


def _ssm_kernel(u_ref, bre_ref, bim_ref, cre_ref, cim_ref, are_ref, aim_ref, d_ref, s0r_ref, s0i_ref,
                y_ref, hr_ref, hi_ref, xr_ref, xi_ref, str_ref, sti_ref, *, nb, tc):
    ti = pl.program_id(1)

    @pl.when(ti == 0)
    def _():
        str_ref[...] = s0r_ref[...]
        sti_ref[...] = s0i_ref[...]

    u = u_ref[...]
    u16 = u.astype(BF16)
    xr_ref[...] = jnp.dot(u16, bre_ref[...], preferred_element_type=F32)
    xi_ref[...] = jnp.dot(u16, bim_ref[...], preferred_element_type=F32)
    a_re = jnp.broadcast_to(are_ref[...], (nb, SSM_CHUNK_STATES))
    a_im = jnp.broadcast_to(aim_ref[...], (nb, SSM_CHUNK_STATES))

    def step(t, carry):
        h_re, h_im = carry
        sl = pl.ds(pl.multiple_of(t * nb, nb), nb)
        n_re = (a_re * h_re - a_im * h_im) + xr_ref[sl, :]
        n_im = (a_re * h_im + a_im * h_re) + xi_ref[sl, :]
        xr_ref[sl, :] = n_re
        xi_ref[sl, :] = n_im
        return n_re, n_im

    h_re, h_im = lax.fori_loop(0, tc, step, (str_ref[...], sti_ref[...]))
    str_ref[...] = h_re
    sti_ref[...] = h_im
    y = (jnp.dot(xr_ref[...].astype(BF16), cre_ref[...], preferred_element_type=F32)
         - jnp.dot(xi_ref[...].astype(BF16), cim_ref[...], preferred_element_type=F32)
         + d_ref[...] * u)
    y_ref[...] = jax.nn.gelu(y)

    @pl.when(ti == pl.num_programs(1) - 1)
    def _():
        hr_ref[...] = h_re
        hi_ref[...] = h_im


def _ssm(u_chunks, s0_re, s0_im, prm, layer, nb, tc):
    c0 = layer * N_SSM_CHUNKS
    n_rows = u_chunks.shape[1]
    nt = n_rows // (tc * nb)
    rows = tc * nb
    kern = functools.partial(_ssm_kernel, nb=nb, tc=tc)
    cs = SSM_CHUNK_STATES
    n_states = SSM_GROUPS * SSM_STATE
    return pl.pallas_call(
        kern,
        out_shape=(
            jax.ShapeDtypeStruct((N_SSM_CHUNKS, n_rows, LANES), F32),
            jax.ShapeDtypeStruct((nb, n_states), F32),
            jax.ShapeDtypeStruct((nb, n_states), F32),
        ),
        grid=(N_SSM_CHUNKS, nt),
        in_specs=[
            pl.BlockSpec((None, rows, LANES), lambda c, t: (c, t, 0)),
            pl.BlockSpec((None, SSM_CHUNK, cs), lambda c, t: (c0 + c, 0, 0)),
            pl.BlockSpec((None, SSM_CHUNK, cs), lambda c, t: (c0 + c, 0, 0)),
            pl.BlockSpec((None, cs, SSM_CHUNK), lambda c, t: (c0 + c, 0, 0)),
            pl.BlockSpec((None, cs, SSM_CHUNK), lambda c, t: (c0 + c, 0, 0)),
            pl.BlockSpec((1, cs), lambda c, t: (0, c0 + c)),
            pl.BlockSpec((1, cs), lambda c, t: (0, c0 + c)),
            pl.BlockSpec((1, SSM_CHUNK), lambda c, t: (0, c0 + c)),
            pl.BlockSpec((nb, cs), lambda c, t: (0, c)),
            pl.BlockSpec((nb, cs), lambda c, t: (0, c)),
        ],
        out_specs=(
            pl.BlockSpec((None, rows, LANES), lambda c, t: (c, t, 0)),
            pl.BlockSpec((nb, cs), lambda c, t: (0, c)),
            pl.BlockSpec((nb, cs), lambda c, t: (0, c)),
        ),
        scratch_shapes=[
            pltpu.VMEM((rows, cs), F32),
            pltpu.VMEM((rows, cs), F32),
            pltpu.VMEM((nb, cs), F32),
            pltpu.VMEM((nb, cs), F32),
        ],
        compiler_params=_cparams(("arbitrary", "arbitrary")),
        name="s5_mixer",
    )(u_chunks, prm["bbd_re"], prm["bbd_im"], prm["cbd_re"], prm["cbd_im"], prm["ab_re"], prm["ab_im"],
      prm["d"], s0_re, s0_im)


def _ssm_params(a_re, a_im, log_dt, b_re, b_im, c_re, c_im, d_skip):
    dt = jnp.exp(log_dt)
    mag = jnp.exp(a_re * dt)
    ab_re = mag * jnp.cos(a_im * dt)
    ab_im = mag * jnp.sin(a_im * dt)
    den = a_re * a_re + a_im * a_im
    f_re = ((ab_re - 1.0) * a_re + ab_im * a_im) / den
    f_im = (ab_im * a_re - (ab_re - 1.0) * a_im) / den
    bb_re = f_re[..., None] * b_re - f_im[..., None] * b_im
    bb_im = f_re[..., None] * b_im + f_im[..., None] * b_re
    gpc = SSM_CHUNK // SSM_GROUP_CH
    eye = jnp.eye(gpc, dtype=F32)

    def b_blockdiag(bb):
        x = bb.reshape(N_SSM_CHUNKS, gpc, SSM_STATE, SSM_GROUP_CH)
        m = jnp.einsum("kgpc,gh->kgchp", x, eye)
        return m.reshape(N_SSM_CHUNKS, gpc * SSM_GROUP_CH, gpc * SSM_STATE).astype(BF16)

    def c_blockdiag(cc):
        x = cc.reshape(N_SSM_CHUNKS, gpc, SSM_GROUP_CH, SSM_STATE)
        m = jnp.einsum("kgcp,gh->kgphc", x, eye)
        return m.reshape(N_SSM_CHUNKS, gpc * SSM_STATE, gpc * SSM_GROUP_CH).astype(BF16)

    return {
        "bbd_re": b_blockdiag(bb_re), "bbd_im": b_blockdiag(bb_im),
        "cbd_re": c_blockdiag(c_re), "cbd_im": c_blockdiag(c_im),
        "ab_re": ab_re.reshape(1, -1), "ab_im": ab_im.reshape(1, -1),
        "d": d_skip.reshape(1, -1),
    }


def _rank_lt(vals, a):
    cnt = jnp.zeros(vals[a].shape, I32)
    for b in range(len(vals)):
        if b == a:
            continue
        before = (vals[b] >= vals[a]) if b < a else (vals[b] > vals[a])
        cnt = cnt + before.astype(I32)
    return cnt


def _router(x1, rwt_ref, rb_ref, eid_ref, gate_ref, cols):
    logits = lax.dot_general(rwt_ref[...], x1, (((1,), (1,)), ((), ())), precision=HIGHEST,
                             preferred_element_type=F32)
    mx = jnp.max(logits, axis=0, keepdims=True)
    ex = jnp.exp(logits - mx)
    probs = ex / jnp.sum(ex, axis=0, keepdims=True)
    biased = probs + rb_ref[...]
    prow = [probs[e:e + 1, :] for e in range(N_EXPERTS)]
    brow = [biased[e:e + 1, :] for e in range(N_EXPERTS)]
    ranks = []
    gscore = []
    for gi in range(N_EXPERT_GROUPS):
        vals = brow[gi * EXPERTS_PER_GROUP:(gi + 1) * EXPERTS_PER_GROUP]
        rk = [_rank_lt(vals, a) for a in range(EXPERTS_PER_GROUP)]
        ranks.append(rk)
        sc = jnp.zeros(vals[0].shape, F32)
        for a in range(EXPERTS_PER_GROUP):
            sc = sc + jnp.where(rk[a] < 2, vals[a], 0.0)
        gscore.append(sc)
    e0 = jnp.zeros(prow[0].shape, I32)
    e1 = jnp.zeros(prow[0].shape, I32)
    p0 = jnp.zeros(prow[0].shape, F32)
    p1 = jnp.zeros(prow[0].shape, F32)
    for gi in range(N_EXPERT_GROUPS):
        chosen = _rank_lt(gscore, gi) == 0
        for a in range(EXPERTS_PER_GROUP):
            e = gi * EXPERTS_PER_GROUP + a
            first = chosen & (ranks[gi][a] == 0)
            second = chosen & (ranks[gi][a] == 1)
            e0 = jnp.where(first, e, e0)
            e1 = jnp.where(second, e, e1)
            p0 = jnp.where(first, prow[e], p0)
            p1 = jnp.where(second, prow[e], p1)
    tot = p0 + p1
    eid_ref[0:1, cols] = e0
    eid_ref[1:2, cols] = e1
    gate_ref[0:1, cols] = p0 / tot
    gate_ref[1:2, cols] = p1 / tot


def _postmix_kernel(att_ref, y_ref, x_ref, wglu_ref, wout_ref, ag_ref, sg_ref, lg_ref, lb_ref, rwt_ref, rb_ref,
                    x1_ref, eid_ref, gate_ref, ybuf_ref, *, n_b, tt, alpha):
    rows = x1_ref.shape[0] if n_b is None else n_b * tt
    if n_b is None:
        att = att_ref[...]
        x = x_ref[...]
        for c in range(N_SSM_CHUNKS):
            ybuf_ref[:, c * LANES:(c + 1) * LANES] = y_ref[c]
    else:
        for c in range(N_SSM_CHUNKS):
            for b in range(n_b):
                ybuf_ref[b * tt:(b + 1) * tt, c * LANES:(c + 1) * LANES] = y_ref[c, pl.ds(b, tt, stride=n_b), :]
    z_all = jnp.dot(ybuf_ref[...].astype(BF16), wglu_ref[...], preferred_element_type=F32)
    hrows = rows // POSTMIX_SPLIT
    for hf in range(POSTMIX_SPLIT):
        r0 = hf * hrows
        if n_b is None:
            att_h = att[r0:r0 + hrows, :]
            x_h = x[r0:r0 + hrows, :]
        else:
            att_h = att_ref[hf * (n_b // POSTMIX_SPLIT):(hf + 1) * (n_b // POSTMIX_SPLIT)].reshape(hrows, ATTN_WIDTH)
            x_h = x_ref[hf * (n_b // POSTMIX_SPLIT):(hf + 1) * (n_b // POSTMIX_SPLIT)].reshape(hrows, D_MODEL)
        y = ybuf_ref[r0:r0 + hrows, :]
        sg = y * jax.nn.sigmoid(z_all[r0:r0 + hrows, :])
        ssm_n = sg * lax.rsqrt(jnp.mean(sg * sg, axis=-1, keepdims=True) + RMS_EPS) * sg_ref[...]
        att_n = att_h * lax.rsqrt(jnp.mean(att_h * att_h, axis=-1, keepdims=True) + RMS_EPS) * ag_ref[...]
        mix = (jnp.dot(att_n.astype(BF16), wout_ref[0:ATTN_WIDTH, :], preferred_element_type=F32)
               + jnp.dot(ssm_n.astype(BF16), wout_ref[ATTN_WIDTH:, :], preferred_element_type=F32))
        r = alpha * x_h + mix
        mu = jnp.mean(r, axis=-1, keepdims=True)
        rc = r - mu
        var = jnp.mean(rc * rc, axis=-1, keepdims=True)
        x1 = rc * lax.rsqrt(var + LN_EPS) * lg_ref[...] + lb_ref[...]
        if n_b is None:
            x1_ref[r0:r0 + hrows, :] = x1
        else:
            nbh = n_b // POSTMIX_SPLIT
            x1_ref[hf * nbh:(hf + 1) * nbh] = x1.reshape(nbh, tt, D_MODEL)
        _router(x1, rwt_ref, rb_ref, eid_ref, gate_ref, slice(r0, r0 + hrows))


def _postmix(att, y_chunks, x, lw, layer, alpha, prompt):
    if prompt:
        n_b, seq, _ = x.shape
        tt = TILE_T
        nt = seq // tt
        rows = n_b * tt
        n_tok = n_b * seq
        att_spec = pl.BlockSpec((n_b, tt, ATTN_WIDTH), lambda i: (0, i, 0))
        x_spec = pl.BlockSpec((n_b, tt, D_MODEL), lambda i: (0, i, 0))
        x1_shape = jax.ShapeDtypeStruct((n_b, seq, D_MODEL), F32)
        kern = functools.partial(_postmix_kernel, n_b=n_b, tt=tt, alpha=alpha)
    else:
        n_tok = x.shape[0]
        rows = 256
        nt = n_tok // rows
        att_spec = pl.BlockSpec((rows, ATTN_WIDTH), lambda i: (i, 0))
        x_spec = pl.BlockSpec((rows, D_MODEL), lambda i: (i, 0))
        x1_shape = jax.ShapeDtypeStruct((n_tok, D_MODEL), F32)
        kern = functools.partial(_postmix_kernel, n_b=None, tt=None, alpha=alpha)
    full = lambda shp: pl.BlockSpec(shp, lambda i: tuple(0 for _ in shp))
    return pl.pallas_call(
        kern,
        out_shape=(x1_shape, jax.ShapeDtypeStruct((2, n_tok), I32), jax.ShapeDtypeStruct((2, n_tok), F32)),
        grid=(nt,),
        in_specs=[
            att_spec,
            pl.BlockSpec((N_SSM_CHUNKS, rows, LANES), lambda i: (0, i, 0)),
            x_spec,
            pl.BlockSpec((None, SSM_WIDTH, SSM_WIDTH), lambda i: (layer, 0, 0)),
            pl.BlockSpec((None, D_MODEL, D_MODEL), lambda i: (layer, 0, 0)),
            full((1, ATTN_WIDTH)),
            full((1, SSM_WIDTH)),
            full((1, D_MODEL)),
            full((1, D_MODEL)),
            full((N_EXPERTS, D_MODEL)),
            full((N_EXPERTS, 1)),
        ],
        out_specs=(x_spec, pl.BlockSpec((2, rows), lambda i: (0, i)), pl.BlockSpec((2, rows), lambda i: (0, i))),
        scratch_shapes=[pltpu.VMEM((rows, SSM_WIDTH), F32)],
        compiler_params=_cparams(("arbitrary",)),
        name="postmix_prompt" if prompt else "postmix_sample",
    )(att, y_chunks, x, lw["w_glu"], lw["w_out"], lw["attn_g"], lw["ssm_g"], lw["ln1_g"], lw["ln1_b"],
      lw["router_wt"], lw["router_b"])


def _slab_rows(row, slab):
    start = row * slab
    return pl.ds(start if isinstance(start, int) else pl.multiple_of(start, slab), slab)


def _slab_copy(src_ref, src_row, dst_ref, dst_row, sem, slab=SLAB):
    return pltpu.make_async_copy(src_ref.at[_slab_rows(src_row, slab), :], dst_ref.at[_slab_rows(dst_row, slab), :],
                                 sem)


def _pack_bf16_pair(a, b):
    hi = lax.bitcast_convert_type(a.astype(BF16).astype(F32), jnp.uint32)
    lo = lax.bitcast_convert_type(b.astype(BF16).astype(F32), jnp.uint32)
    return hi | (lo >> 16)


def _unpack_bf16_pair(w):
    a = lax.bitcast_convert_type(w & jnp.uint32(0xFFFF0000), F32)
    b = lax.bitcast_convert_type(w << 16, F32)
    return a.astype(BF16), b.astype(BF16)


def _dispatch_kernel(dest_ref, pad_ref, xp_ref, xs_ref, out_ref, slab_ref, sem, *, n_b, tt, n_prompt_tiles, n_blocks):
    i = pl.program_id(0)
    last = pl.num_programs(0) - 1
    rows = n_b * tt
    slot = i % 2
    slab = slab_ref.at[slot]

    half = D_MODEL // 2

    @pl.when(i < n_prompt_tiles)
    def _():
        for s in range(XSLAB):
            for b in range(n_b):
                slab[pl.ds(b * tt * XSLAB + s, tt, stride=XSLAB), :] = _pack_bf16_pair(
                    xp_ref[b, :, s * LANES:(s + 1) * LANES], xp_ref[b, :, half + s * LANES:half + (s + 1) * LANES])

    @pl.when(i >= n_prompt_tiles)
    def _():
        for s in range(XSLAB):
            slab[pl.ds(s, rows, stride=XSLAB), :] = _pack_bf16_pair(
                xs_ref[:, s * LANES:(s + 1) * LANES], xs_ref[:, half + s * LANES:half + (s + 1) * LANES])

    base = i * (2 * rows)

    def start(r, c):
        for k in range(2):
            _slab_copy(slab, r, out_ref, dest_ref[base + 2 * r + k], sem.at[slot], XSLAB).start()
        return c

    lax.fori_loop(0, rows, start, 0, unroll=DMA_ISSUE_UNROLL)

    def wait_tile(which):
        for _ in range(2):
            pltpu.make_async_copy(slab_ref.at[which], slab_ref.at[which], sem.at[which]).wait()

    @pl.when(i > 0)
    def _():
        wait_tile(1 - slot)

    @pl.when(i == last)
    def _():
        wait_tile(slot)
        slab[...] = jnp.zeros(slab.shape, jnp.uint32)
        zsem = sem.at[slot]

        def per_expert(e, c):
            lo = pad_ref[e]
            hi = pad_ref[N_EXPERTS + e]

            def zs(s, c2):
                _slab_copy(slab, 0, out_ref, s, zsem, XSLAB).start()
                return c2

            lax.fori_loop(lo, hi, zs, 0)

            def zw(s, c2):
                _slab_copy(slab, 0, out_ref, s, zsem, XSLAB).wait()
                return c2

            lax.fori_loop(lo, hi, zw, 0)
            return c

        lax.fori_loop(0, N_EXPERTS, per_expert, 0)

        blk_rows = MOE_BLOCK * XSLAB

        def block_copy(b):
            start = pl.multiple_of(b * blk_rows, blk_rows)
            return pltpu.make_async_copy(slab, out_ref.at[pl.ds(start, blk_rows), :], zsem)

        def zbs(b, c):
            block_copy(b).start()
            return c

        def zbw(b, c):
            block_copy(b).wait()
            return c

        first_unused = pad_ref[2 * N_EXPERTS - 1] // MOE_BLOCK
        lax.fori_loop(first_unused, n_blocks, zbs, 0)
        lax.fori_loop(first_unused, n_blocks, zbw, 0)


def _dispatch(x1_p, x1_s, dest, pad_bounds, n_slots):
    n_b, seq, _ = x1_p.shape
    tt = TILE_T
    rows = n_b * tt
    assert rows == MOE_BLOCK
    ntp = seq // tt
    nts = x1_s.shape[0] // rows
    kern = functools.partial(_dispatch_kernel, n_b=n_b, tt=tt, n_prompt_tiles=ntp, n_blocks=n_slots // MOE_BLOCK)
    return pl.pallas_call(
        kern,
        out_shape=jax.ShapeDtypeStruct((n_slots * XSLAB, LANES), jnp.uint32),
        grid_spec=pltpu.PrefetchScalarGridSpec(
            num_scalar_prefetch=2,
            grid=(ntp + nts,),
            in_specs=[
                pl.BlockSpec((n_b, tt, D_MODEL), lambda i, d, p: (0, jnp.minimum(i, ntp - 1), 0)),
                pl.BlockSpec((rows, D_MODEL), lambda i, d, p: (jnp.maximum(i - ntp, 0), 0)),
            ],
            out_specs=pl.BlockSpec(memory_space=pl.ANY),
            scratch_shapes=[
                pltpu.VMEM((2, rows * XSLAB, LANES), jnp.uint32),
                pltpu.SemaphoreType.DMA((2,)),
            ],
        ),
        compiler_params=_cparams(("arbitrary",)),
        name="moe_dispatch",
    )(dest, pad_bounds, x1_p, x1_s)


def _new_expert(bexp_ref, i):
    return (i == 0) | (bexp_ref[i] != bexp_ref[jnp.maximum(i - 1, 0)])


def _ffn_up_kernel(bexp_ref, nused_ref, xs_ref, wg_ref, wu_ref, h_ref, xb_ref, wg16_ref, wu16_ref):
    i = pl.program_id(0)
    rows = MOE_BLOCK

    @pl.when(_new_expert(bexp_ref, i))
    def _():
        wg16_ref[...] = wg_ref[...].astype(BF16)
        wu16_ref[...] = wu_ref[...].astype(BF16)

    @pl.when(i < nused_ref[0])
    def _():
        half = D_MODEL // 2
        for s in range(XSLAB):
            a, b = _unpack_bf16_pair(xs_ref[pl.ds(s, rows, stride=XSLAB), :])
            xb_ref[:, s * LANES:(s + 1) * LANES] = a
            xb_ref[:, half + s * LANES:half + (s + 1) * LANES] = b
        xb = xb_ref[...]
        g = jnp.dot(xb, wg16_ref[...], preferred_element_type=F32)
        u = jnp.dot(xb, wu16_ref[...], preferred_element_type=F32)
        h_ref[...] = (g * jax.nn.sigmoid(g) * u).astype(BF16)

    @pl.when(i >= nused_ref[0])
    def _():
        h_ref[...] = jnp.zeros(h_ref.shape, BF16)


def _ffn_down_kernel(bexp_ref, nused_ref, h_ref, wd_ref, ys_ref, wd16_ref):
    i = pl.program_id(0)
    rows = MOE_BLOCK

    @pl.when(_new_expert(bexp_ref, i))
    def _():
        wd16_ref[...] = wd_ref[...].astype(BF16)

    @pl.when(i < nused_ref[0])
    def _():
        y = jnp.dot(h_ref[...], wd16_ref[...], preferred_element_type=F32)
        for s in range(SLAB):
            ys_ref[pl.ds(s, rows, stride=SLAB), :] = y[:, s * LANES:(s + 1) * LANES]

    @pl.when(i >= nused_ref[0])
    def _():
        ys_ref[...] = jnp.zeros(ys_ref.shape, F32)


def _ffn(xs, w_gate, w_up, w_down, blk_exp, n_used, n_blocks, layer):
    rows = MOE_BLOCK
    e0 = layer * N_EXPERTS
    n_slots = n_blocks * rows

    def live_map(i, be, nu):
        return (jnp.minimum(i, nu[0] - 1), 0)

    def w_map(i, be, nu):
        return (e0 + be[i], 0, 0)

    h = pl.pallas_call(
        _ffn_up_kernel,
        out_shape=jax.ShapeDtypeStruct((n_slots, D_FF), BF16),
        grid_spec=pltpu.PrefetchScalarGridSpec(
            num_scalar_prefetch=2,
            grid=(n_blocks,),
            in_specs=[
                pl.BlockSpec((rows * XSLAB, LANES), live_map),
                pl.BlockSpec((None, D_MODEL, D_FF), w_map),
                pl.BlockSpec((None, D_MODEL, D_FF), w_map),
            ],
            out_specs=pl.BlockSpec((rows, D_FF), lambda i, be, nu: (i, 0)),
            scratch_shapes=[
                pltpu.VMEM((rows, D_MODEL), BF16),
                pltpu.VMEM((D_MODEL, D_FF), BF16),
                pltpu.VMEM((D_MODEL, D_FF), BF16),
            ],
        ),
        compiler_params=_cparams(("arbitrary",)),
        name="moe_ffn_up",
    )(blk_exp, n_used, xs, w_gate, w_up)
    return pl.pallas_call(
        _ffn_down_kernel,
        out_shape=jax.ShapeDtypeStruct((n_slots * SLAB, LANES), F32),
        grid_spec=pltpu.PrefetchScalarGridSpec(
            num_scalar_prefetch=2,
            grid=(n_blocks,),
            in_specs=[
                pl.BlockSpec((rows, D_FF), live_map),
                pl.BlockSpec((None, D_FF, D_MODEL), w_map),
            ],
            out_specs=pl.BlockSpec((rows * SLAB, LANES), lambda i, be, nu: (i, 0)),
            scratch_shapes=[pltpu.VMEM((D_FF, D_MODEL), BF16)],
        ),
        compiler_params=_cparams(("arbitrary",)),
        name="moe_ffn_down",
    )(blk_exp, n_used, h, w_down)


def _combine_kernel(dest_ref, x1_ref, gate_ref, lg_ref, lb_ref, ys_ref, x2_ref, y0_ref, y1_ref, sem, *, n_b, tt, alpha):
    i = pl.program_id(0)
    rows = y0_ref.shape[1] // SLAB
    slot = i % 2

    def fetch(tile, which):
        base = tile * (2 * rows)

        def start(r, c):
            _slab_copy(ys_ref, dest_ref[base + 2 * r], y0_ref.at[which], r, sem.at[which]).start()
            _slab_copy(ys_ref, dest_ref[base + 2 * r + 1], y1_ref.at[which], r, sem.at[which]).start()
            return c

        lax.fori_loop(0, rows, start, 0, unroll=DMA_ISSUE_UNROLL)

    @pl.when(i == 0)
    def _():
        fetch(0, 0)

    @pl.when(i + 1 < pl.num_programs(0))
    def _():
        fetch(i + 1, 1 - slot)

    pltpu.make_async_copy(y0_ref.at[slot], y0_ref.at[slot], sem.at[slot]).wait()
    pltpu.make_async_copy(y1_ref.at[slot], y1_ref.at[slot], sem.at[slot]).wait()

    y0 = y0_ref.at[slot]
    y1 = y1_ref.at[slot]
    gate_rows = jnp.concatenate([gate_ref[...], jnp.zeros((SUBLANES - 2, rows), F32)], axis=0)
    gate_cols = gate_rows.T
    g0 = gate_cols[:, 0:1]
    g1 = gate_cols[:, 1:2]
    if n_b is None:
        x1 = x1_ref[...]
    else:
        x1 = x1_ref[...].reshape(rows, D_MODEL)
    parts = []
    for s in range(SLAB):
        moe = g0 * y0[pl.ds(s, rows, stride=SLAB), :] + g1 * y1[pl.ds(s, rows, stride=SLAB), :]
        parts.append(alpha * x1[:, s * LANES:(s + 1) * LANES] + moe)
    r = jnp.concatenate(parts, axis=1)
    mu = jnp.mean(r, axis=-1, keepdims=True)
    rc = r - mu
    var = jnp.mean(rc * rc, axis=-1, keepdims=True)
    x2 = rc * lax.rsqrt(var + LN_EPS) * lg_ref[...] + lb_ref[...]
    if n_b is None:
        x2_ref[...] = x2
    else:
        x2_ref[...] = x2.reshape(n_b, tt, D_MODEL)


def _combine(x1, gate_t, dest, ys, ln_g, ln_b, alpha, prompt):
    if prompt:
        n_b, seq, _ = x1.shape
        tt = TILE_T
        rows = n_b * tt
        nt = seq // tt
        x_spec = pl.BlockSpec((n_b, tt, D_MODEL), lambda i, d: (0, i, 0))
        kern = functools.partial(_combine_kernel, n_b=n_b, tt=tt, alpha=alpha)
    else:
        rows = 256
        nt = x1.shape[0] // rows
        x_spec = pl.BlockSpec((rows, D_MODEL), lambda i, d: (i, 0))
        kern = functools.partial(_combine_kernel, n_b=None, tt=None, alpha=alpha)
    return pl.pallas_call(
        kern,
        out_shape=jax.ShapeDtypeStruct(x1.shape, F32),
        grid_spec=pltpu.PrefetchScalarGridSpec(
            num_scalar_prefetch=1,
            grid=(nt,),
            in_specs=[
                x_spec,
                pl.BlockSpec((2, rows), lambda i, d: (0, i)),
                pl.BlockSpec((1, D_MODEL), lambda i, d: (0, 0)),
                pl.BlockSpec((1, D_MODEL), lambda i, d: (0, 0)),
                pl.BlockSpec(memory_space=pl.ANY),
            ],
            out_specs=x_spec,
            scratch_shapes=[
                pltpu.VMEM((2, rows * SLAB, LANES), F32),
                pltpu.VMEM((2, rows * SLAB, LANES), F32),
                pltpu.SemaphoreType.DMA((2,)),
            ],
        ),
        compiler_params=_cparams(("arbitrary",)),
        name="moe_combine_prompt" if prompt else "moe_combine_sample",
    )(dest, x1, gate_t, ln_g, ln_b, ys)


def _moe_plan(eid_t, n_blocks):
    e_flat = eid_t.T.reshape(-1)
    onehot = (e_flat[:, None] == jnp.arange(N_EXPERTS, dtype=I32)[None, :]).astype(I32)
    csum = jnp.cumsum(onehot, axis=0)
    rank = jnp.sum(csum * onehot, axis=1) - 1
    counts = csum[-1]
    padded = (counts + MOE_BLOCK - 1) // MOE_BLOCK * MOE_BLOCK
    pad_end = jnp.cumsum(padded)
    pad_start = pad_end - padded
    dest = (jnp.sum(onehot * pad_start[None, :], axis=1) + rank).astype(I32)
    n_used = (pad_end[-1] // MOE_BLOCK).astype(I32)
    first_slot = jnp.arange(n_blocks, dtype=I32) * MOE_BLOCK
    blk_exp = jnp.minimum(jnp.sum((first_slot[:, None] >= pad_end[None, :]).astype(I32), axis=1), N_EXPERTS - 1)
    last_exp = jnp.take(blk_exp, jnp.maximum(n_used - 1, 0))
    blk_exp = jnp.where(jnp.arange(n_blocks) < n_used, blk_exp, last_exp).astype(I32)
    pad_bounds = jnp.concatenate([pad_start + counts, pad_end]).astype(I32)
    return dest, pad_bounds, blk_exp, n_used.reshape(1)


def kernel(x_prompt, x_sample, cache_k, cache_v, state_ssm_re, state_ssm_im, page_table, w_in, w_out, attn_norm_g, ssm_norm_g, ssm_a_re, ssm_a_im, ssm_log_dt, ssm_b_re, ssm_b_im, ssm_c_re, ssm_c_im, ssm_d, ssm_w_glu, ln1_g, ln1_b, router_w, router_b, moe_w_gate, moe_w_up, moe_w_down, ln2_g, ln2_b):
    depth = w_in.shape[0]
    n_b, seq, _ = x_prompt.shape
    n_dec, t_new, _ = x_sample.shape
    n_pool = cache_k.shape[1]
    gq = N_HEADS // N_KV_HEADS
    alpha = (2 * depth) ** 0.25
    n_states = SSM_GROUPS * SSM_STATE
    n_sample = n_dec * t_new
    n_tok = n_b * seq + n_sample
    n_blocks = -(-(2 * n_tok + N_EXPERTS * (MOE_BLOCK - 1)) // MOE_BLOCK)
    n_slots = n_blocks * MOE_BLOCK

    slopes = 2.0 ** (-8.0 * jnp.arange(1, N_HEADS + 1, dtype=F32) / N_HEADS)
    cache_k_flat = cache_k.reshape(-1, HEAD_DIM)
    cache_v_flat = cache_v.reshape(-1, HEAD_DIM)
    router_wt = router_w.T
    router_bc = router_b.reshape(N_EXPERTS, 1)
    zeros_state = jnp.zeros((n_b, n_states), F32)
    w_in16 = w_in.astype(BF16)
    w_glu16 = ssm_w_glu.astype(BF16)
    w_out16 = w_out.astype(BF16)
    w_gate = moe_w_gate.reshape(depth * N_EXPERTS, D_MODEL, D_FF)
    w_up = moe_w_up.reshape(depth * N_EXPERTS, D_MODEL, D_FF)
    w_down = moe_w_down.reshape(depth * N_EXPERTS, D_FF, D_MODEL)

    xp = x_prompt
    xs = x_sample.transpose(1, 0, 2).reshape(n_sample, D_MODEL)
    outs = {k: [] for k in ("kp", "vp", "hrp", "hip", "ks", "vs", "hrs", "his")}
    prm = jax.vmap(_ssm_params)(ssm_a_re, ssm_a_im, ssm_log_dt, ssm_b_re, ssm_b_im, ssm_c_re, ssm_c_im, ssm_d)
    prm = {k: (v.reshape((depth * N_SSM_CHUNKS,) + v.shape[2:]) if v.ndim == 4 else v.reshape(1, -1))
           for k, v in prm.items()}
    for l in range(depth):
        lw = {
            "w_glu": w_glu16, "w_out": w_out16,
            "attn_g": attn_norm_g[l].reshape(1, -1), "ssm_g": ssm_norm_g[l].reshape(1, -1),
            "ln1_g": ln1_g[l].reshape(1, -1), "ln1_b": ln1_b[l].reshape(1, -1),
            "router_wt": router_wt, "router_b": router_bc,
        }
        q_p, k_p, v_p, u_p = _in_proj_prompt(xp, w_in16, l)
        att_p = _attn_prompt(q_p, k_p, v_p, slopes)
        y_p, hr_p, hi_p = _ssm(u_p, zeros_state, zeros_state, prm, l, nb=n_b, tc=256)

        h_s = _in_proj_sample(xs, w_in16, l)
        q_s = h_s[:, :ATTN_WIDTH].reshape(t_new, n_dec, N_HEADS, HEAD_DIM)
        q_s = q_s.transpose(1, 2, 0, 3).reshape(n_dec, N_HEADS * t_new, HEAD_DIM)
        k_s = h_s[:, ATTN_WIDTH:ATTN_WIDTH + KV_WIDTH].reshape(t_new, n_dec, N_KV_HEADS, HEAD_DIM)
        v_s = h_s[:, ATTN_WIDTH + KV_WIDTH:ATTN_WIDTH + 2 * KV_WIDTH].reshape(t_new, n_dec, N_KV_HEADS, HEAD_DIM)
        att_s = _attn_sample(q_s, k_s.transpose(1, 2, 0, 3), v_s.transpose(1, 2, 0, 3), cache_k_flat, cache_v_flat,
                             page_table, slopes, l, n_pool)
        att_s = att_s.reshape(n_dec, N_HEADS, t_new, HEAD_DIM).transpose(2, 0, 1, 3).reshape(n_sample, ATTN_WIDTH)
        u_s = h_s[:, ATTN_WIDTH + 2 * KV_WIDTH:].reshape(n_sample, N_SSM_CHUNKS, LANES).transpose(1, 0, 2)
        y_s, hr_s, hi_s = _ssm(u_s, state_ssm_re[l].reshape(n_dec, n_states), state_ssm_im[l].reshape(n_dec, n_states),
                               prm, l, nb=n_dec, tc=t_new)

        x1_p, eid_p, gate_p = _postmix(att_p, y_p, xp, lw, l, alpha, prompt=True)
        x1_s, eid_s, gate_s = _postmix(att_s, y_s, xs, lw, l, alpha, prompt=False)

        eid_t = jnp.concatenate([eid_p, eid_s], axis=1)
        dest, pad_bounds, blk_exp, n_used = _moe_plan(eid_t, n_blocks)
        xs_slots = _dispatch(x1_p, x1_s, dest, pad_bounds, n_slots)
        ys_slots = _ffn(xs_slots, w_gate, w_up, w_down, blk_exp, n_used, n_blocks, l)
        l2g = ln2_g[l].reshape(1, -1)
        l2b = ln2_b[l].reshape(1, -1)
        n_pa = 2 * n_b * seq
        xp = _combine(x1_p, gate_p, dest[:n_pa], ys_slots, l2g, l2b, alpha, prompt=True)
        xs = _combine(x1_s, gate_s, dest[n_pa:], ys_slots, l2g, l2b, alpha, prompt=False)

        outs["kp"].append(k_p.reshape(n_b, seq, N_KV_HEADS, HEAD_DIM))
        outs["vp"].append(v_p.reshape(n_b, seq, N_KV_HEADS, HEAD_DIM))
        outs["hrp"].append(hr_p.reshape(n_b, SSM_GROUPS, SSM_STATE))
        outs["hip"].append(hi_p.reshape(n_b, SSM_GROUPS, SSM_STATE))
        outs["ks"].append(k_s.transpose(1, 0, 2, 3))
        outs["vs"].append(v_s.transpose(1, 0, 2, 3))
        outs["hrs"].append(hr_s.reshape(n_dec, SSM_GROUPS, SSM_STATE))
        outs["his"].append(hi_s.reshape(n_dec, SSM_GROUPS, SSM_STATE))

    y_sample = xs.reshape(t_new, n_dec, D_MODEL).transpose(1, 0, 2)
    return (xp, y_sample,
            jnp.stack(outs["kp"]), jnp.stack(outs["vp"]), jnp.stack(outs["hrp"]), jnp.stack(outs["hip"]),
            jnp.stack(outs["ks"]), jnp.stack(outs["vs"]), jnp.stack(outs["hrs"]), jnp.stack(outs["his"]))
```

```python
import functools
import math

import jax
import jax.numpy as jnp
from jax import lax
from jax.experimental import pallas as pl
from jax.experimental.pallas import tpu as pltpu

F32 = jnp.float32
BF16 = jnp.bfloat16
I32 = jnp.int32
HIGHEST = lax.Precision.HIGHEST

D_MODEL = 2048
ATTN_WIDTH = 1024
SSM_WIDTH = 1024
HEAD_DIM = 128
N_HEADS = 8
N_KV_HEADS = 4
KV_WIDTH = N_KV_HEADS * HEAD_DIM
PROJ_WIDTH = ATTN_WIDTH + 2 * KV_WIDTH + SSM_WIDTH
MOBA_BLOCK = 256
MOBA_TOPK = 3
PAGE_SIZE = 128
SSM_GROUP_CH = 16
SSM_GROUPS = 64
SSM_STATE = 64
N_EXPERTS = 16
N_EXPERT_GROUPS = 4
EXPERTS_PER_GROUP = 4
D_FF = 1024
LN_EPS = 1e-5
RMS_EPS = 1e-6
NEG_INF = float("-inf")
Q_SCALE = HEAD_DIM ** -0.5

LANES = 128
SUBLANES = 8
VMEM_LIMIT = 56 * 1024 * 1024

TILE_T = 64
SLAB = D_MODEL // LANES
XSLAB = D_MODEL // (2 * LANES)
SSM_CHUNK = 8 * SSM_GROUP_CH
SSM_CHUNK_STATES = 8 * SSM_STATE
N_SSM_CHUNKS = SSM_WIDTH // SSM_CHUNK
MOE_BLOCK = 256
POSTMIX_SPLIT = 2
DMA_ISSUE_UNROLL = 8


def _cparams(sem, vmem=VMEM_LIMIT):
    return pltpu.CompilerParams(dimension_semantics=sem, vmem_limit_bytes=vmem)


def _in_proj_prompt_kernel(x_ref, w_ref, q_ref, k_ref, v_ref, u_ref, *, n_b, tt):
    rows = n_b * tt
    x = x_ref[...].reshape(rows, D_MODEL).astype(BF16)
    h = jnp.dot(x, w_ref[...], preferred_element_type=F32)
    q_ref[...] = h[:, :ATTN_WIDTH].reshape(n_b, tt, ATTN_WIDTH)
    for b in range(n_b):
        for g in range(N_KV_HEADS):
            k0 = ATTN_WIDTH + g * HEAD_DIM
            v0 = ATTN_WIDTH + KV_WIDTH + g * HEAD_DIM
            k_ref[b, pl.ds(g, tt, stride=N_KV_HEADS), :] = h[b * tt:(b + 1) * tt, k0:k0 + HEAD_DIM]
            v_ref[b, pl.ds(g, tt, stride=N_KV_HEADS), :] = h[b * tt:(b + 1) * tt, v0:v0 + HEAD_DIM]
    u0 = ATTN_WIDTH + 2 * KV_WIDTH
    for c in range(N_SSM_CHUNKS):
        for b in range(n_b):
            u_ref[c, pl.ds(b, tt, stride=n_b), :] = h[b * tt:(b + 1) * tt, u0 + c * LANES:u0 + (c + 1) * LANES]


def _in_proj_prompt(x, w16, layer):
    n_b, seq, _ = x.shape
    tt = TILE_T
    nt = seq // tt
    kern = functools.partial(_in_proj_prompt_kernel, n_b=n_b, tt=tt)
    return pl.pallas_call(
        kern,
        out_shape=(
            jax.ShapeDtypeStruct((n_b, seq, ATTN_WIDTH), F32),
            jax.ShapeDtypeStruct((n_b, seq * N_KV_HEADS, HEAD_DIM), F32),
            jax.ShapeDtypeStruct((n_b, seq * N_KV_HEADS, HEAD_DIM), F32),
            jax.ShapeDtypeStruct((N_SSM_CHUNKS, seq * n_b, LANES), F32),
        ),
        grid=(nt,),
        in_specs=[
            pl.BlockSpec((n_b, tt, D_MODEL), lambda i: (0, i, 0)),
            pl.BlockSpec((None, D_MODEL, PROJ_WIDTH), lambda i: (layer, 0, 0), pipeline_mode=pl.Buffered(1)),
        ],
        out_specs=(
            pl.BlockSpec((n_b, tt, ATTN_WIDTH), lambda i: (0, i, 0)),
            pl.BlockSpec((n_b, tt * N_KV_HEADS, HEAD_DIM), lambda i: (0, i, 0)),
            pl.BlockSpec((n_b, tt * N_KV_HEADS, HEAD_DIM), lambda i: (0, i, 0)),
            pl.BlockSpec((N_SSM_CHUNKS, tt * n_b, LANES), lambda i: (0, i, 0)),
        ),
        compiler_params=_cparams(("arbitrary",)),
        name="in_proj_prompt",
    )(x, w16)


def _matmul_kernel(x_ref, w_ref, o_ref):
    o_ref[...] = jnp.dot(x_ref[...].astype(BF16), w_ref[...], preferred_element_type=F32)


def _in_proj_sample(x, w16, layer):
    m, k = x.shape
    n = w16.shape[2]
    tm = 256
    return pl.pallas_call(
        _matmul_kernel,
        out_shape=jax.ShapeDtypeStruct((m, n), F32),
        grid=(m // tm,),
        in_specs=[pl.BlockSpec((tm, k), lambda i: (i, 0)), pl.BlockSpec((None, k, n), lambda i: (layer, 0, 0))],
        out_specs=pl.BlockSpec((tm, n), lambda i: (i, 0)),
        compiler_params=_cparams(("arbitrary",)),
        name="in_proj_sample",
    )(x, w16)


def _topk_mask(gate, valid, k, axis):
    nb = gate.shape[axis]
    ids = lax.broadcasted_iota(I32, gate.shape, axis)
    gm = jnp.where(valid, gate, NEG_INF)
    cnt = jnp.zeros(gate.shape, I32)
    for j in range(nb):
        gj = gm[:, j:j + 1] if axis == 1 else gm[j:j + 1, :]
        beats = (gj > gm) | ((gj == gm) & (j < ids))
        cnt = cnt + beats.astype(I32)
    return valid & (cnt < k)


def _attn_prompt_kernel(slopes_ref, q_ref, k_ref, v_ref, o_ref, kmean_ref, k16_ref, vt_ref, bias_ref, acc_ref, s_ref,
                        *, n_blk):
    g = pl.program_id(1)
    ti = pl.program_id(2)
    blk = MOBA_BLOCK
    pair = 2 * blk
    n_pair = n_blk // 2
    gq = N_HEADS // N_KV_HEADS
    width = gq * pair
    nt_dims = (((1,), (1,)), ((), ()))

    lane = lax.broadcasted_iota(I32, (1, width), 1)
    qloc = lane % pair
    slope_vec = jnp.zeros((1, width), F32)
    for hh in range(gq):
        slope_vec = jnp.where(lane // pair == hh, slopes_ref[g * gq + hh], slope_vec)

    @pl.when(ti == 0)
    def _():
        for j in range(n_blk):
            rows_j = pl.ds(j * blk * N_KV_HEADS + g, blk, stride=N_KV_HEADS)
            kb = k_ref[rows_j, :]
            kmean_ref[j:j + 1, :] = jnp.sum(kb, axis=0, keepdims=True) * (1.0 / blk)
            k16_ref[j * blk:(j + 1) * blk, :] = kb.astype(BF16)
            vt_ref[j // 2, :, (j % 2) * blk:(j % 2 + 1) * blk] = v_ref[rows_j, :].T.astype(BF16)
        pair_key = lax.broadcasted_iota(I32, (pair, width), 0)
        bias_ref[...] = -slope_vec * (qloc - pair_key).astype(F32)

    q_all = jnp.concatenate([q_ref[:, hh * HEAD_DIM:(hh + 1) * HEAD_DIM] for hh in range(gq)], axis=0) * Q_SCALE
    q16 = q_all.astype(BF16)
    blk_ids = lax.broadcasted_iota(I32, (n_blk, width), 0)
    own_blk = 2 * ti + qloc // blk
    gate_t = lax.dot_general(kmean_ref[...], q_all, nt_dims, precision=HIGHEST, preferred_element_type=F32)
    sel = _topk_mask(gate_t, blk_ids < own_blk, MOBA_TOPK, axis=0).astype(F32)

    def sel_row(j):
        return jnp.sum(jnp.where(blk_ids == j, sel, 0.0), axis=0, keepdims=True) > 0.5

    def pair_scores(t):
        off = pl.multiple_of(t * pair, pair)
        return lax.dot_general(k16_ref[pl.ds(off, pair), :], q16, nt_dims, preferred_element_type=F32)

    def softmax_pair(s, vis0, vis1, m_prev, cst):
        h0 = jnp.where(vis0, s[0:blk, :] + bias_ref[0:blk, :], NEG_INF)
        h1 = jnp.where(vis1, s[blk:pair, :] + bias_ref[blk:pair, :], NEG_INF)
        mx = jnp.maximum(jnp.max(h0, axis=0, keepdims=True), jnp.max(h1, axis=0, keepdims=True)) + cst
        m_new = mx if m_prev is None else jnp.maximum(m_prev, mx)
        off = m_new - cst
        p0 = jnp.exp(h0 - off)
        p1 = jnp.exp(h1 - off)
        psum = jnp.sum(p0, axis=0, keepdims=True) + jnp.sum(p1, axis=0, keepdims=True)
        return m_new, jnp.concatenate([p0, p1], axis=0).astype(BF16), psum

    key_id = lax.broadcasted_iota(I32, (blk, width), 0)
    vis_top = ((qloc < blk) & (key_id <= qloc)) | ((qloc >= blk) & sel_row(2 * ti))
    vis_bot = key_id + blk <= qloc
    m0, p, l0 = softmax_pair(pair_scores(ti), vis_top, vis_bot, None, jnp.zeros((1, width), F32))
    acc_ref[...] = jnp.dot(vt_ref[ti], p, preferred_element_type=F32)

    s_ref[0] = pair_scores(0)

    def body(t, carry):
        m, l = carry
        s = s_ref[t % 2]
        s_ref[(t + 1) % 2] = pair_scores(jnp.minimum(t + 1, n_pair - 1))
        cst = -slope_vec * ((ti - t) * pair).astype(F32)
        m_new, p, psum = softmax_pair(s, sel_row(2 * t), sel_row(2 * t + 1), m, cst)
        a = jnp.exp(m - m_new)
        acc_ref[...] = a * acc_ref[...] + jnp.dot(vt_ref[t], p, preferred_element_type=F32)
        return m_new, a * l + psum

    _, l = lax.fori_loop(0, ti, body, (m0, l0))
    o_t = acc_ref[...] / l
    for hh in range(gq):
        o_ref[:, hh * HEAD_DIM:(hh + 1) * HEAD_DIM] = o_t[:, hh * pair:(hh + 1) * pair].T


def _attn_prompt(q, k, v, slopes):
    n_b, seq, _ = q.shape
    n_blk = seq // MOBA_BLOCK
    assert n_blk % 2 == 0
    gq = N_HEADS // N_KV_HEADS
    pair = 2 * MOBA_BLOCK
    kern = functools.partial(_attn_prompt_kernel, n_blk=n_blk)
    return pl.pallas_call(
        kern,
        out_shape=jax.ShapeDtypeStruct((n_b, seq, ATTN_WIDTH), F32),
        grid_spec=pltpu.PrefetchScalarGridSpec(
            num_scalar_prefetch=1,
            grid=(n_b, N_KV_HEADS, n_blk // 2),
            in_specs=[
                pl.BlockSpec((None, pair, gq * HEAD_DIM), lambda b, g, i, s: (b, i, g)),
                pl.BlockSpec((None, seq * N_KV_HEADS, HEAD_DIM), lambda b, g, i, s: (b, 0, 0)),
                pl.BlockSpec((None, seq * N_KV_HEADS, HEAD_DIM), lambda b, g, i, s: (b, 0, 0)),
            ],
            out_specs=pl.BlockSpec((None, pair, gq * HEAD_DIM), lambda b, g, i, s: (b, i, g)),
            scratch_shapes=[
                pltpu.VMEM((n_blk, HEAD_DIM), F32),
                pltpu.VMEM((seq, HEAD_DIM), BF16),
                pltpu.VMEM((n_blk // 2, HEAD_DIM, pair), BF16),
                pltpu.VMEM((pair, gq * pair), F32),
                pltpu.VMEM((HEAD_DIM, gq * pair), F32),
                pltpu.VMEM((2, pair, gq * pair), F32),
            ],
        ),
        compiler_params=_cparams(("arbitrary", "arbitrary", "arbitrary")),
        name="moba_prompt",
    )(slopes, q, k, v)


def _attn_sample_kernel(pt_ref, slopes_ref, q_ref, kn_ref, vn_ref, *rest, n_pages, t_new, past):
    k_refs = rest[:n_pages]
    v_refs = rest[n_pages:2 * n_pages]
    o_ref = rest[2 * n_pages]
    gq = N_HEADS // N_KV_HEADS
    rows = N_HEADS * t_new
    ppb = MOBA_BLOCK // PAGE_SIZE
    n_blk = n_pages // ppb
    page_rows = PAGE_SIZE * N_KV_HEADS
    cols = ppb * page_rows
    nt_dims = (((1,), (1,)), ((), ()))

    row = lax.broadcasted_iota(I32, (rows, 1), 0)
    head = row // t_new
    g_row = head // gq
    t_row = row % t_new
    slope_row = jnp.zeros((rows, 1), F32)
    for h in range(N_HEADS):
        slope_row = jnp.where(head == h, slopes_ref[h], slope_row)
    col = lax.broadcasted_iota(I32, (1, cols), 1)
    g_col = col % N_KV_HEADS
    kpos_col = (col // page_rows) * PAGE_SIZE + (col % page_rows) // N_KV_HEADS
    bias0 = jnp.where(g_row == g_col, -slope_row * (past + t_row - kpos_col).astype(F32), NEG_INF)

    def per_head_rows(fn):
        out = jnp.zeros((rows, HEAD_DIM), F32)
        for g in range(N_KV_HEADS):
            out = jnp.where(g_row == g, fn(g), out)
        return out

    q = q_ref[...] * Q_SCALE
    q16 = q.astype(BF16)
    blk_lane = lax.broadcasted_iota(I32, (rows, n_blk), 1)

    gate = jnp.zeros((rows, n_blk), F32)
    k16, v16 = [], []
    for j in range(n_blk):
        kblk = jnp.concatenate([k_refs[ppb * j + p][...] for p in range(ppb)], axis=0)
        fold = jnp.sum(kblk.reshape(cols // SUBLANES, SUBLANES, HEAD_DIM), axis=0)
        ksum = fold[0:N_KV_HEADS, :] + fold[N_KV_HEADS:, :]
        kmean_rows = per_head_rows(lambda g: ksum[g:g + 1, :]) * (1.0 / MOBA_BLOCK)
        gate = jnp.where(blk_lane == j, jnp.sum(q * kmean_rows, axis=1, keepdims=True), gate)
        k16.append(kblk.astype(BF16))
        v16.extend(v_refs[ppb * j + p][...].astype(BF16) for p in range(ppb))
    sel = _topk_mask(gate, blk_lane >= 0, MOBA_TOPK, axis=1).astype(F32)

    bias = jnp.concatenate(
        [jnp.where(sel[:, j:j + 1] > 0.5, bias0 + slope_row * float(j * MOBA_BLOCK), NEG_INF) for j in range(n_blk)],
        axis=1)
    s = lax.dot_general(q16, jnp.concatenate(k16, axis=0), nt_dims, preferred_element_type=F32) + bias

    s_own, v_own = [], []
    for tk in range(t_new):
        kn_rows = per_head_rows(lambda g: kn_ref[g, tk:tk + 1, :])
        sv = jnp.sum(q * kn_rows, axis=1, keepdims=True) - slope_row * (t_row - tk).astype(F32)
        s_own.append(jnp.where(t_row >= tk, sv, NEG_INF))
        v_own.append(per_head_rows(lambda g: vn_ref[g, tk:tk + 1, :]))
    m = jnp.max(s, axis=1, keepdims=True)
    for tk in range(t_new):
        m = jnp.maximum(m, s_own[tk])
    p = jnp.exp(s - m)
    num = jnp.dot(p.astype(BF16), jnp.concatenate(v16, axis=0), preferred_element_type=F32)
    den = jnp.sum(p, axis=1, keepdims=True)
    for tk in range(t_new):
        pw = jnp.exp(s_own[tk] - m)
        num = num + pw * v_own[tk]
        den = den + pw
    o_ref[...] = num / den


def _attn_sample(q, k_new, v_new, cache_k_flat, cache_v_flat, page_table, slopes, layer, n_pool):
    n_dec, rows, _ = q.shape
    t_new = k_new.shape[2]
    n_pages = page_table.shape[1]
    past = n_pages * PAGE_SIZE
    assert past % MOBA_BLOCK == 0
    page_rows = PAGE_SIZE * N_KV_HEADS
    base = layer * n_pool
    pt_flat = page_table.reshape(-1)
    kern = functools.partial(_attn_sample_kernel, n_pages=n_pages, t_new=t_new, past=past)

    def page_spec(p):
        return pl.BlockSpec((page_rows, HEAD_DIM), lambda b, pt, sl: (base + pt[b * n_pages + p], 0))

    q_spec = pl.BlockSpec((None, rows, HEAD_DIM), lambda b, pt, sl: (b, 0, 0))
    new_spec = pl.BlockSpec((None, N_KV_HEADS, t_new, HEAD_DIM), lambda b, pt, sl: (b, 0, 0, 0))
    pages = [page_spec(p) for p in range(n_pages)]
    return pl.pallas_call(
        kern,
        out_shape=jax.ShapeDtypeStruct((n_dec, rows, HEAD_DIM), F32),
        grid_spec=pltpu.PrefetchScalarGridSpec(
            num_scalar_prefetch=2,
            grid=(n_dec,),
            in_specs=[q_spec, new_spec, new_spec] + pages + pages,
            out_specs=q_spec,
        ),
        compiler_params=_cparams(("arbitrary",)),
        name="moba_sample",
    )(pt_flat, slopes, q, k_new, v_new, *([cache_k_flat] * n_pages), *([cache_v_flat] * n_pages))


def _ssm_kernel(u_ref, bre_ref, bim_ref, cre_ref, cim_ref, are_ref, aim_ref, d_ref, s0r_ref, s0i_ref,
                y_ref, hr_ref, hi_ref, xr_ref, xi_ref, str_ref, sti_ref, *, nb, tc):
    ti = pl.program_id(1)

    @pl.when(ti == 0)
    def _():
        str_ref[...] = s0r_ref[...]
        sti_ref[...] = s0i_ref[...]

    u = u_ref[...]
    u16 = u.astype(BF16)
    xr_ref[...] = jnp.dot(u16, bre_ref[...], preferred_element_type=F32)
    xi_ref[...] = jnp.dot(u16, bim_ref[...], preferred_element_type=F32)
    a_re = jnp.broadcast_to(are_ref[...], (nb, SSM_CHUNK_STATES))
    a_im = jnp.broadcast_to(aim_ref[...], (nb, SSM_CHUNK_STATES))

    def step(t, carry):
        h_re, h_im = carry
        sl = pl.ds(pl.multiple_of(t * nb, nb), nb)
        n_re = (a_re * h_re - a_im * h_im) + xr_ref[sl, :]
        n_im = (a_re * h_im + a_im * h_re) + xi_ref[sl, :]
        xr_ref[sl, :] = n_re
        xi_ref[sl, :] = n_im
        return n_re, n_im

    h_re, h_im = lax.fori_loop(0, tc, step, (str_ref[...], sti_ref[...]))
    str_ref[...] = h_re
    sti_ref[...] = h_im
    y = (jnp.dot(xr_ref[...].astype(BF16), cre_ref[...], preferred_element_type=F32)
         - jnp.dot(xi_ref[...].astype(BF16), cim_ref[...], preferred_element_type=F32)
         + d_ref[...] * u)
    y_ref[...] = jax.nn.gelu(y)

    @pl.when(ti == pl.num_programs(1) - 1)
    def _():
        hr_ref[...] = h_re
        hi_ref[...] = h_im


def _ssm(u_chunks, s0_re, s0_im, prm, layer, nb, tc):
    c0 = layer * N_SSM_CHUNKS
    n_rows = u_chunks.shape[1]
    nt = n_rows // (tc * nb)
    rows = tc * nb
    kern = functools.partial(_ssm_kernel, nb=nb, tc=tc)
    cs = SSM_CHUNK_STATES
    n_states = SSM_GROUPS * SSM_STATE
    return pl.pallas_call(
        kern,
        out_shape=(
            jax.ShapeDtypeStruct((N_SSM_CHUNKS, n_rows, LANES), F32),
            jax.ShapeDtypeStruct((nb, n_states), F32),
            jax.ShapeDtypeStruct((nb, n_states), F32),
        ),
        grid=(N_SSM_CHUNKS, nt),
        in_specs=[
            pl.BlockSpec((None, rows, LANES), lambda c, t: (c, t, 0)),
            pl.BlockSpec((None, SSM_CHUNK, cs), lambda c, t: (c0 + c, 0, 0)),
            pl.BlockSpec((None, SSM_CHUNK, cs), lambda c, t: (c0 + c, 0, 0)),
            pl.BlockSpec((None, cs, SSM_CHUNK), lambda c, t: (c0 + c, 0, 0)),
            pl.BlockSpec((None, cs, SSM_CHUNK), lambda c, t: (c0 + c, 0, 0)),
            pl.BlockSpec((1, cs), lambda c, t: (0, c0 + c)),
            pl.BlockSpec((1, cs), lambda c, t: (0, c0 + c)),
            pl.BlockSpec((1, SSM_CHUNK), lambda c, t: (0, c0 + c)),
            pl.BlockSpec((nb, cs), lambda c, t: (0, c)),
            pl.BlockSpec((nb, cs), lambda c, t: (0, c)),
        ],
        out_specs=(
            pl.BlockSpec((None, rows, LANES), lambda c, t: (c, t, 0)),
            pl.BlockSpec((nb, cs), lambda c, t: (0, c)),
            pl.BlockSpec((nb, cs), lambda c, t: (0, c)),
        ),
        scratch_shapes=[
            pltpu.VMEM((rows, cs), F32),
            pltpu.VMEM((rows, cs), F32),
            pltpu.VMEM((nb, cs), F32),
            pltpu.VMEM((nb, cs), F32),
        ],
        compiler_params=_cparams(("arbitrary", "arbitrary")),
        name="s5_mixer",
    )(u_chunks, prm["bbd_re"], prm["bbd_im"], prm["cbd_re"], prm["cbd_im"], prm["ab_re"], prm["ab_im"],
      prm["d"], s0_re, s0_im)


def _ssm_params(a_re, a_im, log_dt, b_re, b_im, c_re, c_im, d_skip):
    dt = jnp.exp(log_dt)
    mag = jnp.exp(a_re * dt)
    ab_re = mag * jnp.cos(a_im * dt)
    ab_im = mag * jnp.sin(a_im * dt)
    den = a_re * a_re + a_im * a_im
    f_re = ((ab_re - 1.0) * a_re + ab_im * a_im) / den
    f_im = (ab_im * a_re - (ab_re - 1.0) * a_im) / den
    bb_re = f_re[..., None] * b_re - f_im[..., None] * b_im
    bb_im = f_re[..., None] * b_im + f_im[..., None] * b_re
    gpc = SSM_CHUNK // SSM_GROUP_CH
    eye = jnp.eye(gpc, dtype=F32)

    def b_blockdiag(bb):
        x = bb.reshape(N_SSM_CHUNKS, gpc, SSM_STATE, SSM_GROUP_CH)
        m = jnp.einsum("kgpc,gh->kgchp", x, eye)
        return m.reshape(N_SSM_CHUNKS, gpc * SSM_GROUP_CH, gpc * SSM_STATE).astype(BF16)

    def c_blockdiag(cc):
        x = cc.reshape(N_SSM_CHUNKS, gpc, SSM_GROUP_CH, SSM_STATE)
        m = jnp.einsum("kgcp,gh->kgphc", x, eye)
        return m.reshape(N_SSM_CHUNKS, gpc * SSM_STATE, gpc * SSM_GROUP_CH).astype(BF16)

    return {
        "bbd_re": b_blockdiag(bb_re), "bbd_im": b_blockdiag(bb_im),
        "cbd_re": c_blockdiag(c_re), "cbd_im": c_blockdiag(c_im),
        "ab_re": ab_re.reshape(1, -1), "ab_im": ab_im.reshape(1, -1),
        "d": d_skip.reshape(1, -1),
    }


def _rank_lt(vals, a):
    cnt = jnp.zeros(vals[a].shape, I32)
    for b in range(len(vals)):
        if b == a:
            continue
        before = (vals[b] >= vals[a]) if b < a else (vals[b] > vals[a])
        cnt = cnt + before.astype(I32)
    return cnt


def _router(x1, rwt_ref, rb_ref, eid_ref, gate_ref, cols):
    logits = lax.dot_general(rwt_ref[...], x1, (((1,), (1,)), ((), ())), precision=HIGHEST,
                             preferred_element_type=F32)
    mx = jnp.max(logits, axis=0, keepdims=True)
    ex = jnp.exp(logits - mx)
    probs = ex / jnp.sum(ex, axis=0, keepdims=True)
    biased = probs + rb_ref[...]
    prow = [probs[e:e + 1, :] for e in range(N_EXPERTS)]
    brow = [biased[e:e + 1, :] for e in range(N_EXPERTS)]
    ranks = []
    gscore = []
    for gi in range(N_EXPERT_GROUPS):
        vals = brow[gi * EXPERTS_PER_GROUP:(gi + 1) * EXPERTS_PER_GROUP]
        rk = [_rank_lt(vals, a) for a in range(EXPERTS_PER_GROUP)]
        ranks.append(rk)
        sc = jnp.zeros(vals[0].shape, F32)
        for a in range(EXPERTS_PER_GROUP):
            sc = sc + jnp.where(rk[a] < 2, vals[a], 0.0)
        gscore.append(sc)
    e0 = jnp.zeros(prow[0].shape, I32)
    e1 = jnp.zeros(prow[0].shape, I32)
    p0 = jnp.zeros(prow[0].shape, F32)
    p1 = jnp.zeros(prow[0].shape, F32)
    for gi in range(N_EXPERT_GROUPS):
        chosen = _rank_lt(gscore, gi) == 0
        for a in range(EXPERTS_PER_GROUP):
            e = gi * EXPERTS_PER_GROUP + a
            first = chosen & (ranks[gi][a] == 0)
            second = chosen & (ranks[gi][a] == 1)
            e0 = jnp.where(first, e, e0)
            e1 = jnp.where(second, e, e1)
            p0 = jnp.where(first, prow[e], p0)
            p1 = jnp.where(second, prow[e], p1)
    tot = p0 + p1
    eid_ref[0:1, cols] = e0
    eid_ref[1:2, cols] = e1
    gate_ref[0:1, cols] = p0 / tot
    gate_ref[1:2, cols] = p1 / tot


def _postmix_kernel(att_ref, y_ref, x_ref, wglu_ref, wout_ref, ag_ref, sg_ref, lg_ref, lb_ref, rwt_ref, rb_ref,
                    x1_ref, eid_ref, gate_ref, ybuf_ref, *, n_b, tt, alpha):
    rows = x1_ref.shape[0] if n_b is None else n_b * tt
    if n_b is None:
        att = att_ref[...]
        x = x_ref[...]
        for c in range(N_SSM_CHUNKS):
            ybuf_ref[:, c * LANES:(c + 1) * LANES] = y_ref[c]
    else:
        for c in range(N_SSM_CHUNKS):
            for b in range(n_b):
                ybuf_ref[b * tt:(b + 1) * tt, c * LANES:(c + 1) * LANES] = y_ref[c, pl.ds(b, tt, stride=n_b), :]
    z_all = jnp.dot(ybuf_ref[...].astype(BF16), wglu_ref[...], preferred_element_type=F32)
    hrows = rows // POSTMIX_SPLIT
    for hf in range(POSTMIX_SPLIT):
        r0 = hf * hrows
        if n_b is None:
            att_h = att[r0:r0 + hrows, :]
            x_h = x[r0:r0 + hrows, :]
        else:
            att_h = att_ref[hf * (n_b // POSTMIX_SPLIT):(hf + 1) * (n_b // POSTMIX_SPLIT)].reshape(hrows, ATTN_WIDTH)
            x_h = x_ref[hf * (n_b // POSTMIX_SPLIT):(hf + 1) * (n_b // POSTMIX_SPLIT)].reshape(hrows, D_MODEL)
        y = ybuf_ref[r0:r0 + hrows, :]
        sg = y * jax.nn.sigmoid(z_all[r0:r0 + hrows, :])
        ssm_n = sg * lax.rsqrt(jnp.mean(sg * sg, axis=-1, keepdims=True) + RMS_EPS) * sg_ref[...]
        att_n = att_h * lax.rsqrt(jnp.mean(att_h * att_h, axis=-1, keepdims=True) + RMS_EPS) * ag_ref[...]
        mix = (jnp.dot(att_n.astype(BF16), wout_ref[0:ATTN_WIDTH, :], preferred_element_type=F32)
               + jnp.dot(ssm_n.astype(BF16), wout_ref[ATTN_WIDTH:, :], preferred_element_type=F32))
        r = alpha * x_h + mix
        mu = jnp.mean(r, axis=-1, keepdims=True)
        rc = r - mu
        var = jnp.mean(rc * rc, axis=-1, keepdims=True)
        x1 = rc * lax.rsqrt(var + LN_EPS) * lg_ref[...] + lb_ref[...]
        if n_b is None:
            x1_ref[r0:r0 + hrows, :] = x1
        else:
            nbh = n_b // POSTMIX_SPLIT
            x1_ref[hf * nbh:(hf + 1) * nbh] = x1.reshape(nbh, tt, D_MODEL)
        _router(x1, rwt_ref, rb_ref, eid_ref, gate_ref, slice(r0, r0 + hrows))


def _postmix(att, y_chunks, x, lw, layer, alpha, prompt):
    if prompt:
        n_b, seq, _ = x.shape
        tt = TILE_T
        nt = seq // tt
        rows = n_b * tt
        n_tok = n_b * seq
        att_spec = pl.BlockSpec((n_b, tt, ATTN_WIDTH), lambda i: (0, i, 0))
        x_spec = pl.BlockSpec((n_b, tt, D_MODEL), lambda i: (0, i, 0))
        x1_shape = jax.ShapeDtypeStruct((n_b, seq, D_MODEL), F32)
        kern = functools.partial(_postmix_kernel, n_b=n_b, tt=tt, alpha=alpha)
    else:
        n_tok = x.shape[0]
        rows = 256
        nt = n_tok // rows
        att_spec = pl.BlockSpec((rows, ATTN_WIDTH), lambda i: (i, 0))
        x_spec = pl.BlockSpec((rows, D_MODEL), lambda i: (i, 0))
        x1_shape = jax.ShapeDtypeStruct((n_tok, D_MODEL), F32)
        kern = functools.partial(_postmix_kernel, n_b=None, tt=None, alpha=alpha)
    full = lambda shp: pl.BlockSpec(shp, lambda i: tuple(0 for _ in shp))
    return pl.pallas_call(
        kern,
        out_shape=(x1_shape, jax.ShapeDtypeStruct((2, n_tok), I32), jax.ShapeDtypeStruct((2, n_tok), F32)),
        grid=(nt,),
        in_specs=[
            att_spec,
            pl.BlockSpec((N_SSM_CHUNKS, rows, LANES), lambda i: (0, i, 0)),
            x_spec,
            pl.BlockSpec((None, SSM_WIDTH, SSM_WIDTH), lambda i: (layer, 0, 0), pipeline_mode=pl.Buffered(1)),
            pl.BlockSpec((None, D_MODEL, D_MODEL), lambda i: (layer, 0, 0), pipeline_mode=pl.Buffered(1)),
            full((1, ATTN_WIDTH)),
            full((1, SSM_WIDTH)),
            full((1, D_MODEL)),
            full((1, D_MODEL)),
            full((N_EXPERTS, D_MODEL)),
            full((N_EXPERTS, 1)),
        ],
        out_specs=(x_spec, pl.BlockSpec((2, rows), lambda i: (0, i)), pl.BlockSpec((2, rows), lambda i: (0, i))),
        scratch_shapes=[pltpu.VMEM((rows, SSM_WIDTH), F32)],
        compiler_params=_cparams(("arbitrary",)),
        name="postmix_prompt" if prompt else "postmix_sample",
    )(att, y_chunks, x, lw["w_glu"], lw["w_out"], lw["attn_g"], lw["ssm_g"], lw["ln1_g"], lw["ln1_b"],
      lw["router_wt"], lw["router_b"])


def _slab_rows(row, slab):
    start = row * slab
    return pl.ds(start if isinstance(start, int) else pl.multiple_of(start, slab), slab)


def _slab_copy(src_ref, src_row, dst_ref, dst_row, sem, slab=SLAB):
    return pltpu.make_async_copy(src_ref.at[_slab_rows(src_row, slab), :], dst_ref.at[_slab_rows(dst_row, slab), :],
                                 sem)


def _pack_bf16_pair(a, b):
    hi = lax.bitcast_convert_type(a.astype(BF16).astype(F32), jnp.uint32)
    lo = lax.bitcast_convert_type(b.astype(BF16).astype(F32), jnp.uint32)
    return hi | (lo >> 16)


def _unpack_bf16_pair(w):
    a = lax.bitcast_convert_type(w & jnp.uint32(0xFFFF0000), F32)
    b = lax.bitcast_convert_type(w << 16, F32)
    return a.astype(BF16), b.astype(BF16)


def _dispatch_kernel(dest_ref, pad_ref, xp_ref, xs_ref, out_ref, slab_ref, sem, *, n_b, tt, n_prompt_tiles, n_blocks):
    i = pl.program_id(0)
    last = pl.num_programs(0) - 1
    rows = n_b * tt
    slot = i % 2
    slab = slab_ref.at[slot]

    half = D_MODEL // 2

    @pl.when(i < n_prompt_tiles)
    def _():
        for s in range(XSLAB):
            for b in range(n_b):
                slab[pl.ds(b * tt * XSLAB + s, tt, stride=XSLAB), :] = _pack_bf16_pair(
                    xp_ref[b, :, s * LANES:(s + 1) * LANES], xp_ref[b, :, half + s * LANES:half + (s + 1) * LANES])

    @pl.when(i >= n_prompt_tiles)
    def _():
        for s in range(XSLAB):
            slab[pl.ds(s, rows, stride=XSLAB), :] = _pack_bf16_pair(
                xs_ref[:, s * LANES:(s + 1) * LANES], xs_ref[:, half + s * LANES:half + (s + 1) * LANES])

    base = i * (2 * rows)

    def start(r, c):
        for k in range(2):
            _slab_copy(slab, r, out_ref, dest_ref[base + 2 * r + k], sem.at[slot], XSLAB).start()
        return c

    lax.fori_loop(0, rows, start, 0, unroll=DMA_ISSUE_UNROLL)

    def wait_tile(which):
        for _ in range(2):
            pltpu.make_async_copy(slab_ref.at[which], slab_ref.at[which], sem.at[which]).wait()

    @pl.when(i > 0)
    def _():
        wait_tile(1 - slot)

    @pl.when(i == last)
    def _():
        wait_tile(slot)
        slab[...] = jnp.zeros(slab.shape, jnp.uint32)
        zsem = sem.at[slot]

        def per_expert(e, c):
            lo = pad_ref[e]
            hi = pad_ref[N_EXPERTS + e]

            def zs(s, c2):
                _slab_copy(slab, 0, out_ref, s, zsem, XSLAB).start()
                return c2

            lax.fori_loop(lo, hi, zs, 0)

            def zw(s, c2):
                _slab_copy(slab, 0, out_ref, s, zsem, XSLAB).wait()
                return c2

            lax.fori_loop(lo, hi, zw, 0)
            return c

        lax.fori_loop(0, N_EXPERTS, per_expert, 0)

        blk_rows = MOE_BLOCK * XSLAB

        def block_copy(b):
            start = pl.multiple_of(b * blk_rows, blk_rows)
            return pltpu.make_async_copy(slab.at[pl.ds(0, blk_rows), :], out_ref.at[pl.ds(start, blk_rows), :], zsem)

        def zbs(b, c):
            block_copy(b).start()
            return c

        def zbw(b, c):
            block_copy(b).wait()
            return c

        first_unused = pad_ref[2 * N_EXPERTS - 1] // MOE_BLOCK
        lax.fori_loop(first_unused, n_blocks, zbs, 0)
        lax.fori_loop(first_unused, n_blocks, zbw, 0)


def _dispatch(x1_p, x1_s, dest, pad_bounds, n_slots):
    n_b, seq, _ = x1_p.shape
    tt = TILE_T
    rows = n_b * tt
    assert rows >= MOE_BLOCK and x1_s.shape[0] % rows == 0
    ntp = seq // tt
    nts = x1_s.shape[0] // rows
    kern = functools.partial(_dispatch_kernel, n_b=n_b, tt=tt, n_prompt_tiles=ntp, n_blocks=n_slots // MOE_BLOCK)
    return pl.pallas_call(
        kern,
        out_shape=jax.ShapeDtypeStruct((n_slots * XSLAB, LANES), jnp.uint32),
        grid_spec=pltpu.PrefetchScalarGridSpec(
            num_scalar_prefetch=2,
            grid=(ntp + nts,),
            in_specs=[
                pl.BlockSpec((n_b, tt, D_MODEL), lambda i, d, p: (0, jnp.minimum(i, ntp - 1), 0)),
                pl.BlockSpec((rows, D_MODEL), lambda i, d, p: (jnp.maximum(i - ntp, 0), 0)),
            ],
            out_specs=pl.BlockSpec(memory_space=pl.ANY),
            scratch_shapes=[
                pltpu.VMEM((2, rows * XSLAB, LANES), jnp.uint32),
                pltpu.SemaphoreType.DMA((2,)),
            ],
        ),
        compiler_params=_cparams(("arbitrary",)),
        name="moe_dispatch",
    )(dest, pad_bounds, x1_p, x1_s)


def _new_expert(bexp_ref, i):
    return (i == 0) | (bexp_ref[i] != bexp_ref[jnp.maximum(i - 1, 0)])


def _ffn_up_kernel(bexp_ref, nused_ref, xs_ref, wg_ref, wu_ref, h_ref, xb_ref, wg16_ref, wu16_ref):
    i = pl.program_id(0)
    rows = MOE_BLOCK

    @pl.when(_new_expert(bexp_ref, i))
    def _():
        wg16_ref[...] = wg_ref[...].astype(BF16)
        wu16_ref[...] = wu_ref[...].astype(BF16)

    @pl.when(i < nused_ref[0])
    def _():
        half = D_MODEL // 2
        for s in range(XSLAB):
            a, b = _unpack_bf16_pair(xs_ref[pl.ds(s, rows, stride=XSLAB), :])
            xb_ref[:, s * LANES:(s + 1) * LANES] = a
            xb_ref[:, half + s * LANES:half + (s + 1) * LANES] = b
        xb = xb_ref[...]
        g = jnp.dot(xb, wg16_ref[...], preferred_element_type=F32)
        u = jnp.dot(xb, wu16_ref[...], preferred_element_type=F32)
        h_ref[...] = (g * jax.nn.sigmoid(g) * u).astype(BF16)

    @pl.when(i >= nused_ref[0])
    def _():
        h_ref[...] = jnp.zeros(h_ref.shape, BF16)


def _ffn_down_kernel(bexp_ref, nused_ref, h_ref, wd_ref, ys_ref, wd16_ref):
    i = pl.program_id(0)
    rows = MOE_BLOCK

    @pl.when(_new_expert(bexp_ref, i))
    def _():
        wd16_ref[...] = wd_ref[...].astype(BF16)

    @pl.when(i < nused_ref[0])
    def _():
        y = jnp.dot(h_ref[...], wd16_ref[...], preferred_element_type=F32)
        for s in range(SLAB):
            ys_ref[pl.ds(s, rows, stride=SLAB), :] = y[:, s * LANES:(s + 1) * LANES]

    @pl.when(i >= nused_ref[0])
    def _():
        ys_ref[...] = jnp.zeros(ys_ref.shape, F32)


def _ffn(xs, w_gate, w_up, w_down, blk_exp, n_used, n_blocks, layer):
    rows = MOE_BLOCK
    e0 = layer * N_EXPERTS
    n_slots = n_blocks * rows

    def live_map(i, be, nu):
        return (jnp.minimum(i, nu[0] - 1), 0)

    def w_map(i, be, nu):
        return (e0 + be[i], 0, 0)

    h = pl.pallas_call(
        _ffn_up_kernel,
        out_shape=jax.ShapeDtypeStruct((n_slots, D_FF), BF16),
        grid_spec=pltpu.PrefetchScalarGridSpec(
            num_scalar_prefetch=2,
            grid=(n_blocks,),
            in_specs=[
                pl.BlockSpec((rows * XSLAB, LANES), live_map),
                pl.BlockSpec((None, D_MODEL, D_FF), w_map),
                pl.BlockSpec((None, D_MODEL, D_FF), w_map),
            ],
            out_specs=pl.BlockSpec((rows, D_FF), lambda i, be, nu: (i, 0)),
            scratch_shapes=[
                pltpu.VMEM((rows, D_MODEL), BF16),
                pltpu.VMEM((D_MODEL, D_FF), BF16),
                pltpu.VMEM((D_MODEL, D_FF), BF16),
            ],
        ),
        compiler_params=_cparams(("arbitrary",)),
        name="moe_ffn_up",
    )(blk_exp, n_used, xs, w_gate, w_up)
    return pl.pallas_call(
        _ffn_down_kernel,
        out_shape=jax.ShapeDtypeStruct((n_slots * SLAB, LANES), F32),
        grid_spec=pltpu.PrefetchScalarGridSpec(
            num_scalar_prefetch=2,
            grid=(n_blocks,),
            in_specs=[
                pl.BlockSpec((rows, D_FF), live_map),
                pl.BlockSpec((None, D_FF, D_MODEL), w_map),
            ],
            out_specs=pl.BlockSpec((rows * SLAB, LANES), lambda i, be, nu: (i, 0)),
            scratch_shapes=[pltpu.VMEM((D_FF, D_MODEL), BF16)],
        ),
        compiler_params=_cparams(("arbitrary",)),
        name="moe_ffn_down",
    )(blk_exp, n_used, h, w_down)


def _combine_kernel(dest_ref, x1_ref, gate_ref, lg_ref, lb_ref, ys_ref, x2_ref, y0_ref, y1_ref, sem, *, n_b, tt, alpha):
    i = pl.program_id(0)
    rows = y0_ref.shape[1] // SLAB
    slot = i % 2

    def fetch(tile, which):
        base = tile * (2 * rows)

        def start(r, c):
            _slab_copy(ys_ref, dest_ref[base + 2 * r], y0_ref.at[which], r, sem.at[which]).start()
            _slab_copy(ys_ref, dest_ref[base + 2 * r + 1], y1_ref.at[which], r, sem.at[which]).start()
            return c

        lax.fori_loop(0, rows, start, 0, unroll=DMA_ISSUE_UNROLL)

    @pl.when(i == 0)
    def _():
        fetch(0, 0)

    @pl.when(i + 1 < pl.num_programs(0))
    def _():
        fetch(i + 1, 1 - slot)

    pltpu.make_async_copy(y0_ref.at[slot], y0_ref.at[slot], sem.at[slot]).wait()
    pltpu.make_async_copy(y1_ref.at[slot], y1_ref.at[slot], sem.at[slot]).wait()

    y0 = y0_ref.at[slot]
    y1 = y1_ref.at[slot]
    gate_rows = jnp.concatenate([gate_ref[...], jnp.zeros((SUBLANES - 2, rows), F32)], axis=0)
    gate_cols = gate_rows.T
    g0 = gate_cols[:, 0:1]
    g1 = gate_cols[:, 1:2]
    if n_b is None:
        x1 = x1_ref[...]
    else:
        x1 = x1_ref[...].reshape(rows, D_MODEL)
    parts = []
    for s in range(SLAB):
        moe = g0 * y0[pl.ds(s, rows, stride=SLAB), :] + g1 * y1[pl.ds(s, rows, stride=SLAB), :]
        parts.append(alpha * x1[:, s * LANES:(s + 1) * LANES] + moe)
    r = jnp.concatenate(parts, axis=1)
    mu = jnp.mean(r, axis=-1, keepdims=True)
    rc = r - mu
    var = jnp.mean(rc * rc, axis=-1, keepdims=True)
    x2 = rc * lax.rsqrt(var + LN_EPS) * lg_ref[...] + lb_ref[...]
    if n_b is None:
        x2_ref[...] = x2
    else:
        x2_ref[...] = x2.reshape(n_b, tt, D_MODEL)


def _combine(x1, gate_t, dest, ys, ln_g, ln_b, alpha, prompt):
    if prompt:
        n_b, seq, _ = x1.shape
        tt = TILE_T
        rows = n_b * tt
        nt = seq // tt
        x_spec = pl.BlockSpec((n_b, tt, D_MODEL), lambda i, d: (0, i, 0))
        kern = functools.partial(_combine_kernel, n_b=n_b, tt=tt, alpha=alpha)
    else:
        rows = 256
        nt = x1.shape[0] // rows
        x_spec = pl.BlockSpec((rows, D_MODEL), lambda i, d: (i, 0))
        kern = functools.partial(_combine_kernel, n_b=None, tt=None, alpha=alpha)
    return pl.pallas_call(
        kern,
        out_shape=jax.ShapeDtypeStruct(x1.shape, F32),
        grid_spec=pltpu.PrefetchScalarGridSpec(
            num_scalar_prefetch=1,
            grid=(nt,),
            in_specs=[
                x_spec,
                pl.BlockSpec((2, rows), lambda i, d: (0, i)),
                pl.BlockSpec((1, D_MODEL), lambda i, d: (0, 0)),
                pl.BlockSpec((1, D_MODEL), lambda i, d: (0, 0)),
                pl.BlockSpec(memory_space=pl.ANY),
            ],
            out_specs=x_spec,
            scratch_shapes=[
                pltpu.VMEM((2, rows * SLAB, LANES), F32),
                pltpu.VMEM((2, rows * SLAB, LANES), F32),
                pltpu.SemaphoreType.DMA((2,)),
            ],
        ),
        compiler_params=_cparams(("arbitrary",)),
        name="moe_combine_prompt" if prompt else "moe_combine_sample",
    )(dest, x1, gate_t, ln_g, ln_b, ys)


def _moe_plan(eid_t, n_blocks):
    e_flat = eid_t.T.reshape(-1)
    onehot = (e_flat[:, None] == jnp.arange(N_EXPERTS, dtype=I32)[None, :]).astype(I32)
    csum = jnp.cumsum(onehot, axis=0)
    rank = jnp.sum(csum * onehot, axis=1) - 1
    counts = csum[-1]
    padded = (counts + MOE_BLOCK - 1) // MOE_BLOCK * MOE_BLOCK
    pad_end = jnp.cumsum(padded)
    pad_start = pad_end - padded
    dest = (jnp.sum(onehot * pad_start[None, :], axis=1) + rank).astype(I32)
    n_used = (pad_end[-1] // MOE_BLOCK).astype(I32)
    first_slot = jnp.arange(n_blocks, dtype=I32) * MOE_BLOCK
    blk_exp = jnp.minimum(jnp.sum((first_slot[:, None] >= pad_end[None, :]).astype(I32), axis=1), N_EXPERTS - 1)
    last_exp = jnp.take(blk_exp, jnp.maximum(n_used - 1, 0))
    blk_exp = jnp.where(jnp.arange(n_blocks) < n_used, blk_exp, last_exp).astype(I32)
    pad_bounds = jnp.concatenate([pad_start + counts, pad_end]).astype(I32)
    return dest, pad_bounds, blk_exp, n_used.reshape(1)


def kernel(x_prompt, x_sample, cache_k, cache_v, state_ssm_re, state_ssm_im, page_table, w_in, w_out, attn_norm_g, ssm_norm_g, ssm_a_re, ssm_a_im, ssm_log_dt, ssm_b_re, ssm_b_im, ssm_c_re, ssm_c_im, ssm_d, ssm_w_glu, ln1_g, ln1_b, router_w, router_b, moe_w_gate, moe_w_up, moe_w_down, ln2_g, ln2_b):
    depth = w_in.shape[0]
    n_b, seq, _ = x_prompt.shape
    n_dec, t_new, _ = x_sample.shape
    n_pool = cache_k.shape[1]
    gq = N_HEADS // N_KV_HEADS
    alpha = (2 * depth) ** 0.25
    n_states = SSM_GROUPS * SSM_STATE
    n_sample = n_dec * t_new
    n_tok = n_b * seq + n_sample
    n_blocks = -(-(2 * n_tok + N_EXPERTS * (MOE_BLOCK - 1)) // MOE_BLOCK)
    n_slots = n_blocks * MOE_BLOCK

    slopes = 2.0 ** (-8.0 * jnp.arange(1, N_HEADS + 1, dtype=F32) / N_HEADS)
    cache_k_flat = cache_k.reshape(-1, HEAD_DIM)
    cache_v_flat = cache_v.reshape(-1, HEAD_DIM)
    router_wt = router_w.T
    router_bc = router_b.reshape(N_EXPERTS, 1)
    zeros_state = jnp.zeros((n_b, n_states), F32)
    w_in16 = w_in.astype(BF16)
    w_glu16 = ssm_w_glu.astype(BF16)
    w_out16 = w_out.astype(BF16)
    w_gate = moe_w_gate.reshape(depth * N_EXPERTS, D_MODEL, D_FF)
    w_up = moe_w_up.reshape(depth * N_EXPERTS, D_MODEL, D_FF)
    w_down = moe_w_down.reshape(depth * N_EXPERTS, D_FF, D_MODEL)

    xp = x_prompt
    xs = x_sample.transpose(1, 0, 2).reshape(n_sample, D_MODEL)
    outs = {k: [] for k in ("kp", "vp", "hrp", "hip", "ks", "vs", "hrs", "his")}
    prm = jax.vmap(_ssm_params)(ssm_a_re, ssm_a_im, ssm_log_dt, ssm_b_re, ssm_b_im, ssm_c_re, ssm_c_im, ssm_d)
    prm = {k: (v.reshape((depth * N_SSM_CHUNKS,) + v.shape[2:]) if v.ndim == 4 else v.reshape(1, -1))
           for k, v in prm.items()}
    for l in range(depth):
        lw = {
            "w_glu": w_glu16, "w_out": w_out16,
            "attn_g": attn_norm_g[l].reshape(1, -1), "ssm_g": ssm_norm_g[l].reshape(1, -1),
            "ln1_g": ln1_g[l].reshape(1, -1), "ln1_b": ln1_b[l].reshape(1, -1),
            "router_wt": router_wt, "router_b": router_bc,
        }
        q_p, k_p, v_p, u_p = _in_proj_prompt(xp, w_in16, l)
        att_p = _attn_prompt(q_p, k_p, v_p, slopes)
        y_p, hr_p, hi_p = _ssm(u_p, zeros_state, zeros_state, prm, l, nb=n_b, tc=256)

        h_s = _in_proj_sample(xs, w_in16, l)
        q_s = h_s[:, :ATTN_WIDTH].reshape(t_new, n_dec, N_HEADS, HEAD_DIM)
        q_s = q_s.transpose(1, 2, 0, 3).reshape(n_dec, N_HEADS * t_new, HEAD_DIM)
        k_s = h_s[:, ATTN_WIDTH:ATTN_WIDTH + KV_WIDTH].reshape(t_new, n_dec, N_KV_HEADS, HEAD_DIM)
        v_s = h_s[:, ATTN_WIDTH + KV_WIDTH:ATTN_WIDTH + 2 * KV_WIDTH].reshape(t_new, n_dec, N_KV_HEADS, HEAD_DIM)
        att_s = _attn_sample(q_s, k_s.transpose(1, 2, 0, 3), v_s.transpose(1, 2, 0, 3), cache_k_flat, cache_v_flat,
                             page_table, slopes, l, n_pool)
        att_s = att_s.reshape(n_dec, N_HEADS, t_new, HEAD_DIM).transpose(2, 0, 1, 3).reshape(n_sample, ATTN_WIDTH)
        u_s = h_s[:, ATTN_WIDTH + 2 * KV_WIDTH:].reshape(n_sample, N_SSM_CHUNKS, LANES).transpose(1, 0, 2)
        y_s, hr_s, hi_s = _ssm(u_s, state_ssm_re[l].reshape(n_dec, n_states), state_ssm_im[l].reshape(n_dec, n_states),
                               prm, l, nb=n_dec, tc=t_new)

        x1_p, eid_p, gate_p = _postmix(att_p, y_p, xp, lw, l, alpha, prompt=True)
        x1_s, eid_s, gate_s = _postmix(att_s, y_s, xs, lw, l, alpha, prompt=False)

        eid_t = jnp.concatenate([eid_p, eid_s], axis=1)
        dest, pad_bounds, blk_exp, n_used = _moe_plan(eid_t, n_blocks)
        xs_slots = _dispatch(x1_p, x1_s, dest, pad_bounds, n_slots)
        ys_slots = _ffn(xs_slots, w_gate, w_up, w_down, blk_exp, n_used, n_blocks, l)
        l2g = ln2_g[l].reshape(1, -1)
        l2b = ln2_b[l].reshape(1, -1)
        n_pa = 2 * n_b * seq
        xp = _combine(x1_p, gate_p, dest[:n_pa], ys_slots, l2g, l2b, alpha, prompt=True)
        xs = _combine(x1_s, gate_s, dest[n_pa:], ys_slots, l2g, l2b, alpha, prompt=False)

        outs["kp"].append(k_p.reshape(n_b, seq, N_KV_HEADS, HEAD_DIM))
        outs["vp"].append(v_p.reshape(n_b, seq, N_KV_HEADS, HEAD_DIM))
        outs["hrp"].append(hr_p.reshape(n_b, SSM_GROUPS, SSM_STATE))
        outs["hip"].append(hi_p.reshape(n_b, SSM_GROUPS, SSM_STATE))
        outs["ks"].append(k_s.transpose(1, 0, 2, 3))
        outs["vs"].append(v_s.transpose(1, 0, 2, 3))
        outs["hrs"].append(hr_s.reshape(n_dec, SSM_GROUPS, SSM_STATE))
        outs["his"].append(hi_s.reshape(n_dec, SSM_GROUPS, SSM_STATE))

    y_sample = xs.reshape(t_new, n_dec, D_MODEL).transpose(1, 0, 2)
    return (xp, y_sample,
            jnp.stack(outs["kp"]), jnp.stack(outs["vp"]), jnp.stack(outs["hrp"]), jnp.stack(outs["hip"]),
            jnp.stack(outs["ks"]), jnp.stack(outs["vs"]), jnp.stack(outs["hrs"]), jnp.stack(outs["his"]))
```

```python
import functools
import math

import jax
import jax.numpy as jnp
from jax import lax
from jax.experimental import pallas as pl
from jax.experimental.pallas import tpu as pltpu

F32 = jnp.float32
BF16 = jnp.bfloat16
I32 = jnp.int32
HIGHEST = lax.Precision.HIGHEST

D_MODEL = 2048
ATTN_WIDTH = 1024
SSM_WIDTH = 1024
HEAD_DIM = 128
N_HEADS = 8
N_KV_HEADS = 4
KV_WIDTH = N_KV_HEADS * HEAD_DIM
PROJ_WIDTH = ATTN_WIDTH + 2 * KV_WIDTH + SSM_WIDTH
MOBA_BLOCK = 256
MOBA_TOPK = 3
PAGE_SIZE = 128
SSM_GROUP_CH = 16
SSM_GROUPS = 64
SSM_STATE = 64
N_EXPERTS = 16
N_EXPERT_GROUPS = 4
EXPERTS_PER_GROUP = 4
D_FF = 1024
LN_EPS = 1e-5
RMS_EPS = 1e-6
NEG_INF = float("-inf")
Q_SCALE = HEAD_DIM ** -0.5

LANES = 128
SUBLANES = 8
VMEM_LIMIT = 56 * 1024 * 1024

TILE_T = 64
SLAB = D_MODEL // LANES
XSLAB = D_MODEL // (2 * LANES)
SSM_CHUNK = 8 * SSM_GROUP_CH
SSM_CHUNK_STATES = 8 * SSM_STATE
N_SSM_CHUNKS = SSM_WIDTH // SSM_CHUNK
MOE_BLOCK = 256
POSTMIX_SPLIT = 4
DMA_ISSUE_UNROLL = 16
COMBINE_ROWS = 256


def _cparams(sem, vmem=VMEM_LIMIT):
    return pltpu.CompilerParams(dimension_semantics=sem, vmem_limit_bytes=vmem)


def _in_proj_prompt_kernel(x_ref, w_ref, q_ref, k_ref, v_ref, u_ref, *, n_b, tt):
    rows = n_b * tt
    x = x_ref[...].reshape(rows, D_MODEL).astype(BF16)
    h = jnp.dot(x, w_ref[...], preferred_element_type=F32)
    q_ref[...] = h[:, :ATTN_WIDTH].reshape(n_b, tt, ATTN_WIDTH)
    for b in range(n_b):
        for g in range(N_KV_HEADS):
            k0 = ATTN_WIDTH + g * HEAD_DIM
            v0 = ATTN_WIDTH + KV_WIDTH + g * HEAD_DIM
            k_ref[b, pl.ds(g, tt, stride=N_KV_HEADS), :] = h[b * tt:(b + 1) * tt, k0:k0 + HEAD_DIM]
            v_ref[b, pl.ds(g, tt, stride=N_KV_HEADS), :] = h[b * tt:(b + 1) * tt, v0:v0 + HEAD_DIM]
    u0 = ATTN_WIDTH + 2 * KV_WIDTH
    for c in range(N_SSM_CHUNKS):
        for b in range(n_b):
            u_ref[c, pl.ds(b, tt, stride=n_b), :] = h[b * tt:(b + 1) * tt, u0 + c * LANES:u0 + (c + 1) * LANES]


def _in_proj_prompt(x, w16, layer):
    n_b, seq, _ = x.shape
    tt = TILE_T
    nt = seq // tt
    kern = functools.partial(_in_proj_prompt_kernel, n_b=n_b, tt=tt)
    return pl.pallas_call(
        kern,
        out_shape=(
            jax.ShapeDtypeStruct((n_b, seq, ATTN_WIDTH), F32),
            jax.ShapeDtypeStruct((n_b, seq * N_KV_HEADS, HEAD_DIM), F32),
            jax.ShapeDtypeStruct((n_b, seq * N_KV_HEADS, HEAD_DIM), F32),
            jax.ShapeDtypeStruct((N_SSM_CHUNKS, seq * n_b, LANES), F32),
        ),
        grid=(nt,),
        in_specs=[
            pl.BlockSpec((n_b, tt, D_MODEL), lambda i: (0, i, 0)),
            pl.BlockSpec((None, D_MODEL, PROJ_WIDTH), lambda i: (layer, 0, 0), pipeline_mode=pl.Buffered(1)),
        ],
        out_specs=(
            pl.BlockSpec((n_b, tt, ATTN_WIDTH), lambda i: (0, i, 0)),
            pl.BlockSpec((n_b, tt * N_KV_HEADS, HEAD_DIM), lambda i: (0, i, 0)),
            pl.BlockSpec((n_b, tt * N_KV_HEADS, HEAD_DIM), lambda i: (0, i, 0)),
            pl.BlockSpec((N_SSM_CHUNKS, tt * n_b, LANES), lambda i: (0, i, 0)),
        ),
        compiler_params=_cparams(("arbitrary",)),
        name="in_proj_prompt",
    )(x, w16)


def _matmul_kernel(x_ref, w_ref, o_ref):
    o_ref[...] = jnp.dot(x_ref[...].astype(BF16), w_ref[...], preferred_element_type=F32)


def _in_proj_sample(x, w16, layer):
    m, k = x.shape
    n = w16.shape[2]
    tm = 256
    return pl.pallas_call(
        _matmul_kernel,
        out_shape=jax.ShapeDtypeStruct((m, n), F32),
        grid=(m // tm,),
        in_specs=[pl.BlockSpec((tm, k), lambda i: (i, 0)), pl.BlockSpec((None, k, n), lambda i: (layer, 0, 0))],
        out_specs=pl.BlockSpec((tm, n), lambda i: (i, 0)),
        compiler_params=_cparams(("arbitrary",)),
        name="in_proj_sample",
    )(x, w16)


def _topk_mask(gate, valid, k, axis):
    nb = gate.shape[axis]
    ids = lax.broadcasted_iota(I32, gate.shape, axis)
    gm = jnp.where(valid, gate, NEG_INF)
    cnt = jnp.zeros(gate.shape, I32)
    for j in range(nb):
        gj = gm[:, j:j + 1] if axis == 1 else gm[j:j + 1, :]
        beats = (gj > gm) | ((gj == gm) & (j < ids))
        cnt = cnt + beats.astype(I32)
    return valid & (cnt < k)


def _attn_prompt_kernel(slopes_ref, q_ref, k_ref, v_ref, o_ref, kmean_ref, k16_ref, vt_ref, bias_ref, acc_ref, s_ref,
                        *, n_blk):
    g = pl.program_id(1)
    ti = pl.program_id(2)
    blk = MOBA_BLOCK
    pair = 2 * blk
    n_pair = n_blk // 2
    gq = N_HEADS // N_KV_HEADS
    width = gq * pair
    nt_dims = (((1,), (1,)), ((), ()))

    lane = lax.broadcasted_iota(I32, (1, width), 1)
    qloc = lane % pair
    slope_vec = jnp.zeros((1, width), F32)
    for hh in range(gq):
        slope_vec = jnp.where(lane // pair == hh, slopes_ref[g * gq + hh], slope_vec)

    @pl.when(ti == 0)
    def _():
        for j in range(n_blk):
            rows_j = pl.ds(j * blk * N_KV_HEADS + g, blk, stride=N_KV_HEADS)
            kb = k_ref[rows_j, :]
            kmean_ref[j:j + 1, :] = jnp.sum(kb, axis=0, keepdims=True) * (1.0 / blk)
            k16_ref[j * blk:(j + 1) * blk, :] = kb.astype(BF16)
            vt_ref[j // 2, :, (j % 2) * blk:(j % 2 + 1) * blk] = v_ref[rows_j, :].T.astype(BF16)
        pair_key = lax.broadcasted_iota(I32, (pair, width), 0)
        bias_ref[...] = -slope_vec * (qloc - pair_key).astype(F32)

    q_all = jnp.concatenate([q_ref[:, hh * HEAD_DIM:(hh + 1) * HEAD_DIM] for hh in range(gq)], axis=0) * Q_SCALE
    q16 = q_all.astype(BF16)
    blk_ids = lax.broadcasted_iota(I32, (n_blk, width), 0)
    own_blk = 2 * ti + qloc // blk
    gate_t = lax.dot_general(kmean_ref[...], q_all, nt_dims, precision=HIGHEST, preferred_element_type=F32)
    sel = _topk_mask(gate_t, blk_ids < own_blk, MOBA_TOPK, axis=0).astype(F32)

    def sel_row(j):
        return jnp.sum(jnp.where(blk_ids == j, sel, 0.0), axis=0, keepdims=True) > 0.5

    def pair_scores(t):
        off = pl.multiple_of(t * pair, pair)
        return lax.dot_general(k16_ref[pl.ds(off, pair), :], q16, nt_dims, preferred_element_type=F32)

    def softmax_pair(s, vis0, vis1, m_prev, cst):
        h0 = jnp.where(vis0, s[0:blk, :] + bias_ref[0:blk, :], NEG_INF)
        h1 = jnp.where(vis1, s[blk:pair, :] + bias_ref[blk:pair, :], NEG_INF)
        mx = jnp.maximum(jnp.max(h0, axis=0, keepdims=True), jnp.max(h1, axis=0, keepdims=True)) + cst
        m_new = mx if m_prev is None else jnp.maximum(m_prev, mx)
        off = m_new - cst
        p0 = jnp.exp(h0 - off)
        p1 = jnp.exp(h1 - off)
        psum = jnp.sum(p0, axis=0, keepdims=True) + jnp.sum(p1, axis=0, keepdims=True)
        return m_new, jnp.concatenate([p0, p1], axis=0).astype(BF16), psum

    key_id = lax.broadcasted_iota(I32, (blk, width), 0)
    vis_top = ((qloc < blk) & (key_id <= qloc)) | ((qloc >= blk) & sel_row(2 * ti))
    vis_bot = key_id + blk <= qloc
    m0, p, l0 = softmax_pair(pair_scores(ti), vis_top, vis_bot, None, jnp.zeros((1, width), F32))
    acc_ref[...] = jnp.dot(vt_ref[ti], p, preferred_element_type=F32)

    s_ref[0] = pair_scores(0)

    def body(t, carry):
        m, l = carry
        s = s_ref[t % 2]
        s_ref[(t + 1) % 2] = pair_scores(jnp.minimum(t + 1, n_pair - 1))
        cst = -slope_vec * ((ti - t) * pair).astype(F32)
        m_new, p, psum = softmax_pair(s, sel_row(2 * t), sel_row(2 * t + 1), m, cst)
        a = jnp.exp(m - m_new)
        acc_ref[...] = a * acc_ref[...] + jnp.dot(vt_ref[t], p, preferred_element_type=F32)
        return m_new, a * l + psum

    _, l = lax.fori_loop(0, ti, body, (m0, l0))
    o_t = acc_ref[...] / l
    for hh in range(gq):
        o_ref[:, hh * HEAD_DIM:(hh + 1) * HEAD_DIM] = o_t[:, hh * pair:(hh + 1) * pair].T


def _attn_prompt(q, k, v, slopes):
    n_b, seq, _ = q.shape
    n_blk = seq // MOBA_BLOCK
    assert n_blk % 2 == 0
    gq = N_HEADS // N_KV_HEADS
    pair = 2 * MOBA_BLOCK
    kern = functools.partial(_attn_prompt_kernel, n_blk=n_blk)
    return pl.pallas_call(
        kern,
        out_shape=jax.ShapeDtypeStruct((n_b, seq, ATTN_WIDTH), F32),
        grid_spec=pltpu.PrefetchScalarGridSpec(
            num_scalar_prefetch=1,
            grid=(n_b, N_KV_HEADS, n_blk // 2),
            in_specs=[
                pl.BlockSpec((None, pair, gq * HEAD_DIM), lambda b, g, i, s: (b, i, g)),
                pl.BlockSpec((None, seq * N_KV_HEADS, HEAD_DIM), lambda b, g, i, s: (b, 0, 0)),
                pl.BlockSpec((None, seq * N_KV_HEADS, HEAD_DIM), lambda b, g, i, s: (b, 0, 0)),
            ],
            out_specs=pl.BlockSpec((None, pair, gq * HEAD_DIM), lambda b, g, i, s: (b, i, g)),
            scratch_shapes=[
                pltpu.VMEM((n_blk, HEAD_DIM), F32),
                pltpu.VMEM((seq, HEAD_DIM), BF16),
                pltpu.VMEM((n_blk // 2, HEAD_DIM, pair), BF16),
                pltpu.VMEM((pair, gq * pair), F32),
                pltpu.VMEM((HEAD_DIM, gq * pair), F32),
                pltpu.VMEM((2, pair, gq * pair), F32),
            ],
        ),
        compiler_params=_cparams(("arbitrary", "arbitrary", "arbitrary")),
        name="moba_prompt",
    )(slopes, q, k, v)


def _attn_sample_kernel(pt_ref, slopes_ref, q_ref, kn_ref, vn_ref, *rest, n_pages, t_new, past):
    k_refs = rest[:n_pages]
    v_refs = rest[n_pages:2 * n_pages]
    o_ref = rest[2 * n_pages]
    gq = N_HEADS // N_KV_HEADS
    rows = N_HEADS * t_new
    ppb = MOBA_BLOCK // PAGE_SIZE
    n_blk = n_pages // ppb
    page_rows = PAGE_SIZE * N_KV_HEADS
    cols = ppb * page_rows
    nt_dims = (((1,), (1,)), ((), ()))

    row = lax.broadcasted_iota(I32, (rows, 1), 0)
    head = row // t_new
    g_row = head // gq
    t_row = row % t_new
    slope_row = jnp.zeros((rows, 1), F32)
    for h in range(N_HEADS):
        slope_row = jnp.where(head == h, slopes_ref[h], slope_row)
    col = lax.broadcasted_iota(I32, (1, cols), 1)
    g_col = col % N_KV_HEADS
    kpos_col = (col // page_rows) * PAGE_SIZE + (col % page_rows) // N_KV_HEADS
    bias0 = jnp.where(g_row == g_col, -slope_row * (past + t_row - kpos_col).astype(F32), NEG_INF)

    def per_head_rows(fn):
        out = jnp.zeros((rows, HEAD_DIM), F32)
        for g in range(N_KV_HEADS):
            out = jnp.where(g_row == g, fn(g), out)
        return out

    q = q_ref[...] * Q_SCALE
    q16 = q.astype(BF16)
    blk_lane = lax.broadcasted_iota(I32, (rows, n_blk), 1)

    gate = jnp.zeros((rows, n_blk), F32)
    k16, v16 = [], []
    for j in range(n_blk):
        kblk = jnp.concatenate([k_refs[ppb * j + p][...] for p in range(ppb)], axis=0)
        fold = jnp.sum(kblk.reshape(cols // SUBLANES, SUBLANES, HEAD_DIM), axis=0)
        ksum = fold[0:N_KV_HEADS, :] + fold[N_KV_HEADS:, :]
        kmean_rows = per_head_rows(lambda g: ksum[g:g + 1, :]) * (1.0 / MOBA_BLOCK)
        gate = jnp.where(blk_lane == j, jnp.sum(q * kmean_rows, axis=1, keepdims=True), gate)
        k16.append(kblk.astype(BF16))
        v16.extend(v_refs[ppb * j + p][...].astype(BF16) for p in range(ppb))
    sel = _topk_mask(gate, blk_lane >= 0, MOBA_TOPK, axis=1).astype(F32)

    bias = jnp.concatenate(
        [jnp.where(sel[:, j:j + 1] > 0.5, bias0 + slope_row * float(j * MOBA_BLOCK), NEG_INF) for j in range(n_blk)],
        axis=1)
    s = lax.dot_general(q16, jnp.concatenate(k16, axis=0), nt_dims, preferred_element_type=F32) + bias

    s_own, v_own = [], []
    for tk in range(t_new):
        kn_rows = per_head_rows(lambda g: kn_ref[g, tk:tk + 1, :])
        sv = jnp.sum(q * kn_rows, axis=1, keepdims=True) - slope_row * (t_row - tk).astype(F32)
        s_own.append(jnp.where(t_row >= tk, sv, NEG_INF))
        v_own.append(per_head_rows(lambda g: vn_ref[g, tk:tk + 1, :]))
    m = jnp.max(s, axis=1, keepdims=True)
    for tk in range(t_new):
        m = jnp.maximum(m, s_own[tk])
    p = jnp.exp(s - m)
    num = jnp.dot(p.astype(BF16), jnp.concatenate(v16, axis=0), preferred_element_type=F32)
    den = jnp.sum(p, axis=1, keepdims=True)
    for tk in range(t_new):
        pw = jnp.exp(s_own[tk] - m)
        num = num + pw * v_own[tk]
        den = den + pw
    o_ref[...] = num / den


def _attn_sample(q, k_new, v_new, cache_k_flat, cache_v_flat, page_table, slopes, layer, n_pool):
    n_dec, rows, _ = q.shape
    t_new = k_new.shape[2]
    n_pages = page_table.shape[1]
    past = n_pages * PAGE_SIZE
    assert past % MOBA_BLOCK == 0
    page_rows = PAGE_SIZE * N_KV_HEADS
    base = layer * n_pool
    pt_flat = page_table.reshape(-1)
    kern = functools.partial(_attn_sample_kernel, n_pages=n_pages, t_new=t_new, past=past)

    def page_spec(p):
        return pl.BlockSpec((page_rows, HEAD_DIM), lambda b, pt, sl: (base + pt[b * n_pages + p], 0))

    q_spec = pl.BlockSpec((None, rows, HEAD_DIM), lambda b, pt, sl: (b, 0, 0))
    new_spec = pl.BlockSpec((None, N_KV_HEADS, t_new, HEAD_DIM), lambda b, pt, sl: (b, 0, 0, 0))
    pages = [page_spec(p) for p in range(n_pages)]
    return pl.pallas_call(
        kern,
        out_shape=jax.ShapeDtypeStruct((n_dec, rows, HEAD_DIM), F32),
        grid_spec=pltpu.PrefetchScalarGridSpec(
            num_scalar_prefetch=2,
            grid=(n_dec,),
            in_specs=[q_spec, new_spec, new_spec] + pages + pages,
            out_specs=q_spec,
        ),
        compiler_params=_cparams(("arbitrary",)),
        name="moba_sample",
    )(pt_flat, slopes, q, k_new, v_new, *([cache_k_flat] * n_pages), *([cache_v_flat] * n_pages))


def _ssm_kernel(u_ref, bre_ref, bim_ref, cre_ref, cim_ref, are_ref, aim_ref, d_ref, s0r_ref, s0i_ref,
                y_ref, hr_ref, hi_ref, xr_ref, xi_ref, str_ref, sti_ref, *, nb, tc):
    ti = pl.program_id(1)

    @pl.when(ti == 0)
    def _():
        str_ref[...] = s0r_ref[...]
        sti_ref[...] = s0i_ref[...]

    u = u_ref[...]
    u16 = u.astype(BF16)
    xr_ref[...] = jnp.dot(u16, bre_ref[...], preferred_element_type=F32)
    xi_ref[...] = jnp.dot(u16, bim_ref[...], preferred_element_type=F32)
    a_re = jnp.broadcast_to(are_ref[...], (nb, SSM_CHUNK_STATES))
    a_im = jnp.broadcast_to(aim_ref[...], (nb, SSM_CHUNK_STATES))

    def step(t, carry):
        h_re, h_im = carry
        sl = pl.ds(pl.multiple_of(t * nb, nb), nb)
        n_re = (a_re * h_re - a_im * h_im) + xr_ref[sl, :]
        n_im = (a_re * h_im + a_im * h_re) + xi_ref[sl, :]
        xr_ref[sl, :] = n_re
        xi_ref[sl, :] = n_im
        return n_re, n_im

    h_re, h_im = lax.fori_loop(0, tc, step, (str_ref[...], sti_ref[...]))
    str_ref[...] = h_re
    sti_ref[...] = h_im
    y = (jnp.dot(xr_ref[...].astype(BF16), cre_ref[...], preferred_element_type=F32)
         - jnp.dot(xi_ref[...].astype(BF16), cim_ref[...], preferred_element_type=F32)
         + d_ref[...] * u)
    y_ref[...] = jax.nn.gelu(y)

    @pl.when(ti == pl.num_programs(1) - 1)
    def _():
        hr_ref[...] = h_re
        hi_ref[...] = h_im


def _ssm(u_chunks, s0_re, s0_im, prm, layer, nb, tc):
    c0 = layer * N_SSM_CHUNKS
    n_rows = u_chunks.shape[1]
    nt = n_rows // (tc * nb)
    rows = tc * nb
    kern = functools.partial(_ssm_kernel, nb=nb, tc=tc)
    cs = SSM_CHUNK_STATES
    n_states = SSM_GROUPS * SSM_STATE
    return pl.pallas_call(
        kern,
        out_shape=(
            jax.ShapeDtypeStruct((N_SSM_CHUNKS, n_rows, LANES), F32),
            jax.ShapeDtypeStruct((nb, n_states), F32),
            jax.ShapeDtypeStruct((nb, n_states), F32),
        ),
        grid=(N_SSM_CHUNKS, nt),
        in_specs=[
            pl.BlockSpec((None, rows, LANES), lambda c, t: (c, t, 0)),
            pl.BlockSpec((None, SSM_CHUNK, cs), lambda c, t: (c0 + c, 0, 0)),
            pl.BlockSpec((None, SSM_CHUNK, cs), lambda c, t: (c0 + c, 0, 0)),
            pl.BlockSpec((None, cs, SSM_CHUNK), lambda c, t: (c0 + c, 0, 0)),
            pl.BlockSpec((None, cs, SSM_CHUNK), lambda c, t: (c0 + c, 0, 0)),
            pl.BlockSpec((1, cs), lambda c, t: (0, c0 + c)),
            pl.BlockSpec((1, cs), lambda c, t: (0, c0 + c)),
            pl.BlockSpec((1, SSM_CHUNK), lambda c, t: (0, c0 + c)),
            pl.BlockSpec((nb, cs), lambda c, t: (0, c)),
            pl.BlockSpec((nb, cs), lambda c, t: (0, c)),
        ],
        out_specs=(
            pl.BlockSpec((None, rows, LANES), lambda c, t: (c, t, 0)),
            pl.BlockSpec((nb, cs), lambda c, t: (0, c)),
            pl.BlockSpec((nb, cs), lambda c, t: (0, c)),
        ),
        scratch_shapes=[
            pltpu.VMEM((rows, cs), F32),
            pltpu.VMEM((rows, cs), F32),
            pltpu.VMEM((nb, cs), F32),
            pltpu.VMEM((nb, cs), F32),
        ],
        compiler_params=_cparams(("arbitrary", "arbitrary")),
        name="s5_mixer",
    )(u_chunks, prm["bbd_re"], prm["bbd_im"], prm["cbd_re"], prm["cbd_im"], prm["ab_re"], prm["ab_im"],
      prm["d"], s0_re, s0_im)


def _ssm_params(a_re, a_im, log_dt, b_re, b_im, c_re, c_im, d_skip):
    dt = jnp.exp(log_dt)
    mag = jnp.exp(a_re * dt)
    ab_re = mag * jnp.cos(a_im * dt)
    ab_im = mag * jnp.sin(a_im * dt)
    den = a_re * a_re + a_im * a_im
    f_re = ((ab_re - 1.0) * a_re + ab_im * a_im) / den
    f_im = (ab_im * a_re - (ab_re - 1.0) * a_im) / den
    bb_re = f_re[..., None] * b_re - f_im[..., None] * b_im
    bb_im = f_re[..., None] * b_im + f_im[..., None] * b_re
    gpc = SSM_CHUNK // SSM_GROUP_CH
    eye = jnp.eye(gpc, dtype=F32)

    def b_blockdiag(bb):
        x = bb.reshape(N_SSM_CHUNKS, gpc, SSM_STATE, SSM_GROUP_CH)
        m = jnp.einsum("kgpc,gh->kgchp", x, eye)
        return m.reshape(N_SSM_CHUNKS, gpc * SSM_GROUP_CH, gpc * SSM_STATE).astype(BF16)

    def c_blockdiag(cc):
        x = cc.reshape(N_SSM_CHUNKS, gpc, SSM_GROUP_CH, SSM_STATE)
        m = jnp.einsum("kgcp,gh->kgphc", x, eye)
        return m.reshape(N_SSM_CHUNKS, gpc * SSM_STATE, gpc * SSM_GROUP_CH).astype(BF16)

    return {
        "bbd_re": b_blockdiag(bb_re), "bbd_im": b_blockdiag(bb_im),
        "cbd_re": c_blockdiag(c_re), "cbd_im": c_blockdiag(c_im),
        "ab_re": ab_re.reshape(1, -1), "ab_im": ab_im.reshape(1, -1),
        "d": d_skip.reshape(1, -1),
    }


def _rank_lt(vals, a):
    cnt = jnp.zeros(vals[a].shape, I32)
    for b in range(len(vals)):
        if b == a:
            continue
        before = (vals[b] >= vals[a]) if b < a else (vals[b] > vals[a])
        cnt = cnt + before.astype(I32)
    return cnt


def _router(x1, rwt_ref, rb_ref, eid_ref, gate_ref, cols):
    logits = lax.dot_general(rwt_ref[...], x1, (((1,), (1,)), ((), ())), precision=HIGHEST,
                             preferred_element_type=F32)
    mx = jnp.max(logits, axis=0, keepdims=True)
    ex = jnp.exp(logits - mx)
    probs = ex / jnp.sum(ex, axis=0, keepdims=True)
    biased = probs + rb_ref[...]
    prow = [probs[e:e + 1, :] for e in range(N_EXPERTS)]
    brow = [biased[e:e + 1, :] for e in range(N_EXPERTS)]
    ranks = []
    gscore = []
    for gi in range(N_EXPERT_GROUPS):
        vals = brow[gi * EXPERTS_PER_GROUP:(gi + 1) * EXPERTS_PER_GROUP]
        rk = [_rank_lt(vals, a) for a in range(EXPERTS_PER_GROUP)]
        ranks.append(rk)
        sc = jnp.zeros(vals[0].shape, F32)
        for a in range(EXPERTS_PER_GROUP):
            sc = sc + jnp.where(rk[a] < 2, vals[a], 0.0)
        gscore.append(sc)
    e0 = jnp.zeros(prow[0].shape, I32)
    e1 = jnp.zeros(prow[0].shape, I32)
    p0 = jnp.zeros(prow[0].shape, F32)
    p1 = jnp.zeros(prow[0].shape, F32)
    for gi in range(N_EXPERT_GROUPS):
        chosen = _rank_lt(gscore, gi) == 0
        for a in range(EXPERTS_PER_GROUP):
            e = gi * EXPERTS_PER_GROUP + a
            first = chosen & (ranks[gi][a] == 0)
            second = chosen & (ranks[gi][a] == 1)
            e0 = jnp.where(first, e, e0)
            e1 = jnp.where(second, e, e1)
            p0 = jnp.where(first, prow[e], p0)
            p1 = jnp.where(second, prow[e], p1)
    tot = p0 + p1
    eid_ref[0:1, cols] = e0
    eid_ref[1:2, cols] = e1
    gate_ref[0:1, cols] = p0 / tot
    gate_ref[1:2, cols] = p1 / tot


def _postmix_kernel(att_ref, y_ref, x_ref, wglu_ref, wout_ref, ag_ref, sg_ref, lg_ref, lb_ref, rwt_ref, rb_ref,
                    x1_ref, eid_ref, gate_ref, ybuf_ref, *, n_b, tt, alpha):
    rows = x1_ref.shape[0] if n_b is None else n_b * tt
    if n_b is None:
        att = att_ref[...]
        x = x_ref[...]
        for c in range(N_SSM_CHUNKS):
            ybuf_ref[:, c * LANES:(c + 1) * LANES] = y_ref[c]
    else:
        for c in range(N_SSM_CHUNKS):
            for b in range(n_b):
                ybuf_ref[b * tt:(b + 1) * tt, c * LANES:(c + 1) * LANES] = y_ref[c, pl.ds(b, tt, stride=n_b), :]
    z_all = jnp.dot(ybuf_ref[...].astype(BF16), wglu_ref[...], preferred_element_type=F32)
    hrows = rows // POSTMIX_SPLIT
    for hf in range(POSTMIX_SPLIT):
        r0 = hf * hrows
        if n_b is None:
            att_h = att[r0:r0 + hrows, :]
            x_h = x[r0:r0 + hrows, :]
        else:
            att_h = att_ref[hf * (n_b // POSTMIX_SPLIT):(hf + 1) * (n_b // POSTMIX_SPLIT)].reshape(hrows, ATTN_WIDTH)
            x_h = x_ref[hf * (n_b // POSTMIX_SPLIT):(hf + 1) * (n_b // POSTMIX_SPLIT)].reshape(hrows, D_MODEL)
        y = ybuf_ref[r0:r0 + hrows, :]
        sg = y * jax.nn.sigmoid(z_all[r0:r0 + hrows, :])
        ssm_n = sg * lax.rsqrt(jnp.mean(sg * sg, axis=-1, keepdims=True) + RMS_EPS) * sg_ref[...]
        att_n = att_h * lax.rsqrt(jnp.mean(att_h * att_h, axis=-1, keepdims=True) + RMS_EPS) * ag_ref[...]
        mix = (jnp.dot(att_n.astype(BF16), wout_ref[0:ATTN_WIDTH, :], preferred_element_type=F32)
               + jnp.dot(ssm_n.astype(BF16), wout_ref[ATTN_WIDTH:, :], preferred_element_type=F32))
        r = alpha * x_h + mix
        mu = jnp.mean(r, axis=-1, keepdims=True)
        rc = r - mu
        var = jnp.mean(rc * rc, axis=-1, keepdims=True)
        x1 = rc * lax.rsqrt(var + LN_EPS) * lg_ref[...] + lb_ref[...]
        if n_b is None:
            x1_ref[r0:r0 + hrows, :] = x1
        else:
            nbh = n_b // POSTMIX_SPLIT
            x1_ref[hf * nbh:(hf + 1) * nbh] = x1.reshape(nbh, tt, D_MODEL)
        _router(x1, rwt_ref, rb_ref, eid_ref, gate_ref, slice(r0, r0 + hrows))


def _postmix(att, y_chunks, x, lw, layer, alpha, prompt):
    if prompt:
        n_b, seq, _ = x.shape
        tt = TILE_T
        nt = seq // tt
        rows = n_b * tt
        n_tok = n_b * seq
        att_spec = pl.BlockSpec((n_b, tt, ATTN_WIDTH), lambda i: (0, i, 0))
        x_spec = pl.BlockSpec((n_b, tt, D_MODEL), lambda i: (0, i, 0))
        x1_shape = jax.ShapeDtypeStruct((n_b, seq, D_MODEL), F32)
        kern = functools.partial(_postmix_kernel, n_b=n_b, tt=tt, alpha=alpha)
    else:
        n_tok = x.shape[0]
        rows = 256
        nt = n_tok // rows
        att_spec = pl.BlockSpec((rows, ATTN_WIDTH), lambda i: (i, 0))
        x_spec = pl.BlockSpec((rows, D_MODEL), lambda i: (i, 0))
        x1_shape = jax.ShapeDtypeStruct((n_tok, D_MODEL), F32)
        kern = functools.partial(_postmix_kernel, n_b=None, tt=None, alpha=alpha)
    full = lambda shp: pl.BlockSpec(shp, lambda i: tuple(0 for _ in shp))
    return pl.pallas_call(
        kern,
        out_shape=(x1_shape, jax.ShapeDtypeStruct((2, n_tok), I32), jax.ShapeDtypeStruct((2, n_tok), F32)),
        grid=(nt,),
        in_specs=[
            att_spec,
            pl.BlockSpec((N_SSM_CHUNKS, rows, LANES), lambda i: (0, i, 0)),
            x_spec,
            pl.BlockSpec((None, SSM_WIDTH, SSM_WIDTH), lambda i: (layer, 0, 0), pipeline_mode=pl.Buffered(1)),
            pl.BlockSpec((None, D_MODEL, D_MODEL), lambda i: (layer, 0, 0), pipeline_mode=pl.Buffered(1)),
            full((1, ATTN_WIDTH)),
            full((1, SSM_WIDTH)),
            full((1, D_MODEL)),
            full((1, D_MODEL)),
            full((N_EXPERTS, D_MODEL)),
            full((N_EXPERTS, 1)),
        ],
        out_specs=(x_spec, pl.BlockSpec((2, rows), lambda i: (0, i)), pl.BlockSpec((2, rows), lambda i: (0, i))),
        scratch_shapes=[pltpu.VMEM((rows, SSM_WIDTH), F32)],
        compiler_params=_cparams(("arbitrary",)),
        name="postmix_prompt" if prompt else "postmix_sample",
    )(att, y_chunks, x, lw["w_glu"], lw["w_out"], lw["attn_g"], lw["ssm_g"], lw["ln1_g"], lw["ln1_b"],
      lw["router_wt"], lw["router_b"])


def _slab_rows(row, slab):
    start = row * slab
    return pl.ds(start if isinstance(start, int) else pl.multiple_of(start, slab), slab)


def _slab_copy(src_ref, src_row, dst_ref, dst_row, sem, slab=SLAB):
    return pltpu.make_async_copy(src_ref.at[_slab_rows(src_row, slab), :], dst_ref.at[_slab_rows(dst_row, slab), :],
                                 sem)


def _pack_bf16_pair(a, b):
    hi = lax.bitcast_convert_type(a.astype(BF16).astype(F32), jnp.uint32)
    lo = lax.bitcast_convert_type(b.astype(BF16).astype(F32), jnp.uint32)
    return hi | (lo >> 16)


def _unpack_bf16_pair(w):
    a = lax.bitcast_convert_type(w & jnp.uint32(0xFFFF0000), F32)
    b = lax.bitcast_convert_type(w << 16, F32)
    return a.astype(BF16), b.astype(BF16)


def _dispatch_kernel(dest_ref, pad_ref, xp_ref, xs_ref, out_ref, slab_ref, sem, *, n_b, tt, n_prompt_tiles, n_blocks):
    i = pl.program_id(0)
    last = pl.num_programs(0) - 1
    rows = n_b * tt
    slot = i % 2
    slab = slab_ref.at[slot]

    half = D_MODEL // 2

    @pl.when(i < n_prompt_tiles)
    def _():
        for s in range(XSLAB):
            for b in range(n_b):
                slab[pl.ds(b * tt * XSLAB + s, tt, stride=XSLAB), :] = _pack_bf16_pair(
                    xp_ref[b, :, s * LANES:(s + 1) * LANES], xp_ref[b, :, half + s * LANES:half + (s + 1) * LANES])

    @pl.when(i >= n_prompt_tiles)
    def _():
        for s in range(XSLAB):
            slab[pl.ds(s, rows, stride=XSLAB), :] = _pack_bf16_pair(
                xs_ref[:, s * LANES:(s + 1) * LANES], xs_ref[:, half + s * LANES:half + (s + 1) * LANES])

    base = i * (2 * rows)

    def start(r, c):
        for k in range(2):
            _slab_copy(slab, r, out_ref, dest_ref[base + 2 * r + k], sem.at[slot], XSLAB).start()
        return c

    lax.fori_loop(0, rows, start, 0, unroll=DMA_ISSUE_UNROLL)

    def wait_tile(which):
        for _ in range(2):
            pltpu.make_async_copy(slab_ref.at[which], slab_ref.at[which], sem.at[which]).wait()

    @pl.when(i > 0)
    def _():
        wait_tile(1 - slot)

    @pl.when(i == last)
    def _():
        wait_tile(slot)
        slab[...] = jnp.zeros(slab.shape, jnp.uint32)
        zsem = sem.at[slot]

        def per_expert(e, c):
            lo = pad_ref[e]
            hi = pad_ref[N_EXPERTS + e]

            def zs(s, c2):
                _slab_copy(slab, 0, out_ref, s, zsem, XSLAB).start()
                return c2

            lax.fori_loop(lo, hi, zs, 0)

            def zw(s, c2):
                _slab_copy(slab, 0, out_ref, s, zsem, XSLAB).wait()
                return c2

            lax.fori_loop(lo, hi, zw, 0)
            return c

        lax.fori_loop(0, N_EXPERTS, per_expert, 0)

        blk_rows = MOE_BLOCK * XSLAB

        def block_copy(b):
            start = pl.multiple_of(b * blk_rows, blk_rows)
            return pltpu.make_async_copy(slab.at[pl.ds(0, blk_rows), :], out_ref.at[pl.ds(start, blk_rows), :], zsem)

        def zbs(b, c):
            block_copy(b).start()
            return c

        def zbw(b, c):
            block_copy(b).wait()
            return c

        first_unused = pad_ref[2 * N_EXPERTS - 1] // MOE_BLOCK
        lax.fori_loop(first_unused, n_blocks, zbs, 0)
        lax.fori_loop(first_unused, n_blocks, zbw, 0)


def _dispatch(x1_p, x1_s, dest, pad_bounds, n_slots):
    n_b, seq, _ = x1_p.shape
    tt = TILE_T
    rows = n_b * tt
    assert rows >= MOE_BLOCK and x1_s.shape[0] % rows == 0
    ntp = seq // tt
    nts = x1_s.shape[0] // rows
    kern = functools.partial(_dispatch_kernel, n_b=n_b, tt=tt, n_prompt_tiles=ntp, n_blocks=n_slots // MOE_BLOCK)
    return pl.pallas_call(
        kern,
        out_shape=jax.ShapeDtypeStruct((n_slots * XSLAB, LANES), jnp.uint32),
        grid_spec=pltpu.PrefetchScalarGridSpec(
            num_scalar_prefetch=2,
            grid=(ntp + nts,),
            in_specs=[
                pl.BlockSpec((n_b, tt, D_MODEL), lambda i, d, p: (0, jnp.minimum(i, ntp - 1), 0)),
                pl.BlockSpec((rows, D_MODEL), lambda i, d, p: (jnp.maximum(i - ntp, 0), 0)),
            ],
            out_specs=pl.BlockSpec(memory_space=pl.ANY),
            scratch_shapes=[
                pltpu.VMEM((2, rows * XSLAB, LANES), jnp.uint32),
                pltpu.SemaphoreType.DMA((2,)),
            ],
        ),
        compiler_params=_cparams(("arbitrary",)),
        name="moe_dispatch",
    )(dest, pad_bounds, x1_p, x1_s)


def _new_expert(bexp_ref, i):
    return (i == 0) | (bexp_ref[i] != bexp_ref[jnp.maximum(i - 1, 0)])


def _ffn_up_kernel(bexp_ref, nused_ref, xs_ref, wg_ref, wu_ref, h_ref, xb_ref, wg16_ref, wu16_ref):
    i = pl.program_id(0)
    rows = MOE_BLOCK

    @pl.when(_new_expert(bexp_ref, i))
    def _():
        wg16_ref[...] = wg_ref[...].astype(BF16)
        wu16_ref[...] = wu_ref[...].astype(BF16)

    @pl.when(i < nused_ref[0])
    def _():
        half = D_MODEL // 2
        for s in range(XSLAB):
            a, b = _unpack_bf16_pair(xs_ref[pl.ds(s, rows, stride=XSLAB), :])
            xb_ref[:, s * LANES:(s + 1) * LANES] = a
            xb_ref[:, half + s * LANES:half + (s + 1) * LANES] = b
        xb = xb_ref[...]
        g = jnp.dot(xb, wg16_ref[...], preferred_element_type=F32)
        u = jnp.dot(xb, wu16_ref[...], preferred_element_type=F32)
        h_ref[...] = (g * jax.nn.sigmoid(g) * u).astype(BF16)

    @pl.when(i >= nused_ref[0])
    def _():
        h_ref[...] = jnp.zeros(h_ref.shape, BF16)


def _ffn_down_kernel(bexp_ref, nused_ref, h_ref, wd_ref, ys_ref, wd16_ref):
    i = pl.program_id(0)
    rows = MOE_BLOCK

    @pl.when(_new_expert(bexp_ref, i))
    def _():
        wd16_ref[...] = wd_ref[...].astype(BF16)

    @pl.when(i < nused_ref[0])
    def _():
        y = jnp.dot(h_ref[...], wd16_ref[...], preferred_element_type=F32)
        for s in range(SLAB):
            ys_ref[pl.ds(s, rows, stride=SLAB), :] = y[:, s * LANES:(s + 1) * LANES]

    @pl.when(i >= nused_ref[0])
    def _():
        ys_ref[...] = jnp.zeros(ys_ref.shape, F32)


def _ffn(xs, w_gate, w_up, w_down, blk_exp, n_used, n_blocks, layer):
    rows = MOE_BLOCK
    e0 = layer * N_EXPERTS
    n_slots = n_blocks * rows

    def live_map(i, be, nu):
        return (jnp.minimum(i, nu[0] - 1), 0)

    def w_map(i, be, nu):
        return (e0 + be[i], 0, 0)

    h = pl.pallas_call(
        _ffn_up_kernel,
        out_shape=jax.ShapeDtypeStruct((n_slots, D_FF), BF16),
        grid_spec=pltpu.PrefetchScalarGridSpec(
            num_scalar_prefetch=2,
            grid=(n_blocks,),
            in_specs=[
                pl.BlockSpec((rows * XSLAB, LANES), live_map),
                pl.BlockSpec((None, D_MODEL, D_FF), w_map),
                pl.BlockSpec((None, D_MODEL, D_FF), w_map),
            ],
            out_specs=pl.BlockSpec((rows, D_FF), lambda i, be, nu: (i, 0)),
            scratch_shapes=[
                pltpu.VMEM((rows, D_MODEL), BF16),
                pltpu.VMEM((D_MODEL, D_FF), BF16),
                pltpu.VMEM((D_MODEL, D_FF), BF16),
            ],
        ),
        compiler_params=_cparams(("arbitrary",)),
        name="moe_ffn_up",
    )(blk_exp, n_used, xs, w_gate, w_up)
    return pl.pallas_call(
        _ffn_down_kernel,
        out_shape=jax.ShapeDtypeStruct((n_slots * SLAB, LANES), F32),
        grid_spec=pltpu.PrefetchScalarGridSpec(
            num_scalar_prefetch=2,
            grid=(n_blocks,),
            in_specs=[
                pl.BlockSpec((rows, D_FF), live_map),
                pl.BlockSpec((None, D_FF, D_MODEL), w_map),
            ],
            out_specs=pl.BlockSpec((rows * SLAB, LANES), lambda i, be, nu: (i, 0)),
            scratch_shapes=[pltpu.VMEM((D_FF, D_MODEL), BF16)],
        ),
        compiler_params=_cparams(("arbitrary",)),
        name="moe_ffn_down",
    )(blk_exp, n_used, h, w_down)


def _combine_kernel(dest_ref, x1_ref, gate_ref, lg_ref, lb_ref, ys_ref, x2_ref, y0_ref, y1_ref, sem, *, n_b, tt, alpha):
    i = pl.program_id(0)
    rows = y0_ref.shape[1] // SLAB
    slot = i % 2

    def fetch(tile, which):
        base = tile * (2 * rows)

        def start(r, c):
            _slab_copy(ys_ref, dest_ref[base + 2 * r], y0_ref.at[which], r, sem.at[which]).start()
            _slab_copy(ys_ref, dest_ref[base + 2 * r + 1], y1_ref.at[which], r, sem.at[which]).start()
            return c

        lax.fori_loop(0, rows, start, 0, unroll=DMA_ISSUE_UNROLL)

    @pl.when(i == 0)
    def _():
        fetch(0, 0)

    @pl.when(i + 1 < pl.num_programs(0))
    def _():
        fetch(i + 1, 1 - slot)

    pltpu.make_async_copy(y0_ref.at[slot], y0_ref.at[slot], sem.at[slot]).wait()
    pltpu.make_async_copy(y1_ref.at[slot], y1_ref.at[slot], sem.at[slot]).wait()

    y0 = y0_ref.at[slot]
    y1 = y1_ref.at[slot]
    gate_rows = jnp.concatenate([gate_ref[...], jnp.zeros((SUBLANES - 2, rows), F32)], axis=0)
    gate_cols = gate_rows.T
    g0 = gate_cols[:, 0:1]
    g1 = gate_cols[:, 1:2]
    if n_b is None:
        x1 = x1_ref[...]
    else:
        x1 = x1_ref[...].reshape(rows, D_MODEL)
    parts = []
    for s in range(SLAB):
        moe = g0 * y0[pl.ds(s, rows, stride=SLAB), :] + g1 * y1[pl.ds(s, rows, stride=SLAB), :]
        parts.append(alpha * x1[:, s * LANES:(s + 1) * LANES] + moe)
    r = jnp.concatenate(parts, axis=1)
    mu = jnp.mean(r, axis=-1, keepdims=True)
    rc = r - mu
    var = jnp.mean(rc * rc, axis=-1, keepdims=True)
    x2 = rc * lax.rsqrt(var + LN_EPS) * lg_ref[...] + lb_ref[...]
    if n_b is None:
        x2_ref[...] = x2
    else:
        x2_ref[...] = x2.reshape(n_b, tt, D_MODEL)


def _combine(x1, gate_t, dest, ys, ln_g, ln_b, alpha, prompt):
    if prompt:
        n_b, seq, _ = x1.shape
        tt = TILE_T
        nbs = COMBINE_ROWS // tt
        parts = n_b // nbs
        rows = nbs * tt
        nt = (seq // tt) * parts
        x_spec = pl.BlockSpec((nbs, tt, D_MODEL), lambda i, d: (i % parts, i // parts, 0))
        kern = functools.partial(_combine_kernel, n_b=nbs, tt=tt, alpha=alpha)
    else:
        rows = 256
        nt = x1.shape[0] // rows
        x_spec = pl.BlockSpec((rows, D_MODEL), lambda i, d: (i, 0))
        kern = functools.partial(_combine_kernel, n_b=None, tt=None, alpha=alpha)
    return pl.pallas_call(
        kern,
        out_shape=jax.ShapeDtypeStruct(x1.shape, F32),
        grid_spec=pltpu.PrefetchScalarGridSpec(
            num_scalar_prefetch=1,
            grid=(nt,),
            in_specs=[
                x_spec,
                pl.BlockSpec((2, rows), lambda i, d: (0, i)),
                pl.BlockSpec((1, D_MODEL), lambda i, d: (0, 0)),
                pl.BlockSpec((1, D_MODEL), lambda i, d: (0, 0)),
                pl.BlockSpec(memory_space=pl.ANY),
            ],
            out_specs=x_spec,
            scratch_shapes=[
                pltpu.VMEM((2, rows * SLAB, LANES), F32),
                pltpu.VMEM((2, rows * SLAB, LANES), F32),
                pltpu.SemaphoreType.DMA((2,)),
            ],
        ),
        compiler_params=_cparams(("arbitrary",)),
        name="moe_combine_prompt" if prompt else "moe_combine_sample",
    )(dest, x1, gate_t, ln_g, ln_b, ys)


def _moe_plan(eid_t, n_blocks):
    e_flat = eid_t.T.reshape(-1)
    onehot = (e_flat[:, None] == jnp.arange(N_EXPERTS, dtype=I32)[None, :]).astype(I32)
    csum = jnp.cumsum(onehot, axis=0)
    rank = jnp.sum(csum * onehot, axis=1) - 1
    counts = csum[-1]
    padded = (counts + MOE_BLOCK - 1) // MOE_BLOCK * MOE_BLOCK
    pad_end = jnp.cumsum(padded)
    pad_start = pad_end - padded
    dest = (jnp.sum(onehot * pad_start[None, :], axis=1) + rank).astype(I32)
    n_used = (pad_end[-1] // MOE_BLOCK).astype(I32)
    first_slot = jnp.arange(n_blocks, dtype=I32) * MOE_BLOCK
    blk_exp = jnp.minimum(jnp.sum((first_slot[:, None] >= pad_end[None, :]).astype(I32), axis=1), N_EXPERTS - 1)
    last_exp = jnp.take(blk_exp, jnp.maximum(n_used - 1, 0))
    blk_exp = jnp.where(jnp.arange(n_blocks) < n_used, blk_exp, last_exp).astype(I32)
    pad_bounds = jnp.concatenate([pad_start + counts, pad_end]).astype(I32)
    return dest, pad_bounds, blk_exp, n_used.reshape(1)


def kernel(x_prompt, x_sample, cache_k, cache_v, state_ssm_re, state_ssm_im, page_table, w_in, w_out, attn_norm_g, ssm_norm_g, ssm_a_re, ssm_a_im, ssm_log_dt, ssm_b_re, ssm_b_im, ssm_c_re, ssm_c_im, ssm_d, ssm_w_glu, ln1_g, ln1_b, router_w, router_b, moe_w_gate, moe_w_up, moe_w_down, ln2_g, ln2_b):
    depth = w_in.shape[0]
    n_b, seq, _ = x_prompt.shape
    n_dec, t_new, _ = x_sample.shape
    n_pool = cache_k.shape[1]
    gq = N_HEADS // N_KV_HEADS
    alpha = (2 * depth) ** 0.25
    n_states = SSM_GROUPS * SSM_STATE
    n_sample = n_dec * t_new
    n_tok = n_b * seq + n_sample
    n_blocks = -(-(2 * n_tok + N_EXPERTS * (MOE_BLOCK - 1)) // MOE_BLOCK)
    n_slots = n_blocks * MOE_BLOCK

    slopes = 2.0 ** (-8.0 * jnp.arange(1, N_HEADS + 1, dtype=F32) / N_HEADS)
    cache_k_flat = cache_k.reshape(-1, HEAD_DIM)
    cache_v_flat = cache_v.reshape(-1, HEAD_DIM)
    router_wt = router_w.T
    router_bc = router_b.reshape(N_EXPERTS, 1)
    zeros_state = jnp.zeros((n_b, n_states), F32)
    w_in16 = w_in.astype(BF16)
    w_glu16 = ssm_w_glu.astype(BF16)
    w_out16 = w_out.astype(BF16)
    w_gate = moe_w_gate.reshape(depth * N_EXPERTS, D_MODEL, D_FF)
    w_up = moe_w_up.reshape(depth * N_EXPERTS, D_MODEL, D_FF)
    w_down = moe_w_down.reshape(depth * N_EXPERTS, D_FF, D_MODEL)

    xp = x_prompt
    xs = x_sample.transpose(1, 0, 2).reshape(n_sample, D_MODEL)
    outs = {k: [] for k in ("kp", "vp", "hrp", "hip", "ks", "vs", "hrs", "his")}
    prm = jax.vmap(_ssm_params)(ssm_a_re, ssm_a_im, ssm_log_dt, ssm_b_re, ssm_b_im, ssm_c_re, ssm_c_im, ssm_d)
    prm = {k: (v.reshape((depth * N_SSM_CHUNKS,) + v.shape[2:]) if v.ndim == 4 else v.reshape(1, -1))
           for k, v in prm.items()}
    for l in range(depth):
        lw = {
            "w_glu": w_glu16, "w_out": w_out16,
            "attn_g": attn_norm_g[l].reshape(1, -1), "ssm_g": ssm_norm_g[l].reshape(1, -1),
            "ln1_g": ln1_g[l].reshape(1, -1), "ln1_b": ln1_b[l].reshape(1, -1),
            "router_wt": router_wt, "router_b": router_bc,
        }
        q_p, k_p, v_p, u_p = _in_proj_prompt(xp, w_in16, l)
        att_p = _attn_prompt(q_p, k_p, v_p, slopes)
        y_p, hr_p, hi_p = _ssm(u_p, zeros_state, zeros_state, prm, l, nb=n_b, tc=256)

        h_s = _in_proj_sample(xs, w_in16, l)
        q_s = h_s[:, :ATTN_WIDTH].reshape(t_new, n_dec, N_HEADS, HEAD_DIM)
        q_s = q_s.transpose(1, 2, 0, 3).reshape(n_dec, N_HEADS * t_new, HEAD_DIM)
        k_s = h_s[:, ATTN_WIDTH:ATTN_WIDTH + KV_WIDTH].reshape(t_new, n_dec, N_KV_HEADS, HEAD_DIM)
        v_s = h_s[:, ATTN_WIDTH + KV_WIDTH:ATTN_WIDTH + 2 * KV_WIDTH].reshape(t_new, n_dec, N_KV_HEADS, HEAD_DIM)
        att_s = _attn_sample(q_s, k_s.transpose(1, 2, 0, 3), v_s.transpose(1, 2, 0, 3), cache_k_flat, cache_v_flat,
                             page_table, slopes, l, n_pool)
        att_s = att_s.reshape(n_dec, N_HEADS, t_new, HEAD_DIM).transpose(2, 0, 1, 3).reshape(n_sample, ATTN_WIDTH)
        u_s = h_s[:, ATTN_WIDTH + 2 * KV_WIDTH:].reshape(n_sample, N_SSM_CHUNKS, LANES).transpose(1, 0, 2)
        y_s, hr_s, hi_s = _ssm(u_s, state_ssm_re[l].reshape(n_dec, n_states), state_ssm_im[l].reshape(n_dec, n_states),
                               prm, l, nb=n_dec, tc=t_new)

        x1_p, eid_p, gate_p = _postmix(att_p, y_p, xp, lw, l, alpha, prompt=True)
        x1_s, eid_s, gate_s = _postmix(att_s, y_s, xs, lw, l, alpha, prompt=False)

        eid_t = jnp.concatenate([eid_p, eid_s], axis=1)
        dest, pad_bounds, blk_exp, n_used = _moe_plan(eid_t, n_blocks)
        xs_slots = _dispatch(x1_p, x1_s, dest, pad_bounds, n_slots)
        ys_slots = _ffn(xs_slots, w_gate, w_up, w_down, blk_exp, n_used, n_blocks, l)
        l2g = ln2_g[l].reshape(1, -1)
        l2b = ln2_b[l].reshape(1, -1)
        n_pa = 2 * n_b * seq
        xp = _combine(x1_p, gate_p, dest[:n_pa], ys_slots, l2g, l2b, alpha, prompt=True)
        xs = _combine(x1_s, gate_s, dest[n_pa:], ys_slots, l2g, l2b, alpha, prompt=False)

        outs["kp"].append(k_p.reshape(n_b, seq, N_KV_HEADS, HEAD_DIM))
        outs["vp"].append(v_p.reshape(n_b, seq, N_KV_HEADS, HEAD_DIM))
        outs["hrp"].append(hr_p.reshape(n_b, SSM_GROUPS, SSM_STATE))
        outs["hip"].append(hi_p.reshape(n_b, SSM_GROUPS, SSM_STATE))
        outs["ks"].append(k_s.transpose(1, 0, 2, 3))
        outs["vs"].append(v_s.transpose(1, 0, 2, 3))
        outs["hrs"].append(hr_s.reshape(n_dec, SSM_GROUPS, SSM_STATE))
        outs["his"].append(hi_s.reshape(n_dec, SSM_GROUPS, SSM_STATE))

    y_sample = xs.reshape(t_new, n_dec, D_MODEL).transpose(1, 0, 2)
    return (xp, y_sample,
            jnp.stack(outs["kp"]), jnp.stack(outs["vp"]), jnp.stack(outs["hrp"]), jnp.stack(outs["hip"]),
            jnp.stack(outs["ks"]), jnp.stack(outs["vs"]), jnp.stack(outs["hrs"]), jnp.stack(outs["his"]))
```

```python
import functools
import math

import jax
import jax.numpy as jnp
from jax import lax
from jax.experimental import pallas as pl
from jax.experimental.pallas import tpu as pltpu

F32 = jnp.float32
BF16 = jnp.bfloat16
I32 = jnp.int32
HIGHEST = lax.Precision.HIGHEST

D_MODEL = 2048
ATTN_WIDTH = 1024
SSM_WIDTH = 1024
HEAD_DIM = 128
N_HEADS = 8
N_KV_HEADS = 4
KV_WIDTH = N_KV_HEADS * HEAD_DIM
PROJ_WIDTH = ATTN_WIDTH + 2 * KV_WIDTH + SSM_WIDTH
MOBA_BLOCK = 256
MOBA_TOPK = 3
PAGE_SIZE = 128
SSM_GROUP_CH = 16
SSM_GROUPS = 64
SSM_STATE = 64
N_EXPERTS = 16
N_EXPERT_GROUPS = 4
EXPERTS_PER_GROUP = 4
D_FF = 1024
LN_EPS = 1e-5
RMS_EPS = 1e-6
NEG_INF = float("-inf")
Q_SCALE = HEAD_DIM ** -0.5

LANES = 128
SUBLANES = 8
VMEM_LIMIT = 56 * 1024 * 1024

TILE_T = 64
SLAB = D_MODEL // LANES
XSLAB = D_MODEL // (2 * LANES)
SSM_CHUNK = 8 * SSM_GROUP_CH
SSM_CHUNK_STATES = 8 * SSM_STATE
N_SSM_CHUNKS = SSM_WIDTH // SSM_CHUNK
MOE_BLOCK = 256
POSTMIX_SPLIT = 4
DMA_ISSUE_UNROLL = 16
COMBINE_ROWS = 256
SAMPLE_SEQS_PER_STEP = 2


def _cparams(sem, vmem=VMEM_LIMIT):
    return pltpu.CompilerParams(dimension_semantics=sem, vmem_limit_bytes=vmem)


def _in_proj_prompt_kernel(x_ref, w_ref, q_ref, k_ref, v_ref, u_ref, *, n_b, tt):
    rows = n_b * tt
    x = x_ref[...].reshape(rows, D_MODEL).astype(BF16)
    h = jnp.dot(x, w_ref[...], preferred_element_type=F32)
    q_ref[...] = h[:, :ATTN_WIDTH].reshape(n_b, tt, ATTN_WIDTH)
    for b in range(n_b):
        for g in range(N_KV_HEADS):
            k0 = ATTN_WIDTH + g * HEAD_DIM
            v0 = ATTN_WIDTH + KV_WIDTH + g * HEAD_DIM
            k_ref[b, pl.ds(g, tt, stride=N_KV_HEADS), :] = h[b * tt:(b + 1) * tt, k0:k0 + HEAD_DIM]
            v_ref[b, pl.ds(g, tt, stride=N_KV_HEADS), :] = h[b * tt:(b + 1) * tt, v0:v0 + HEAD_DIM]
    u0 = ATTN_WIDTH + 2 * KV_WIDTH
    for c in range(N_SSM_CHUNKS):
        for b in range(n_b):
            u_ref[c, pl.ds(b, tt, stride=n_b), :] = h[b * tt:(b + 1) * tt, u0 + c * LANES:u0 + (c + 1) * LANES]


def _in_proj_prompt(x, w16, layer):
    n_b, seq, _ = x.shape
    tt = TILE_T
    nt = seq // tt
    kern = functools.partial(_in_proj_prompt_kernel, n_b=n_b, tt=tt)
    return pl.pallas_call(
        kern,
        out_shape=(
            jax.ShapeDtypeStruct((n_b, seq, ATTN_WIDTH), F32),
            jax.ShapeDtypeStruct((n_b, seq * N_KV_HEADS, HEAD_DIM), F32),
            jax.ShapeDtypeStruct((n_b, seq * N_KV_HEADS, HEAD_DIM), F32),
            jax.ShapeDtypeStruct((N_SSM_CHUNKS, seq * n_b, LANES), F32),
        ),
        grid=(nt,),
        in_specs=[
            pl.BlockSpec((n_b, tt, D_MODEL), lambda i: (0, i, 0)),
            pl.BlockSpec((None, D_MODEL, PROJ_WIDTH), lambda i: (layer, 0, 0), pipeline_mode=pl.Buffered(1)),
        ],
        out_specs=(
            pl.BlockSpec((n_b, tt, ATTN_WIDTH), lambda i: (0, i, 0)),
            pl.BlockSpec((n_b, tt * N_KV_HEADS, HEAD_DIM), lambda i: (0, i, 0)),
            pl.BlockSpec((n_b, tt * N_KV_HEADS, HEAD_DIM), lambda i: (0, i, 0)),
            pl.BlockSpec((N_SSM_CHUNKS, tt * n_b, LANES), lambda i: (0, i, 0)),
        ),
        compiler_params=_cparams(("arbitrary",)),
        name="in_proj_prompt",
    )(x, w16)


def _matmul_kernel(x_ref, w_ref, o_ref):
    o_ref[...] = jnp.dot(x_ref[...].astype(BF16), w_ref[...], preferred_element_type=F32)


def _in_proj_sample(x, w16, layer):
    m, k = x.shape
    n = w16.shape[2]
    tm = 256
    return pl.pallas_call(
        _matmul_kernel,
        out_shape=jax.ShapeDtypeStruct((m, n), F32),
        grid=(m // tm,),
        in_specs=[pl.BlockSpec((tm, k), lambda i: (i, 0)), pl.BlockSpec((None, k, n), lambda i: (layer, 0, 0))],
        out_specs=pl.BlockSpec((tm, n), lambda i: (i, 0)),
        compiler_params=_cparams(("arbitrary",)),
        name="in_proj_sample",
    )(x, w16)


def _topk_mask(gate, valid, k, axis):
    nb = gate.shape[axis]
    ids = lax.broadcasted_iota(I32, gate.shape, axis)
    gm = jnp.where(valid, gate, NEG_INF)
    cnt = jnp.zeros(gate.shape, I32)
    for j in range(nb):
        gj = gm[:, j:j + 1] if axis == 1 else gm[j:j + 1, :]
        beats = (gj > gm) | ((gj == gm) & (j < ids))
        cnt = cnt + beats.astype(I32)
    return valid & (cnt < k)


def _attn_prompt_kernel(slopes_ref, q_ref, k_ref, v_ref, o_ref, kmean_ref, k16_ref, vt_ref, bias_ref, acc_ref, s_ref,
                        *, n_blk):
    g = pl.program_id(1)
    ti = pl.program_id(2)
    blk = MOBA_BLOCK
    pair = 2 * blk
    n_pair = n_blk // 2
    gq = N_HEADS // N_KV_HEADS
    width = gq * pair
    nt_dims = (((1,), (1,)), ((), ()))

    lane = lax.broadcasted_iota(I32, (1, width), 1)
    qloc = lane % pair
    slope_vec = jnp.zeros((1, width), F32)
    for hh in range(gq):
        slope_vec = jnp.where(lane // pair == hh, slopes_ref[g * gq + hh], slope_vec)

    @pl.when(ti == 0)
    def _():
        for j in range(n_blk):
            rows_j = pl.ds(j * blk * N_KV_HEADS + g, blk, stride=N_KV_HEADS)
            kb = k_ref[rows_j, :]
            kmean_ref[j:j + 1, :] = jnp.sum(kb, axis=0, keepdims=True) * (1.0 / blk)
            k16_ref[j * blk:(j + 1) * blk, :] = kb.astype(BF16)
            vt_ref[j // 2, :, (j % 2) * blk:(j % 2 + 1) * blk] = v_ref[rows_j, :].T.astype(BF16)
        pair_key = lax.broadcasted_iota(I32, (pair, width), 0)
        bias_ref[...] = -slope_vec * (qloc - pair_key).astype(F32)

    q_all = jnp.concatenate([q_ref[:, hh * HEAD_DIM:(hh + 1) * HEAD_DIM] for hh in range(gq)], axis=0) * Q_SCALE
    q16 = q_all.astype(BF16)
    blk_ids = lax.broadcasted_iota(I32, (n_blk, width), 0)
    own_blk = 2 * ti + qloc // blk
    gate_t = lax.dot_general(kmean_ref[...], q_all, nt_dims, precision=HIGHEST, preferred_element_type=F32)
    sel = _topk_mask(gate_t, blk_ids < own_blk, MOBA_TOPK, axis=0).astype(F32)

    def sel_row(j):
        return jnp.sum(jnp.where(blk_ids == j, sel, 0.0), axis=0, keepdims=True) > 0.5

    def pair_scores(t):
        off = pl.multiple_of(t * pair, pair)
        return lax.dot_general(k16_ref[pl.ds(off, pair), :], q16, nt_dims, preferred_element_type=F32)

    def softmax_pair(s, vis0, vis1, m_prev, cst):
        h0 = jnp.where(vis0, s[0:blk, :] + bias_ref[0:blk, :], NEG_INF)
        h1 = jnp.where(vis1, s[blk:pair, :] + bias_ref[blk:pair, :], NEG_INF)
        mx = jnp.maximum(jnp.max(h0, axis=0, keepdims=True), jnp.max(h1, axis=0, keepdims=True)) + cst
        m_new = mx if m_prev is None else jnp.maximum(m_prev, mx)
        off = m_new - cst
        p0 = jnp.exp(h0 - off)
        p1 = jnp.exp(h1 - off)
        psum = jnp.sum(p0, axis=0, keepdims=True) + jnp.sum(p1, axis=0, keepdims=True)
        return m_new, jnp.concatenate([p0, p1], axis=0).astype(BF16), psum

    key_id = lax.broadcasted_iota(I32, (blk, width), 0)
    vis_top = ((qloc < blk) & (key_id <= qloc)) | ((qloc >= blk) & sel_row(2 * ti))
    vis_bot = key_id + blk <= qloc
    m0, p, l0 = softmax_pair(pair_scores(ti), vis_top, vis_bot, None, jnp.zeros((1, width), F32))
    acc_ref[...] = jnp.dot(vt_ref[ti], p, preferred_element_type=F32)

    s_ref[0] = pair_scores(0)

    def body(t, carry):
        m, l = carry
        s = s_ref[t % 2]
        s_ref[(t + 1) % 2] = pair_scores(jnp.minimum(t + 1, n_pair - 1))
        cst = -slope_vec * ((ti - t) * pair).astype(F32)
        m_new, p, psum = softmax_pair(s, sel_row(2 * t), sel_row(2 * t + 1), m, cst)
        a = jnp.exp(m - m_new)
        acc_ref[...] = a * acc_ref[...] + jnp.dot(vt_ref[t], p, preferred_element_type=F32)
        return m_new, a * l + psum

    _, l = lax.fori_loop(0, ti, body, (m0, l0))
    o_t = acc_ref[...] / l
    for hh in range(gq):
        o_ref[:, hh * HEAD_DIM:(hh + 1) * HEAD_DIM] = o_t[:, hh * pair:(hh + 1) * pair].T


def _attn_prompt(q, k, v, slopes):
    n_b, seq, _ = q.shape
    n_blk = seq // MOBA_BLOCK
    assert n_blk % 2 == 0
    gq = N_HEADS // N_KV_HEADS
    pair = 2 * MOBA_BLOCK
    kern = functools.partial(_attn_prompt_kernel, n_blk=n_blk)
    return pl.pallas_call(
        kern,
        out_shape=jax.ShapeDtypeStruct((n_b, seq, ATTN_WIDTH), F32),
        grid_spec=pltpu.PrefetchScalarGridSpec(
            num_scalar_prefetch=1,
            grid=(n_b, N_KV_HEADS, n_blk // 2),
            in_specs=[
                pl.BlockSpec((None, pair, gq * HEAD_DIM), lambda b, g, i, s: (b, i, g)),
                pl.BlockSpec((None, seq * N_KV_HEADS, HEAD_DIM), lambda b, g, i, s: (b, 0, 0)),
                pl.BlockSpec((None, seq * N_KV_HEADS, HEAD_DIM), lambda b, g, i, s: (b, 0, 0)),
            ],
            out_specs=pl.BlockSpec((None, pair, gq * HEAD_DIM), lambda b, g, i, s: (b, i, g)),
            scratch_shapes=[
                pltpu.VMEM((n_blk, HEAD_DIM), F32),
                pltpu.VMEM((seq, HEAD_DIM), BF16),
                pltpu.VMEM((n_blk // 2, HEAD_DIM, pair), BF16),
                pltpu.VMEM((pair, gq * pair), F32),
                pltpu.VMEM((HEAD_DIM, gq * pair), F32),
                pltpu.VMEM((2, pair, gq * pair), F32),
            ],
        ),
        compiler_params=_cparams(("arbitrary", "arbitrary", "arbitrary")),
        name="moba_prompt",
    )(slopes, q, k, v)


def _attn_sample_kernel(pt_ref, slopes_ref, q_ref, kn_ref, vn_ref, *rest, n_pages, n_seq, t_new, past):
    k_refs = rest[:n_seq * n_pages]
    v_refs = rest[n_seq * n_pages:2 * n_seq * n_pages]
    o_ref = rest[2 * n_seq * n_pages]
    gq = N_HEADS // N_KV_HEADS
    rows = N_HEADS * t_new
    ppb = MOBA_BLOCK // PAGE_SIZE
    n_blk = n_pages // ppb
    page_rows = PAGE_SIZE * N_KV_HEADS
    cols = ppb * page_rows
    nt_dims = (((1,), (1,)), ((), ()))

    row = lax.broadcasted_iota(I32, (rows, 1), 0)
    head = row // t_new
    g_row = head // gq
    t_row = row % t_new
    slope_row = jnp.zeros((rows, 1), F32)
    for h in range(N_HEADS):
        slope_row = jnp.where(head == h, slopes_ref[h], slope_row)
    col = lax.broadcasted_iota(I32, (1, cols), 1)
    g_col = col % N_KV_HEADS
    kpos_col = (col // page_rows) * PAGE_SIZE + (col % page_rows) // N_KV_HEADS
    bias0 = jnp.where(g_row == g_col, -slope_row * (past + t_row - kpos_col).astype(F32), NEG_INF)

    def per_head_rows(fn):
        out = jnp.zeros((rows, HEAD_DIM), F32)
        for g in range(N_KV_HEADS):
            out = jnp.where(g_row == g, fn(g), out)
        return out

    blk_lane = lax.broadcasted_iota(I32, (rows, n_blk), 1)

    def one_sequence(sq):
        q = q_ref[sq] * Q_SCALE
        q16 = q.astype(BF16)
        page0 = sq * n_pages

        gate = jnp.zeros((rows, n_blk), F32)
        k16, v16 = [], []
        for j in range(n_blk):
            kblk = jnp.concatenate([k_refs[page0 + ppb * j + p][...] for p in range(ppb)], axis=0)
            fold = jnp.sum(kblk.reshape(cols // SUBLANES, SUBLANES, HEAD_DIM), axis=0)
            ksum = fold[0:N_KV_HEADS, :] + fold[N_KV_HEADS:, :]
            kmean_rows = per_head_rows(lambda g: ksum[g:g + 1, :]) * (1.0 / MOBA_BLOCK)
            gate = jnp.where(blk_lane == j, jnp.sum(q * kmean_rows, axis=1, keepdims=True), gate)
            k16.append(kblk.astype(BF16))
            v16.extend(v_refs[page0 + ppb * j + p][...].astype(BF16) for p in range(ppb))
        sel = _topk_mask(gate, blk_lane >= 0, MOBA_TOPK, axis=1).astype(F32)

        bias = jnp.concatenate(
            [jnp.where(sel[:, j:j + 1] > 0.5, bias0 + slope_row * float(j * MOBA_BLOCK), NEG_INF)
             for j in range(n_blk)], axis=1)
        s = lax.dot_general(q16, jnp.concatenate(k16, axis=0), nt_dims, preferred_element_type=F32) + bias

        s_own, v_own = [], []
        for tk in range(t_new):
            kn_rows = per_head_rows(lambda g: kn_ref[sq, g, tk:tk + 1, :])
            sv = jnp.sum(q * kn_rows, axis=1, keepdims=True) - slope_row * (t_row - tk).astype(F32)
            s_own.append(jnp.where(t_row >= tk, sv, NEG_INF))
            v_own.append(per_head_rows(lambda g: vn_ref[sq, g, tk:tk + 1, :]))
        m = jnp.max(s, axis=1, keepdims=True)
        for tk in range(t_new):
            m = jnp.maximum(m, s_own[tk])
        p = jnp.exp(s - m)
        num = jnp.dot(p.astype(BF16), jnp.concatenate(v16, axis=0), preferred_element_type=F32)
        den = jnp.sum(p, axis=1, keepdims=True)
        for tk in range(t_new):
            pw = jnp.exp(s_own[tk] - m)
            num = num + pw * v_own[tk]
            den = den + pw
        o_ref[sq] = num / den

    for sq in range(n_seq):
        one_sequence(sq)


def _attn_sample(q, k_new, v_new, cache_k_flat, cache_v_flat, page_table, slopes, layer, n_pool):
    n_dec, rows, _ = q.shape
    t_new = k_new.shape[2]
    n_pages = page_table.shape[1]
    past = n_pages * PAGE_SIZE
    assert past % MOBA_BLOCK == 0
    page_rows = PAGE_SIZE * N_KV_HEADS
    base = layer * n_pool
    pt_flat = page_table.reshape(-1)
    n_seq = SAMPLE_SEQS_PER_STEP
    assert n_dec % n_seq == 0
    kern = functools.partial(_attn_sample_kernel, n_pages=n_pages, n_seq=n_seq, t_new=t_new, past=past)

    def page_spec(p):
        return pl.BlockSpec((page_rows, HEAD_DIM), lambda b, pt, sl: (base + pt[b * (n_seq * n_pages) + p], 0))

    q_spec = pl.BlockSpec((n_seq, rows, HEAD_DIM), lambda b, pt, sl: (b, 0, 0))
    new_spec = pl.BlockSpec((n_seq, N_KV_HEADS, t_new, HEAD_DIM), lambda b, pt, sl: (b, 0, 0, 0))
    pages = [page_spec(p) for p in range(n_seq * n_pages)]
    return pl.pallas_call(
        kern,
        out_shape=jax.ShapeDtypeStruct((n_dec, rows, HEAD_DIM), F32),
        grid_spec=pltpu.PrefetchScalarGridSpec(
            num_scalar_prefetch=2,
            grid=(n_dec // n_seq,),
            in_specs=[q_spec, new_spec, new_spec] + pages + pages,
            out_specs=q_spec,
        ),
        compiler_params=_cparams(("arbitrary",)),
        name="moba_sample",
    )(pt_flat, slopes, q, k_new, v_new, *([cache_k_flat] * len(pages)), *([cache_v_flat] * len(pages)))


def _ssm_kernel(u_ref, bre_ref, bim_ref, cre_ref, cim_ref, are_ref, aim_ref, d_ref, s0r_ref, s0i_ref,
                y_ref, hr_ref, hi_ref, xr_ref, xi_ref, str_ref, sti_ref, *, nb, tc):
    ti = pl.program_id(1)

    @pl.when(ti == 0)
    def _():
        str_ref[...] = s0r_ref[...]
        sti_ref[...] = s0i_ref[...]

    u = u_ref[...]
    u16 = u.astype(BF16)
    xr_ref[...] = jnp.dot(u16, bre_ref[...], preferred_element_type=F32)
    xi_ref[...] = jnp.dot(u16, bim_ref[...], preferred_element_type=F32)
    a_re = jnp.broadcast_to(are_ref[...], (nb, SSM_CHUNK_STATES))
    a_im = jnp.broadcast_to(aim_ref[...], (nb, SSM_CHUNK_STATES))

    def step(t, carry):
        h_re, h_im = carry
        sl = pl.ds(pl.multiple_of(t * nb, nb), nb)
        n_re = (a_re * h_re - a_im * h_im) + xr_ref[sl, :]
        n_im = (a_re * h_im + a_im * h_re) + xi_ref[sl, :]
        xr_ref[sl, :] = n_re
        xi_ref[sl, :] = n_im
        return n_re, n_im

    h_re, h_im = lax.fori_loop(0, tc, step, (str_ref[...], sti_ref[...]))
    str_ref[...] = h_re
    sti_ref[...] = h_im
    y = (jnp.dot(xr_ref[...].astype(BF16), cre_ref[...], preferred_element_type=F32)
         - jnp.dot(xi_ref[...].astype(BF16), cim_ref[...], preferred_element_type=F32)
         + d_ref[...] * u)
    y_ref[...] = jax.nn.gelu(y)

    @pl.when(ti == pl.num_programs(1) - 1)
    def _():
        hr_ref[...] = h_re
        hi_ref[...] = h_im


def _ssm(u_chunks, s0_re, s0_im, prm, layer, nb, tc):
    c0 = layer * N_SSM_CHUNKS
    n_rows = u_chunks.shape[1]
    nt = n_rows // (tc * nb)
    rows = tc * nb
    kern = functools.partial(_ssm_kernel, nb=nb, tc=tc)
    cs = SSM_CHUNK_STATES
    n_states = SSM_GROUPS * SSM_STATE
    return pl.pallas_call(
        kern,
        out_shape=(
            jax.ShapeDtypeStruct((N_SSM_CHUNKS, n_rows, LANES), F32),
            jax.ShapeDtypeStruct((nb, n_states), F32),
            jax.ShapeDtypeStruct((nb, n_states), F32),
        ),
        grid=(N_SSM_CHUNKS, nt),
        in_specs=[
            pl.BlockSpec((None, rows, LANES), lambda c, t: (c, t, 0)),
            pl.BlockSpec((None, SSM_CHUNK, cs), lambda c, t: (c0 + c, 0, 0)),
            pl.BlockSpec((None, SSM_CHUNK, cs), lambda c, t: (c0 + c, 0, 0)),
            pl.BlockSpec((None, cs, SSM_CHUNK), lambda c, t: (c0 + c, 0, 0)),
            pl.BlockSpec((None, cs, SSM_CHUNK), lambda c, t: (c0 + c, 0, 0)),
            pl.BlockSpec((1, cs), lambda c, t: (0, c0 + c)),
            pl.BlockSpec((1, cs), lambda c, t: (0, c0 + c)),
            pl.BlockSpec((1, SSM_CHUNK), lambda c, t: (0, c0 + c)),
            pl.BlockSpec((nb, cs), lambda c, t: (0, c)),
            pl.BlockSpec((nb, cs), lambda c, t: (0, c)),
        ],
        out_specs=(
            pl.BlockSpec((None, rows, LANES), lambda c, t: (c, t, 0)),
            pl.BlockSpec((nb, cs), lambda c, t: (0, c)),
            pl.BlockSpec((nb, cs), lambda c, t: (0, c)),
        ),
        scratch_shapes=[
            pltpu.VMEM((rows, cs), F32),
            pltpu.VMEM((rows, cs), F32),
            pltpu.VMEM((nb, cs), F32),
            pltpu.VMEM((nb, cs), F32),
        ],
        compiler_params=_cparams(("arbitrary", "arbitrary")),
        name="s5_mixer",
    )(u_chunks, prm["bbd_re"], prm["bbd_im"], prm["cbd_re"], prm["cbd_im"], prm["ab_re"], prm["ab_im"],
      prm["d"], s0_re, s0_im)


def _ssm_params(a_re, a_im, log_dt, b_re, b_im, c_re, c_im, d_skip):
    dt = jnp.exp(log_dt)
    mag = jnp.exp(a_re * dt)
    ab_re = mag * jnp.cos(a_im * dt)
    ab_im = mag * jnp.sin(a_im * dt)
    den = a_re * a_re + a_im * a_im
    f_re = ((ab_re - 1.0) * a_re + ab_im * a_im) / den
    f_im = (ab_im * a_re - (ab_re - 1.0) * a_im) / den
    bb_re = f_re[..., None] * b_re - f_im[..., None] * b_im
    bb_im = f_re[..., None] * b_im + f_im[..., None] * b_re
    gpc = SSM_CHUNK // SSM_GROUP_CH
    eye = jnp.eye(gpc, dtype=F32)

    def b_blockdiag(bb):
        x = bb.reshape(N_SSM_CHUNKS, gpc, SSM_STATE, SSM_GROUP_CH)
        m = jnp.einsum("kgpc,gh->kgchp", x, eye)
        return m.reshape(N_SSM_CHUNKS, gpc * SSM_GROUP_CH, gpc * SSM_STATE).astype(BF16)

    def c_blockdiag(cc):
        x = cc.reshape(N_SSM_CHUNKS, gpc, SSM_GROUP_CH, SSM_STATE)
        m = jnp.einsum("kgcp,gh->kgphc", x, eye)
        return m.reshape(N_SSM_CHUNKS, gpc * SSM_STATE, gpc * SSM_GROUP_CH).astype(BF16)

    return {
        "bbd_re": b_blockdiag(bb_re), "bbd_im": b_blockdiag(bb_im),
        "cbd_re": c_blockdiag(c_re), "cbd_im": c_blockdiag(c_im),
        "ab_re": ab_re.reshape(1, -1), "ab_im": ab_im.reshape(1, -1),
        "d": d_skip.reshape(1, -1),
    }


def _rank_lt(vals, a):
    cnt = jnp.zeros(vals[a].shape, I32)
    for b in range(len(vals)):
        if b == a:
            continue
        before = (vals[b] >= vals[a]) if b < a else (vals[b] > vals[a])
        cnt = cnt + before.astype(I32)
    return cnt


def _router(x1, rwt_ref, rb_ref, eid_ref, gate_ref, cols):
    logits = lax.dot_general(rwt_ref[...], x1, (((1,), (1,)), ((), ())), precision=HIGHEST,
                             preferred_element_type=F32)
    mx = jnp.max(logits, axis=0, keepdims=True)
    ex = jnp.exp(logits - mx)
    probs = ex / jnp.sum(ex, axis=0, keepdims=True)
    biased = probs + rb_ref[...]
    prow = [probs[e:e + 1, :] for e in range(N_EXPERTS)]
    brow = [biased[e:e + 1, :] for e in range(N_EXPERTS)]
    ranks = []
    gscore = []
    for gi in range(N_EXPERT_GROUPS):
        vals = brow[gi * EXPERTS_PER_GROUP:(gi + 1) * EXPERTS_PER_GROUP]
        rk = [_rank_lt(vals, a) for a in range(EXPERTS_PER_GROUP)]
        ranks.append(rk)
        sc = jnp.zeros(vals[0].shape, F32)
        for a in range(EXPERTS_PER_GROUP):
            sc = sc + jnp.where(rk[a] < 2, vals[a], 0.0)
        gscore.append(sc)
    e0 = jnp.zeros(prow[0].shape, I32)
    e1 = jnp.zeros(prow[0].shape, I32)
    p0 = jnp.zeros(prow[0].shape, F32)
    p1 = jnp.zeros(prow[0].shape, F32)
    for gi in range(N_EXPERT_GROUPS):
        chosen = _rank_lt(gscore, gi) == 0
        for a in range(EXPERTS_PER_GROUP):
            e = gi * EXPERTS_PER_GROUP + a
            first = chosen & (ranks[gi][a] == 0)
            second = chosen & (ranks[gi][a] == 1)
            e0 = jnp.where(first, e, e0)
            e1 = jnp.where(second, e, e1)
            p0 = jnp.where(first, prow[e], p0)
            p1 = jnp.where(second, prow[e], p1)
    tot = p0 + p1
    eid_ref[0:1, cols] = e0
    eid_ref[1:2, cols] = e1
    gate_ref[0:1, cols] = p0 / tot
    gate_ref[1:2, cols] = p1 / tot


def _postmix_kernel(att_ref, y_ref, x_ref, wglu_ref, wout_ref, ag_ref, sg_ref, lg_ref, lb_ref, rwt_ref, rb_ref,
                    x1_ref, eid_ref, gate_ref, ybuf_ref, *, n_b, tt, alpha):
    rows = x1_ref.shape[0] if n_b is None else n_b * tt
    if n_b is None:
        att = att_ref[...]
        x = x_ref[...]
        for c in range(N_SSM_CHUNKS):
            ybuf_ref[:, c * LANES:(c + 1) * LANES] = y_ref[c]
    else:
        for c in range(N_SSM_CHUNKS):
            for b in range(n_b):
                ybuf_ref[b * tt:(b + 1) * tt, c * LANES:(c + 1) * LANES] = y_ref[c, pl.ds(b, tt, stride=n_b), :]
    z_all = jnp.dot(ybuf_ref[...].astype(BF16), wglu_ref[...], preferred_element_type=F32)
    hrows = rows // POSTMIX_SPLIT
    for hf in range(POSTMIX_SPLIT):
        r0 = hf * hrows
        if n_b is None:
            att_h = att[r0:r0 + hrows, :]
            x_h = x[r0:r0 + hrows, :]
        else:
            att_h = att_ref[hf * (n_b // POSTMIX_SPLIT):(hf + 1) * (n_b // POSTMIX_SPLIT)].reshape(hrows, ATTN_WIDTH)
            x_h = x_ref[hf * (n_b // POSTMIX_SPLIT):(hf + 1) * (n_b // POSTMIX_SPLIT)].reshape(hrows, D_MODEL)
        y = ybuf_ref[r0:r0 + hrows, :]
        sg = y * jax.nn.sigmoid(z_all[r0:r0 + hrows, :])
        ssm_n = sg * lax.rsqrt(jnp.mean(sg * sg, axis=-1, keepdims=True) + RMS_EPS) * sg_ref[...]
        att_n = att_h * lax.rsqrt(jnp.mean(att_h * att_h, axis=-1, keepdims=True) + RMS_EPS) * ag_ref[...]
        mix = (jnp.dot(att_n.astype(BF16), wout_ref[0:ATTN_WIDTH, :], preferred_element_type=F32)
               + jnp.dot(ssm_n.astype(BF16), wout_ref[ATTN_WIDTH:, :], preferred_element_type=F32))
        r = alpha * x_h + mix
        mu = jnp.mean(r, axis=-1, keepdims=True)
        rc = r - mu
        var = jnp.mean(rc * rc, axis=-1, keepdims=True)
        x1 = rc * lax.rsqrt(var + LN_EPS) * lg_ref[...] + lb_ref[...]
        if n_b is None:
            x1_ref[r0:r0 + hrows, :] = x1
        else:
            nbh = n_b // POSTMIX_SPLIT
            x1_ref[hf * nbh:(hf + 1) * nbh] = x1.reshape(nbh, tt, D_MODEL)
        _router(x1, rwt_ref, rb_ref, eid_ref, gate_ref, slice(r0, r0 + hrows))


def _postmix(att, y_chunks, x, lw, layer, alpha, prompt):
    if prompt:
        n_b, seq, _ = x.shape
        tt = TILE_T
        nt = seq // tt
        rows = n_b * tt
        n_tok = n_b * seq
        att_spec = pl.BlockSpec((n_b, tt, ATTN_WIDTH), lambda i: (0, i, 0))
        x_spec = pl.BlockSpec((n_b, tt, D_MODEL), lambda i: (0, i, 0))
        x1_shape = jax.ShapeDtypeStruct((n_b, seq, D_MODEL), F32)
        kern = functools.partial(_postmix_kernel, n_b=n_b, tt=tt, alpha=alpha)
    else:
        n_tok = x.shape[0]
        rows = 256
        nt = n_tok // rows
        att_spec = pl.BlockSpec((rows, ATTN_WIDTH), lambda i: (i, 0))
        x_spec = pl.BlockSpec((rows, D_MODEL), lambda i: (i, 0))
        x1_shape = jax.ShapeDtypeStruct((n_tok, D_MODEL), F32)
        kern = functools.partial(_postmix_kernel, n_b=None, tt=None, alpha=alpha)
    full = lambda shp: pl.BlockSpec(shp, lambda i: tuple(0 for _ in shp))
    return pl.pallas_call(
        kern,
        out_shape=(x1_shape, jax.ShapeDtypeStruct((2, n_tok), I32), jax.ShapeDtypeStruct((2, n_tok), F32)),
        grid=(nt,),
        in_specs=[
            att_spec,
            pl.BlockSpec((N_SSM_CHUNKS, rows, LANES), lambda i: (0, i, 0)),
            x_spec,
            pl.BlockSpec((None, SSM_WIDTH, SSM_WIDTH), lambda i: (layer, 0, 0), pipeline_mode=pl.Buffered(1)),
            pl.BlockSpec((None, D_MODEL, D_MODEL), lambda i: (layer, 0, 0), pipeline_mode=pl.Buffered(1)),
            full((1, ATTN_WIDTH)),
            full((1, SSM_WIDTH)),
            full((1, D_MODEL)),
            full((1, D_MODEL)),
            full((N_EXPERTS, D_MODEL)),
            full((N_EXPERTS, 1)),
        ],
        out_specs=(x_spec, pl.BlockSpec((2, rows), lambda i: (0, i)), pl.BlockSpec((2, rows), lambda i: (0, i))),
        scratch_shapes=[pltpu.VMEM((rows, SSM_WIDTH), F32)],
        compiler_params=_cparams(("arbitrary",)),
        name="postmix_prompt" if prompt else "postmix_sample",
    )(att, y_chunks, x, lw["w_glu"], lw["w_out"], lw["attn_g"], lw["ssm_g"], lw["ln1_g"], lw["ln1_b"],
      lw["router_wt"], lw["router_b"])


def _slab_rows(row, slab):
    start = row * slab
    return pl.ds(start if isinstance(start, int) else pl.multiple_of(start, slab), slab)


def _slab_copy(src_ref, src_row, dst_ref, dst_row, sem, slab=SLAB):
    return pltpu.make_async_copy(src_ref.at[_slab_rows(src_row, slab), :], dst_ref.at[_slab_rows(dst_row, slab), :],
                                 sem)


def _pack_bf16_pair(a, b):
    hi = lax.bitcast_convert_type(a.astype(BF16).astype(F32), jnp.uint32)
    lo = lax.bitcast_convert_type(b.astype(BF16).astype(F32), jnp.uint32)
    return hi | (lo >> 16)


def _unpack_bf16_pair(w):
    a = lax.bitcast_convert_type(w & jnp.uint32(0xFFFF0000), F32)
    b = lax.bitcast_convert_type(w << 16, F32)
    return a.astype(BF16), b.astype(BF16)


def _dispatch_kernel(dest_ref, pad_ref, xp_ref, xs_ref, out_ref, slab_ref, sem, *, n_b, tt, n_prompt_tiles, n_blocks):
    i = pl.program_id(0)
    last = pl.num_programs(0) - 1
    rows = n_b * tt
    slot = i % 2
    slab = slab_ref.at[slot]

    half = D_MODEL // 2

    @pl.when(i < n_prompt_tiles)
    def _():
        for s in range(XSLAB):
            for b in range(n_b):
                slab[pl.ds(b * tt * XSLAB + s, tt, stride=XSLAB), :] = _pack_bf16_pair(
                    xp_ref[b, :, s * LANES:(s + 1) * LANES], xp_ref[b, :, half + s * LANES:half + (s + 1) * LANES])

    @pl.when(i >= n_prompt_tiles)
    def _():
        for s in range(XSLAB):
            slab[pl.ds(s, rows, stride=XSLAB), :] = _pack_bf16_pair(
                xs_ref[:, s * LANES:(s + 1) * LANES], xs_ref[:, half + s * LANES:half + (s + 1) * LANES])

    base = i * (2 * rows)

    def start(r, c):
        for k in range(2):
            _slab_copy(slab, r, out_ref, dest_ref[base + 2 * r + k], sem.at[slot], XSLAB).start()
        return c

    lax.fori_loop(0, rows, start, 0, unroll=DMA_ISSUE_UNROLL)

    def wait_tile(which):
        for _ in range(2):
            pltpu.make_async_copy(slab_ref.at[which], slab_ref.at[which], sem.at[which]).wait()

    @pl.when(i > 0)
    def _():
        wait_tile(1 - slot)

    @pl.when(i == last)
    def _():
        wait_tile(slot)
        slab[...] = jnp.zeros(slab.shape, jnp.uint32)
        zsem = sem.at[slot]

        def per_expert(e, c):
            lo = pad_ref[e]
            hi = pad_ref[N_EXPERTS + e]

            def zs(s, c2):
                _slab_copy(slab, 0, out_ref, s, zsem, XSLAB).start()
                return c2

            lax.fori_loop(lo, hi, zs, 0)

            def zw(s, c2):
                _slab_copy(slab, 0, out_ref, s, zsem, XSLAB).wait()
                return c2

            lax.fori_loop(lo, hi, zw, 0)
            return c

        lax.fori_loop(0, N_EXPERTS, per_expert, 0)

        blk_rows = MOE_BLOCK * XSLAB

        def block_copy(b):
            start = pl.multiple_of(b * blk_rows, blk_rows)
            return pltpu.make_async_copy(slab.at[pl.ds(0, blk_rows), :], out_ref.at[pl.ds(start, blk_rows), :], zsem)

        def zbs(b, c):
            block_copy(b).start()
            return c

        def zbw(b, c):
            block_copy(b).wait()
            return c

        first_unused = pad_ref[2 * N_EXPERTS - 1] // MOE_BLOCK
        lax.fori_loop(first_unused, n_blocks, zbs, 0)
        lax.fori_loop(first_unused, n_blocks, zbw, 0)


def _dispatch(x1_p, x1_s, dest, pad_bounds, n_slots):
    n_b, seq, _ = x1_p.shape
    tt = TILE_T
    rows = n_b * tt
    assert rows >= MOE_BLOCK and x1_s.shape[0] % rows == 0
    ntp = seq // tt
    nts = x1_s.shape[0] // rows
    kern = functools.partial(_dispatch_kernel, n_b=n_b, tt=tt, n_prompt_tiles=ntp, n_blocks=n_slots // MOE_BLOCK)
    return pl.pallas_call(
        kern,
        out_shape=jax.ShapeDtypeStruct((n_slots * XSLAB, LANES), jnp.uint32),
        grid_spec=pltpu.PrefetchScalarGridSpec(
            num_scalar_prefetch=2,
            grid=(ntp + nts,),
            in_specs=[
                pl.BlockSpec((n_b, tt, D_MODEL), lambda i, d, p: (0, jnp.minimum(i, ntp - 1), 0)),
                pl.BlockSpec((rows, D_MODEL), lambda i, d, p: (jnp.maximum(i - ntp, 0), 0)),
            ],
            out_specs=pl.BlockSpec(memory_space=pl.ANY),
            scratch_shapes=[
                pltpu.VMEM((2, rows * XSLAB, LANES), jnp.uint32),
                pltpu.SemaphoreType.DMA((2,)),
            ],
        ),
        compiler_params=_cparams(("arbitrary",)),
        name="moe_dispatch",
    )(dest, pad_bounds, x1_p, x1_s)


def _new_expert(bexp_ref, i):
    return (i == 0) | (bexp_ref[i] != bexp_ref[jnp.maximum(i - 1, 0)])


def _ffn_up_kernel(bexp_ref, nused_ref, xs_ref, wg_ref, wu_ref, h_ref, xb_ref, wg16_ref, wu16_ref):
    i = pl.program_id(0)
    rows = MOE_BLOCK

    @pl.when(_new_expert(bexp_ref, i))
    def _():
        wg16_ref[...] = wg_ref[...].astype(BF16)
        wu16_ref[...] = wu_ref[...].astype(BF16)

    @pl.when(i < nused_ref[0])
    def _():
        half = D_MODEL // 2
        for s in range(XSLAB):
            a, b = _unpack_bf16_pair(xs_ref[pl.ds(s, rows, stride=XSLAB), :])
            xb_ref[:, s * LANES:(s + 1) * LANES] = a
            xb_ref[:, half + s * LANES:half + (s + 1) * LANES] = b
        xb = xb_ref[...]
        g = jnp.dot(xb, wg16_ref[...], preferred_element_type=F32)
        u = jnp.dot(xb, wu16_ref[...], preferred_element_type=F32)
        h_ref[...] = (g * jax.nn.sigmoid(g) * u).astype(BF16)

    @pl.when(i >= nused_ref[0])
    def _():
        h_ref[...] = jnp.zeros(h_ref.shape, BF16)


def _ffn_down_kernel(bexp_ref, nused_ref, h_ref, wd_ref, ys_ref, wd16_ref):
    i = pl.program_id(0)
    rows = MOE_BLOCK

    @pl.when(_new_expert(bexp_ref, i))
    def _():
        wd16_ref[...] = wd_ref[...].astype(BF16)

    @pl.when(i < nused_ref[0])
    def _():
        y = jnp.dot(h_ref[...], wd16_ref[...], preferred_element_type=F32)
        for s in range(SLAB):
            ys_ref[pl.ds(s, rows, stride=SLAB), :] = y[:, s * LANES:(s + 1) * LANES]

    @pl.when(i >= nused_ref[0])
    def _():
        ys_ref[...] = jnp.zeros(ys_ref.shape, F32)


def _ffn(xs, w_gate, w_up, w_down, blk_exp, n_used, n_blocks, layer):
    rows = MOE_BLOCK
    e0 = layer * N_EXPERTS
    n_slots = n_blocks * rows

    def live_map(i, be, nu):
        return (jnp.minimum(i, nu[0] - 1), 0)

    def w_map(i, be, nu):
        return (e0 + be[i], 0, 0)

    h = pl.pallas_call(
        _ffn_up_kernel,
        out_shape=jax.ShapeDtypeStruct((n_slots, D_FF), BF16),
        grid_spec=pltpu.PrefetchScalarGridSpec(
            num_scalar_prefetch=2,
            grid=(n_blocks,),
            in_specs=[
                pl.BlockSpec((rows * XSLAB, LANES), live_map),
                pl.BlockSpec((None, D_MODEL, D_FF), w_map),
                pl.BlockSpec((None, D_MODEL, D_FF), w_map),
            ],
            out_specs=pl.BlockSpec((rows, D_FF), lambda i, be, nu: (i, 0)),
            scratch_shapes=[
                pltpu.VMEM((rows, D_MODEL), BF16),
                pltpu.VMEM((D_MODEL, D_FF), BF16),
                pltpu.VMEM((D_MODEL, D_FF), BF16),
            ],
        ),
        compiler_params=_cparams(("arbitrary",)),
        name="moe_ffn_up",
    )(blk_exp, n_used, xs, w_gate, w_up)
    return pl.pallas_call(
        _ffn_down_kernel,
        out_shape=jax.ShapeDtypeStruct((n_slots * SLAB, LANES), F32),
        grid_spec=pltpu.PrefetchScalarGridSpec(
            num_scalar_prefetch=2,
            grid=(n_blocks,),
            in_specs=[
                pl.BlockSpec((rows, D_FF), live_map),
                pl.BlockSpec((None, D_FF, D_MODEL), w_map),
            ],
            out_specs=pl.BlockSpec((rows * SLAB, LANES), lambda i, be, nu: (i, 0)),
            scratch_shapes=[pltpu.VMEM((D_FF, D_MODEL), BF16)],
        ),
        compiler_params=_cparams(("arbitrary",)),
        name="moe_ffn_down",
    )(blk_exp, n_used, h, w_down)


def _combine_kernel(dest_ref, x1_ref, gate_ref, lg_ref, lb_ref, ys_ref, x2_ref, y0_ref, y1_ref, sem, *, n_b, tt, alpha):
    i = pl.program_id(0)
    rows = y0_ref.shape[1] // SLAB
    slot = i % 2

    def fetch(tile, which):
        base = tile * (2 * rows)

        def start(r, c):
            _slab_copy(ys_ref, dest_ref[base + 2 * r], y0_ref.at[which], r, sem.at[which]).start()
            _slab_copy(ys_ref, dest_ref[base + 2 * r + 1], y1_ref.at[which], r, sem.at[which]).start()
            return c

        lax.fori_loop(0, rows, start, 0, unroll=DMA_ISSUE_UNROLL)

    @pl.when(i == 0)
    def _():
        fetch(0, 0)

    @pl.when(i + 1 < pl.num_programs(0))
    def _():
        fetch(i + 1, 1 - slot)

    pltpu.make_async_copy(y0_ref.at[slot], y0_ref.at[slot], sem.at[slot]).wait()
    pltpu.make_async_copy(y1_ref.at[slot], y1_ref.at[slot], sem.at[slot]).wait()

    y0 = y0_ref.at[slot]
    y1 = y1_ref.at[slot]
    gate_rows = jnp.concatenate([gate_ref[...], jnp.zeros((SUBLANES - 2, rows), F32)], axis=0)
    gate_cols = gate_rows.T
    g0 = gate_cols[:, 0:1]
    g1 = gate_cols[:, 1:2]
    if n_b is None:
        x1 = x1_ref[...]
    else:
        x1 = x1_ref[...].reshape(rows, D_MODEL)
    parts = []
    for s in range(SLAB):
        moe = g0 * y0[pl.ds(s, rows, stride=SLAB), :] + g1 * y1[pl.ds(s, rows, stride=SLAB), :]
        parts.append(alpha * x1[:, s * LANES:(s + 1) * LANES] + moe)
    r = jnp.concatenate(parts, axis=1)
    mu = jnp.mean(r, axis=-1, keepdims=True)
    rc = r - mu
    var = jnp.mean(rc * rc, axis=-1, keepdims=True)
    x2 = rc * lax.rsqrt(var + LN_EPS) * lg_ref[...] + lb_ref[...]
    if n_b is None:
        x2_ref[...] = x2
    else:
        x2_ref[...] = x2.reshape(n_b, tt, D_MODEL)


def _combine(x1, gate_t, dest, ys, ln_g, ln_b, alpha, prompt):
    if prompt:
        n_b, seq, _ = x1.shape
        tt = TILE_T
        nbs = COMBINE_ROWS // tt
        parts = n_b // nbs
        rows = nbs * tt
        nt = (seq // tt) * parts
        x_spec = pl.BlockSpec((nbs, tt, D_MODEL), lambda i, d: (i % parts, i // parts, 0))
        kern = functools.partial(_combine_kernel, n_b=nbs, tt=tt, alpha=alpha)
    else:
        rows = 256
        nt = x1.shape[0] // rows
        x_spec = pl.BlockSpec((rows, D_MODEL), lambda i, d: (i, 0))
        kern = functools.partial(_combine_kernel, n_b=None, tt=None, alpha=alpha)
    return pl.pallas_call(
        kern,
        out_shape=jax.ShapeDtypeStruct(x1.shape, F32),
        grid_spec=pltpu.PrefetchScalarGridSpec(
            num_scalar_prefetch=1,
            grid=(nt,),
            in_specs=[
                x_spec,
                pl.BlockSpec((2, rows), lambda i, d: (0, i)),
                pl.BlockSpec((1, D_MODEL), lambda i, d: (0, 0)),
                pl.BlockSpec((1, D_MODEL), lambda i, d: (0, 0)),
                pl.BlockSpec(memory_space=pl.ANY),
            ],
            out_specs=x_spec,
            scratch_shapes=[
                pltpu.VMEM((2, rows * SLAB, LANES), F32),
                pltpu.VMEM((2, rows * SLAB, LANES), F32),
                pltpu.SemaphoreType.DMA((2,)),
            ],
        ),
        compiler_params=_cparams(("arbitrary",)),
        name="moe_combine_prompt" if prompt else "moe_combine_sample",
    )(dest, x1, gate_t, ln_g, ln_b, ys)


def _moe_plan(eid_t, n_blocks):
    e_flat = eid_t.T.reshape(-1)
    onehot = (e_flat[:, None] == jnp.arange(N_EXPERTS, dtype=I32)[None, :]).astype(I32)
    csum = jnp.cumsum(onehot, axis=0)
    rank = jnp.sum(csum * onehot, axis=1) - 1
    counts = csum[-1]
    padded = (counts + MOE_BLOCK - 1) // MOE_BLOCK * MOE_BLOCK
    pad_end = jnp.cumsum(padded)
    pad_start = pad_end - padded
    dest = (jnp.sum(onehot * pad_start[None, :], axis=1) + rank).astype(I32)
    n_used = (pad_end[-1] // MOE_BLOCK).astype(I32)
    first_slot = jnp.arange(n_blocks, dtype=I32) * MOE_BLOCK
    blk_exp = jnp.minimum(jnp.sum((first_slot[:, None] >= pad_end[None, :]).astype(I32), axis=1), N_EXPERTS - 1)
    last_exp = jnp.take(blk_exp, jnp.maximum(n_used - 1, 0))
    blk_exp = jnp.where(jnp.arange(n_blocks) < n_used, blk_exp, last_exp).astype(I32)
    pad_bounds = jnp.concatenate([pad_start + counts, pad_end]).astype(I32)
    return dest, pad_bounds, blk_exp, n_used.reshape(1)


def kernel(x_prompt, x_sample, cache_k, cache_v, state_ssm_re, state_ssm_im, page_table, w_in, w_out, attn_norm_g, ssm_norm_g, ssm_a_re, ssm_a_im, ssm_log_dt, ssm_b_re, ssm_b_im, ssm_c_re, ssm_c_im, ssm_d, ssm_w_glu, ln1_g, ln1_b, router_w, router_b, moe_w_gate, moe_w_up, moe_w_down, ln2_g, ln2_b):
    depth = w_in.shape[0]
    n_b, seq, _ = x_prompt.shape
    n_dec, t_new, _ = x_sample.shape
    n_pool = cache_k.shape[1]
    gq = N_HEADS // N_KV_HEADS
    alpha = (2 * depth) ** 0.25
    n_states = SSM_GROUPS * SSM_STATE
    n_sample = n_dec * t_new
    n_tok = n_b * seq + n_sample
    n_blocks = -(-(2 * n_tok + N_EXPERTS * (MOE_BLOCK - 1)) // MOE_BLOCK)
    n_slots = n_blocks * MOE_BLOCK

    slopes = 2.0 ** (-8.0 * jnp.arange(1, N_HEADS + 1, dtype=F32) / N_HEADS)
    cache_k_flat = cache_k.reshape(-1, HEAD_DIM)
    cache_v_flat = cache_v.reshape(-1, HEAD_DIM)
    router_wt = router_w.T
    router_bc = router_b.reshape(N_EXPERTS, 1)
    zeros_state = jnp.zeros((n_b, n_states), F32)
    w_in16 = w_in.astype(BF16)
    w_glu16 = ssm_w_glu.astype(BF16)
    w_out16 = w_out.astype(BF16)
    w_gate = moe_w_gate.reshape(depth * N_EXPERTS, D_MODEL, D_FF)
    w_up = moe_w_up.reshape(depth * N_EXPERTS, D_MODEL, D_FF)
    w_down = moe_w_down.reshape(depth * N_EXPERTS, D_FF, D_MODEL)

    xp = x_prompt
    xs = x_sample.transpose(1, 0, 2).reshape(n_sample, D_MODEL)
    outs = {k: [] for k in ("kp", "vp", "hrp", "hip", "ks", "vs", "hrs", "his")}
    prm = jax.vmap(_ssm_params)(ssm_a_re, ssm_a_im, ssm_log_dt, ssm_b_re, ssm_b_im, ssm_c_re, ssm_c_im, ssm_d)
    prm = {k: (v.reshape((depth * N_SSM_CHUNKS,) + v.shape[2:]) if v.ndim == 4 else v.reshape(1, -1))
           for k, v in prm.items()}
    for l in range(depth):
        lw = {
            "w_glu": w_glu16, "w_out": w_out16,
            "attn_g": attn_norm_g[l].reshape(1, -1), "ssm_g": ssm_norm_g[l].reshape(1, -1),
            "ln1_g": ln1_g[l].reshape(1, -1), "ln1_b": ln1_b[l].reshape(1, -1),
            "router_wt": router_wt, "router_b": router_bc,
        }
        q_p, k_p, v_p, u_p = _in_proj_prompt(xp, w_in16, l)
        att_p = _attn_prompt(q_p, k_p, v_p, slopes)
        y_p, hr_p, hi_p = _ssm(u_p, zeros_state, zeros_state, prm, l, nb=n_b, tc=256)

        h_s = _in_proj_sample(xs, w_in16, l)
        q_s = h_s[:, :ATTN_WIDTH].reshape(t_new, n_dec, N_HEADS, HEAD_DIM)
        q_s = q_s.transpose(1, 2, 0, 3).reshape(n_dec, N_HEADS * t_new, HEAD_DIM)
        k_s = h_s[:, ATTN_WIDTH:ATTN_WIDTH + KV_WIDTH].reshape(t_new, n_dec, N_KV_HEADS, HEAD_DIM)
        v_s = h_s[:, ATTN_WIDTH + KV_WIDTH:ATTN_WIDTH + 2 * KV_WIDTH].reshape(t_new, n_dec, N_KV_HEADS, HEAD_DIM)
        att_s = _attn_sample(q_s, k_s.transpose(1, 2, 0, 3), v_s.transpose(1, 2, 0, 3), cache_k_flat, cache_v_flat,
                             page_table, slopes, l, n_pool)
        att_s = att_s.reshape(n_dec, N_HEADS, t_new, HEAD_DIM).transpose(2, 0, 1, 3).reshape(n_sample, ATTN_WIDTH)
        u_s = h_s[:, ATTN_WIDTH + 2 * KV_WIDTH:].reshape(n_sample, N_SSM_CHUNKS, LANES).transpose(1, 0, 2)
        y_s, hr_s, hi_s = _ssm(u_s, state_ssm_re[l].reshape(n_dec, n_states), state_ssm_im[l].reshape(n_dec, n_states),
                               prm, l, nb=n_dec, tc=t_new)

        x1_p, eid_p, gate_p = _postmix(att_p, y_p, xp, lw, l, alpha, prompt=True)
        x1_s, eid_s, gate_s = _postmix(att_s, y_s, xs, lw, l, alpha, prompt=False)

        eid_t = jnp.concatenate([eid_p, eid_s], axis=1)
        dest, pad_bounds, blk_exp, n_used = _moe_plan(eid_t, n_blocks)
        xs_slots = _dispatch(x1_p, x1_s, dest, pad_bounds, n_slots)
        ys_slots = _ffn(xs_slots, w_gate, w_up, w_down, blk_exp, n_used, n_blocks, l)
        l2g = ln2_g[l].reshape(1, -1)
        l2b = ln2_b[l].reshape(1, -1)
        n_pa = 2 * n_b * seq
        xp = _combine(x1_p, gate_p, dest[:n_pa], ys_slots, l2g, l2b, alpha, prompt=True)
        xs = _combine(x1_s, gate_s, dest[n_pa:], ys_slots, l2g, l2b, alpha, prompt=False)

        outs["kp"].append(k_p.reshape(n_b, seq, N_KV_HEADS, HEAD_DIM))
        outs["vp"].append(v_p.reshape(n_b, seq, N_KV_HEADS, HEAD_DIM))
        outs["hrp"].append(hr_p.reshape(n_b, SSM_GROUPS, SSM_STATE))
        outs["hip"].append(hi_p.reshape(n_b, SSM_GROUPS, SSM_STATE))
        outs["ks"].append(k_s.transpose(1, 0, 2, 3))
        outs["vs"].append(v_s.transpose(1, 0, 2, 3))
        outs["hrs"].append(hr_s.reshape(n_dec, SSM_GROUPS, SSM_STATE))
        outs["his"].append(hi_s.reshape(n_dec, SSM_GROUPS, SSM_STATE))

    y_sample = xs.reshape(t_new, n_dec, D_MODEL).transpose(1, 0, 2)
    return (xp, y_sample,
            jnp.stack(outs["kp"]), jnp.stack(outs["vp"]), jnp.stack(outs["hrp"]), jnp.stack(outs["hip"]),
            jnp.stack(outs["ks"]), jnp.stack(outs["vs"]), jnp.stack(outs["hrs"]), jnp.stack(outs["his"]))
```

```python
import functools
import math

import jax
import jax.numpy as jnp
from jax import lax
from jax.experimental import pallas as pl
from jax.experimental.pallas import tpu as pltpu

F32 = jnp.float32
BF16 = jnp.bfloat16
I32 = jnp.int32
HIGHEST = lax.Precision.HIGHEST

D_MODEL = 2048
ATTN_WIDTH = 1024
SSM_WIDTH = 1024
HEAD_DIM = 128
N_HEADS = 8
N_KV_HEADS = 4
KV_WIDTH = N_KV_HEADS * HEAD_DIM
PROJ_WIDTH = ATTN_WIDTH + 2 * KV_WIDTH + SSM_WIDTH
MOBA_BLOCK = 256
MOBA_TOPK = 3
PAGE_SIZE = 128
SSM_GROUP_CH = 16
SSM_GROUPS = 64
SSM_STATE = 64
N_EXPERTS = 16
N_EXPERT_GROUPS = 4
EXPERTS_PER_GROUP = 4
D_FF = 1024
LN_EPS = 1e-5
RMS_EPS = 1e-6
NEG_INF = float("-inf")
Q_SCALE = HEAD_DIM ** -0.5

LANES = 128
SUBLANES = 8
VMEM_LIMIT = 56 * 1024 * 1024

TILE_T = 64
SLAB = D_MODEL // LANES
XSLAB = D_MODEL // (2 * LANES)
SSM_CHUNK = 8 * SSM_GROUP_CH
SSM_CHUNK_STATES = 8 * SSM_STATE
N_SSM_CHUNKS = SSM_WIDTH // SSM_CHUNK
MOE_BLOCK = 256
POSTMIX_SPLIT = 4
DMA_ISSUE_UNROLL = 16
COMBINE_ROWS = 256
SAMPLE_SEQS_PER_STEP = 2


def _cparams(sem, vmem=VMEM_LIMIT):
    return pltpu.CompilerParams(dimension_semantics=sem, vmem_limit_bytes=vmem)


def _in_proj_prompt_kernel(x_ref, w_ref, q_ref, k_ref, v_ref, u_ref, *, n_b, tt):
    rows = n_b * tt
    x = x_ref[...].reshape(rows, D_MODEL).astype(BF16)
    h = jnp.dot(x, w_ref[...], preferred_element_type=F32)
    q_ref[...] = h[:, :ATTN_WIDTH].reshape(n_b, tt, ATTN_WIDTH)
    for b in range(n_b):
        for g in range(N_KV_HEADS):
            k0 = ATTN_WIDTH + g * HEAD_DIM
            v0 = ATTN_WIDTH + KV_WIDTH + g * HEAD_DIM
            k_ref[b, pl.ds(g, tt, stride=N_KV_HEADS), :] = h[b * tt:(b + 1) * tt, k0:k0 + HEAD_DIM]
            v_ref[b, pl.ds(g, tt, stride=N_KV_HEADS), :] = h[b * tt:(b + 1) * tt, v0:v0 + HEAD_DIM]
    u0 = ATTN_WIDTH + 2 * KV_WIDTH
    for c in range(N_SSM_CHUNKS):
        for b in range(n_b):
            u_ref[c, pl.ds(b, tt, stride=n_b), :] = h[b * tt:(b + 1) * tt, u0 + c * LANES:u0 + (c + 1) * LANES]


def _in_proj_prompt(x, w16, layer):
    n_b, seq, _ = x.shape
    tt = TILE_T
    nt = seq // tt
    kern = functools.partial(_in_proj_prompt_kernel, n_b=n_b, tt=tt)
    return pl.pallas_call(
        kern,
        out_shape=(
            jax.ShapeDtypeStruct((n_b, seq, ATTN_WIDTH), F32),
            jax.ShapeDtypeStruct((n_b, seq * N_KV_HEADS, HEAD_DIM), F32),
            jax.ShapeDtypeStruct((n_b, seq * N_KV_HEADS, HEAD_DIM), F32),
            jax.ShapeDtypeStruct((N_SSM_CHUNKS, seq * n_b, LANES), F32),
        ),
        grid=(nt,),
        in_specs=[
            pl.BlockSpec((n_b, tt, D_MODEL), lambda i: (0, i, 0)),
            pl.BlockSpec((None, D_MODEL, PROJ_WIDTH), lambda i: (layer, 0, 0), pipeline_mode=pl.Buffered(1)),
        ],
        out_specs=(
            pl.BlockSpec((n_b, tt, ATTN_WIDTH), lambda i: (0, i, 0)),
            pl.BlockSpec((n_b, tt * N_KV_HEADS, HEAD_DIM), lambda i: (0, i, 0)),
            pl.BlockSpec((n_b, tt * N_KV_HEADS, HEAD_DIM), lambda i: (0, i, 0)),
            pl.BlockSpec((N_SSM_CHUNKS, tt * n_b, LANES), lambda i: (0, i, 0)),
        ),
        compiler_params=_cparams(("arbitrary",)),
        name="in_proj_prompt",
    )(x, w16)


def _matmul_kernel(x_ref, w_ref, o_ref):
    o_ref[...] = jnp.dot(x_ref[...].astype(BF16), w_ref[...], preferred_element_type=F32)


def _in_proj_sample(x, w16, layer):
    m, k = x.shape
    n = w16.shape[2]
    tm = 256
    return pl.pallas_call(
        _matmul_kernel,
        out_shape=jax.ShapeDtypeStruct((m, n), F32),
        grid=(m // tm,),
        in_specs=[pl.BlockSpec((tm, k), lambda i: (i, 0)), pl.BlockSpec((None, k, n), lambda i: (layer, 0, 0))],
        out_specs=pl.BlockSpec((tm, n), lambda i: (i, 0)),
        compiler_params=_cparams(("arbitrary",)),
        name="in_proj_sample",
    )(x, w16)


def _topk_mask(gate, valid, k, axis):
    nb = gate.shape[axis]
    ids = lax.broadcasted_iota(I32, gate.shape, axis)
    gm = jnp.where(valid, gate, NEG_INF)
    cnt = jnp.zeros(gate.shape, I32)
    for j in range(nb):
        gj = gm[:, j:j + 1] if axis == 1 else gm[j:j + 1, :]
        beats = (gj > gm) | ((gj == gm) & (j < ids))
        cnt = cnt + beats.astype(I32)
    return valid & (cnt < k)


def _attn_prompt_kernel(slopes_ref, q_ref, k_ref, v_ref, o_ref, kmean_ref, k16_ref, vt_ref, bias_ref, acc_ref, s_ref,
                        *, n_blk):
    g = pl.program_id(1)
    ti = pl.program_id(2)
    blk = MOBA_BLOCK
    pair = 2 * blk
    n_pair = n_blk // 2
    gq = N_HEADS // N_KV_HEADS
    width = gq * pair
    nt_dims = (((1,), (1,)), ((), ()))

    lane = lax.broadcasted_iota(I32, (1, width), 1)
    qloc = lane % pair
    slope_vec = jnp.zeros((1, width), F32)
    for hh in range(gq):
        slope_vec = jnp.where(lane // pair == hh, slopes_ref[g * gq + hh], slope_vec)

    @pl.when(ti == 0)
    def _():
        for j in range(n_blk):
            rows_j = pl.ds(j * blk * N_KV_HEADS + g, blk, stride=N_KV_HEADS)
            kb = k_ref[rows_j, :]
            kmean_ref[j:j + 1, :] = jnp.sum(kb, axis=0, keepdims=True) * (1.0 / blk)
            k16_ref[j * blk:(j + 1) * blk, :] = kb.astype(BF16)
            vt_ref[j // 2, :, (j % 2) * blk:(j % 2 + 1) * blk] = v_ref[rows_j, :].T.astype(BF16)
        pair_key = lax.broadcasted_iota(I32, (pair, width), 0)
        bias_ref[...] = -slope_vec * (qloc - pair_key).astype(F32)

    q_all = jnp.concatenate([q_ref[:, hh * HEAD_DIM:(hh + 1) * HEAD_DIM] for hh in range(gq)], axis=0) * Q_SCALE
    q16 = q_all.astype(BF16)
    blk_ids = lax.broadcasted_iota(I32, (n_blk, width), 0)
    own_blk = 2 * ti + qloc // blk
    gate_t = lax.dot_general(kmean_ref[...], q_all, nt_dims, precision=HIGHEST, preferred_element_type=F32)
    sel = _topk_mask(gate_t, blk_ids < own_blk, MOBA_TOPK, axis=0).astype(F32)

    def sel_row(j):
        return jnp.sum(jnp.where(blk_ids == j, sel, 0.0), axis=0, keepdims=True) > 0.5

    def pair_scores(t):
        off = pl.multiple_of(t * pair, pair)
        return lax.dot_general(k16_ref[pl.ds(off, pair), :], q16, nt_dims, preferred_element_type=F32)

    def softmax_pair(s, vis0, vis1, m_prev, cst):
        h0 = jnp.where(vis0, s[0:blk, :] + bias_ref[0:blk, :], NEG_INF)
        h1 = jnp.where(vis1, s[blk:pair, :] + bias_ref[blk:pair, :], NEG_INF)
        mx = jnp.maximum(jnp.max(h0, axis=0, keepdims=True), jnp.max(h1, axis=0, keepdims=True)) + cst
        m_new = mx if m_prev is None else jnp.maximum(m_prev, mx)
        off = m_new - cst
        p0 = jnp.exp(h0 - off)
        p1 = jnp.exp(h1 - off)
        psum = jnp.sum(p0, axis=0, keepdims=True) + jnp.sum(p1, axis=0, keepdims=True)
        return m_new, jnp.concatenate([p0, p1], axis=0).astype(BF16), psum

    key_id = lax.broadcasted_iota(I32, (blk, width), 0)
    vis_top = ((qloc < blk) & (key_id <= qloc)) | ((qloc >= blk) & sel_row(2 * ti))
    vis_bot = key_id + blk <= qloc
    m0, p, l0 = softmax_pair(pair_scores(ti), vis_top, vis_bot, None, jnp.zeros((1, width), F32))
    acc_ref[...] = jnp.dot(vt_ref[ti], p, preferred_element_type=F32)

    s_ref[0] = pair_scores(0)

    def body(t, carry):
        m, l = carry
        s = s_ref[t % 2]
        s_ref[(t + 1) % 2] = pair_scores(jnp.minimum(t + 1, n_pair - 1))
        cst = -slope_vec * ((ti - t) * pair).astype(F32)
        m_new, p, psum = softmax_pair(s, sel_row(2 * t), sel_row(2 * t + 1), m, cst)
        a = jnp.exp(m - m_new)
        acc_ref[...] = a * acc_ref[...] + jnp.dot(vt_ref[t], p, preferred_element_type=F32)
        return m_new, a * l + psum

    _, l = lax.fori_loop(0, ti, body, (m0, l0))
    o_t = acc_ref[...] / l
    for hh in range(gq):
        o_ref[:, hh * HEAD_DIM:(hh + 1) * HEAD_DIM] = o_t[:, hh * pair:(hh + 1) * pair].T


def _attn_prompt(q, k, v, slopes):
    n_b, seq, _ = q.shape
    n_blk = seq // MOBA_BLOCK
    assert n_blk % 2 == 0
    gq = N_HEADS // N_KV_HEADS
    pair = 2 * MOBA_BLOCK
    kern = functools.partial(_attn_prompt_kernel, n_blk=n_blk)
    return pl.pallas_call(
        kern,
        out_shape=jax.ShapeDtypeStruct((n_b, seq, ATTN_WIDTH), F32),
        grid_spec=pltpu.PrefetchScalarGridSpec(
            num_scalar_prefetch=1,
            grid=(n_b, N_KV_HEADS, n_blk // 2),
            in_specs=[
                pl.BlockSpec((None, pair, gq * HEAD_DIM), lambda b, g, i, s: (b, i, g)),
                pl.BlockSpec((None, seq * N_KV_HEADS, HEAD_DIM), lambda b, g, i, s: (b, 0, 0)),
                pl.BlockSpec((None, seq * N_KV_HEADS, HEAD_DIM), lambda b, g, i, s: (b, 0, 0)),
            ],
            out_specs=pl.BlockSpec((None, pair, gq * HEAD_DIM), lambda b, g, i, s: (b, i, g)),
            scratch_shapes=[
                pltpu.VMEM((n_blk, HEAD_DIM), F32),
                pltpu.VMEM((seq, HEAD_DIM), BF16),
                pltpu.VMEM((n_blk // 2, HEAD_DIM, pair), BF16),
                pltpu.VMEM((pair, gq * pair), F32),
                pltpu.VMEM((HEAD_DIM, gq * pair), F32),
                pltpu.VMEM((2, pair, gq * pair), F32),
            ],
        ),
        compiler_params=_cparams(("arbitrary", "arbitrary", "arbitrary")),
        name="moba_prompt",
    )(slopes, q, k, v)


def _attn_sample_kernel(pt_ref, slopes_ref, q_ref, kn_ref, vn_ref, *rest, n_pages, n_seq, t_new, past):
    k_refs = rest[:n_seq * n_pages]
    v_refs = rest[n_seq * n_pages:2 * n_seq * n_pages]
    o_ref = rest[2 * n_seq * n_pages]
    gq = N_HEADS // N_KV_HEADS
    rows = N_HEADS * t_new
    ppb = MOBA_BLOCK // PAGE_SIZE
    n_blk = n_pages // ppb
    page_rows = PAGE_SIZE * N_KV_HEADS
    cols = ppb * page_rows
    nt_dims = (((1,), (1,)), ((), ()))

    row = lax.broadcasted_iota(I32, (rows, 1), 0)
    head = row // t_new
    g_row = head // gq
    t_row = row % t_new
    slope_row = jnp.zeros((rows, 1), F32)
    for h in range(N_HEADS):
        slope_row = jnp.where(head == h, slopes_ref[h], slope_row)
    col = lax.broadcasted_iota(I32, (1, cols), 1)
    g_col = col % N_KV_HEADS
    kpos_col = (col // page_rows) * PAGE_SIZE + (col % page_rows) // N_KV_HEADS
    bias0 = jnp.where(g_row == g_col, -slope_row * (past + t_row - kpos_col).astype(F32), NEG_INF)

    def per_head_rows(fn):
        out = jnp.zeros((rows, HEAD_DIM), F32)
        for g in range(N_KV_HEADS):
            out = jnp.where(g_row == g, fn(g), out)
        return out

    blk_lane = lax.broadcasted_iota(I32, (rows, n_blk), 1)

    def one_sequence(sq):
        q = q_ref[sq] * Q_SCALE
        q16 = q.astype(BF16)
        page0 = sq * n_pages

        gate = jnp.zeros((rows, n_blk), F32)
        k16, v16 = [], []
        for j in range(n_blk):
            kblk = jnp.concatenate([k_refs[page0 + ppb * j + p][...] for p in range(ppb)], axis=0)
            fold = jnp.sum(kblk.reshape(cols // SUBLANES, SUBLANES, HEAD_DIM), axis=0)
            ksum = fold[0:N_KV_HEADS, :] + fold[N_KV_HEADS:, :]
            kmean_rows = per_head_rows(lambda g: ksum[g:g + 1, :]) * (1.0 / MOBA_BLOCK)
            gate = jnp.where(blk_lane == j, jnp.sum(q * kmean_rows, axis=1, keepdims=True), gate)
            k16.append(kblk.astype(BF16))
            v16.extend(v_refs[page0 + ppb * j + p][...].astype(BF16) for p in range(ppb))
        sel = _topk_mask(gate, blk_lane >= 0, MOBA_TOPK, axis=1).astype(F32)

        bias = jnp.concatenate(
            [jnp.where(sel[:, j:j + 1] > 0.5, bias0 + slope_row * float(j * MOBA_BLOCK), NEG_INF)
             for j in range(n_blk)], axis=1)
        s = lax.dot_general(q16, jnp.concatenate(k16, axis=0), nt_dims, preferred_element_type=F32) + bias

        s_own, v_own = [], []
        for tk in range(t_new):
            kn_rows = per_head_rows(lambda g: kn_ref[sq, g, tk:tk + 1, :])
            sv = jnp.sum(q * kn_rows, axis=1, keepdims=True) - slope_row * (t_row - tk).astype(F32)
            s_own.append(jnp.where(t_row >= tk, sv, NEG_INF))
            v_own.append(per_head_rows(lambda g: vn_ref[sq, g, tk:tk + 1, :]))
        m = jnp.max(s, axis=1, keepdims=True)
        for tk in range(t_new):
            m = jnp.maximum(m, s_own[tk])
        p = jnp.exp(s - m)
        num = jnp.dot(p.astype(BF16), jnp.concatenate(v16, axis=0), preferred_element_type=F32)
        den = jnp.sum(p, axis=1, keepdims=True)
        for tk in range(t_new):
            pw = jnp.exp(s_own[tk] - m)
            num = num + pw * v_own[tk]
            den = den + pw
        o_ref[sq] = num / den

    for sq in range(n_seq):
        one_sequence(sq)


def _attn_sample(q, k_new, v_new, cache_k_flat, cache_v_flat, page_table, slopes, layer, n_pool):
    n_dec, rows, _ = q.shape
    t_new = k_new.shape[2]
    n_pages = page_table.shape[1]
    past = n_pages * PAGE_SIZE
    assert past % MOBA_BLOCK == 0
    page_rows = PAGE_SIZE * N_KV_HEADS
    base = layer * n_pool
    pt_flat = page_table.reshape(-1)
    n_seq = SAMPLE_SEQS_PER_STEP
    assert n_dec % n_seq == 0
    kern = functools.partial(_attn_sample_kernel, n_pages=n_pages, n_seq=n_seq, t_new=t_new, past=past)

    def page_spec(p):
        return pl.BlockSpec((page_rows, HEAD_DIM), lambda b, pt, sl: (base + pt[b * (n_seq * n_pages) + p], 0))

    q_spec = pl.BlockSpec((n_seq, rows, HEAD_DIM), lambda b, pt, sl: (b, 0, 0))
    new_spec = pl.BlockSpec((n_seq, N_KV_HEADS, t_new, HEAD_DIM), lambda b, pt, sl: (b, 0, 0, 0))
    pages = [page_spec(p) for p in range(n_seq * n_pages)]
    return pl.pallas_call(
        kern,
        out_shape=jax.ShapeDtypeStruct((n_dec, rows, HEAD_DIM), F32),
        grid_spec=pltpu.PrefetchScalarGridSpec(
            num_scalar_prefetch=2,
            grid=(n_dec // n_seq,),
            in_specs=[q_spec, new_spec, new_spec] + pages + pages,
            out_specs=q_spec,
        ),
        compiler_params=_cparams(("arbitrary",)),
        name="moba_sample",
    )(pt_flat, slopes, q, k_new, v_new, *([cache_k_flat] * len(pages)), *([cache_v_flat] * len(pages)))


def _ssm_kernel(u_ref, bre_ref, bim_ref, cre_ref, cim_ref, are_ref, aim_ref, d_ref, s0r_ref, s0i_ref,
                y_ref, hr_ref, hi_ref, xr_ref, xi_ref, str_ref, sti_ref, *, nb, tc):
    ti = pl.program_id(1)

    @pl.when(ti == 0)
    def _():
        str_ref[...] = s0r_ref[...]
        sti_ref[...] = s0i_ref[...]

    u = u_ref[...]
    u16 = u.astype(BF16)
    xr_ref[...] = jnp.dot(u16, bre_ref[...], preferred_element_type=F32)
    xi_ref[...] = jnp.dot(u16, bim_ref[...], preferred_element_type=F32)
    a_re = jnp.broadcast_to(are_ref[...], (nb, SSM_CHUNK_STATES))
    a_im = jnp.broadcast_to(aim_ref[...], (nb, SSM_CHUNK_STATES))

    def step(t, carry):
        h_re, h_im = carry
        sl = pl.ds(pl.multiple_of(t * nb, nb), nb)
        n_re = (a_re * h_re - a_im * h_im) + xr_ref[sl, :]
        n_im = (a_re * h_im + a_im * h_re) + xi_ref[sl, :]
        xr_ref[sl, :] = n_re
        xi_ref[sl, :] = n_im
        return n_re, n_im

    h_re, h_im = lax.fori_loop(0, tc, step, (str_ref[...], sti_ref[...]))
    str_ref[...] = h_re
    sti_ref[...] = h_im
    y = (jnp.dot(xr_ref[...].astype(BF16), cre_ref[...], preferred_element_type=F32)
         - jnp.dot(xi_ref[...].astype(BF16), cim_ref[...], preferred_element_type=F32)
         + d_ref[...] * u)
    y_ref[...] = jax.nn.gelu(y)

    @pl.when(ti == pl.num_programs(1) - 1)
    def _():
        hr_ref[...] = h_re
        hi_ref[...] = h_im


def _ssm(u_chunks, s0_re, s0_im, prm, layer, nb, tc):
    c0 = layer * N_SSM_CHUNKS
    n_rows = u_chunks.shape[1]
    nt = n_rows // (tc * nb)
    rows = tc * nb
    kern = functools.partial(_ssm_kernel, nb=nb, tc=tc)
    cs = SSM_CHUNK_STATES
    n_states = SSM_GROUPS * SSM_STATE
    return pl.pallas_call(
        kern,
        out_shape=(
            jax.ShapeDtypeStruct((N_SSM_CHUNKS, n_rows, LANES), F32),
            jax.ShapeDtypeStruct((nb, n_states), F32),
            jax.ShapeDtypeStruct((nb, n_states), F32),
        ),
        grid=(N_SSM_CHUNKS, nt),
        in_specs=[
            pl.BlockSpec((None, rows, LANES), lambda c, t: (c, t, 0)),
            pl.BlockSpec((None, SSM_CHUNK, cs), lambda c, t: (c0 + c, 0, 0)),
            pl.BlockSpec((None, SSM_CHUNK, cs), lambda c, t: (c0 + c, 0, 0)),
            pl.BlockSpec((None, cs, SSM_CHUNK), lambda c, t: (c0 + c, 0, 0)),
            pl.BlockSpec((None, cs, SSM_CHUNK), lambda c, t: (c0 + c, 0, 0)),
            pl.BlockSpec((1, cs), lambda c, t: (0, c0 + c)),
            pl.BlockSpec((1, cs), lambda c, t: (0, c0 + c)),
            pl.BlockSpec((1, SSM_CHUNK), lambda c, t: (0, c0 + c)),
            pl.BlockSpec((nb, cs), lambda c, t: (0, c)),
            pl.BlockSpec((nb, cs), lambda c, t: (0, c)),
        ],
        out_specs=(
            pl.BlockSpec((None, rows, LANES), lambda c, t: (c, t, 0)),
            pl.BlockSpec((nb, cs), lambda c, t: (0, c)),
            pl.BlockSpec((nb, cs), lambda c, t: (0, c)),
        ),
        scratch_shapes=[
            pltpu.VMEM((rows, cs), F32),
            pltpu.VMEM((rows, cs), F32),
            pltpu.VMEM((nb, cs), F32),
            pltpu.VMEM((nb, cs), F32),
        ],
        compiler_params=_cparams(("arbitrary", "arbitrary")),
        name="s5_mixer",
    )(u_chunks, prm["bbd_re"], prm["bbd_im"], prm["cbd_re"], prm["cbd_im"], prm["ab_re"], prm["ab_im"],
      prm["d"], s0_re, s0_im)


def _ssm_params(a_re, a_im, log_dt, b_re, b_im, c_re, c_im, d_skip):
    dt = jnp.exp(log_dt)
    mag = jnp.exp(a_re * dt)
    ab_re = mag * jnp.cos(a_im * dt)
    ab_im = mag * jnp.sin(a_im * dt)
    den = a_re * a_re + a_im * a_im
    f_re = ((ab_re - 1.0) * a_re + ab_im * a_im) / den
    f_im = (ab_im * a_re - (ab_re - 1.0) * a_im) / den
    bb_re = f_re[..., None] * b_re - f_im[..., None] * b_im
    bb_im = f_re[..., None] * b_im + f_im[..., None] * b_re
    gpc = SSM_CHUNK // SSM_GROUP_CH
    eye = jnp.eye(gpc, dtype=F32)

    def b_blockdiag(bb):
        x = bb.reshape(N_SSM_CHUNKS, gpc, SSM_STATE, SSM_GROUP_CH)
        m = jnp.einsum("kgpc,gh->kgchp", x, eye)
        return m.reshape(N_SSM_CHUNKS, gpc * SSM_GROUP_CH, gpc * SSM_STATE).astype(BF16)

    def c_blockdiag(cc):
        x = cc.reshape(N_SSM_CHUNKS, gpc, SSM_GROUP_CH, SSM_STATE)
        m = jnp.einsum("kgcp,gh->kgphc", x, eye)
        return m.reshape(N_SSM_CHUNKS, gpc * SSM_STATE, gpc * SSM_GROUP_CH).astype(BF16)

    return {
        "bbd_re": b_blockdiag(bb_re), "bbd_im": b_blockdiag(bb_im),
        "cbd_re": c_blockdiag(c_re), "cbd_im": c_blockdiag(c_im),
        "ab_re": ab_re.reshape(1, -1), "ab_im": ab_im.reshape(1, -1),
        "d": d_skip.reshape(1, -1),
    }


def _rank_lt(vals, a):
    cnt = jnp.zeros(vals[a].shape, I32)
    for b in range(len(vals)):
        if b == a:
            continue
        before = (vals[b] >= vals[a]) if b < a else (vals[b] > vals[a])
        cnt = cnt + before.astype(I32)
    return cnt


def _router(x1, rwt_ref, rb_ref, eid_ref, gate_ref, cols):
    logits = lax.dot_general(rwt_ref[...], x1, (((1,), (1,)), ((), ())), precision=HIGHEST,
                             preferred_element_type=F32)
    mx = jnp.max(logits, axis=0, keepdims=True)
    ex = jnp.exp(logits - mx)
    probs = ex / jnp.sum(ex, axis=0, keepdims=True)
    biased = probs + rb_ref[...]
    prow = [probs[e:e + 1, :] for e in range(N_EXPERTS)]
    brow = [biased[e:e + 1, :] for e in range(N_EXPERTS)]
    ranks = []
    gscore = []
    for gi in range(N_EXPERT_GROUPS):
        vals = brow[gi * EXPERTS_PER_GROUP:(gi + 1) * EXPERTS_PER_GROUP]
        rk = [_rank_lt(vals, a) for a in range(EXPERTS_PER_GROUP)]
        ranks.append(rk)
        sc = jnp.zeros(vals[0].shape, F32)
        for a in range(EXPERTS_PER_GROUP):
            sc = sc + jnp.where(rk[a] < 2, vals[a], 0.0)
        gscore.append(sc)
    e0 = jnp.zeros(prow[0].shape, I32)
    e1 = jnp.zeros(prow[0].shape, I32)
    p0 = jnp.zeros(prow[0].shape, F32)
    p1 = jnp.zeros(prow[0].shape, F32)
    for gi in range(N_EXPERT_GROUPS):
        chosen = _rank_lt(gscore, gi) == 0
        for a in range(EXPERTS_PER_GROUP):
            e = gi * EXPERTS_PER_GROUP + a
            first = chosen & (ranks[gi][a] == 0)
            second = chosen & (ranks[gi][a] == 1)
            e0 = jnp.where(first, e, e0)
            e1 = jnp.where(second, e, e1)
            p0 = jnp.where(first, prow[e], p0)
            p1 = jnp.where(second, prow[e], p1)
    tot = p0 + p1
    eid_ref[0:1, cols] = e0
    eid_ref[1:2, cols] = e1
    gate_ref[0:1, cols] = p0 / tot
    gate_ref[1:2, cols] = p1 / tot


def _postmix_kernel(att_ref, y_ref, x_ref, wglu_ref, wout_ref, ag_ref, sg_ref, lg_ref, lb_ref, rwt_ref, rb_ref,
                    x1_ref, eid_ref, gate_ref, ybuf_ref, *, n_b, tt, alpha):
    rows = x1_ref.shape[0] if n_b is None else n_b * tt
    if n_b is None:
        att = att_ref[...]
        x = x_ref[...]
        for c in range(N_SSM_CHUNKS):
            ybuf_ref[:, c * LANES:(c + 1) * LANES] = y_ref[c]
    else:
        for c in range(N_SSM_CHUNKS):
            for b in range(n_b):
                ybuf_ref[b * tt:(b + 1) * tt, c * LANES:(c + 1) * LANES] = y_ref[c, pl.ds(b, tt, stride=n_b), :]
    z_all = jnp.dot(ybuf_ref[...].astype(BF16), wglu_ref[...], preferred_element_type=F32)
    hrows = rows // POSTMIX_SPLIT
    for hf in range(POSTMIX_SPLIT):
        r0 = hf * hrows
        if n_b is None:
            att_h = att[r0:r0 + hrows, :]
            x_h = x[r0:r0 + hrows, :]
        else:
            att_h = att_ref[hf * (n_b // POSTMIX_SPLIT):(hf + 1) * (n_b // POSTMIX_SPLIT)].reshape(hrows, ATTN_WIDTH)
            x_h = x_ref[hf * (n_b // POSTMIX_SPLIT):(hf + 1) * (n_b // POSTMIX_SPLIT)].reshape(hrows, D_MODEL)
        y = ybuf_ref[r0:r0 + hrows, :]
        sg = y * jax.nn.sigmoid(z_all[r0:r0 + hrows, :])
        ssm_n = sg * lax.rsqrt(jnp.mean(sg * sg, axis=-1, keepdims=True) + RMS_EPS) * sg_ref[...]
        att_n = att_h * lax.rsqrt(jnp.mean(att_h * att_h, axis=-1, keepdims=True) + RMS_EPS) * ag_ref[...]
        mix = (jnp.dot(att_n.astype(BF16), wout_ref[0:ATTN_WIDTH, :], preferred_element_type=F32)
               + jnp.dot(ssm_n.astype(BF16), wout_ref[ATTN_WIDTH:, :], preferred_element_type=F32))
        r = alpha * x_h + mix
        mu = jnp.mean(r, axis=-1, keepdims=True)
        rc = r - mu
        var = jnp.mean(rc * rc, axis=-1, keepdims=True)
        x1 = rc * lax.rsqrt(var + LN_EPS) * lg_ref[...] + lb_ref[...]
        if n_b is None:
            x1_ref[r0:r0 + hrows, :] = x1
        else:
            nbh = n_b // POSTMIX_SPLIT
            x1_ref[hf * nbh:(hf + 1) * nbh] = x1.reshape(nbh, tt, D_MODEL)
        _router(x1, rwt_ref, rb_ref, eid_ref, gate_ref, slice(r0, r0 + hrows))


def _postmix(att, y_chunks, x, lw, layer, alpha, prompt):
    if prompt:
        n_b, seq, _ = x.shape
        tt = TILE_T
        nt = seq // tt
        rows = n_b * tt
        n_tok = n_b * seq
        att_spec = pl.BlockSpec((n_b, tt, ATTN_WIDTH), lambda i: (0, i, 0))
        x_spec = pl.BlockSpec((n_b, tt, D_MODEL), lambda i: (0, i, 0))
        x1_shape = jax.ShapeDtypeStruct((n_b, seq, D_MODEL), F32)
        kern = functools.partial(_postmix_kernel, n_b=n_b, tt=tt, alpha=alpha)
    else:
        n_tok = x.shape[0]
        rows = 256
        nt = n_tok // rows
        att_spec = pl.BlockSpec((rows, ATTN_WIDTH), lambda i: (i, 0))
        x_spec = pl.BlockSpec((rows, D_MODEL), lambda i: (i, 0))
        x1_shape = jax.ShapeDtypeStruct((n_tok, D_MODEL), F32)
        kern = functools.partial(_postmix_kernel, n_b=None, tt=None, alpha=alpha)
    full = lambda shp: pl.BlockSpec(shp, lambda i: tuple(0 for _ in shp))
    return pl.pallas_call(
        kern,
        out_shape=(x1_shape, jax.ShapeDtypeStruct((2, n_tok), I32), jax.ShapeDtypeStruct((2, n_tok), F32)),
        grid=(nt,),
        in_specs=[
            att_spec,
            pl.BlockSpec((N_SSM_CHUNKS, rows, LANES), lambda i: (0, i, 0)),
            x_spec,
            pl.BlockSpec((None, SSM_WIDTH, SSM_WIDTH), lambda i: (layer, 0, 0), pipeline_mode=pl.Buffered(1)),
            pl.BlockSpec((None, D_MODEL, D_MODEL), lambda i: (layer, 0, 0), pipeline_mode=pl.Buffered(1)),
            full((1, ATTN_WIDTH)),
            full((1, SSM_WIDTH)),
            full((1, D_MODEL)),
            full((1, D_MODEL)),
            full((N_EXPERTS, D_MODEL)),
            full((N_EXPERTS, 1)),
        ],
        out_specs=(x_spec, pl.BlockSpec((2, rows), lambda i: (0, i)), pl.BlockSpec((2, rows), lambda i: (0, i))),
        scratch_shapes=[pltpu.VMEM((rows, SSM_WIDTH), F32)],
        compiler_params=_cparams(("arbitrary",)),
        name="postmix_prompt" if prompt else "postmix_sample",
    )(att, y_chunks, x, lw["w_glu"], lw["w_out"], lw["attn_g"], lw["ssm_g"], lw["ln1_g"], lw["ln1_b"],
      lw["router_wt"], lw["router_b"])


def _slab_rows(row, slab):
    start = row * slab
    return pl.ds(start if isinstance(start, int) else pl.multiple_of(start, slab), slab)


def _slab_copy(src_ref, src_row, dst_ref, dst_row, sem, slab=SLAB):
    return pltpu.make_async_copy(src_ref.at[_slab_rows(src_row, slab), :], dst_ref.at[_slab_rows(dst_row, slab), :],
                                 sem)


def _pack_bf16_pair(a, b):
    hi = lax.bitcast_convert_type(a.astype(BF16).astype(F32), jnp.uint32)
    lo = lax.bitcast_convert_type(b.astype(BF16).astype(F32), jnp.uint32)
    return hi | (lo >> 16)


def _unpack_bf16_pair(w):
    a = lax.bitcast_convert_type(w & jnp.uint32(0xFFFF0000), F32)
    b = lax.bitcast_convert_type(w << 16, F32)
    return a.astype(BF16), b.astype(BF16)


def _dispatch_kernel(dest_ref, pad_ref, xp_ref, xs_ref, out_ref, slab_ref, sem, *, n_b, tt, n_prompt_tiles, n_blocks):
    i = pl.program_id(0)
    last = pl.num_programs(0) - 1
    rows = n_b * tt
    slot = i % 2
    slab = slab_ref.at[slot]

    half = D_MODEL // 2

    @pl.when(i < n_prompt_tiles)
    def _():
        for s in range(XSLAB):
            for b in range(n_b):
                slab[pl.ds(b * tt * XSLAB + s, tt, stride=XSLAB), :] = _pack_bf16_pair(
                    xp_ref[b, :, s * LANES:(s + 1) * LANES], xp_ref[b, :, half + s * LANES:half + (s + 1) * LANES])

    @pl.when(i >= n_prompt_tiles)
    def _():
        for s in range(XSLAB):
            slab[pl.ds(s, rows, stride=XSLAB), :] = _pack_bf16_pair(
                xs_ref[:, s * LANES:(s + 1) * LANES], xs_ref[:, half + s * LANES:half + (s + 1) * LANES])

    base = i * (2 * rows)

    def start(r, c):
        for k in range(2):
            _slab_copy(slab, r, out_ref, dest_ref[base + 2 * r + k], sem.at[slot], XSLAB).start()
        return c

    lax.fori_loop(0, rows, start, 0, unroll=DMA_ISSUE_UNROLL)

    def wait_tile(which):
        for _ in range(2):
            pltpu.make_async_copy(slab_ref.at[which], slab_ref.at[which], sem.at[which]).wait()

    @pl.when(i > 0)
    def _():
        wait_tile(1 - slot)

    @pl.when(i == last)
    def _():
        wait_tile(slot)
        slab[...] = jnp.zeros(slab.shape, jnp.uint32)
        zsem = sem.at[slot]

        def per_expert(e, c):
            lo = pad_ref[e]
            hi = pad_ref[N_EXPERTS + e]

            def zs(s, c2):
                _slab_copy(slab, 0, out_ref, s, zsem, XSLAB).start()
                return c2

            lax.fori_loop(lo, hi, zs, 0)

            def zw(s, c2):
                _slab_copy(slab, 0, out_ref, s, zsem, XSLAB).wait()
                return c2

            lax.fori_loop(lo, hi, zw, 0)
            return c

        lax.fori_loop(0, N_EXPERTS, per_expert, 0)

        blk_rows = MOE_BLOCK * XSLAB

        def block_copy(b):
            start = pl.multiple_of(b * blk_rows, blk_rows)
            return pltpu.make_async_copy(slab.at[pl.ds(0, blk_rows), :], out_ref.at[pl.ds(start, blk_rows), :], zsem)

        def zbs(b, c):
            block_copy(b).start()
            return c

        def zbw(b, c):
            block_copy(b).wait()
            return c

        first_unused = pad_ref[2 * N_EXPERTS - 1] // MOE_BLOCK
        lax.fori_loop(first_unused, n_blocks, zbs, 0)
        lax.fori_loop(first_unused, n_blocks, zbw, 0)


def _dispatch(x1_p, x1_s, dest, pad_bounds, n_slots):
    n_b, seq, _ = x1_p.shape
    tt = TILE_T
    rows = n_b * tt
    assert rows >= MOE_BLOCK and x1_s.shape[0] % rows == 0
    ntp = seq // tt
    nts = x1_s.shape[0] // rows
    kern = functools.partial(_dispatch_kernel, n_b=n_b, tt=tt, n_prompt_tiles=ntp, n_blocks=n_slots // MOE_BLOCK)
    return pl.pallas_call(
        kern,
        out_shape=jax.ShapeDtypeStruct((n_slots * XSLAB, LANES), jnp.uint32),
        grid_spec=pltpu.PrefetchScalarGridSpec(
            num_scalar_prefetch=2,
            grid=(ntp + nts,),
            in_specs=[
                pl.BlockSpec((n_b, tt, D_MODEL), lambda i, d, p: (0, jnp.minimum(i, ntp - 1), 0)),
                pl.BlockSpec((rows, D_MODEL), lambda i, d, p: (jnp.maximum(i - ntp, 0), 0)),
            ],
            out_specs=pl.BlockSpec(memory_space=pl.ANY),
            scratch_shapes=[
                pltpu.VMEM((2, rows * XSLAB, LANES), jnp.uint32),
                pltpu.SemaphoreType.DMA((2,)),
            ],
        ),
        compiler_params=_cparams(("arbitrary",)),
        name="moe_dispatch",
    )(dest, pad_bounds, x1_p, x1_s)


def _new_expert(bexp_ref, i):
    return (i == 0) | (bexp_ref[i] != bexp_ref[jnp.maximum(i - 1, 0)])


def _ffn_up_kernel(bexp_ref, nused_ref, xs_ref, wg_ref, wu_ref, h_ref, xb_ref, wg16_ref, wu16_ref):
    i = pl.program_id(0)
    rows = MOE_BLOCK

    @pl.when(_new_expert(bexp_ref, i))
    def _():
        wg16_ref[...] = wg_ref[...].astype(BF16)
        wu16_ref[...] = wu_ref[...].astype(BF16)

    @pl.when(i < nused_ref[0])
    def _():
        half = D_MODEL // 2
        for s in range(XSLAB):
            a, b = _unpack_bf16_pair(xs_ref[pl.ds(s, rows, stride=XSLAB), :])
            xb_ref[:, s * LANES:(s + 1) * LANES] = a
            xb_ref[:, half + s * LANES:half + (s + 1) * LANES] = b
        xb = xb_ref[...]
        g = jnp.dot(xb, wg16_ref[...], preferred_element_type=F32)
        u = jnp.dot(xb, wu16_ref[...], preferred_element_type=F32)
        h_ref[...] = (g * jax.nn.sigmoid(g) * u).astype(BF16)

    @pl.when(i >= nused_ref[0])
    def _():
        h_ref[...] = jnp.zeros(h_ref.shape, BF16)


def _ffn_down_kernel(bexp_ref, nused_ref, h_ref, wd_ref, ys_ref, wd16_ref):
    i = pl.program_id(0)
    rows = MOE_BLOCK

    @pl.when(_new_expert(bexp_ref, i))
    def _():
        wd16_ref[...] = wd_ref[...].astype(BF16)

    @pl.when(i < nused_ref[0])
    def _():
        y = jnp.dot(h_ref[...], wd16_ref[...], preferred_element_type=F32)
        for s in range(SLAB):
            ys_ref[pl.ds(s, rows, stride=SLAB), :] = y[:, s * LANES:(s + 1) * LANES]

    @pl.when(i >= nused_ref[0])
    def _():
        ys_ref[...] = jnp.zeros(ys_ref.shape, F32)


def _ffn(xs, w_gate, w_up, w_down, blk_exp, n_used, n_blocks, layer):
    rows = MOE_BLOCK
    e0 = layer * N_EXPERTS
    n_slots = n_blocks * rows

    def live_map(i, be, nu):
        return (jnp.minimum(i, nu[0] - 1), 0)

    def w_map(i, be, nu):
        return (e0 + be[i], 0, 0)

    h = pl.pallas_call(
        _ffn_up_kernel,
        out_shape=jax.ShapeDtypeStruct((n_slots, D_FF), BF16),
        grid_spec=pltpu.PrefetchScalarGridSpec(
            num_scalar_prefetch=2,
            grid=(n_blocks,),
            in_specs=[
                pl.BlockSpec((rows * XSLAB, LANES), live_map),
                pl.BlockSpec((None, D_MODEL, D_FF), w_map),
                pl.BlockSpec((None, D_MODEL, D_FF), w_map),
            ],
            out_specs=pl.BlockSpec((rows, D_FF), lambda i, be, nu: (i, 0)),
            scratch_shapes=[
                pltpu.VMEM((rows, D_MODEL), BF16),
                pltpu.VMEM((D_MODEL, D_FF), BF16),
                pltpu.VMEM((D_MODEL, D_FF), BF16),
            ],
        ),
        compiler_params=_cparams(("arbitrary",)),
        name="moe_ffn_up",
    )(blk_exp, n_used, xs, w_gate, w_up)
    return pl.pallas_call(
        _ffn_down_kernel,
        out_shape=jax.ShapeDtypeStruct((n_slots * SLAB, LANES), F32),
        grid_spec=pltpu.PrefetchScalarGridSpec(
            num_scalar_prefetch=2,
            grid=(n_blocks,),
            in_specs=[
                pl.BlockSpec((rows, D_FF), live_map),
                pl.BlockSpec((None, D_FF, D_MODEL), w_map),
            ],
            out_specs=pl.BlockSpec((rows * SLAB, LANES), lambda i, be, nu: (i, 0)),
            scratch_shapes=[pltpu.VMEM((D_FF, D_MODEL), BF16)],
        ),
        compiler_params=_cparams(("arbitrary",)),
        name="moe_ffn_down",
    )(blk_exp, n_used, h, w_down)


def _combine_kernel(dest_ref, x1_ref, gate_ref, lg_ref, lb_ref, ys_ref, x2_ref, y0_ref, y1_ref, sem, *, n_b, tt, alpha):
    i = pl.program_id(0)
    rows = y0_ref.shape[1] // SLAB
    slot = i % 2

    def fetch(tile, which):
        base = tile * (2 * rows)

        def start(r, c):
            _slab_copy(ys_ref, dest_ref[base + 2 * r], y0_ref.at[which], r, sem.at[which]).start()
            _slab_copy(ys_ref, dest_ref[base + 2 * r + 1], y1_ref.at[which], r, sem.at[which]).start()
            return c

        lax.fori_loop(0, rows, start, 0, unroll=DMA_ISSUE_UNROLL)

    @pl.when(i == 0)
    def _():
        fetch(0, 0)

    @pl.when(i + 1 < pl.num_programs(0))
    def _():
        fetch(i + 1, 1 - slot)

    pltpu.make_async_copy(y0_ref.at[slot], y0_ref.at[slot], sem.at[slot]).wait()
    pltpu.make_async_copy(y1_ref.at[slot], y1_ref.at[slot], sem.at[slot]).wait()

    y0 = y0_ref.at[slot]
    y1 = y1_ref.at[slot]
    gate_rows = jnp.concatenate([gate_ref[...], jnp.zeros((SUBLANES - 2, rows), F32)], axis=0)
    gate_cols = gate_rows.T
    g0 = gate_cols[:, 0:1]
    g1 = gate_cols[:, 1:2]
    if n_b is None:
        x1 = x1_ref[...]
    else:
        x1 = x1_ref[...].reshape(rows, D_MODEL)
    parts = []
    for s in range(SLAB):
        moe = g0 * y0[pl.ds(s, rows, stride=SLAB), :] + g1 * y1[pl.ds(s, rows, stride=SLAB), :]
        parts.append(alpha * x1[:, s * LANES:(s + 1) * LANES] + moe)
    r = jnp.concatenate(parts, axis=1)
    mu = jnp.mean(r, axis=-1, keepdims=True)
    rc = r - mu
    var = jnp.mean(rc * rc, axis=-1, keepdims=True)
    x2 = rc * lax.rsqrt(var + LN_EPS) * lg_ref[...] + lb_ref[...]
    if n_b is None:
        x2_ref[...] = x2
    else:
        x2_ref[...] = x2.reshape(n_b, tt, D_MODEL)


def _combine(x1, gate_t, dest, ys, ln_g, ln_b, alpha, prompt):
    if prompt:
        n_b, seq, _ = x1.shape
        tt = TILE_T
        nbs = COMBINE_ROWS // tt
        parts = n_b // nbs
        rows = nbs * tt
        nt = (seq // tt) * parts
        x_spec = pl.BlockSpec((nbs, tt, D_MODEL), lambda i, d: (i % parts, i // parts, 0))
        kern = functools.partial(_combine_kernel, n_b=nbs, tt=tt, alpha=alpha)
    else:
        rows = 256
        nt = x1.shape[0] // rows
        x_spec = pl.BlockSpec((rows, D_MODEL), lambda i, d: (i, 0))
        kern = functools.partial(_combine_kernel, n_b=None, tt=None, alpha=alpha)
    return pl.pallas_call(
        kern,
        out_shape=jax.ShapeDtypeStruct(x1.shape, F32),
        grid_spec=pltpu.PrefetchScalarGridSpec(
            num_scalar_prefetch=1,
            grid=(nt,),
            in_specs=[
                x_spec,
                pl.BlockSpec((2, rows), lambda i, d: (0, i)),
                pl.BlockSpec((1, D_MODEL), lambda i, d: (0, 0)),
                pl.BlockSpec((1, D_MODEL), lambda i, d: (0, 0)),
                pl.BlockSpec(memory_space=pl.ANY),
            ],
            out_specs=x_spec,
            scratch_shapes=[
                pltpu.VMEM((2, rows * SLAB, LANES), F32),
                pltpu.VMEM((2, rows * SLAB, LANES), F32),
                pltpu.SemaphoreType.DMA((2,)),
            ],
        ),
        compiler_params=_cparams(("arbitrary",)),
        name="moe_combine_prompt" if prompt else "moe_combine_sample",
    )(dest, x1, gate_t, ln_g, ln_b, ys)


def _moe_plan(eid_t, n_blocks):
    e_flat = eid_t.T.reshape(-1)
    onehot = (e_flat[:, None] == jnp.arange(N_EXPERTS, dtype=I32)[None, :]).astype(I32)
    csum = jnp.cumsum(onehot, axis=0)
    rank = jnp.sum(csum * onehot, axis=1) - 1
    counts = csum[-1]
    padded = (counts + MOE_BLOCK - 1) // MOE_BLOCK * MOE_BLOCK
    pad_end = jnp.cumsum(padded)
    pad_start = pad_end - padded
    dest = (jnp.sum(onehot * pad_start[None, :], axis=1) + rank).astype(I32)
    n_used = (pad_end[-1] // MOE_BLOCK).astype(I32)
    first_slot = jnp.arange(n_blocks, dtype=I32) * MOE_BLOCK
    blk_exp = jnp.minimum(jnp.sum((first_slot[:, None] >= pad_end[None, :]).astype(I32), axis=1), N_EXPERTS - 1)
    last_exp = jnp.take(blk_exp, jnp.maximum(n_used - 1, 0))
    blk_exp = jnp.where(jnp.arange(n_blocks) < n_used, blk_exp, last_exp).astype(I32)
    pad_bounds = jnp.concatenate([pad_start + counts, pad_end]).astype(I32)
    return dest, pad_bounds, blk_exp, n_used.reshape(1)


def kernel(x_prompt, x_sample, cache_k, cache_v, state_ssm_re, state_ssm_im, page_table, w_in, w_out, attn_norm_g, ssm_norm_g, ssm_a_re, ssm_a_im, ssm_log_dt, ssm_b_re, ssm_b_im, ssm_c_re, ssm_c_im, ssm_d, ssm_w_glu, ln1_g, ln1_b, router_w, router_b, moe_w_gate, moe_w_up, moe_w_down, ln2_g, ln2_b):
    depth = w_in.shape[0]
    n_b, seq, _ = x_prompt.shape
    n_dec, t_new, _ = x_sample.shape
    n_pool = cache_k.shape[1]
    gq = N_HEADS // N_KV_HEADS
    alpha = (2 * depth) ** 0.25
    n_states = SSM_GROUPS * SSM_STATE
    n_sample = n_dec * t_new
    n_tok = n_b * seq + n_sample
    n_blocks = -(-(2 * n_tok + N_EXPERTS * (MOE_BLOCK - 1)) // MOE_BLOCK)
    n_slots = n_blocks * MOE_BLOCK

    slopes = 2.0 ** (-8.0 * jnp.arange(1, N_HEADS + 1, dtype=F32) / N_HEADS)
    cache_k_flat = cache_k.reshape(-1, HEAD_DIM)
    cache_v_flat = cache_v.reshape(-1, HEAD_DIM)
    router_wt = router_w.T
    router_bc = router_b.reshape(N_EXPERTS, 1)
    zeros_state = jnp.zeros((n_b, n_states), F32)
    w_in16 = w_in.astype(BF16)
    w_glu16 = ssm_w_glu.astype(BF16)
    w_out16 = w_out.astype(BF16)
    w_gate = moe_w_gate.reshape(depth * N_EXPERTS, D_MODEL, D_FF)
    w_up = moe_w_up.reshape(depth * N_EXPERTS, D_MODEL, D_FF)
    w_down = moe_w_down.reshape(depth * N_EXPERTS, D_FF, D_MODEL)

    xp = x_prompt
    xs = x_sample.transpose(1, 0, 2).reshape(n_sample, D_MODEL)
    outs = {k: [] for k in ("kp", "vp", "hrp", "hip", "ks", "vs", "hrs", "his")}
    prm = jax.vmap(_ssm_params)(ssm_a_re, ssm_a_im, ssm_log_dt, ssm_b_re, ssm_b_im, ssm_c_re, ssm_c_im, ssm_d)
    prm = {k: (v.reshape((depth * N_SSM_CHUNKS,) + v.shape[2:]) if v.ndim == 4 else v.reshape(1, -1))
           for k, v in prm.items()}
    for l in range(depth):
        lw = {
            "w_glu": w_glu16, "w_out": w_out16,
            "attn_g": attn_norm_g[l].reshape(1, -1), "ssm_g": ssm_norm_g[l].reshape(1, -1),
            "ln1_g": ln1_g[l].reshape(1, -1), "ln1_b": ln1_b[l].reshape(1, -1),
            "router_wt": router_wt, "router_b": router_bc,
        }
        q_p, k_p, v_p, u_p = _in_proj_prompt(xp, w_in16, l)
        att_p = _attn_prompt(q_p, k_p, v_p, slopes)
        y_p, hr_p, hi_p = _ssm(u_p, zeros_state, zeros_state, prm, l, nb=n_b, tc=512)

        h_s = _in_proj_sample(xs, w_in16, l)
        q_s = h_s[:, :ATTN_WIDTH].reshape(t_new, n_dec, N_HEADS, HEAD_DIM)
        q_s = q_s.transpose(1, 2, 0, 3).reshape(n_dec, N_HEADS * t_new, HEAD_DIM)
        k_s = h_s[:, ATTN_WIDTH:ATTN_WIDTH + KV_WIDTH].reshape(t_new, n_dec, N_KV_HEADS, HEAD_DIM)
        v_s = h_s[:, ATTN_WIDTH + KV_WIDTH:ATTN_WIDTH + 2 * KV_WIDTH].reshape(t_new, n_dec, N_KV_HEADS, HEAD_DIM)
        att_s = _attn_sample(q_s, k_s.transpose(1, 2, 0, 3), v_s.transpose(1, 2, 0, 3), cache_k_flat, cache_v_flat,
                             page_table, slopes, l, n_pool)
        att_s = att_s.reshape(n_dec, N_HEADS, t_new, HEAD_DIM).transpose(2, 0, 1, 3).reshape(n_sample, ATTN_WIDTH)
        u_s = h_s[:, ATTN_WIDTH + 2 * KV_WIDTH:].reshape(n_sample, N_SSM_CHUNKS, LANES).transpose(1, 0, 2)
        y_s, hr_s, hi_s = _ssm(u_s, state_ssm_re[l].reshape(n_dec, n_states), state_ssm_im[l].reshape(n_dec, n_states),
                               prm, l, nb=n_dec, tc=t_new)

        x1_p, eid_p, gate_p = _postmix(att_p, y_p, xp, lw, l, alpha, prompt=True)
        x1_s, eid_s, gate_s = _postmix(att_s, y_s, xs, lw, l, alpha, prompt=False)

        eid_t = jnp.concatenate([eid_p, eid_s], axis=1)
        dest, pad_bounds, blk_exp, n_used = _moe_plan(eid_t, n_blocks)
        xs_slots = _dispatch(x1_p, x1_s, dest, pad_bounds, n_slots)
        ys_slots = _ffn(xs_slots, w_gate, w_up, w_down, blk_exp, n_used, n_blocks, l)
        l2g = ln2_g[l].reshape(1, -1)
        l2b = ln2_b[l].reshape(1, -1)
        n_pa = 2 * n_b * seq
        xp = _combine(x1_p, gate_p, dest[:n_pa], ys_slots, l2g, l2b, alpha, prompt=True)
        xs = _combine(x1_s, gate_s, dest[n_pa:], ys_slots, l2g, l2b, alpha, prompt=False)

        outs["kp"].append(k_p.reshape(n_b, seq, N_KV_HEADS, HEAD_DIM))
        outs["vp"].append(v_p.reshape(n_b, seq, N_KV_HEADS, HEAD_DIM))
        outs["hrp"].append(hr_p.reshape(n_b, SSM_GROUPS, SSM_STATE))
        outs["hip"].append(hi_p.reshape(n_b, SSM_GROUPS, SSM_STATE))
        outs["ks"].append(k_s.transpose(1, 0, 2, 3))
        outs["vs"].append(v_s.transpose(1, 0, 2, 3))
        outs["hrs"].append(hr_s.reshape(n_dec, SSM_GROUPS, SSM_STATE))
        outs["his"].append(hi_s.reshape(n_dec, SSM_GROUPS, SSM_STATE))

    y_sample = xs.reshape(t_new, n_dec, D_MODEL).transpose(1, 0, 2)
    return (xp, y_sample,
            jnp.stack(outs["kp"]), jnp.stack(outs["vp"]), jnp.stack(outs["hrp"]), jnp.stack(outs["hip"]),
            jnp.stack(outs["ks"]), jnp.stack(outs["vs"]), jnp.stack(outs["hrs"]), jnp.stack(outs["his"]))
```

```python
import functools
import math

import jax
import jax.numpy as jnp
from jax import lax
from jax.experimental import pallas as pl
from jax.experimental.pallas import tpu as pltpu

F32 = jnp.float32
BF16 = jnp.bfloat16
I32 = jnp.int32
HIGHEST = lax.Precision.HIGHEST

D_MODEL = 2048
ATTN_WIDTH = 1024
SSM_WIDTH = 1024
HEAD_DIM = 128
N_HEADS = 8
N_KV_HEADS = 4
KV_WIDTH = N_KV_HEADS * HEAD_DIM
PROJ_WIDTH = ATTN_WIDTH + 2 * KV_WIDTH + SSM_WIDTH
MOBA_BLOCK = 256
MOBA_TOPK = 3
PAGE_SIZE = 128
SSM_GROUP_CH = 16
SSM_GROUPS = 64
SSM_STATE = 64
N_EXPERTS = 16
N_EXPERT_GROUPS = 4
EXPERTS_PER_GROUP = 4
D_FF = 1024
LN_EPS = 1e-5
RMS_EPS = 1e-6
NEG_INF = float("-inf")
Q_SCALE = HEAD_DIM ** -0.5

LANES = 128
SUBLANES = 8
VMEM_LIMIT = 56 * 1024 * 1024

TILE_T = 64
SLAB = D_MODEL // LANES
XSLAB = D_MODEL // (2 * LANES)
SSM_CHUNK = 8 * SSM_GROUP_CH
SSM_CHUNK_STATES = 8 * SSM_STATE
N_SSM_CHUNKS = SSM_WIDTH // SSM_CHUNK
MOE_BLOCK = 256
POSTMIX_SPLIT = 4
DMA_ISSUE_UNROLL = 16
COMBINE_ROWS = 256
SAMPLE_SEQS_PER_STEP = 2


def _cparams(sem, vmem=VMEM_LIMIT):
    return pltpu.CompilerParams(dimension_semantics=sem, vmem_limit_bytes=vmem)


def _in_proj_prompt_kernel(x_ref, w_ref, q_ref, k_ref, v_ref, u_ref, *, n_b, tt):
    rows = n_b * tt
    x = x_ref[...].reshape(rows, D_MODEL).astype(BF16)
    h = jnp.dot(x, w_ref[...], preferred_element_type=F32)
    q_ref[...] = h[:, :ATTN_WIDTH].reshape(n_b, tt, ATTN_WIDTH)
    for b in range(n_b):
        for g in range(N_KV_HEADS):
            k0 = ATTN_WIDTH + g * HEAD_DIM
            v0 = ATTN_WIDTH + KV_WIDTH + g * HEAD_DIM
            k_ref[b, pl.ds(g, tt, stride=N_KV_HEADS), :] = h[b * tt:(b + 1) * tt, k0:k0 + HEAD_DIM]
            v_ref[b, pl.ds(g, tt, stride=N_KV_HEADS), :] = h[b * tt:(b + 1) * tt, v0:v0 + HEAD_DIM]
    u0 = ATTN_WIDTH + 2 * KV_WIDTH
    for c in range(N_SSM_CHUNKS):
        for b in range(n_b):
            u_ref[c, pl.ds(b, tt, stride=n_b), :] = h[b * tt:(b + 1) * tt, u0 + c * LANES:u0 + (c + 1) * LANES]


def _in_proj_prompt(x, w16, layer):
    n_b, seq, _ = x.shape
    tt = TILE_T
    nt = seq // tt
    kern = functools.partial(_in_proj_prompt_kernel, n_b=n_b, tt=tt)
    return pl.pallas_call(
        kern,
        out_shape=(
            jax.ShapeDtypeStruct((n_b, seq, ATTN_WIDTH), F32),
            jax.ShapeDtypeStruct((n_b, seq * N_KV_HEADS, HEAD_DIM), F32),
            jax.ShapeDtypeStruct((n_b, seq * N_KV_HEADS, HEAD_DIM), F32),
            jax.ShapeDtypeStruct((N_SSM_CHUNKS, seq * n_b, LANES), F32),
        ),
        grid=(nt,),
        in_specs=[
            pl.BlockSpec((n_b, tt, D_MODEL), lambda i: (0, i, 0)),
            pl.BlockSpec((None, D_MODEL, PROJ_WIDTH), lambda i: (layer, 0, 0), pipeline_mode=pl.Buffered(1)),
        ],
        out_specs=(
            pl.BlockSpec((n_b, tt, ATTN_WIDTH), lambda i: (0, i, 0)),
            pl.BlockSpec((n_b, tt * N_KV_HEADS, HEAD_DIM), lambda i: (0, i, 0)),
            pl.BlockSpec((n_b, tt * N_KV_HEADS, HEAD_DIM), lambda i: (0, i, 0)),
            pl.BlockSpec((N_SSM_CHUNKS, tt * n_b, LANES), lambda i: (0, i, 0)),
        ),
        compiler_params=_cparams(("arbitrary",)),
        name="in_proj_prompt",
    )(x, w16)


def _matmul_kernel(x_ref, w_ref, o_ref):
    o_ref[...] = jnp.dot(x_ref[...].astype(BF16), w_ref[...], preferred_element_type=F32)


def _in_proj_sample(x, w16, layer):
    m, k = x.shape
    n = w16.shape[2]
    tm = 256
    return pl.pallas_call(
        _matmul_kernel,
        out_shape=jax.ShapeDtypeStruct((m, n), F32),
        grid=(m // tm,),
        in_specs=[pl.BlockSpec((tm, k), lambda i: (i, 0)), pl.BlockSpec((None, k, n), lambda i: (layer, 0, 0))],
        out_specs=pl.BlockSpec((tm, n), lambda i: (i, 0)),
        compiler_params=_cparams(("arbitrary",)),
        name="in_proj_sample",
    )(x, w16)


def _topk_mask(gate, valid, k, axis):
    nb = gate.shape[axis]
    ids = lax.broadcasted_iota(I32, gate.shape, axis)
    gm = jnp.where(valid, gate, NEG_INF)
    cnt = jnp.zeros(gate.shape, I32)
    for j in range(nb):
        gj = gm[:, j:j + 1] if axis == 1 else gm[j:j + 1, :]
        beats = (gj > gm) | ((gj == gm) & (j < ids))
        cnt = cnt + beats.astype(I32)
    return valid & (cnt < k)


def _attn_prompt_kernel(slopes_ref, q_ref, k_ref, v_ref, o_ref, kmean_ref, k16_ref, vt_ref, bias_ref, acc_ref, s_ref,
                        *, n_blk):
    g = pl.program_id(1)
    ti = pl.program_id(2)
    blk = MOBA_BLOCK
    pair = 2 * blk
    n_pair = n_blk // 2
    gq = N_HEADS // N_KV_HEADS
    width = gq * pair
    nt_dims = (((1,), (1,)), ((), ()))

    lane = lax.broadcasted_iota(I32, (1, width), 1)
    qloc = lane % pair
    slope_vec = jnp.zeros((1, width), F32)
    for hh in range(gq):
        slope_vec = jnp.where(lane // pair == hh, slopes_ref[g * gq + hh], slope_vec)

    @pl.when(ti == 0)
    def _():
        for j in range(n_blk):
            rows_j = pl.ds(j * blk * N_KV_HEADS + g, blk, stride=N_KV_HEADS)
            kb = k_ref[rows_j, :]
            kmean_ref[j:j + 1, :] = jnp.sum(kb, axis=0, keepdims=True) * (1.0 / blk)
            k16_ref[j * blk:(j + 1) * blk, :] = kb.astype(BF16)
            vt_ref[j // 2, :, (j % 2) * blk:(j % 2 + 1) * blk] = v_ref[rows_j, :].T.astype(BF16)
        pair_key = lax.broadcasted_iota(I32, (pair, width), 0)
        bias_ref[...] = -slope_vec * (qloc - pair_key).astype(F32)

    q_all = jnp.concatenate([q_ref[:, hh * HEAD_DIM:(hh + 1) * HEAD_DIM] for hh in range(gq)], axis=0) * Q_SCALE
    q16 = q_all.astype(BF16)
    blk_ids = lax.broadcasted_iota(I32, (n_blk, width), 0)
    own_blk = 2 * ti + qloc // blk
    gate_t = lax.dot_general(kmean_ref[...], q_all, nt_dims, precision=HIGHEST, preferred_element_type=F32)
    sel = _topk_mask(gate_t, blk_ids < own_blk, MOBA_TOPK, axis=0).astype(F32)

    def sel_row(j):
        return jnp.sum(jnp.where(blk_ids == j, sel, 0.0), axis=0, keepdims=True) > 0.5

    def pair_scores(t):
        off = pl.multiple_of(t * pair, pair)
        return lax.dot_general(k16_ref[pl.ds(off, pair), :], q16, nt_dims, preferred_element_type=F32)

    def softmax_pair(s, vis0, vis1, m_prev, cst):
        h0 = jnp.where(vis0, s[0:blk, :] + bias_ref[0:blk, :], NEG_INF)
        h1 = jnp.where(vis1, s[blk:pair, :] + bias_ref[blk:pair, :], NEG_INF)
        mx = jnp.maximum(jnp.max(h0, axis=0, keepdims=True), jnp.max(h1, axis=0, keepdims=True)) + cst
        m_new = mx if m_prev is None else jnp.maximum(m_prev, mx)
        off = m_new - cst
        p0 = jnp.exp(h0 - off)
        p1 = jnp.exp(h1 - off)
        psum = jnp.sum(p0, axis=0, keepdims=True) + jnp.sum(p1, axis=0, keepdims=True)
        return m_new, jnp.concatenate([p0, p1], axis=0).astype(BF16), psum

    key_id = lax.broadcasted_iota(I32, (blk, width), 0)
    vis_top = ((qloc < blk) & (key_id <= qloc)) | ((qloc >= blk) & sel_row(2 * ti))
    vis_bot = key_id + blk <= qloc
    m0, p, l0 = softmax_pair(pair_scores(ti), vis_top, vis_bot, None, jnp.zeros((1, width), F32))
    acc_ref[...] = jnp.dot(vt_ref[ti], p, preferred_element_type=F32)

    s_ref[0] = pair_scores(0)

    def body(t, carry):
        m, l = carry
        s = s_ref[t % 2]
        s_ref[(t + 1) % 2] = pair_scores(jnp.minimum(t + 1, n_pair - 1))
        cst = -slope_vec * ((ti - t) * pair).astype(F32)
        m_new, p, psum = softmax_pair(s, sel_row(2 * t), sel_row(2 * t + 1), m, cst)
        a = jnp.exp(m - m_new)
        acc_ref[...] = a * acc_ref[...] + jnp.dot(vt_ref[t], p, preferred_element_type=F32)
        return m_new, a * l + psum

    _, l = lax.fori_loop(0, ti, body, (m0, l0))
    o_t = acc_ref[...] / l
    for hh in range(gq):
        o_ref[:, hh * HEAD_DIM:(hh + 1) * HEAD_DIM] = o_t[:, hh * pair:(hh + 1) * pair].T


def _attn_prompt(q, k, v, slopes):
    n_b, seq, _ = q.shape
    n_blk = seq // MOBA_BLOCK
    assert n_blk % 2 == 0
    gq = N_HEADS // N_KV_HEADS
    pair = 2 * MOBA_BLOCK
    kern = functools.partial(_attn_prompt_kernel, n_blk=n_blk)
    return pl.pallas_call(
        kern,
        out_shape=jax.ShapeDtypeStruct((n_b, seq, ATTN_WIDTH), F32),
        grid_spec=pltpu.PrefetchScalarGridSpec(
            num_scalar_prefetch=1,
            grid=(n_b, N_KV_HEADS, n_blk // 2),
            in_specs=[
                pl.BlockSpec((None, pair, gq * HEAD_DIM), lambda b, g, i, s: (b, i, g)),
                pl.BlockSpec((None, seq * N_KV_HEADS, HEAD_DIM), lambda b, g, i, s: (b, 0, 0)),
                pl.BlockSpec((None, seq * N_KV_HEADS, HEAD_DIM), lambda b, g, i, s: (b, 0, 0)),
            ],
            out_specs=pl.BlockSpec((None, pair, gq * HEAD_DIM), lambda b, g, i, s: (b, i, g)),
            scratch_shapes=[
                pltpu.VMEM((n_blk, HEAD_DIM), F32),
                pltpu.VMEM((seq, HEAD_DIM), BF16),
                pltpu.VMEM((n_blk // 2, HEAD_DIM, pair), BF16),
                pltpu.VMEM((pair, gq * pair), F32),
                pltpu.VMEM((HEAD_DIM, gq * pair), F32),
                pltpu.VMEM((2, pair, gq * pair), F32),
            ],
        ),
        compiler_params=_cparams(("arbitrary", "arbitrary", "arbitrary")),
        name="moba_prompt",
    )(slopes, q, k, v)


def _attn_sample_kernel(pt_ref, slopes_ref, q_ref, kn_ref, vn_ref, *rest, n_pages, n_seq, t_new, past):
    k_refs = rest[:n_seq * n_pages]
    v_refs = rest[n_seq * n_pages:2 * n_seq * n_pages]
    o_ref = rest[2 * n_seq * n_pages]
    gq = N_HEADS // N_KV_HEADS
    rows = N_HEADS * t_new
    ppb = MOBA_BLOCK // PAGE_SIZE
    n_blk = n_pages // ppb
    page_rows = PAGE_SIZE * N_KV_HEADS
    cols = ppb * page_rows
    nt_dims = (((1,), (1,)), ((), ()))

    row = lax.broadcasted_iota(I32, (rows, 1), 0)
    head = row // t_new
    g_row = head // gq
    t_row = row % t_new
    slope_row = jnp.zeros((rows, 1), F32)
    for h in range(N_HEADS):
        slope_row = jnp.where(head == h, slopes_ref[h], slope_row)
    col = lax.broadcasted_iota(I32, (1, cols), 1)
    g_col = col % N_KV_HEADS
    kpos_col = (col // page_rows) * PAGE_SIZE + (col % page_rows) // N_KV_HEADS
    bias0 = jnp.where(g_row == g_col, -slope_row * (past + t_row - kpos_col).astype(F32), NEG_INF)

    def per_head_rows(fn):
        out = jnp.zeros((rows, HEAD_DIM), F32)
        for g in range(N_KV_HEADS):
            out = jnp.where(g_row == g, fn(g), out)
        return out

    blk_lane = lax.broadcasted_iota(I32, (rows, n_blk), 1)

    def one_sequence(sq):
        q = q_ref[sq] * Q_SCALE
        q16 = q.astype(BF16)
        page0 = sq * n_pages

        gate = jnp.zeros((rows, n_blk), F32)
        k16, v16 = [], []
        for j in range(n_blk):
            kblk = jnp.concatenate([k_refs[page0 + ppb * j + p][...] for p in range(ppb)], axis=0)
            fold = jnp.sum(kblk.reshape(cols // SUBLANES, SUBLANES, HEAD_DIM), axis=0)
            ksum = fold[0:N_KV_HEADS, :] + fold[N_KV_HEADS:, :]
            kmean_rows = per_head_rows(lambda g: ksum[g:g + 1, :]) * (1.0 / MOBA_BLOCK)
            gate = jnp.where(blk_lane == j, jnp.sum(q * kmean_rows, axis=1, keepdims=True), gate)
            k16.append(kblk.astype(BF16))
            v16.extend(v_refs[page0 + ppb * j + p][...].astype(BF16) for p in range(ppb))
        sel = _topk_mask(gate, blk_lane >= 0, MOBA_TOPK, axis=1).astype(F32)

        bias = jnp.concatenate(
            [jnp.where(sel[:, j:j + 1] > 0.5, bias0 + slope_row * float(j * MOBA_BLOCK), NEG_INF)
             for j in range(n_blk)], axis=1)
        s = lax.dot_general(q16, jnp.concatenate(k16, axis=0), nt_dims, preferred_element_type=F32) + bias

        s_own, v_own = [], []
        for tk in range(t_new):
            kn_rows = per_head_rows(lambda g: kn_ref[sq, g, tk:tk + 1, :])
            sv = jnp.sum(q * kn_rows, axis=1, keepdims=True) - slope_row * (t_row - tk).astype(F32)
            s_own.append(jnp.where(t_row >= tk, sv, NEG_INF))
            v_own.append(per_head_rows(lambda g: vn_ref[sq, g, tk:tk + 1, :]))
        m = jnp.max(s, axis=1, keepdims=True)
        for tk in range(t_new):
            m = jnp.maximum(m, s_own[tk])
        p = jnp.exp(s - m)
        num = jnp.dot(p.astype(BF16), jnp.concatenate(v16, axis=0), preferred_element_type=F32)
        den = jnp.sum(p, axis=1, keepdims=True)
        for tk in range(t_new):
            pw = jnp.exp(s_own[tk] - m)
            num = num + pw * v_own[tk]
            den = den + pw
        o_ref[sq] = num / den

    for sq in range(n_seq):
        one_sequence(sq)


def _attn_sample(q, k_new, v_new, cache_k_flat, cache_v_flat, page_table, slopes, layer, n_pool):
    n_dec, rows, _ = q.shape
    t_new = k_new.shape[2]
    n_pages = page_table.shape[1]
    past = n_pages * PAGE_SIZE
    assert past % MOBA_BLOCK == 0
    page_rows = PAGE_SIZE * N_KV_HEADS
    base = layer * n_pool
    pt_flat = page_table.reshape(-1)
    n_seq = SAMPLE_SEQS_PER_STEP
    assert n_dec % n_seq == 0
    kern = functools.partial(_attn_sample_kernel, n_pages=n_pages, n_seq=n_seq, t_new=t_new, past=past)

    def page_spec(p):
        return pl.BlockSpec((page_rows, HEAD_DIM), lambda b, pt, sl: (base + pt[b * (n_seq * n_pages) + p], 0))

    q_spec = pl.BlockSpec((n_seq, rows, HEAD_DIM), lambda b, pt, sl: (b, 0, 0))
    new_spec = pl.BlockSpec((n_seq, N_KV_HEADS, t_new, HEAD_DIM), lambda b, pt, sl: (b, 0, 0, 0))
    pages = [page_spec(p) for p in range(n_seq * n_pages)]
    return pl.pallas_call(
        kern,
        out_shape=jax.ShapeDtypeStruct((n_dec, rows, HEAD_DIM), F32),
        grid_spec=pltpu.PrefetchScalarGridSpec(
            num_scalar_prefetch=2,
            grid=(n_dec // n_seq,),
            in_specs=[q_spec, new_spec, new_spec] + pages + pages,
            out_specs=q_spec,
        ),
        compiler_params=_cparams(("arbitrary",)),
        name="moba_sample",
    )(pt_flat, slopes, q, k_new, v_new, *([cache_k_flat] * len(pages)), *([cache_v_flat] * len(pages)))


def _ssm_kernel(u_ref, bre_ref, bim_ref, cre_ref, cim_ref, are_ref, aim_ref, d_ref, s0r_ref, s0i_ref,
                y_ref, hr_ref, hi_ref, xr_ref, xi_ref, str_ref, sti_ref, *, nb, tc):
    ti = pl.program_id(1)

    @pl.when(ti == 0)
    def _():
        str_ref[...] = s0r_ref[...]
        sti_ref[...] = s0i_ref[...]

    u = u_ref[...]
    u16 = u.astype(BF16)
    xr_ref[...] = jnp.dot(u16, bre_ref[...], preferred_element_type=F32)
    xi_ref[...] = jnp.dot(u16, bim_ref[...], preferred_element_type=F32)
    a_re = jnp.broadcast_to(are_ref[...], (nb, SSM_CHUNK_STATES))
    a_im = jnp.broadcast_to(aim_ref[...], (nb, SSM_CHUNK_STATES))

    def step(t, carry):
        h_re, h_im = carry
        sl = pl.ds(pl.multiple_of(t * nb, nb), nb)
        n_re = (a_re * h_re - a_im * h_im) + xr_ref[sl, :]
        n_im = (a_re * h_im + a_im * h_re) + xi_ref[sl, :]
        xr_ref[sl, :] = n_re
        xi_ref[sl, :] = n_im
        return n_re, n_im

    h_re, h_im = lax.fori_loop(0, tc, step, (str_ref[...], sti_ref[...]))
    str_ref[...] = h_re
    sti_ref[...] = h_im
    y = (jnp.dot(xr_ref[...].astype(BF16), cre_ref[...], preferred_element_type=F32)
         - jnp.dot(xi_ref[...].astype(BF16), cim_ref[...], preferred_element_type=F32)
         + d_ref[...] * u)
    y_ref[...] = jax.nn.gelu(y)

    @pl.when(ti == pl.num_programs(1) - 1)
    def _():
        hr_ref[...] = h_re
        hi_ref[...] = h_im


def _ssm(u_chunks, s0_re, s0_im, prm, layer, nb, tc):
    c0 = layer * N_SSM_CHUNKS
    n_rows = u_chunks.shape[1]
    nt = n_rows // (tc * nb)
    rows = tc * nb
    kern = functools.partial(_ssm_kernel, nb=nb, tc=tc)
    cs = SSM_CHUNK_STATES
    n_states = SSM_GROUPS * SSM_STATE
    return pl.pallas_call(
        kern,
        out_shape=(
            jax.ShapeDtypeStruct((N_SSM_CHUNKS, n_rows, LANES), F32),
            jax.ShapeDtypeStruct((nb, n_states), F32),
            jax.ShapeDtypeStruct((nb, n_states), F32),
        ),
        grid=(N_SSM_CHUNKS, nt),
        in_specs=[
            pl.BlockSpec((None, rows, LANES), lambda c, t: (c, t, 0)),
            pl.BlockSpec((None, SSM_CHUNK, cs), lambda c, t: (c0 + c, 0, 0)),
            pl.BlockSpec((None, SSM_CHUNK, cs), lambda c, t: (c0 + c, 0, 0)),
            pl.BlockSpec((None, cs, SSM_CHUNK), lambda c, t: (c0 + c, 0, 0)),
            pl.BlockSpec((None, cs, SSM_CHUNK), lambda c, t: (c0 + c, 0, 0)),
            pl.BlockSpec((1, cs), lambda c, t: (0, c0 + c)),
            pl.BlockSpec((1, cs), lambda c, t: (0, c0 + c)),
            pl.BlockSpec((1, SSM_CHUNK), lambda c, t: (0, c0 + c)),
            pl.BlockSpec((nb, cs), lambda c, t: (0, c)),
            pl.BlockSpec((nb, cs), lambda c, t: (0, c)),
        ],
        out_specs=(
            pl.BlockSpec((None, rows, LANES), lambda c, t: (c, t, 0)),
            pl.BlockSpec((nb, cs), lambda c, t: (0, c)),
            pl.BlockSpec((nb, cs), lambda c, t: (0, c)),
        ),
        scratch_shapes=[
            pltpu.VMEM((rows, cs), F32),
            pltpu.VMEM((rows, cs), F32),
            pltpu.VMEM((nb, cs), F32),
            pltpu.VMEM((nb, cs), F32),
        ],
        compiler_params=_cparams(("arbitrary", "arbitrary")),
        name="s5_mixer",
    )(u_chunks, prm["bbd_re"], prm["bbd_im"], prm["cbd_re"], prm["cbd_im"], prm["ab_re"], prm["ab_im"],
      prm["d"], s0_re, s0_im)


def _ssm_params(a_re, a_im, log_dt, b_re, b_im, c_re, c_im, d_skip):
    dt = jnp.exp(log_dt)
    mag = jnp.exp(a_re * dt)
    ab_re = mag * jnp.cos(a_im * dt)
    ab_im = mag * jnp.sin(a_im * dt)
    den = a_re * a_re + a_im * a_im
    f_re = ((ab_re - 1.0) * a_re + ab_im * a_im) / den
    f_im = (ab_im * a_re - (ab_re - 1.0) * a_im) / den
    bb_re = f_re[..., None] * b_re - f_im[..., None] * b_im
    bb_im = f_re[..., None] * b_im + f_im[..., None] * b_re
    gpc = SSM_CHUNK // SSM_GROUP_CH
    eye = jnp.eye(gpc, dtype=F32)

    def b_blockdiag(bb):
        x = bb.reshape(N_SSM_CHUNKS, gpc, SSM_STATE, SSM_GROUP_CH)
        m = jnp.einsum("kgpc,gh->kgchp", x, eye)
        return m.reshape(N_SSM_CHUNKS, gpc * SSM_GROUP_CH, gpc * SSM_STATE).astype(BF16)

    def c_blockdiag(cc):
        x = cc.reshape(N_SSM_CHUNKS, gpc, SSM_GROUP_CH, SSM_STATE)
        m = jnp.einsum("kgcp,gh->kgphc", x, eye)
        return m.reshape(N_SSM_CHUNKS, gpc * SSM_STATE, gpc * SSM_GROUP_CH).astype(BF16)

    return {
        "bbd_re": b_blockdiag(bb_re), "bbd_im": b_blockdiag(bb_im),
        "cbd_re": c_blockdiag(c_re), "cbd_im": c_blockdiag(c_im),
        "ab_re": ab_re.reshape(1, -1), "ab_im": ab_im.reshape(1, -1),
        "d": d_skip.reshape(1, -1),
    }


def _rank_lt(vals, a):
    cnt = jnp.zeros(vals[a].shape, I32)
    for b in range(len(vals)):
        if b == a:
            continue
        before = (vals[b] >= vals[a]) if b < a else (vals[b] > vals[a])
        cnt = cnt + before.astype(I32)
    return cnt


def _router(x1, rwt_ref, rb_ref, eid_ref, gate_ref, cols):
    logits = lax.dot_general(rwt_ref[...], x1, (((1,), (1,)), ((), ())), precision=HIGHEST,
                             preferred_element_type=F32)
    mx = jnp.max(logits, axis=0, keepdims=True)
    ex = jnp.exp(logits - mx)
    probs = ex / jnp.sum(ex, axis=0, keepdims=True)
    biased = probs + rb_ref[...]
    prow = [probs[e:e + 1, :] for e in range(N_EXPERTS)]
    brow = [biased[e:e + 1, :] for e in range(N_EXPERTS)]
    ranks = []
    gscore = []
    for gi in range(N_EXPERT_GROUPS):
        vals = brow[gi * EXPERTS_PER_GROUP:(gi + 1) * EXPERTS_PER_GROUP]
        rk = [_rank_lt(vals, a) for a in range(EXPERTS_PER_GROUP)]
        ranks.append(rk)
        sc = jnp.zeros(vals[0].shape, F32)
        for a in range(EXPERTS_PER_GROUP):
            sc = sc + jnp.where(rk[a] < 2, vals[a], 0.0)
        gscore.append(sc)
    e0 = jnp.zeros(prow[0].shape, I32)
    e1 = jnp.zeros(prow[0].shape, I32)
    p0 = jnp.zeros(prow[0].shape, F32)
    p1 = jnp.zeros(prow[0].shape, F32)
    for gi in range(N_EXPERT_GROUPS):
        chosen = _rank_lt(gscore, gi) == 0
        for a in range(EXPERTS_PER_GROUP):
            e = gi * EXPERTS_PER_GROUP + a
            first = chosen & (ranks[gi][a] == 0)
            second = chosen & (ranks[gi][a] == 1)
            e0 = jnp.where(first, e, e0)
            e1 = jnp.where(second, e, e1)
            p0 = jnp.where(first, prow[e], p0)
            p1 = jnp.where(second, prow[e], p1)
    tot = p0 + p1
    eid_ref[0:1, cols] = e0
    eid_ref[1:2, cols] = e1
    gate_ref[0:1, cols] = p0 / tot
    gate_ref[1:2, cols] = p1 / tot


def _postmix_kernel(att_ref, y_ref, x_ref, wglu_ref, wout_ref, ag_ref, sg_ref, lg_ref, lb_ref, rwt_ref, rb_ref,
                    x1_ref, eid_ref, gate_ref, ybuf_ref, *, n_b, tt, alpha):
    rows = x1_ref.shape[0] if n_b is None else n_b * tt
    if n_b is None:
        att = att_ref[...]
        x = x_ref[...]
        for c in range(N_SSM_CHUNKS):
            ybuf_ref[:, c * LANES:(c + 1) * LANES] = y_ref[c]
    else:
        for c in range(N_SSM_CHUNKS):
            for b in range(n_b):
                ybuf_ref[b * tt:(b + 1) * tt, c * LANES:(c + 1) * LANES] = y_ref[c, pl.ds(b, tt, stride=n_b), :]
    z_all = jnp.dot(ybuf_ref[...].astype(BF16), wglu_ref[...], preferred_element_type=F32)
    hrows = rows // POSTMIX_SPLIT
    for hf in range(POSTMIX_SPLIT):
        r0 = hf * hrows
        if n_b is None:
            att_h = att[r0:r0 + hrows, :]
            x_h = x[r0:r0 + hrows, :]
        else:
            att_h = att_ref[hf * (n_b // POSTMIX_SPLIT):(hf + 1) * (n_b // POSTMIX_SPLIT)].reshape(hrows, ATTN_WIDTH)
            x_h = x_ref[hf * (n_b // POSTMIX_SPLIT):(hf + 1) * (n_b // POSTMIX_SPLIT)].reshape(hrows, D_MODEL)
        y = ybuf_ref[r0:r0 + hrows, :]
        sg = y * jax.nn.sigmoid(z_all[r0:r0 + hrows, :])
        ssm_n = sg * lax.rsqrt(jnp.mean(sg * sg, axis=-1, keepdims=True) + RMS_EPS) * sg_ref[...]
        att_n = att_h * lax.rsqrt(jnp.mean(att_h * att_h, axis=-1, keepdims=True) + RMS_EPS) * ag_ref[...]
        mix = (jnp.dot(att_n.astype(BF16), wout_ref[0:ATTN_WIDTH, :], preferred_element_type=F32)
               + jnp.dot(ssm_n.astype(BF16), wout_ref[ATTN_WIDTH:, :], preferred_element_type=F32))
        r = alpha * x_h + mix
        mu = jnp.mean(r, axis=-1, keepdims=True)
        rc = r - mu
        var = jnp.mean(rc * rc, axis=-1, keepdims=True)
        x1 = rc * lax.rsqrt(var + LN_EPS) * lg_ref[...] + lb_ref[...]
        if n_b is None:
            x1_ref[r0:r0 + hrows, :] = x1
        else:
            nbh = n_b // POSTMIX_SPLIT
            x1_ref[hf * nbh:(hf + 1) * nbh] = x1.reshape(nbh, tt, D_MODEL)
        _router(x1, rwt_ref, rb_ref, eid_ref, gate_ref, slice(r0, r0 + hrows))


def _postmix(att, y_chunks, x, lw, layer, alpha, prompt):
    if prompt:
        n_b, seq, _ = x.shape
        tt = TILE_T
        nt = seq // tt
        rows = n_b * tt
        n_tok = n_b * seq
        att_spec = pl.BlockSpec((n_b, tt, ATTN_WIDTH), lambda i: (0, i, 0))
        x_spec = pl.BlockSpec((n_b, tt, D_MODEL), lambda i: (0, i, 0))
        x1_shape = jax.ShapeDtypeStruct((n_b, seq, D_MODEL), F32)
        kern = functools.partial(_postmix_kernel, n_b=n_b, tt=tt, alpha=alpha)
    else:
        n_tok = x.shape[0]
        rows = 256
        nt = n_tok // rows
        att_spec = pl.BlockSpec((rows, ATTN_WIDTH), lambda i: (i, 0))
        x_spec = pl.BlockSpec((rows, D_MODEL), lambda i: (i, 0))
        x1_shape = jax.ShapeDtypeStruct((n_tok, D_MODEL), F32)
        kern = functools.partial(_postmix_kernel, n_b=None, tt=None, alpha=alpha)
    full = lambda shp: pl.BlockSpec(shp, lambda i: tuple(0 for _ in shp))
    return pl.pallas_call(
        kern,
        out_shape=(x1_shape, jax.ShapeDtypeStruct((2, n_tok), I32), jax.ShapeDtypeStruct((2, n_tok), F32)),
        grid=(nt,),
        in_specs=[
            att_spec,
            pl.BlockSpec((N_SSM_CHUNKS, rows, LANES), lambda i: (0, i, 0)),
            x_spec,
            pl.BlockSpec((None, SSM_WIDTH, SSM_WIDTH), lambda i: (layer, 0, 0), pipeline_mode=pl.Buffered(1)),
            pl.BlockSpec((None, D_MODEL, D_MODEL), lambda i: (layer, 0, 0), pipeline_mode=pl.Buffered(1)),
            full((1, ATTN_WIDTH)),
            full((1, SSM_WIDTH)),
            full((1, D_MODEL)),
            full((1, D_MODEL)),
            full((N_EXPERTS, D_MODEL)),
            full((N_EXPERTS, 1)),
        ],
        out_specs=(x_spec, pl.BlockSpec((2, rows), lambda i: (0, i)), pl.BlockSpec((2, rows), lambda i: (0, i))),
        scratch_shapes=[pltpu.VMEM((rows, SSM_WIDTH), F32)],
        compiler_params=_cparams(("arbitrary",)),
        name="postmix_prompt" if prompt else "postmix_sample",
    )(att, y_chunks, x, lw["w_glu"], lw["w_out"], lw["attn_g"], lw["ssm_g"], lw["ln1_g"], lw["ln1_b"],
      lw["router_wt"], lw["router_b"])


def _slab_rows(row, slab):
    start = row * slab
    return pl.ds(start if isinstance(start, int) else pl.multiple_of(start, slab), slab)


def _slab_copy(src_ref, src_row, dst_ref, dst_row, sem, slab=SLAB):
    return pltpu.make_async_copy(src_ref.at[_slab_rows(src_row, slab), :], dst_ref.at[_slab_rows(dst_row, slab), :],
                                 sem)


def _pack_bf16_pair(a, b):
    hi = lax.bitcast_convert_type(a.astype(BF16).astype(F32), jnp.uint32)
    lo = lax.bitcast_convert_type(b.astype(BF16).astype(F32), jnp.uint32)
    return hi | (lo >> 16)


def _unpack_bf16_pair(w):
    a = lax.bitcast_convert_type(w & jnp.uint32(0xFFFF0000), F32)
    b = lax.bitcast_convert_type(w << 16, F32)
    return a.astype(BF16), b.astype(BF16)


def _dispatch_kernel(dest_ref, pad_ref, xp_ref, xs_ref, out_ref, slab_ref, sem, *, n_b, tt, n_prompt_tiles, n_blocks):
    i = pl.program_id(0)
    last = pl.num_programs(0) - 1
    rows = n_b * tt
    slot = i % 2
    slab = slab_ref.at[slot]

    half = D_MODEL // 2

    @pl.when(i < n_prompt_tiles)
    def _():
        for s in range(XSLAB):
            for b in range(n_b):
                slab[pl.ds(b * tt * XSLAB + s, tt, stride=XSLAB), :] = _pack_bf16_pair(
                    xp_ref[b, :, s * LANES:(s + 1) * LANES], xp_ref[b, :, half + s * LANES:half + (s + 1) * LANES])

    @pl.when(i >= n_prompt_tiles)
    def _():
        for s in range(XSLAB):
            slab[pl.ds(s, rows, stride=XSLAB), :] = _pack_bf16_pair(
                xs_ref[:, s * LANES:(s + 1) * LANES], xs_ref[:, half + s * LANES:half + (s + 1) * LANES])

    base = i * (2 * rows)

    def start(r, c):
        for k in range(2):
            _slab_copy(slab, r, out_ref, dest_ref[base + 2 * r + k], sem.at[slot], XSLAB).start(priority=k)
        return c

    lax.fori_loop(0, rows, start, 0, unroll=DMA_ISSUE_UNROLL)

    def wait_tile(which):
        for _ in range(2):
            pltpu.make_async_copy(slab_ref.at[which], slab_ref.at[which], sem.at[which]).wait()

    @pl.when(i > 0)
    def _():
        wait_tile(1 - slot)

    @pl.when(i == last)
    def _():
        wait_tile(slot)
        slab[...] = jnp.zeros(slab.shape, jnp.uint32)
        zsem = sem.at[slot]

        def per_expert(e, c):
            lo = pad_ref[e]
            hi = pad_ref[N_EXPERTS + e]

            def zs(s, c2):
                _slab_copy(slab, 0, out_ref, s, zsem, XSLAB).start()
                return c2

            lax.fori_loop(lo, hi, zs, 0)

            def zw(s, c2):
                _slab_copy(slab, 0, out_ref, s, zsem, XSLAB).wait()
                return c2

            lax.fori_loop(lo, hi, zw, 0)
            return c

        lax.fori_loop(0, N_EXPERTS, per_expert, 0)

        blk_rows = MOE_BLOCK * XSLAB

        def block_copy(b):
            start = pl.multiple_of(b * blk_rows, blk_rows)
            return pltpu.make_async_copy(slab.at[pl.ds(0, blk_rows), :], out_ref.at[pl.ds(start, blk_rows), :], zsem)

        def zbs(b, c):
            block_copy(b).start()
            return c

        def zbw(b, c):
            block_copy(b).wait()
            return c

        first_unused = pad_ref[2 * N_EXPERTS - 1] // MOE_BLOCK
        lax.fori_loop(first_unused, n_blocks, zbs, 0)
        lax.fori_loop(first_unused, n_blocks, zbw, 0)


def _dispatch(x1_p, x1_s, dest, pad_bounds, n_slots):
    n_b, seq, _ = x1_p.shape
    tt = TILE_T
    rows = n_b * tt
    assert rows >= MOE_BLOCK and x1_s.shape[0] % rows == 0
    ntp = seq // tt
    nts = x1_s.shape[0] // rows
    kern = functools.partial(_dispatch_kernel, n_b=n_b, tt=tt, n_prompt_tiles=ntp, n_blocks=n_slots // MOE_BLOCK)
    return pl.pallas_call(
        kern,
        out_shape=jax.ShapeDtypeStruct((n_slots * XSLAB, LANES), jnp.uint32),
        grid_spec=pltpu.PrefetchScalarGridSpec(
            num_scalar_prefetch=2,
            grid=(ntp + nts,),
            in_specs=[
                pl.BlockSpec((n_b, tt, D_MODEL), lambda i, d, p: (0, jnp.minimum(i, ntp - 1), 0)),
                pl.BlockSpec((rows, D_MODEL), lambda i, d, p: (jnp.maximum(i - ntp, 0), 0)),
            ],
            out_specs=pl.BlockSpec(memory_space=pl.ANY),
            scratch_shapes=[
                pltpu.VMEM((2, rows * XSLAB, LANES), jnp.uint32),
                pltpu.SemaphoreType.DMA((2,)),
            ],
        ),
        compiler_params=_cparams(("arbitrary",)),
        name="moe_dispatch",
    )(dest, pad_bounds, x1_p, x1_s)


def _new_expert(bexp_ref, i):
    return (i == 0) | (bexp_ref[i] != bexp_ref[jnp.maximum(i - 1, 0)])


def _ffn_up_kernel(bexp_ref, nused_ref, xs_ref, wg_ref, wu_ref, h_ref, xb_ref, wg16_ref, wu16_ref):
    i = pl.program_id(0)
    rows = MOE_BLOCK

    @pl.when(_new_expert(bexp_ref, i))
    def _():
        wg16_ref[...] = wg_ref[...].astype(BF16)
        wu16_ref[...] = wu_ref[...].astype(BF16)

    @pl.when(i < nused_ref[0])
    def _():
        half = D_MODEL // 2
        for s in range(XSLAB):
            a, b = _unpack_bf16_pair(xs_ref[pl.ds(s, rows, stride=XSLAB), :])
            xb_ref[:, s * LANES:(s + 1) * LANES] = a
            xb_ref[:, half + s * LANES:half + (s + 1) * LANES] = b
        xb = xb_ref[...]
        g = jnp.dot(xb, wg16_ref[...], preferred_element_type=F32)
        u = jnp.dot(xb, wu16_ref[...], preferred_element_type=F32)
        h_ref[...] = (g * jax.nn.sigmoid(g) * u).astype(BF16)

    @pl.when(i >= nused_ref[0])
    def _():
        h_ref[...] = jnp.zeros(h_ref.shape, BF16)


def _ffn_down_kernel(bexp_ref, nused_ref, h_ref, wd_ref, ys_ref, wd16_ref):
    i = pl.program_id(0)
    rows = MOE_BLOCK

    @pl.when(_new_expert(bexp_ref, i))
    def _():
        wd16_ref[...] = wd_ref[...].astype(BF16)

    @pl.when(i < nused_ref[0])
    def _():
        y = jnp.dot(h_ref[...], wd16_ref[...], preferred_element_type=F32)
        for s in range(SLAB):
            ys_ref[pl.ds(s, rows, stride=SLAB), :] = y[:, s * LANES:(s + 1) * LANES]

    @pl.when(i >= nused_ref[0])
    def _():
        ys_ref[...] = jnp.zeros(ys_ref.shape, F32)


def _ffn(xs, w_gate, w_up, w_down, blk_exp, n_used, n_blocks, layer):
    rows = MOE_BLOCK
    e0 = layer * N_EXPERTS
    n_slots = n_blocks * rows

    def live_map(i, be, nu):
        return (jnp.minimum(i, nu[0] - 1), 0)

    def w_map(i, be, nu):
        return (e0 + be[i], 0, 0)

    h = pl.pallas_call(
        _ffn_up_kernel,
        out_shape=jax.ShapeDtypeStruct((n_slots, D_FF), BF16),
        grid_spec=pltpu.PrefetchScalarGridSpec(
            num_scalar_prefetch=2,
            grid=(n_blocks,),
            in_specs=[
                pl.BlockSpec((rows * XSLAB, LANES), live_map),
                pl.BlockSpec((None, D_MODEL, D_FF), w_map),
                pl.BlockSpec((None, D_MODEL, D_FF), w_map),
            ],
            out_specs=pl.BlockSpec((rows, D_FF), lambda i, be, nu: (i, 0)),
            scratch_shapes=[
                pltpu.VMEM((rows, D_MODEL), BF16),
                pltpu.VMEM((D_MODEL, D_FF), BF16),
                pltpu.VMEM((D_MODEL, D_FF), BF16),
            ],
        ),
        compiler_params=_cparams(("arbitrary",)),
        name="moe_ffn_up",
    )(blk_exp, n_used, xs, w_gate, w_up)
    return pl.pallas_call(
        _ffn_down_kernel,
        out_shape=jax.ShapeDtypeStruct((n_slots * SLAB, LANES), F32),
        grid_spec=pltpu.PrefetchScalarGridSpec(
            num_scalar_prefetch=2,
            grid=(n_blocks,),
            in_specs=[
                pl.BlockSpec((rows, D_FF), live_map),
                pl.BlockSpec((None, D_FF, D_MODEL), w_map),
            ],
            out_specs=pl.BlockSpec((rows * SLAB, LANES), lambda i, be, nu: (i, 0)),
            scratch_shapes=[pltpu.VMEM((D_FF, D_MODEL), BF16)],
        ),
        compiler_params=_cparams(("arbitrary",)),
        name="moe_ffn_down",
    )(blk_exp, n_used, h, w_down)


def _combine_kernel(dest_ref, x1_ref, gate_ref, lg_ref, lb_ref, ys_ref, x2_ref, y0_ref, y1_ref, sem, *, n_b, tt, alpha):
    i = pl.program_id(0)
    rows = y0_ref.shape[1] // SLAB
    slot = i % 2

    def fetch(tile, which):
        base = tile * (2 * rows)

        def start(r, c):
            _slab_copy(ys_ref, dest_ref[base + 2 * r], y0_ref.at[which], r, sem.at[which]).start(priority=0)
            _slab_copy(ys_ref, dest_ref[base + 2 * r + 1], y1_ref.at[which], r, sem.at[which]).start(priority=1)
            return c

        lax.fori_loop(0, rows, start, 0, unroll=DMA_ISSUE_UNROLL)

    @pl.when(i == 0)
    def _():
        fetch(0, 0)

    @pl.when(i + 1 < pl.num_programs(0))
    def _():
        fetch(i + 1, 1 - slot)

    pltpu.make_async_copy(y0_ref.at[slot], y0_ref.at[slot], sem.at[slot]).wait()
    pltpu.make_async_copy(y1_ref.at[slot], y1_ref.at[slot], sem.at[slot]).wait()

    y0 = y0_ref.at[slot]
    y1 = y1_ref.at[slot]
    gate_rows = jnp.concatenate([gate_ref[...], jnp.zeros((SUBLANES - 2, rows), F32)], axis=0)
    gate_cols = gate_rows.T
    g0 = gate_cols[:, 0:1]
    g1 = gate_cols[:, 1:2]
    if n_b is None:
        x1 = x1_ref[...]
    else:
        x1 = x1_ref[...].reshape(rows, D_MODEL)
    parts = []
    for s in range(SLAB):
        moe = g0 * y0[pl.ds(s, rows, stride=SLAB), :] + g1 * y1[pl.ds(s, rows, stride=SLAB), :]
        parts.append(alpha * x1[:, s * LANES:(s + 1) * LANES] + moe)
    r = jnp.concatenate(parts, axis=1)
    mu = jnp.mean(r, axis=-1, keepdims=True)
    rc = r - mu
    var = jnp.mean(rc * rc, axis=-1, keepdims=True)
    x2 = rc * lax.rsqrt(var + LN_EPS) * lg_ref[...] + lb_ref[...]
    if n_b is None:
        x2_ref[...] = x2
    else:
        x2_ref[...] = x2.reshape(n_b, tt, D_MODEL)


def _combine(x1, gate_t, dest, ys, ln_g, ln_b, alpha, prompt):
    if prompt:
        n_b, seq, _ = x1.shape
        tt = TILE_T
        nbs = COMBINE_ROWS // tt
        parts = n_b // nbs
        rows = nbs * tt
        nt = (seq // tt) * parts
        x_spec = pl.BlockSpec((nbs, tt, D_MODEL), lambda i, d: (i % parts, i // parts, 0))
        kern = functools.partial(_combine_kernel, n_b=nbs, tt=tt, alpha=alpha)
    else:
        rows = 256
        nt = x1.shape[0] // rows
        x_spec = pl.BlockSpec((rows, D_MODEL), lambda i, d: (i, 0))
        kern = functools.partial(_combine_kernel, n_b=None, tt=None, alpha=alpha)
    return pl.pallas_call(
        kern,
        out_shape=jax.ShapeDtypeStruct(x1.shape, F32),
        grid_spec=pltpu.PrefetchScalarGridSpec(
            num_scalar_prefetch=1,
            grid=(nt,),
            in_specs=[
                x_spec,
                pl.BlockSpec((2, rows), lambda i, d: (0, i)),
                pl.BlockSpec((1, D_MODEL), lambda i, d: (0, 0)),
                pl.BlockSpec((1, D_MODEL), lambda i, d: (0, 0)),
                pl.BlockSpec(memory_space=pl.ANY),
            ],
            out_specs=x_spec,
            scratch_shapes=[
                pltpu.VMEM((2, rows * SLAB, LANES), F32),
                pltpu.VMEM((2, rows * SLAB, LANES), F32),
                pltpu.SemaphoreType.DMA((2,)),
            ],
        ),
        compiler_params=_cparams(("arbitrary",)),
        name="moe_combine_prompt" if prompt else "moe_combine_sample",
    )(dest, x1, gate_t, ln_g, ln_b, ys)


def _moe_plan(eid_t, n_blocks):
    e_flat = eid_t.T.reshape(-1)
    onehot = (e_flat[:, None] == jnp.arange(N_EXPERTS, dtype=I32)[None, :]).astype(I32)
    csum = jnp.cumsum(onehot, axis=0)
    rank = jnp.sum(csum * onehot, axis=1) - 1
    counts = csum[-1]
    padded = (counts + MOE_BLOCK - 1) // MOE_BLOCK * MOE_BLOCK
    pad_end = jnp.cumsum(padded)
    pad_start = pad_end - padded
    dest = (jnp.sum(onehot * pad_start[None, :], axis=1) + rank).astype(I32)
    n_used = (pad_end[-1] // MOE_BLOCK).astype(I32)
    first_slot = jnp.arange(n_blocks, dtype=I32) * MOE_BLOCK
    blk_exp = jnp.minimum(jnp.sum((first_slot[:, None] >= pad_end[None, :]).astype(I32), axis=1), N_EXPERTS - 1)
    last_exp = jnp.take(blk_exp, jnp.maximum(n_used - 1, 0))
    blk_exp = jnp.where(jnp.arange(n_blocks) < n_used, blk_exp, last_exp).astype(I32)
    pad_bounds = jnp.concatenate([pad_start + counts, pad_end]).astype(I32)
    return dest, pad_bounds, blk_exp, n_used.reshape(1)


def kernel(x_prompt, x_sample, cache_k, cache_v, state_ssm_re, state_ssm_im, page_table, w_in, w_out, attn_norm_g, ssm_norm_g, ssm_a_re, ssm_a_im, ssm_log_dt, ssm_b_re, ssm_b_im, ssm_c_re, ssm_c_im, ssm_d, ssm_w_glu, ln1_g, ln1_b, router_w, router_b, moe_w_gate, moe_w_up, moe_w_down, ln2_g, ln2_b):
    depth = w_in.shape[0]
    n_b, seq, _ = x_prompt.shape
    n_dec, t_new, _ = x_sample.shape
    n_pool = cache_k.shape[1]
    gq = N_HEADS // N_KV_HEADS
    alpha = (2 * depth) ** 0.25
    n_states = SSM_GROUPS * SSM_STATE
    n_sample = n_dec * t_new
    n_tok = n_b * seq + n_sample
    n_blocks = -(-(2 * n_tok + N_EXPERTS * (MOE_BLOCK - 1)) // MOE_BLOCK)
    n_slots = n_blocks * MOE_BLOCK

    slopes = 2.0 ** (-8.0 * jnp.arange(1, N_HEADS + 1, dtype=F32) / N_HEADS)
    cache_k_flat = cache_k.reshape(-1, HEAD_DIM)
    cache_v_flat = cache_v.reshape(-1, HEAD_DIM)
    router_wt = router_w.T
    router_bc = router_b.reshape(N_EXPERTS, 1)
    zeros_state = jnp.zeros((n_b, n_states), F32)
    w_in16 = w_in.astype(BF16)
    w_glu16 = ssm_w_glu.astype(BF16)
    w_out16 = w_out.astype(BF16)
    w_gate = moe_w_gate.reshape(depth * N_EXPERTS, D_MODEL, D_FF)
    w_up = moe_w_up.reshape(depth * N_EXPERTS, D_MODEL, D_FF)
    w_down = moe_w_down.reshape(depth * N_EXPERTS, D_FF, D_MODEL)

    xp = x_prompt
    xs = x_sample.transpose(1, 0, 2).reshape(n_sample, D_MODEL)
    outs = {k: [] for k in ("kp", "vp", "hrp", "hip", "ks", "vs", "hrs", "his")}
    prm = jax.vmap(_ssm_params)(ssm_a_re, ssm_a_im, ssm_log_dt, ssm_b_re, ssm_b_im, ssm_c_re, ssm_c_im, ssm_d)
    prm = {k: (v.reshape((depth * N_SSM_CHUNKS,) + v.shape[2:]) if v.ndim == 4 else v.reshape(1, -1))
           for k, v in prm.items()}
    for l in range(depth):
        lw = {
            "w_glu": w_glu16, "w_out": w_out16,
            "attn_g": attn_norm_g[l].reshape(1, -1), "ssm_g": ssm_norm_g[l].reshape(1, -1),
            "ln1_g": ln1_g[l].reshape(1, -1), "ln1_b": ln1_b[l].reshape(1, -1),
            "router_wt": router_wt, "router_b": router_bc,
        }
        q_p, k_p, v_p, u_p = _in_proj_prompt(xp, w_in16, l)
        att_p = _attn_prompt(q_p, k_p, v_p, slopes)
        y_p, hr_p, hi_p = _ssm(u_p, zeros_state, zeros_state, prm, l, nb=n_b, tc=512)

        h_s = _in_proj_sample(xs, w_in16, l)
        q_s = h_s[:, :ATTN_WIDTH].reshape(t_new, n_dec, N_HEADS, HEAD_DIM)
        q_s = q_s.transpose(1, 2, 0, 3).reshape(n_dec, N_HEADS * t_new, HEAD_DIM)
        k_s = h_s[:, ATTN_WIDTH:ATTN_WIDTH + KV_WIDTH].reshape(t_new, n_dec, N_KV_HEADS, HEAD_DIM)
        v_s = h_s[:, ATTN_WIDTH + KV_WIDTH:ATTN_WIDTH + 2 * KV_WIDTH].reshape(t_new, n_dec, N_KV_HEADS, HEAD_DIM)
        att_s = _attn_sample(q_s, k_s.transpose(1, 2, 0, 3), v_s.transpose(1, 2, 0, 3), cache_k_flat, cache_v_flat,
                             page_table, slopes, l, n_pool)
        att_s = att_s.reshape(n_dec, N_HEADS, t_new, HEAD_DIM).transpose(2, 0, 1, 3).reshape(n_sample, ATTN_WIDTH)
        u_s = h_s[:, ATTN_WIDTH + 2 * KV_WIDTH:].reshape(n_sample, N_SSM_CHUNKS, LANES).transpose(1, 0, 2)
        y_s, hr_s, hi_s = _ssm(u_s, state_ssm_re[l].reshape(n_dec, n_states), state_ssm_im[l].reshape(n_dec, n_states),
                               prm, l, nb=n_dec, tc=t_new)

        x1_p, eid_p, gate_p = _postmix(att_p, y_p, xp, lw, l, alpha, prompt=True)
        x1_s, eid_s, gate_s = _postmix(att_s, y_s, xs, lw, l, alpha, prompt=False)

        eid_t = jnp.concatenate([eid_p, eid_s], axis=1)
        dest, pad_bounds, blk_exp, n_used = _moe_plan(eid_t, n_blocks)
        xs_slots = _dispatch(x1_p, x1_s, dest, pad_bounds, n_slots)
        ys_slots = _ffn(xs_slots, w_gate, w_up, w_down, blk_exp, n_used, n_blocks, l)
        l2g = ln2_g[l].reshape(1, -1)
        l2b = ln2_b[l].reshape(1, -1)
        n_pa = 2 * n_b * seq
        xp = _combine(x1_p, gate_p, dest[:n_pa], ys_slots, l2g, l2b, alpha, prompt=True)
        xs = _combine(x1_s, gate_s, dest[n_pa:], ys_slots, l2g, l2b, alpha, prompt=False)

        outs["kp"].append(k_p.reshape(n_b, seq, N_KV_HEADS, HEAD_DIM))
        outs["vp"].append(v_p.reshape(n_b, seq, N_KV_HEADS, HEAD_DIM))
        outs["hrp"].append(hr_p.reshape(n_b, SSM_GROUPS, SSM_STATE))
        outs["hip"].append(hi_p.reshape(n_b, SSM_GROUPS, SSM_STATE))
        outs["ks"].append(k_s.transpose(1, 0, 2, 3))
        outs["vs"].append(v_s.transpose(1, 0, 2, 3))
        outs["hrs"].append(hr_s.reshape(n_dec, SSM_GROUPS, SSM_STATE))
        outs["his"].append(hi_s.reshape(n_dec, SSM_GROUPS, SSM_STATE))

    y_sample = xs.reshape(t_new, n_dec, D_MODEL).transpose(1, 0, 2)
    return (xp, y_sample,
            jnp.stack(outs["kp"]), jnp.stack(outs["vp"]), jnp.stack(outs["hrp"]), jnp.stack(outs["hip"]),
            jnp.stack(outs["ks"]), jnp.stack(outs["vs"]), jnp.stack(outs["hrs"]), jnp.stack(outs["his"]))
```
